```python
import math
import jax, jax.numpy as jnp
from jax import lax
import numpy as np

D_MODEL = 1024
BATCH = 4
SEQ = 4096
DEPTH = 4

N_EVEN = (DEPTH + 1) // 2
N_ODD = DEPTH // 2
EPS = 1e-6
ROPE_THETA = 10000.0
NEG_INF = -1e30
TINY = 1e-30
FORCE_SCORE = 1e9
Q_BLOCK = 128

S5_WIDTH = D_MODEL // 2
S5_GROUP = 16
S5_GROUPS = S5_WIDTH // S5_GROUP
S5_STATE = 64
DT_MIN = 1e-3
DT_MAX = 1e-1

MLA_HEADS = 4
MLA_NOPE = 128
MLA_ROPE = 64
MLA_V = 128
MLA_WIDTH = MLA_HEADS * MLA_V
MLA_KV_RANK = D_MODEL // 4
MLA_Q_RANK = 3 * MLA_KV_RANK
EVEN_SPLITS = (S5_WIDTH, S5_WIDTH, MLA_Q_RANK, MLA_KV_RANK, MLA_ROPE, MLA_WIDTH)
EVEN_IN = sum(EVEN_SPLITS)
EVEN_MIX = S5_WIDTH + MLA_WIDTH

NSA_HEADS = 8
NSA_GROUPS = 2
NSA_REP = NSA_HEADS // NSA_GROUPS
NSA_DIM = 128
NSA_WIDTH = NSA_HEADS * NSA_DIM
NSA_KV = NSA_GROUPS * NSA_DIM
CMP_LEN = 32
CMP_STRIDE = 16
CMP_HIDDEN = 256
SEL_BLOCK = 64
SEL_TOP = 16
SEL_Q_CHUNK = 32
WINDOW = 512
ODD_SPLITS = (NSA_WIDTH, NSA_KV, NSA_KV, NSA_KV, NSA_KV, NSA_KV, NSA_KV, 3 * NSA_HEADS, NSA_WIDTH)
ODD_IN = sum(ODD_SPLITS)

kernel_name = 'hybrid_s5_mla_nsa_adaln_trunk'


def _split(x, sizes):
    bounds = [int(v) for v in np.cumsum(sizes)[:-1]]
    return jnp.split(x, bounds, axis=-1)


def rms_norm(x, g):
    xf = x.astype(jnp.float32)
    y = xf * lax.rsqrt(jnp.mean(xf * xf, axis=-1, keepdims=True) + EPS)
    return (y * g.astype(jnp.float32)).astype(x.dtype)


def rope(x, pos):
    half = x.shape[-1] // 2
    inv_freq = ROPE_THETA ** (-jnp.arange(half, dtype=jnp.float32) / half)
    ang = pos.astype(jnp.float32)[..., None] * inv_freq
    ang = ang.reshape(ang.shape[:2] + (1,) * (x.ndim - 3) + (half,))
    cos, sin = jnp.cos(ang), jnp.sin(ang)
    xf = x.astype(jnp.float32)
    x1, x2 = xf[..., :half], xf[..., half:]
    return jnp.concatenate([x1 * cos - x2 * sin, x2 * cos + x1 * sin], axis=-1).astype(x.dtype)


def masked_probs(scores, mask):
    s = jnp.where(mask, scores.astype(jnp.float32), NEG_INF)
    m = jnp.max(s, axis=-1, keepdims=True)
    e = jnp.where(mask, jnp.exp(s - m), 0.0)
    return e / jnp.maximum(jnp.sum(e, axis=-1, keepdims=True), TINY)


def modulation(c, w_ada, b_ada):
    mod = (jax.nn.silu(c) @ w_ada + b_ada)[:, None, :]
    shift, scale, gate = jnp.split(mod, 3, axis=-1)
    return shift, scale, gate


def _diag_combine(left, right):
    a1r, a1i, b1r, b1i = left
    a2r, a2i, b2r, b2i = right
    return (a2r * a1r - a2i * a1i, a2r * a1i + a2i * a1r,
            a2r * b1r - a2i * b1i + b2r, a2r * b1i + a2i * b1r + b2i)


def s5_mixer(u, lam_re, lam_im, log_dt, b_re, b_im, c_re, c_im, d_skip, w_glu, b_glu):
    bsz, seq, _ = u.shape
    f32 = jnp.float32
    lr, li = lam_re.astype(f32), lam_im.astype(f32)
    dt = jnp.exp(log_dt.astype(f32))[:, None]
    decay = jnp.exp(lr * dt)
    ab_re, ab_im = decay * jnp.cos(li * dt), decay * jnp.sin(li * dt)
    den = lr * lr + li * li
    nr, ni = ab_re - 1.0, ab_im
    f_re = (nr * lr + ni * li) / den
    f_im = (ni * lr - nr * li) / den
    br, bi = b_re.astype(f32), b_im.astype(f32)
    bb_re = f_re[..., None] * br - f_im[..., None] * bi
    bb_im = f_re[..., None] * bi + f_im[..., None] * br
    uf = u.astype(f32).reshape(bsz, seq, S5_GROUPS, S5_GROUP)
    bu_re = jnp.einsum('bsgh,gph->sbgp', uf, bb_re)
    bu_im = jnp.einsum('bsgh,gph->sbgp', uf, bb_im)
    a_re = jnp.broadcast_to(ab_re[None, None], (seq, 1, S5_GROUPS, S5_STATE))
    a_im = jnp.broadcast_to(ab_im[None, None], (seq, 1, S5_GROUPS, S5_STATE))
    _, _, xr, xi = lax.associative_scan(_diag_combine, (a_re, a_im, bu_re, bu_im), axis=0)
    y = (jnp.einsum('sbgp,ghp->bsgh', xr, c_re.astype(f32))
         - jnp.einsum('sbgp,ghp->bsgh', xi, c_im.astype(f32)))
    y = (y + d_skip.astype(f32) * uf).reshape(bsz, seq, S5_WIDTH)
    g = jax.nn.gelu(y)
    out = g * jax.nn.sigmoid(g @ w_glu.astype(f32) + b_glu.astype(f32))
    return out.astype(u.dtype)


def mla_mixer(c_q, c_kv, k_pe, pos, q_norm_g, kv_norm_g, w_uq, w_ukv):
    bsz, seq, _ = c_q.shape
    q = (rms_norm(c_q, q_norm_g) @ w_uq).reshape(bsz, seq, MLA_HEADS, MLA_NOPE + MLA_ROPE)
    q_nope, q_pe = q[..., :MLA_NOPE], rope(q[..., MLA_NOPE:], pos)
    kv = (rms_norm(c_kv, kv_norm_g) @ w_ukv).reshape(bsz, seq, MLA_HEADS, MLA_NOPE + MLA_V)
    k_nope, v = kv[..., :MLA_NOPE], kv[..., MLA_NOPE:]
    k_pe = rope(k_pe, pos)
    scale = (MLA_NOPE + MLA_ROPE) ** -0.5
    nb = seq // Q_BLOCK
    qn = q_nope.reshape(bsz, nb, Q_BLOCK, MLA_HEADS, MLA_NOPE).transpose(1, 0, 2, 3, 4)
    qp = q_pe.reshape(bsz, nb, Q_BLOCK, MLA_HEADS, MLA_ROPE).transpose(1, 0, 2, 3, 4)
    k_idx = jnp.arange(seq)

    def block(args):
        qn_b, qp_b, start = args
        s = (jnp.einsum('bqhd,bkhd->bhqk', qn_b, k_nope)
             + jnp.einsum('bqhd,bkd->bhqk', qp_b, k_pe)) * scale
        mask = k_idx[None, :] <= (start + jnp.arange(Q_BLOCK))[:, None]
        p = masked_probs(s, mask)
        return jnp.einsum('bhqk,bkhd->bqhd', p.astype(v.dtype), v)

    o = lax.map(block, (qn, qp, jnp.arange(nb) * Q_BLOCK))
    return o.transpose(1, 0, 2, 3, 4).reshape(bsz, seq, MLA_WIDTH)


def even_mixer(h, pos, w_in, lam_re, lam_im, log_dt, b_re, b_im, c_re, c_im, d_skip,
               w_glu, b_glu, q_norm_g, kv_norm_g, w_uq, w_ukv, w_out):
    u, z_a, c_q, c_kv, k_pe, z_b = _split(h @ w_in, EVEN_SPLITS)
    y_a = s5_mixer(u, lam_re, lam_im, log_dt, b_re, b_im, c_re, c_im, d_skip, w_glu, b_glu) * jax.nn.silu(z_a)
    y_b = mla_mixer(c_q, c_kv, k_pe, pos, q_norm_g, kv_norm_g, w_uq, w_ukv) * jax.nn.silu(z_b)
    return jnp.concatenate([y_a, y_b], axis=-1) @ w_out


def _compress(kv, tok, pe, w1, w2):
    bsz, seq, _ = kv.shape
    n_cmp = tok.shape[0]
    blk = kv.reshape(bsz, seq, NSA_GROUPS, NSA_DIM)[:, tok]
    blk = blk + pe[None, None, :, None, :]
    blk = blk.transpose(0, 1, 3, 2, 4).reshape(bsz, n_cmp, NSA_GROUPS, CMP_LEN * NSA_DIM)
    return jax.nn.gelu(blk @ w1) @ w2


def _to_blocks(t, block):
    b, g, r, s, d = t.shape
    return t.reshape(b, g, r, s // block, block, d).transpose(3, 0, 1, 2, 4, 5)


def _from_blocks(t):
    nb, b, g, r, blk, d = t.shape
    return t.transpose(1, 2, 3, 0, 4, 5).reshape(b, g, r, nb * blk, d)


def nsa_mixer(h, pos, w_in, cmp_k_pe, cmp_k_w1, cmp_k_w2, cmp_v_pe, cmp_v_w1, cmp_v_w2, w_out):
    bsz, seq, _ = h.shape
    G, R, d = NSA_GROUPS, NSA_REP, NSA_DIM
    q, k_c, v_c, k_s, v_s, k_w, v_w, gate_logits, z = _split(h @ w_in, ODD_SPLITS)
    scale = d ** -0.5
    q = rope(q.reshape(bsz, seq, NSA_HEADS, d), pos)
    qg = q.reshape(bsz, seq, G, R, d).transpose(0, 2, 3, 1, 4)
    q_idx = jnp.arange(seq)

    n_cmp = (seq - CMP_LEN) // CMP_STRIDE + 1
    starts = jnp.arange(n_cmp) * CMP_STRIDE
    tok = starts[:, None] + jnp.arange(CMP_LEN)[None, :]
    end_pos = starts + CMP_LEN - 1
    kc = rope(_compress(k_c, tok, cmp_k_pe, cmp_k_w1, cmp_k_w2), pos[:, end_pos])
    vc = _compress(v_c, tok, cmp_v_pe, cmp_v_w1, cmp_v_w2)
    p_cmp = masked_probs(jnp.einsum('bgrqd,bngd->bgrqn', qg, kc) * scale,
                         end_pos[None, :] <= q_idx[:, None])
    o_cmp = jnp.einsum('bgrqn,bngd->bgrqd', p_cmp.astype(vc.dtype), vc)

    n_sel = seq // SEL_BLOCK
    ratio = SEL_BLOCK // CMP_STRIDE
    n_top = min(SEL_TOP, n_sel)
    imp = jnp.pad(p_cmp.sum(axis=2), ((0, 0), (0, 0), (0, 0), (0, n_sel * ratio - n_cmp)))
    imp = imp.reshape(bsz, G, seq, n_sel, ratio).sum(axis=-1)
    blk_id = jnp.arange(n_sel)[None, :]
    cur = (q_idx // SEL_BLOCK)[:, None]
    forced = (blk_id == 0) | (blk_id == cur) | (blk_id == cur - 1)
    imp = jnp.where(forced, FORCE_SCORE, jnp.where(blk_id <= cur, imp, -1.0))
    _, sel_idx = lax.top_k(imp, n_top)
    k_s = rope(k_s.reshape(bsz, seq, G, d), pos)
    ks_blk = k_s.reshape(bsz, n_sel, SEL_BLOCK, G, d).transpose(0, 3, 1, 2, 4)
    vs_blk = v_s.reshape(bsz, n_sel, SEL_BLOCK, G, d).transpose(0, 3, 1, 2, 4)
    n_qc = seq // SEL_Q_CHUNK
    idx_ch = sel_idx.reshape(bsz, G, n_qc, SEL_Q_CHUNK, n_top).transpose(2, 0, 1, 3, 4)
    b_i = jnp.arange(bsz)[:, None, None, None]
    g_i = jnp.arange(G)[None, :, None, None]
    n_key = n_top * SEL_BLOCK

    def sel_chunk(args):
        q_b, idx_b, start = args
        kk = ks_blk[b_i, g_i, idx_b].reshape(bsz, G, SEL_Q_CHUNK, n_key, d)
        vv = vs_blk[b_i, g_i, idx_b].reshape(bsz, G, SEL_Q_CHUNK, n_key, d)
        key_pos = (idx_b[..., None] * SEL_BLOCK + jnp.arange(SEL_BLOCK)).reshape(bsz, G, SEL_Q_CHUNK, n_key)
        mask = key_pos <= (start + jnp.arange(SEL_Q_CHUNK))[None, None, :, None]
        p = masked_probs(jnp.einsum('bgrqd,bgqkd->bgrqk', q_b, kk) * scale, mask[:, :, None])
        return jnp.einsum('bgrqk,bgqkd->bgrqd', p.astype(vv.dtype), vv)

    o_sel = _from_blocks(lax.map(sel_chunk, (_to_blocks(qg, SEL_Q_CHUNK), idx_ch,
                                             jnp.arange(n_qc) * SEL_Q_CHUNK)))

    pad = ((0, 0), (WINDOW, 0), (0, 0), (0, 0))
    kw_p = jnp.pad(rope(k_w.reshape(bsz, seq, G, d), pos), pad)
    vw_p = jnp.pad(v_w.reshape(bsz, seq, G, d), pad)
    span = WINDOW + Q_BLOCK

    def win_block(args):
        q_b, start = args
        kk = lax.dynamic_slice_in_dim(kw_p, start, span, axis=1)
        vv = lax.dynamic_slice_in_dim(vw_p, start, span, axis=1)
        key_pos = start - WINDOW + jnp.arange(span)
        diff = (start + jnp.arange(Q_BLOCK))[:, None] - key_pos[None, :]
        mask = (diff >= 0) & (diff < WINDOW) & (key_pos[None, :] >= 0)
        p = masked_probs(jnp.einsum('bgrqd,bkgd->bgrqk', q_b, kk) * scale, mask)
        return jnp.einsum('bgrqk,bkgd->bgrqd', p.astype(vv.dtype), vv)

    o_win = _from_blocks(lax.map(win_block, (_to_blocks(qg, Q_BLOCK), jnp.arange(seq // Q_BLOCK) * Q_BLOCK)))

    gates = jax.nn.sigmoid(gate_logits.reshape(bsz, seq, G, R, 3)).transpose(0, 2, 3, 1, 4)[..., None]
    o = gates[..., 0, :] * o_cmp + gates[..., 1, :] * o_sel + gates[..., 2, :] * o_win
    o = o.transpose(0, 3, 1, 2, 4).reshape(bsz, seq, NSA_WIDTH)
    return (o * jax.nn.silu(z)) @ w_out


def setup_inputs(seed: int = 0) -> dict:
    key = jax.random.key(seed)
    keys = iter(jax.random.split(key, 64))

    def nrm(shape, std):
        return std * jax.random.normal(next(keys), shape, jnp.float32)

    def gain(shape):
        return 1.0 + nrm(shape, 0.02)

    ne, no = N_EVEN, N_ODD
    G, P, H = S5_GROUPS, S5_STATE, S5_GROUP
    return {
        'x': nrm((BATCH, SEQ, D_MODEL), 1.0),
        'c': nrm((BATCH, D_MODEL), 1.0),
        'positions': jnp.broadcast_to(jnp.arange(SEQ, dtype=jnp.int32), (BATCH, SEQ)),
        'pre_norm_g': gain((DEPTH, D_MODEL)),
        'post_norm_g': gain((DEPTH, D_MODEL)),
        'w_ada': nrm((DEPTH, D_MODEL, 3 * D_MODEL), 0.5 * D_MODEL ** -0.5),
        'b_ada': nrm((DEPTH, 3 * D_MODEL), 0.01),
        'ev_w_in': nrm((ne, D_MODEL, EVEN_IN), D_MODEL ** -0.5),
        'ev_lam_re': -0.5 + nrm((ne, G, P), 0.01),
        'ev_lam_im': math.pi * jnp.arange(P, dtype=jnp.float32) + nrm((ne, G, P), 0.01),
        'ev_log_dt': jax.random.uniform(next(keys), (ne, G), jnp.float32, math.log(DT_MIN), math.log(DT_MAX)),
        'ev_b_re': nrm((ne, G, P, H), (2 * H) ** -0.5),
        'ev_b_im': nrm((ne, G, P, H), (2 * H) ** -0.5),
        'ev_c_re': nrm((ne, G, H, P), P ** -0.5),
        'ev_c_im': nrm((ne, G, H, P), P ** -0.5),
        'ev_d_skip': nrm((ne, G, H), 1.0),
        'ev_w_glu': nrm((ne, S5_WIDTH, S5_WIDTH), S5_WIDTH ** -0.5),
        'ev_b_glu': nrm((ne, S5_WIDTH), 0.01),
        'ev_q_norm_g': gain((ne, MLA_Q_RANK)),
        'ev_kv_norm_g': gain((ne, MLA_KV_RANK)),
        'ev_w_uq': nrm((ne, MLA_Q_RANK, MLA_HEADS * (MLA_NOPE + MLA_ROPE)), MLA_Q_RANK ** -0.5),
        'ev_w_ukv': nrm((ne, MLA_KV_RANK, MLA_HEADS * (MLA_NOPE + MLA_V)), MLA_KV_RANK ** -0.5),
        'ev_w_out': nrm((ne, EVEN_MIX, D_MODEL), EVEN_MIX ** -0.5),
        'od_w_in': nrm((no, D_MODEL, ODD_IN), D_MODEL ** -0.5),
        'od_cmp_k_pe': nrm((no, CMP_LEN, NSA_DIM), 0.02),
        'od_cmp_k_w1': nrm((no, CMP_LEN * NSA_DIM, CMP_HIDDEN), (CMP_LEN * NSA_DIM) ** -0.5),
        'od_cmp_k_w2': nrm((no, CMP_HIDDEN, NSA_DIM), CMP_HIDDEN ** -0.5),
        'od_cmp_v_pe': nrm((no, CMP_LEN, NSA_DIM), 0.02),
        'od_cmp_v_w1': nrm((no, CMP_LEN * NSA_DIM, CMP_HIDDEN), (CMP_LEN * NSA_DIM) ** -0.5),
        'od_cmp_v_w2': nrm((no, CMP_HIDDEN, NSA_DIM), CMP_HIDDEN ** -0.5),
        'od_w_out': nrm((no, NSA_WIDTH, D_MODEL), NSA_WIDTH ** -0.5),
    }


def reference(x, c, positions, pre_norm_g, post_norm_g, w_ada, b_ada,
              ev_w_in, ev_lam_re, ev_lam_im, ev_log_dt, ev_b_re, ev_b_im, ev_c_re, ev_c_im,
              ev_d_skip, ev_w_glu, ev_b_glu, ev_q_norm_g, ev_kv_norm_g, ev_w_uq, ev_w_ukv, ev_w_out,
              od_w_in, od_cmp_k_pe, od_cmp_k_w1, od_cmp_k_w2, od_cmp_v_pe, od_cmp_v_w1, od_cmp_v_w2,
              od_w_out):
    for layer in range(DEPTH):
        shift, scale, gate = modulation(c, w_ada[layer], b_ada[layer])
        h = rms_norm(x, pre_norm_g[layer]) * (1.0 + scale) + shift
        i = layer // 2
        if layer % 2 == 0:
            y = even_mixer(h, positions, ev_w_in[i], ev_lam_re[i], ev_lam_im[i], ev_log_dt[i],
                           ev_b_re[i], ev_b_im[i], ev_c_re[i], ev_c_im[i], ev_d_skip[i],
                           ev_w_glu[i], ev_b_glu[i], ev_q_norm_g[i], ev_kv_norm_g[i],
                           ev_w_uq[i], ev_w_ukv[i], ev_w_out[i])
        else:
            y = nsa_mixer(h, positions, od_w_in[i], od_cmp_k_pe[i], od_cmp_k_w1[i], od_cmp_k_w2[i],
                          od_cmp_v_pe[i], od_cmp_v_w1[i], od_cmp_v_w2[i], od_w_out[i])
        x = x + gate * rms_norm(y, post_norm_g[layer])
    return x
```

```python
import functools
import math

import numpy as np
import jax
import jax.numpy as jnp
from jax import lax
from jax.experimental import pallas as pl
from jax.experimental.pallas import tpu as pltpu

F32 = jnp.float32
BF16 = jnp.bfloat16
HIGHEST = lax.Precision.HIGHEST

EPS = 1e-6
ROPE_THETA = 10000.0
NEG_INF = -1e30
TINY = 1e-30
FORCE_SCORE = 1e9

S5_GROUP = 16
S5_STATE = 64
MLA_HEADS = 4
MLA_NOPE = 128
MLA_ROPE = 64
MLA_V = 128
NSA_HEADS = 8
NSA_GROUPS = 2
NSA_REP = NSA_HEADS // NSA_GROUPS
NSA_DIM = 128
CMP_LEN = 32
CMP_STRIDE = 16
CMP_HIDDEN = 256
SEL_BLOCK = 64
SEL_TOP = 16
WINDOW = 512

LANES = 128
SUBLANES = 8
VMEM_LIMIT = 48 * 1024 * 1024

TOKEN_TILE = 512
S5_CHUNK = 256
S5_COLS = 512
ATT_TQ = 512
ATT_TK = 512
NSA_TQ = 256
NSA_TK = 512


def _params(sem):
    return pltpu.CompilerParams(dimension_semantics=sem, vmem_limit_bytes=VMEM_LIMIT)


def _dot(a, b, precision=None):
    return jnp.dot(a, b, preferred_element_type=F32, precision=precision)


def _dot_nt(a, b, precision=None):
    return lax.dot_general(a, b, (((1,), (1,)), ((), ())), preferred_element_type=F32,
                           precision=precision)


def _silu(v):
    return v * jax.nn.sigmoid(v)


def _gelu_tanh(v):
    return 0.5 * v * (1.0 + jnp.tanh(math.sqrt(2.0 / math.pi) * (v + 0.044715 * (v * v * v))))


def _rms(v, g):
    return v * lax.rsqrt(jnp.mean(v * v, axis=-1, keepdims=True) + EPS) * g


def _rope_tab_kernel(pos_ref, fm_ref, fn_ref, sgn_ref, cm_ref, sm_ref, cn_ref, sn_ref):
    p = pos_ref[0]
    am = p * fm_ref[...]
    cm_ref[0] = jnp.cos(am)
    sm_ref[0] = jnp.sin(am)
    an = p * fn_ref[...]
    cn_ref[0] = jnp.cos(an)
    sn_ref[0] = jnp.sin(an) * sgn_ref[...]


def _rope_tables(positions):
    bsz, seq = positions.shape
    ts = TOKEN_TILE
    pos = positions.astype(F32)[..., None]
    half_m, half_n = MLA_ROPE // 2, NSA_DIM // 2
    fm = ROPE_THETA ** (-jnp.arange(half_m, dtype=F32) / half_m)
    fn = ROPE_THETA ** (-jnp.arange(half_n, dtype=F32) / half_n)
    fm = jnp.tile(fm, LANES // half_m)[None]
    fn = jnp.tile(fn, LANES // half_n)[None]
    sgn = jnp.concatenate([-jnp.ones((half_n,), F32), jnp.ones((half_n,), F32)])[None]
    row = pl.BlockSpec((1, LANES), lambda b, i: (0, 0))
    tab = pl.BlockSpec((1, ts, LANES), lambda b, i: (b, i, 0))
    shp = jax.ShapeDtypeStruct((bsz, seq, LANES), F32)
    return pl.pallas_call(
        _rope_tab_kernel,
        out_shape=(shp, shp, shp, shp),
        grid=(bsz, seq // ts),
        in_specs=[pl.BlockSpec((1, ts, 1), lambda b, i: (b, i, 0)), row, row, row],
        out_specs=(tab, tab, tab, tab),
        compiler_params=_params(("parallel", "parallel")),
        name="rope_tables",
    )(pos, fm, fn, sgn)


def _mod_kernel(c_ref, w_ref, b_ref, o_ref):
    s = _silu(c_ref[...])
    o_ref[0] = _dot(s, w_ref[0], precision=HIGHEST) + b_ref[0]


def _modulation(c, w_ada, b_ada):
    depth, d, _ = w_ada.shape
    bsz = c.shape[0]
    rows = -(-bsz // SUBLANES) * SUBLANES
    c_pad = jnp.pad(c, ((0, rows - bsz), (0, 0)))
    out = pl.pallas_call(
        _mod_kernel,
        out_shape=jax.ShapeDtypeStruct((depth, rows, 3 * d), F32),
        grid=(depth, 3),
        in_specs=[pl.BlockSpec((rows, d), lambda l, j: (0, 0)),
                  pl.BlockSpec((1, d, d), lambda l, j: (l, 0, j)),
                  pl.BlockSpec((1, 1, d), lambda l, j: (l, 0, j))],
        out_specs=pl.BlockSpec((1, rows, d), lambda l, j: (l, 0, j)),
        compiler_params=_params(("parallel", "parallel")),
        name="modulation",
    )(c_pad, w_ada, b_ada[:, None, :])
    return out[:, :bsz]


def _modulated_input(x_ref, mod_ref, g_ref):
    d = x_ref.shape[-1]
    x = x_ref[0]
    mod = mod_ref[0]
    shift, scale = mod[:, :d], mod[:, d:2 * d]
    return (_rms(x, g_ref[...]) * (1.0 + scale) + shift).astype(BF16)


EV_U, EV_ZA, EV_CQ, EV_CKV, EV_KPE, EV_ZB = 512, 512, 768, 256, 256, 512
EV_OFF = np.cumsum([0, EV_U, EV_ZA, EV_CQ, EV_CKV, EV_KPE, EV_ZB])


def _ev_in_kernel(x_ref, mod_ref, g_ref, w_ref, gq_ref, gkv_ref, wuq_ref, wukv_ref, cm_ref, sm_ref,
                  u_ref, za_ref, zb_ref, qn_ref, qpe_ref, kn_ref, v_ref, kpe_ref):
    h = _modulated_input(x_ref, mod_ref, g_ref)
    p = _dot(h, w_ref[...])
    o = EV_OFF
    u_ref[0] = p[:, o[0]:o[1]].astype(BF16)
    za_ref[0] = p[:, o[1]:o[2]].astype(BF16)
    zb_ref[0] = p[:, o[5]:o[6]].astype(BF16)
    cos, sin = cm_ref[0], sm_ref[0]

    def rope_split(a, b):
        return jnp.concatenate([a * cos - b * sin, b * cos + a * sin], axis=-1)

    cq = _rms(p[:, o[2]:o[3]], gq_ref[...]).astype(BF16)
    q = _dot(cq, wuq_ref[...]) * ((MLA_NOPE + MLA_ROPE) ** -0.5)
    nope = MLA_HEADS * MLA_NOPE
    qn_ref[0] = q[:, :nope].astype(BF16)
    qpe_ref[0] = rope_split(q[:, nope:nope + LANES], q[:, nope + LANES:]).astype(BF16)
    ckv = _rms(p[:, o[3]:o[4]], gkv_ref[...]).astype(BF16)
    kv = _dot(ckv, wukv_ref[...])
    kn_ref[0] = kv[:, :nope].astype(BF16)
    v_ref[0] = kv[:, nope:].astype(BF16)
    kpe_ref[0] = rope_split(p[:, o[4]:o[4] + LANES], p[:, o[4] + LANES:o[5]]).astype(BF16)


def _ev_in(x, mod, pre_g, w_in, q_norm_g, kv_norm_g, w_uq, w_ukv, cos_m, sin_m):
    bsz, seq, d = x.shape
    tm = TOKEN_TILE
    hq = MLA_ROPE // 2
    c0 = np.cumsum([0, 512, 512, 768, 256, 64, 512])
    kpe_cols = c0[4] + np.concatenate([np.tile(np.arange(hq), MLA_HEADS),
                                       np.tile(hq + np.arange(hq), MLA_HEADS)])
    cols = np.concatenate([np.arange(c0[0], c0[4]), kpe_cols, np.arange(c0[5], c0[6])])
    w = w_in[:, cols].astype(BF16)
    per = MLA_NOPE + MLA_ROPE
    heads = np.arange(MLA_HEADS)[:, None] * per
    q_cols = np.concatenate([(heads + np.arange(MLA_NOPE)).ravel(),
                             (heads + MLA_NOPE + np.arange(hq)).ravel(),
                             (heads + MLA_NOPE + hq + np.arange(hq)).ravel()])
    wuq = w_uq[:, q_cols].astype(BF16)
    per = MLA_NOPE + MLA_V
    heads = np.arange(MLA_HEADS)[:, None] * per
    kv_cols = np.concatenate([(heads + np.arange(MLA_NOPE)).ravel(),
                              (heads + MLA_NOPE + np.arange(MLA_V)).ravel()])
    wukv = w_ukv[:, kv_cols].astype(BF16)

    tok = lambda n: pl.BlockSpec((1, tm, n), lambda b, i: (b, i, 0))
    full = lambda a: pl.BlockSpec(a.shape, lambda b, i: (0,) * a.ndim)
    out = lambda n: jax.ShapeDtypeStruct((bsz, seq, n), BF16)
    gq, gkv, g = q_norm_g[None], kv_norm_g[None], pre_g[None]
    return pl.pallas_call(
        _ev_in_kernel,
        out_shape=(out(512), out(512), out(512), out(512), out(256), out(512), out(512), out(256)),
        grid=(bsz, seq // tm),
        in_specs=[tok(d), pl.BlockSpec((1, 1, 3 * d), lambda b, i: (b, 0, 0)), full(g), full(w),
                  full(gq), full(gkv), full(wuq), full(wukv), tok(LANES), tok(LANES)],
        out_specs=(tok(512), tok(512), tok(512), tok(512), tok(256), tok(512), tok(512), tok(256)),
        compiler_params=_params(("parallel", "parallel")),
        name="even_in_proj",
    )(x, mod[:, None, :], g, w, gq, gkv, wuq, wukv, cos_m, sin_m)


def _s5_kernel(u_ref, z_ref, perm_ref, permt_ref, wb_ref, wc_ref, a_ref, at_ref, pow_ref,
               d_ref, wglu_ref, bglu_ref, o_ref, bu_ref, xb_ref, state_ref, carry_ref):
    t_len = u_ref.shape[1]
    n = a_ref.shape[1] // 2
    steps = t_len // SUBLANES

    @pl.when(pl.program_id(1) == 0)
    def _():
        state_ref[...] = jnp.zeros_like(state_ref)

    perm = perm_ref[...]
    u_p = _dot(perm, u_ref[0])
    z_p = _dot(perm, z_ref[0])
    bu_ref[...] = _dot(u_p.astype(BF16), wb_ref[...])

    for cb in range(n // S5_COLS):
        re = pl.ds(cb * S5_COLS, S5_COLS)
        im = pl.ds(n + cb * S5_COLS, S5_COLS)
        ar = jnp.broadcast_to(a_ref[:, re], (SUBLANES, S5_COLS))
        ai = jnp.broadcast_to(a_ref[:, im], (SUBLANES, S5_COLS))

        def step(t, carry):
            xr, xi = carry
            rows = pl.ds(pl.multiple_of(t * SUBLANES, SUBLANES), SUBLANES)
            nr = ar * xr - ai * xi + bu_ref[rows, re]
            ni = ar * xi + ai * xr + bu_ref[rows, im]
            bu_ref[rows, re] = nr
            bu_ref[rows, im] = ni
            return nr, ni

        zero = jnp.zeros((SUBLANES, S5_COLS), F32)
        er, ei = lax.fori_loop(0, steps, step, (zero, zero), unroll=4)

        sr, si = state_ref[:, re], state_ref[:, im]
        tr, ti = at_ref[:, re], at_ref[:, im]
        for j in range(SUBLANES):
            carry_ref[j:j + 1, re] = sr
            carry_ref[j:j + 1, im] = si
            sr, si = (tr * sr - ti * si + er[j:j + 1], tr * si + ti * sr + ei[j:j + 1])
        state_ref[:, re] = sr
        state_ref[:, im] = si
        cr, ci = carry_ref[:, re], carry_ref[:, im]

        def fix(t2, _):
            xr, xi = [], []
            for k in range(2):
                t = 2 * t2 + k
                rows = pl.ds(pl.multiple_of(t * SUBLANES, SUBLANES), SUBLANES)
                pr, pi = pow_ref[pl.ds(t, 1), re], pow_ref[pl.ds(t, 1), im]
                xr.append(bu_ref[rows, re] + pr * cr - pi * ci)
                xi.append(bu_ref[rows, im] + pr * ci + pi * cr)
            rows = pl.ds(pl.multiple_of(t2 * 2 * SUBLANES, 2 * SUBLANES), 2 * SUBLANES)
            xb_ref[rows, re] = jnp.concatenate(xr, axis=0).astype(BF16)
            xb_ref[rows, im] = jnp.concatenate(xi, axis=0).astype(BF16)
            return 0

        lax.fori_loop(0, steps // 2, fix, 0, unroll=2)

    y = _dot(xb_ref[...], wc_ref[...]) + d_ref[...] * u_p
    g = _gelu_tanh(y)
    gate = jax.nn.sigmoid(_dot(g.astype(BF16), wglu_ref[...]) + bglu_ref[...])
    out = (g * gate * _silu(z_p)).astype(BF16)
    o_ref[0] = _dot(permt_ref[...], out).astype(BF16)


def _s5(u, z_a, lam_re, lam_im, log_dt, b_re, b_im, c_re, c_im, d_skip, w_glu, b_glu):
    bsz, seq, width = u.shape
    groups, state = lam_re.shape
    t_len = S5_CHUNK
    steps = t_len // SUBLANES
    n = groups * state
    dt = jnp.exp(log_dt)[:, None]
    lam_dt_re, lam_dt_im = lam_re * dt, lam_im * dt
    decay = jnp.exp(lam_dt_re)
    ab_re, ab_im = decay * jnp.cos(lam_dt_im), decay * jnp.sin(lam_dt_im)
    den = lam_re * lam_re + lam_im * lam_im
    nr, ni = ab_re - 1.0, ab_im
    f_re = (nr * lam_re + ni * lam_im) / den
    f_im = (ni * lam_re - nr * lam_im) / den
    bb_re = f_re[..., None] * b_re - f_im[..., None] * b_im
    bb_im = f_re[..., None] * b_im + f_im[..., None] * b_re
    eye = jnp.eye(groups, dtype=F32)
    wb = jnp.concatenate([jnp.einsum('gph,gk->ghkp', bb_re, eye).reshape(width, n),
                          jnp.einsum('gph,gk->ghkp', bb_im, eye).reshape(width, n)], axis=1).astype(BF16)
    wc = jnp.concatenate([jnp.einsum('ghp,gk->gpkh', c_re, eye).reshape(n, width),
                          jnp.einsum('ghp,gk->gpkh', -c_im, eye).reshape(n, width)], axis=0).astype(BF16)
    a_vec = jnp.concatenate([ab_re.reshape(1, n), ab_im.reshape(1, n)], axis=1)
    ks = jnp.arange(1, steps + 1, dtype=F32)[:, None, None]
    pw_mag = jnp.exp(lam_dt_re[None] * ks)
    pw_re, pw_im = pw_mag * jnp.cos(lam_dt_im[None] * ks), pw_mag * jnp.sin(lam_dt_im[None] * ks)
    pow_tab = jnp.concatenate([pw_re.reshape(steps, n), pw_im.reshape(steps, n)], axis=1)
    at_vec = pow_tab[steps - 1:steps]
    r = np.arange(t_len)
    perm_np = np.zeros((t_len, t_len), np.float32)
    perm_np[r, (r % SUBLANES) * steps + r // SUBLANES] = 1.0
    perm = jnp.asarray(perm_np, BF16)
    permt = jnp.asarray(perm_np.T, BF16)
    d_vec = d_skip.reshape(1, width)
    wglu = w_glu.astype(BF16)
    bglu = b_glu[None]

    tok = pl.BlockSpec((1, t_len, width), lambda b, i: (b, i, 0))
    full = lambda a: pl.BlockSpec(a.shape, lambda b, i: (0,) * a.ndim)
    consts = (perm, permt, wb, wc, a_vec, at_vec, pow_tab, d_vec, wglu, bglu)
    return pl.pallas_call(
        _s5_kernel,
        out_shape=jax.ShapeDtypeStruct((bsz, seq, width), BF16),
        grid=(bsz, seq // t_len),
        in_specs=[tok, tok] + [full(a) for a in consts],
        out_specs=tok,
        scratch_shapes=[pltpu.VMEM((t_len, 2 * n), F32), pltpu.VMEM((t_len, 2 * n), BF16),
                        pltpu.VMEM((1, 2 * n), F32), pltpu.VMEM((SUBLANES, 2 * n), F32)],
        compiler_params=_params(("parallel", "arbitrary")),
        name="s5_mixer",
    )(u, z_a, *consts)


def _mla_kernel(qn_ref, qpe_ref, kn_ref, kpe_ref, v_ref, o_ref, m_ref, l_ref, acc_ref):
    h, qi, ki = pl.program_id(1), pl.program_id(2), pl.program_id(3)
    tq, tk = qn_ref.shape[1], kn_ref.shape[1]

    @pl.when(ki == 0)
    def _():
        m_ref[...] = jnp.full_like(m_ref, NEG_INF)
        l_ref[...] = jnp.zeros_like(l_ref)
        acc_ref[...] = jnp.zeros_like(acc_ref)

    @pl.when(ki * tk <= qi * tq + tq - 1)
    def _():
        lane = lax.broadcasted_iota(jnp.int32, qpe_ref.shape[1:], 1)
        own = (lane % LANES) // (MLA_ROPE // 2) == h
        qpe = jnp.where(own, qpe_ref[0], jnp.zeros_like(qpe_ref[0]))
        s = _dot_nt(qn_ref[0], kn_ref[0]) + _dot_nt(qpe, kpe_ref[0])
        qpos = qi * tq + lax.broadcasted_iota(jnp.int32, (tq, tk), 0)
        kpos = ki * tk + lax.broadcasted_iota(jnp.int32, (tq, tk), 1)
        mask = kpos <= qpos
        s = jnp.where(mask, s, NEG_INF)
        m_old = m_ref[...]
        m_new = jnp.maximum(m_old, jnp.max(s, axis=-1, keepdims=True))
        e = jnp.where(mask, jnp.exp(s - m_new), 0.0)
        alpha = jnp.exp(m_old - m_new)
        l_ref[...] = alpha * l_ref[...] + jnp.sum(e, axis=-1, keepdims=True)
        acc_ref[...] = alpha * acc_ref[...] + _dot(e.astype(BF16), v_ref[0])
        m_ref[...] = m_new

    @pl.when(ki == pl.num_programs(3) - 1)
    def _():
        o_ref[0] = (acc_ref[...] / jnp.maximum(l_ref[...], TINY)).astype(o_ref.dtype)


def _mla(qn, qpe, kn, kpe, v):
    bsz, seq, _ = qn.shape
    tq, tk = ATT_TQ, ATT_TK
    last = lambda qi, ki: jnp.minimum(ki, (qi * tq + tq - 1) // tk)
    return pl.pallas_call(
        _mla_kernel,
        out_shape=jax.ShapeDtypeStruct((bsz, seq, MLA_HEADS * MLA_V), BF16),
        grid=(bsz, MLA_HEADS, seq // tq, seq // tk),
        in_specs=[pl.BlockSpec((1, tq, MLA_NOPE), lambda b, h, qi, ki: (b, qi, h)),
                  pl.BlockSpec((1, tq, 2 * LANES), lambda b, h, qi, ki: (b, qi, 0)),
                  pl.BlockSpec((1, tk, MLA_NOPE), lambda b, h, qi, ki: (b, last(qi, ki), h)),
                  pl.BlockSpec((1, tk, 2 * LANES), lambda b, h, qi, ki: (b, last(qi, ki), 0)),
                  pl.BlockSpec((1, tk, MLA_V), lambda b, h, qi, ki: (b, last(qi, ki), h))],
        out_specs=pl.BlockSpec((1, tq, MLA_V), lambda b, h, qi, ki: (b, qi, h)),
        scratch_shapes=[pltpu.VMEM((tq, 1), F32), pltpu.VMEM((tq, 1), F32),
                        pltpu.VMEM((tq, MLA_V), F32)],
        compiler_params=_params(("parallel", "parallel", "parallel", "arbitrary")),
        name="mla_attention",
    )(qn, qpe, kn, kpe, v)


def _ev_out_kernel(ya_ref, o_ref, zb_ref, x_ref, mod_ref, g_ref, wa_ref, wb_ref, out_ref):
    d = x_ref.shape[-1]
    yb = (o_ref[0].astype(F32) * _silu(zb_ref[0].astype(F32))).astype(BF16)
    y = _dot(ya_ref[0], wa_ref[...]) + _dot(yb, wb_ref[...])
    gate = mod_ref[0][:, 2 * d:]
    out_ref[0] = x_ref[0] + gate * _rms(y, g_ref[...])


def _ev_out(y_a, o_mla, z_b, x, mod, post_g, w_out):
    bsz, seq, d = x.shape
    tm = TOKEN_TILE
    wa = w_out[:y_a.shape[-1]].astype(BF16)
    wb = w_out[y_a.shape[-1]:].astype(BF16)
    g = post_g[None]
    tok = lambda n: pl.BlockSpec((1, tm, n), lambda b, i: (b, i, 0))
    full = lambda a: pl.BlockSpec(a.shape, lambda b, i: (0,) * a.ndim)
    return pl.pallas_call(
        _ev_out_kernel,
        out_shape=jax.ShapeDtypeStruct(x.shape, F32),
        grid=(bsz, seq // tm),
        in_specs=[tok(y_a.shape[-1]), tok(o_mla.shape[-1]), tok(z_b.shape[-1]), tok(d),
                  pl.BlockSpec((1, 1, 3 * d), lambda b, i: (b, 0, 0)), full(g), full(wa), full(wb)],
        out_specs=tok(d),
        compiler_params=_params(("parallel", "parallel")),
        name="even_out_proj",
    )(y_a, o_mla, z_b, x, mod[:, None, :], g, wa, wb)


OD_Q, OD_KV, OD_GATE, OD_Z = 1024, 256, 128, 1024
OD_OFF = np.cumsum([0, OD_Q] + [OD_KV] * 6 + [OD_GATE, OD_Z])


def _od_in_kernel(x_ref, mod_ref, g_ref, w_ref, cn_ref, sn_ref,
                  q_ref, kc_ref, vc_ref, ks_ref, vs_ref, kw_ref, vw_ref, gate_ref, z_ref):
    h = _modulated_input(x_ref, mod_ref, g_ref)
    p = _dot(h, w_ref[...])
    o = OD_OFF
    cos, sin = cn_ref[0], sn_ref[0]

    def rope(t):
        return t * cos + pltpu.roll(t, NSA_DIM // 2, axis=1) * sin

    scale = NSA_DIM ** -0.5
    for hd in range(NSA_HEADS):
        t = p[:, hd * NSA_DIM:(hd + 1) * NSA_DIM]
        q_ref[0, hd // NSA_REP, hd % NSA_REP] = (rope(t) * scale).astype(BF16)
    plain = (kc_ref, vc_ref, None, vs_ref, None, vw_ref)
    roped = (None, None, ks_ref, None, kw_ref, None)
    for j in range(6):
        for g in range(NSA_GROUPS):
            lo = o[1 + j] + g * NSA_DIM
            t = p[:, lo:lo + NSA_DIM]
            if plain[j] is not None:
                plain[j][0, g] = t.astype(BF16)
            else:
                roped[j][0, g] = rope(t).astype(BF16)
    gate_ref[0] = jax.nn.sigmoid(p[:, o[7]:o[8]])
    z_ref[0] = p[:, o[8]:o[9]].astype(BF16)


def _od_in(x, mod, pre_g, w_in, cos_n, sin_n):
    bsz, seq, d = x.shape
    tm = TOKEN_TILE
    n_gate = 3 * NSA_HEADS
    c0 = OD_Q + 6 * OD_KV
    w = jnp.concatenate([w_in[:, :c0 + n_gate], jnp.zeros((d, OD_GATE - n_gate), F32),
                         w_in[:, c0 + n_gate:]], axis=1).astype(BF16)
    g = pre_g[None]
    tok = lambda n: pl.BlockSpec((1, tm, n), lambda b, i: (b, i, 0))
    full = lambda a: pl.BlockSpec(a.shape, lambda b, i: (0,) * a.ndim)
    q_spec = pl.BlockSpec((1, NSA_GROUPS, NSA_REP, tm, NSA_DIM), lambda b, i: (b, 0, 0, i, 0))
    kv_spec = pl.BlockSpec((1, NSA_GROUPS, tm, NSA_DIM), lambda b, i: (b, 0, i, 0))
    kv_shape = jax.ShapeDtypeStruct((bsz, NSA_GROUPS, seq, NSA_DIM), BF16)
    return pl.pallas_call(
        _od_in_kernel,
        out_shape=(jax.ShapeDtypeStruct((bsz, NSA_GROUPS, NSA_REP, seq, NSA_DIM), BF16),)
        + (kv_shape,) * 6
        + (jax.ShapeDtypeStruct((bsz, seq, OD_GATE), F32), jax.ShapeDtypeStruct((bsz, seq, OD_Z), BF16)),
        grid=(bsz, seq // tm),
        in_specs=[tok(d), pl.BlockSpec((1, 1, 3 * d), lambda b, i: (b, 0, 0)), full(g), full(w),
                  tok(LANES), tok(LANES)],
        out_specs=(q_spec,) + (kv_spec,) * 6 + (tok(OD_GATE), tok(OD_Z)),
        compiler_params=_params(("parallel", "parallel")),
        name="odd_in_proj",
    )(x, mod[:, None, :], g, w, cos_n, sin_n)


def _compress_kernel(x_ref, shift_ref, pe_ref, w1_ref, w2_ref, cos_ref, sin_ref, o_ref, *, use_rope):
    x = x_ref[0, 0]
    half = x.shape[1]
    x_next = _dot(shift_ref[...], x).astype(BF16)
    pre = (_dot(x, w1_ref[:half]) + _dot(x_next, w1_ref[half:])
           + _dot(pe_ref[...], w1_ref[...])[0:1])
    out = _dot(_gelu_tanh(pre).astype(BF16), w2_ref[...])
    if use_rope:
        out = out * cos_ref[0] + pltpu.roll(out, NSA_DIM // 2, axis=1) * sin_ref[0]
    o_ref[0, 0] = out.astype(BF16)


def _compress(kv, pe, w1, w2, cos_end, sin_end, use_rope):
    bsz, groups, seq, d = kv.shape
    nb = seq // CMP_STRIDE
    x = kv.reshape(bsz, groups, nb, CMP_STRIDE * d)
    shift = jnp.asarray(np.eye(nb, k=1, dtype=np.float32), BF16)
    pe_rows = jnp.broadcast_to(pe.reshape(1, CMP_LEN * d), (SUBLANES, CMP_LEN * d)).astype(BF16)
    w1b, w2b = w1.astype(BF16), w2.astype(BF16)
    full = lambda a: pl.BlockSpec(a.shape, lambda b, g: (0,) * a.ndim)
    end = pl.BlockSpec((1, nb, d), lambda b, g: (b, 0, 0))
    return pl.pallas_call(
        functools.partial(_compress_kernel, use_rope=use_rope),
        out_shape=jax.ShapeDtypeStruct((bsz, groups, nb, d), BF16),
        grid=(bsz, groups),
        in_specs=[pl.BlockSpec((1, 1, nb, CMP_STRIDE * d), lambda b, g: (b, g, 0, 0)),
                  full(shift), full(pe_rows), full(w1b), full(w2b), end, end],
        out_specs=pl.BlockSpec((1, 1, nb, d), lambda b, g: (b, g, 0, 0)),
        compiler_params=_params(("parallel", "parallel")),
        name="nsa_compress",
    )(x, shift, pe_rows, w1b, w2b, cos_end, sin_end)


def _cmp_sel_kernel(q_ref, kc_ref, vc_ref, pool_ref, o_ref, sel_ref, *, n_cmp, n_sel, n_top):
    qi = pl.program_id(2)
    rep, tq, d = q_ref.shape[2:]
    nb = kc_ref.shape[2]
    q = q_ref[0, 0].reshape(rep * tq, d)
    s = _dot_nt(q, kc_ref[0, 0]).reshape(rep, tq, nb)
    qpos = qi * tq + lax.broadcasted_iota(jnp.int32, (tq, nb), 0)
    blk = lax.broadcasted_iota(jnp.int32, (tq, nb), 1)
    mask = ((blk * CMP_STRIDE + (CMP_LEN - 1) <= qpos) & (blk < n_cmp))[None]
    s = jnp.where(mask, s, NEG_INF)
    m = jnp.max(s, axis=-1, keepdims=True)
    e = jnp.where(mask, jnp.exp(s - m), 0.0)
    p = e / jnp.maximum(jnp.sum(e, axis=-1, keepdims=True), TINY)
    o = _dot(p.reshape(rep * tq, nb).astype(BF16), vc_ref[0, 0])
    o_ref[0, 0] = o.reshape(rep, tq, d).astype(o_ref.dtype)

    rows = pool_ref.shape[0]
    imp = _dot_nt(pool_ref[...], jnp.sum(p, axis=0), precision=HIGHEST)
    bid = lax.broadcasted_iota(jnp.int32, (rows, tq), 0)
    cur = (qi * tq + lax.broadcasted_iota(jnp.int32, (rows, tq), 1)) // SEL_BLOCK
    forced = (bid == 0) | (bid == cur) | (bid == cur - 1)
    imp = jnp.where(forced, FORCE_SCORE, jnp.where(bid <= cur, imp, -1.0))
    imp = jnp.where(bid < n_sel, imp, -2.0)
    rank = jnp.zeros((rows, tq), jnp.int32)
    for j in range(n_sel):
        vj = imp[j:j + 1, :]
        first = (vj > imp) | ((vj == imp) & (j < bid))
        rank = rank + first.astype(jnp.int32)
    chosen = ((rank < n_top) & (bid < n_sel)).astype(F32)
    sel_ref[0, 0] = chosen.T.astype(sel_ref.dtype)


def _cmp_sel(q, kc, vc):
    bsz, groups, rep, seq, d = q.shape
    nb = kc.shape[2]
    tq = NSA_TQ
    n_cmp = (seq - CMP_LEN) // CMP_STRIDE + 1
    n_sel = seq // SEL_BLOCK
    n_top = min(SEL_TOP, n_sel)
    ratio = SEL_BLOCK // CMP_STRIDE
    assert n_sel <= LANES and n_sel * ratio == nb
    pool_np = np.zeros((LANES, nb), np.float32)
    pool_np[np.arange(nb) // ratio, np.arange(nb)] = 1.0
    pool = jnp.asarray(pool_np)
    kv_spec = pl.BlockSpec((1, 1, nb, d), lambda b, g, i: (b, g, 0, 0))
    return pl.pallas_call(
        functools.partial(_cmp_sel_kernel, n_cmp=n_cmp, n_sel=n_sel, n_top=n_top),
        out_shape=(jax.ShapeDtypeStruct(q.shape, BF16),
                   jax.ShapeDtypeStruct((bsz, groups, seq, LANES), BF16)),
        grid=(bsz, groups, seq // tq),
        in_specs=[pl.BlockSpec((1, 1, rep, tq, d), lambda b, g, i: (b, g, 0, i, 0)), kv_spec, kv_spec,
                  pl.BlockSpec(pool.shape, lambda b, g, i: (0, 0))],
        out_specs=(pl.BlockSpec((1, 1, rep, tq, d), lambda b, g, i: (b, g, 0, i, 0)),
                   pl.BlockSpec((1, 1, tq, LANES), lambda b, g, i: (b, g, i, 0))),
        compiler_params=_params(("parallel", "parallel", "parallel")),
        name="nsa_cmp_select",
    )(q, kc, vc, pool)


def _flash_update(s, mask, v, m_ref, l_ref, acc_ref):
    s = jnp.where(mask, s, NEG_INF)
    m_old = m_ref[...]
    m_new = jnp.maximum(m_old, jnp.max(s, axis=-1, keepdims=True))
    e = jnp.where(mask, jnp.exp(s - m_new), 0.0)
    alpha = jnp.exp(m_old - m_new)
    l_ref[...] = alpha * l_ref[...] + jnp.sum(e, axis=-1, keepdims=True)
    rows, tk = e.shape[0] * e.shape[1], e.shape[2]
    pv = _dot(e.reshape(rows, tk).astype(BF16), v).reshape(acc_ref.shape)
    acc_ref[...] = alpha * acc_ref[...] + pv
    m_ref[...] = m_new


def _flash_init(m_ref, l_ref, acc_ref):
    m_ref[...] = jnp.full_like(m_ref, NEG_INF)
    l_ref[...] = jnp.zeros_like(l_ref)
    acc_ref[...] = jnp.zeros_like(acc_ref)


def _sel_kernel(q_ref, sel_ref, exp_ref, k_ref, v_ref, o_ref, m_ref, l_ref, acc_ref):
    qi, ki = pl.program_id(2), pl.program_id(3)
    rep, tq, d = q_ref.shape[2:]
    tk = k_ref.shape[2]

    @pl.when(ki == 0)
    def _():
        _flash_init(m_ref, l_ref, acc_ref)

    @pl.when(ki * tk <= qi * tq + tq - 1)
    def _():
        q = q_ref[0, 0].reshape(rep * tq, d)
        s = _dot_nt(q, k_ref[0, 0]).reshape(rep, tq, tk)
        picked = _dot(sel_ref[0, 0], exp_ref[...])
        qpos = qi * tq + lax.broadcasted_iota(jnp.int32, (tq, tk), 0)
        kpos = ki * tk + lax.broadcasted_iota(jnp.int32, (tq, tk), 1)
        mask = ((picked > 0.5) & (kpos <= qpos))[None]
        _flash_update(s, mask, v_ref[0, 0], m_ref, l_ref, acc_ref)

    @pl.when(ki == pl.num_programs(3) - 1)
    def _():
        o_ref[0, 0] = (acc_ref[...] / jnp.maximum(l_ref[...], TINY)).astype(o_ref.dtype)


def _sel_attention(q, sel, k, v):
    bsz, groups, rep, seq, d = q.shape
    tq, tk = NSA_TQ, NSA_TK
    expand_np = np.zeros((LANES, seq), np.float32)
    expand_np[np.arange(seq) // SEL_BLOCK, np.arange(seq)] = 1.0
    expand = jnp.asarray(expand_np, BF16)
    last = lambda qi, ki: jnp.minimum(ki, (qi * tq + tq - 1) // tk)
    q_spec = pl.BlockSpec((1, 1, rep, tq, d), lambda b, g, qi, ki: (b, g, 0, qi, 0))
    kv_spec = pl.BlockSpec((1, 1, tk, d), lambda b, g, qi, ki: (b, g, last(qi, ki), 0))
    return pl.pallas_call(
        _sel_kernel,
        out_shape=jax.ShapeDtypeStruct(q.shape, BF16),
        grid=(bsz, groups, seq // tq, seq // tk),
        in_specs=[q_spec, pl.BlockSpec((1, 1, tq, LANES), lambda b, g, qi, ki: (b, g, qi, 0)),
                  pl.BlockSpec((LANES, tk), lambda b, g, qi, ki: (0, last(qi, ki))), kv_spec, kv_spec],
        out_specs=q_spec,
        scratch_shapes=[pltpu.VMEM((rep, tq, 1), F32), pltpu.VMEM((rep, tq, 1), F32),
                        pltpu.VMEM((rep, tq, d), F32)],
        compiler_params=_params(("parallel", "parallel", "parallel", "arbitrary")),
        name="nsa_selected_attention",
    )(q, sel, expand, k, v)


def _win_kernel(q_ref, k_ref, v_ref, o_ref, m_ref, l_ref, acc_ref, *, n_back):
    qi, j = pl.program_id(2), pl.program_id(3)
    rep, tq, d = q_ref.shape[2:]
    tk = k_ref.shape[2]
    ki = qi * tq // tk - n_back + j

    @pl.when(j == 0)
    def _():
        _flash_init(m_ref, l_ref, acc_ref)

    @pl.when(ki >= 0)
    def _():
        q = q_ref[0, 0].reshape(rep * tq, d)
        s = _dot_nt(q, k_ref[0, 0]).reshape(rep, tq, tk)
        qpos = qi * tq + lax.broadcasted_iota(jnp.int32, (tq, tk), 0)
        kpos = ki * tk + lax.broadcasted_iota(jnp.int32, (tq, tk), 1)
        diff = qpos - kpos
        mask = ((diff >= 0) & (diff < WINDOW))[None]
        _flash_update(s, mask, v_ref[0, 0], m_ref, l_ref, acc_ref)

    @pl.when(j == pl.num_programs(3) - 1)
    def _():
        o_ref[0, 0] = (acc_ref[...] / jnp.maximum(l_ref[...], TINY)).astype(o_ref.dtype)


def _win_attention(q, k, v):
    bsz, groups, rep, seq, d = q.shape
    tq = tk = NSA_TQ
    n_back = -(-(WINDOW - 1) // tk)
    kidx = lambda qi, j: jnp.maximum(qi - n_back + j, 0)
    q_spec = pl.BlockSpec((1, 1, rep, tq, d), lambda b, g, qi, j: (b, g, 0, qi, 0))
    kv_spec = pl.BlockSpec((1, 1, tk, d), lambda b, g, qi, j: (b, g, kidx(qi, j), 0))
    return pl.pallas_call(
        functools.partial(_win_kernel, n_back=n_back),
        out_shape=jax.ShapeDtypeStruct(q.shape, BF16),
        grid=(bsz, groups, seq // tq, n_back + 1),
        in_specs=[q_spec, kv_spec, kv_spec],
        out_specs=q_spec,
        scratch_shapes=[pltpu.VMEM((rep, tq, 1), F32), pltpu.VMEM((rep, tq, 1), F32),
                        pltpu.VMEM((rep, tq, d), F32)],
        compiler_params=_params(("parallel", "parallel", "parallel", "arbitrary")),
        name="nsa_window_attention",
    )(q, k, v)


def _od_out_kernel(oc_ref, os_ref, ow_ref, gate_ref, z_ref, x_ref, mod_ref, g_ref, w_ref, out_ref):
    d = x_ref.shape[-1]
    gates = gate_ref[0]
    z = z_ref[0].astype(F32)
    parts = []
    for hd in range(NSA_HEADS):
        g, r = hd // NSA_REP, hd % NSA_REP
        o = (gates[:, 3 * hd:3 * hd + 1] * oc_ref[0, g, r].astype(F32)
             + gates[:, 3 * hd + 1:3 * hd + 2] * os_ref[0, g, r].astype(F32)
             + gates[:, 3 * hd + 2:3 * hd + 3] * ow_ref[0, g, r].astype(F32))
        parts.append((o * _silu(z[:, hd * NSA_DIM:(hd + 1) * NSA_DIM])).astype(BF16))
    y = _dot(jnp.concatenate(parts, axis=-1), w_ref[...])
    gate = mod_ref[0][:, 2 * d:]
    out_ref[0] = x_ref[0] + gate * _rms(y, g_ref[...])


def _od_out(o_cmp, o_sel, o_win, gates, z, x, mod, post_g, w_out):
    bsz, seq, d = x.shape
    tm = TOKEN_TILE
    w = w_out.astype(BF16)
    g = post_g[None]
    tok = lambda n: pl.BlockSpec((1, tm, n), lambda b, i: (b, i, 0))
    full = lambda a: pl.BlockSpec(a.shape, lambda b, i: (0,) * a.ndim)
    o_spec = pl.BlockSpec((1, NSA_GROUPS, NSA_REP, tm, NSA_DIM), lambda b, i: (b, 0, 0, i, 0))
    return pl.pallas_call(
        _od_out_kernel,
        out_shape=jax.ShapeDtypeStruct(x.shape, F32),
        grid=(bsz, seq // tm),
        in_specs=[o_spec, o_spec, o_spec, tok(OD_GATE), tok(OD_Z), tok(d),
                  pl.BlockSpec((1, 1, 3 * d), lambda b, i: (b, 0, 0)), full(g), full(w)],
        out_specs=tok(d),
        compiler_params=_params(("parallel", "parallel")),
        name="odd_out_proj",
    )(o_cmp, o_sel, o_win, gates, z, x, mod[:, None, :], g, w)


def _even_layer(x, mod, pre_g, post_g, tabs, w_in, lam_re, lam_im, log_dt, b_re, b_im, c_re, c_im,
                d_skip, w_glu, b_glu, q_norm_g, kv_norm_g, w_uq, w_ukv, w_out):
    cos_m, sin_m = tabs[0], tabs[1]
    u, z_a, z_b, qn, qpe, kn, v, kpe = _ev_in(x, mod, pre_g, w_in, q_norm_g, kv_norm_g, w_uq, w_ukv,
                                              cos_m, sin_m)
    y_a = _s5(u, z_a, lam_re, lam_im, log_dt, b_re, b_im, c_re, c_im, d_skip, w_glu, b_glu)
    o_mla = _mla(qn, qpe, kn, kpe, v)
    return _ev_out(y_a, o_mla, z_b, x, mod, post_g, w_out)


def _odd_layer(x, mod, pre_g, post_g, tabs, w_in, k_pe, k_w1, k_w2, v_pe, v_w1, v_w2, w_out):
    bsz, seq, _ = x.shape
    cos_n, sin_n = tabs[2], tabs[3]
    q, k_c, v_c, k_s, v_s, k_w, v_w, gates, z = _od_in(x, mod, pre_g, w_in, cos_n, sin_n)
    nb = seq // CMP_STRIDE

    def at_block_end(t):
        t = t.reshape(bsz, nb, CMP_STRIDE, LANES)[:, 1:, CMP_STRIDE - 1]
        return jnp.pad(t, ((0, 0), (0, 1), (0, 0)))

    kc = _compress(k_c, k_pe, k_w1, k_w2, at_block_end(cos_n), at_block_end(sin_n), True)
    vc = _compress(v_c, v_pe, v_w1, v_w2, at_block_end(cos_n), at_block_end(sin_n), False)
    o_cmp, sel = _cmp_sel(q, kc, vc)
    o_sel = _sel_attention(q, sel, k_s, v_s)
    o_win = _win_attention(q, k_w, v_w)
    return _od_out(o_cmp, o_sel, o_win, gates, z, x, mod, post_g, w_out)


def kernel(x, c, positions, pre_norm_g, post_norm_g, w_ada, b_ada, ev_w_in, ev_lam_re, ev_lam_im, ev_log_dt, ev_b_re, ev_b_im, ev_c_re, ev_c_im, ev_d_skip, ev_w_glu, ev_b_glu, ev_q_norm_g, ev_kv_norm_g, ev_w_uq, ev_w_ukv, ev_w_out, od_w_in, od_cmp_k_pe, od_cmp_k_w1, od_cmp_k_w2, od_cmp_v_pe, od_cmp_v_w1, od_cmp_v_w2, od_w_out):
    depth = pre_norm_g.shape[0]
    tabs = _rope_tables(positions)
    mods = _modulation(c, w_ada, b_ada)
    for layer in range(depth):
        i = layer // 2
        if layer % 2 == 0:
            x = _even_layer(x, mods[layer], pre_norm_g[layer], post_norm_g[layer], tabs,
                            ev_w_in[i], ev_lam_re[i], ev_lam_im[i], ev_log_dt[i], ev_b_re[i], ev_b_im[i],
                            ev_c_re[i], ev_c_im[i], ev_d_skip[i], ev_w_glu[i], ev_b_glu[i],
                            ev_q_norm_g[i], ev_kv_norm_g[i], ev_w_uq[i], ev_w_ukv[i], ev_w_out[i])
        else:
            x = _odd_layer(x, mods[layer], pre_norm_g[layer], post_norm_g[layer], tabs,
                           od_w_in[i], od_cmp_k_pe[i], od_cmp_k_w1[i], od_cmp_k_w2[i],
                           od_cmp_v_pe[i], od_cmp_v_w1[i], od_cmp_v_w2[i], od_w_out[i])
    return x
```

```python
import functools
import math

import numpy as np
import jax
import jax.numpy as jnp
from jax import lax
from jax.experimental import pallas as pl
from jax.experimental.pallas import tpu as pltpu

F32 = jnp.float32
BF16 = jnp.bfloat16
HIGHEST = lax.Precision.HIGHEST

EPS = 1e-6
ROPE_THETA = 10000.0
NEG_INF = -1e30
TINY = 1e-30
FORCE_SCORE = 1e9
LOG2E = 1.0 / math.log(2.0)

S5_GROUP = 16
S5_STATE = 64
MLA_HEADS = 4
MLA_NOPE = 128
MLA_ROPE = 64
MLA_V = 128
NSA_HEADS = 8
NSA_GROUPS = 2
NSA_REP = NSA_HEADS // NSA_GROUPS
NSA_DIM = 128
CMP_LEN = 32
CMP_STRIDE = 16
CMP_HIDDEN = 256
SEL_BLOCK = 64
SEL_TOP = 16
WINDOW = 512

LANES = 128
SUBLANES = 8
VMEM_LIMIT = 48 * 1024 * 1024

TOKEN_TILE = 512
S5_CHUNK = 256
S5_COLS = 512
ATT_TQ = 512
ATT_TK = 512
NSA_TQ = 256
NSA_TK = 512


def _params(sem):
    return pltpu.CompilerParams(dimension_semantics=sem, vmem_limit_bytes=VMEM_LIMIT)


def _dot(a, b, precision=None):
    return jnp.dot(a, b, preferred_element_type=F32, precision=precision)


def _dot_nt(a, b, precision=None):
    return lax.dot_general(a, b, (((1,), (1,)), ((), ())), preferred_element_type=F32,
                           precision=precision)


def _silu(v):
    return v * jax.nn.sigmoid(v)


def _gelu_tanh(v):
    return 0.5 * v * (1.0 + jnp.tanh(math.sqrt(2.0 / math.pi) * (v + 0.044715 * (v * v * v))))


def _rms(v, g):
    return v * lax.rsqrt(jnp.mean(v * v, axis=-1, keepdims=True) + EPS) * g


def _rope_tab_kernel(pos_ref, fm_ref, fn_ref, sgm_ref, sgn_ref, cm_ref, sm_ref, cn_ref, sn_ref):
    p = pos_ref[0]
    am = p * fm_ref[...]
    cm_ref[0] = jnp.cos(am)
    sm_ref[0] = jnp.sin(am) * sgm_ref[...]
    an = p * fn_ref[...]
    cn_ref[0] = jnp.cos(an)
    sn_ref[0] = jnp.sin(an) * sgn_ref[...]


def _rope_tables(positions):
    bsz, seq = positions.shape
    ts = TOKEN_TILE
    pos = positions.astype(F32)[..., None]
    half_m, half_n = MLA_ROPE // 2, NSA_DIM // 2
    fm = ROPE_THETA ** (-jnp.arange(half_m, dtype=F32) / half_m)
    fn = ROPE_THETA ** (-jnp.arange(half_n, dtype=F32) / half_n)
    fm = jnp.tile(fm, LANES // half_m)[None]
    fn = jnp.tile(fn, LANES // half_n)[None]
    sgm = jnp.tile(jnp.concatenate([-jnp.ones((half_m,), F32), jnp.ones((half_m,), F32)]),
                   LANES // MLA_ROPE)[None]
    sgn = jnp.concatenate([-jnp.ones((half_n,), F32), jnp.ones((half_n,), F32)])[None]
    row = pl.BlockSpec((1, LANES), lambda b, i: (0, 0))
    tab = pl.BlockSpec((1, ts, LANES), lambda b, i: (b, i, 0))
    shp = jax.ShapeDtypeStruct((bsz, seq, LANES), F32)
    return pl.pallas_call(
        _rope_tab_kernel,
        out_shape=(shp, shp, shp, shp),
        grid=(bsz, seq // ts),
        in_specs=[pl.BlockSpec((1, ts, 1), lambda b, i: (b, i, 0)), row, row, row, row],
        out_specs=(tab, tab, tab, tab),
        compiler_params=_params(("parallel", "parallel")),
        name="rope_tables",
    )(pos, fm, fn, sgm, sgn)


def _mod_kernel(c_ref, w_ref, b_ref, o_ref):
    s = _silu(c_ref[...])
    o_ref[0] = _dot(s, w_ref[0], precision=HIGHEST) + b_ref[0]


def _modulation(c, w_ada, b_ada):
    depth, d, _ = w_ada.shape
    bsz = c.shape[0]
    rows = -(-bsz // SUBLANES) * SUBLANES
    c_pad = jnp.pad(c, ((0, rows - bsz), (0, 0)))
    out = pl.pallas_call(
        _mod_kernel,
        out_shape=jax.ShapeDtypeStruct((depth, rows, 3 * d), F32),
        grid=(depth, 3),
        in_specs=[pl.BlockSpec((rows, d), lambda l, j: (0, 0)),
                  pl.BlockSpec((1, d, d), lambda l, j: (l, 0, j)),
                  pl.BlockSpec((1, 1, d), lambda l, j: (l, 0, j))],
        out_specs=pl.BlockSpec((1, rows, d), lambda l, j: (l, 0, j)),
        compiler_params=_params(("parallel", "parallel")),
        name="modulation",
    )(c_pad, w_ada, b_ada[:, None, :])
    return out[:, :bsz]


def _modulated_input(x_ref, mod_ref, g_ref):
    d = x_ref.shape[-1]
    x = x_ref[0]
    mod = mod_ref[0]
    shift, scale = mod[:, :d], mod[:, d:2 * d]
    return (_rms(x, g_ref[...]) * (1.0 + scale) + shift).astype(BF16)


EV_U, EV_ZA, EV_CQ, EV_CKV, EV_KPE, EV_ZB = 512, 512, 768, 256, 256, 512
EV_OFF = np.cumsum([0, EV_U, EV_ZA, EV_CQ, EV_CKV, EV_KPE, EV_ZB])


def _ev_in_kernel(x_ref, mod_ref, g_ref, w_ref, gq_ref, gkv_ref, wuq_ref, wukv_ref, cm_ref, sm_ref,
                  u_ref, za_ref, zb_ref, q_ref, k_ref, v_ref):
    h = _modulated_input(x_ref, mod_ref, g_ref)
    p = _dot(h, w_ref[...])
    o = EV_OFF
    u_ref[0] = p[:, o[0]:o[1]].astype(BF16)
    za_ref[0] = p[:, o[1]:o[2]].astype(BF16)
    zb_ref[0] = p[:, o[5]:o[6]].astype(BF16)
    cos, sin = cm_ref[0], sm_ref[0]

    cq = _rms(p[:, o[2]:o[3]], gq_ref[...]).astype(BF16)
    q = _dot(cq, wuq_ref[...]) * ((MLA_NOPE + MLA_ROPE) ** -0.5 * LOG2E)
    ckv = _rms(p[:, o[3]:o[4]], gkv_ref[...]).astype(BF16)
    kv = _dot(ckv, wukv_ref[...])
    kpe = (p[:, o[4]:o[4] + LANES] * cos + p[:, o[4] + LANES:o[5]] * sin).astype(BF16)
    nope = MLA_HEADS * MLA_NOPE
    for hd in range(MLA_HEADS):
        b0 = hd * (MLA_NOPE + 2 * LANES)
        qpe = q[:, b0 + MLA_NOPE:b0 + MLA_NOPE + LANES] * cos + q[:, b0 + MLA_NOPE + LANES:b0 + MLA_NOPE + 2 * LANES] * sin
        q_ref[0, hd] = jnp.concatenate([q[:, b0:b0 + MLA_NOPE], qpe], axis=-1).astype(BF16)
        k_ref[0, hd] = jnp.concatenate([kv[:, hd * MLA_NOPE:(hd + 1) * MLA_NOPE].astype(BF16), kpe], axis=-1)
        v_ref[0, hd] = kv[:, nope + hd * MLA_V:nope + (hd + 1) * MLA_V].astype(BF16)


def _ev_in(x, mod, pre_g, w_in, q_norm_g, kv_norm_g, w_uq, w_ukv, cos_m, sin_m):
    bsz, seq, d = x.shape
    tm = TOKEN_TILE
    hq = MLA_ROPE // 2
    pad = LANES - MLA_ROPE

    def rot_cols(base, zero):
        x1, x2, z = base + np.arange(hq), base + hq + np.arange(hq), np.full(pad, zero)
        return np.concatenate([x1, x2, z, x2, x1, z])

    c0 = np.cumsum([0, 512, 512, 768, 256, 64, 512])
    cols = np.concatenate([np.arange(c0[0], c0[4]), rot_cols(c0[4], c0[6]), np.arange(c0[5], c0[6])])
    w = jnp.pad(w_in, ((0, 0), (0, 1)))[:, cols].astype(BF16)
    per = MLA_NOPE + MLA_ROPE
    q_cols = np.concatenate([np.concatenate([hd * per + np.arange(MLA_NOPE),
                                             rot_cols(hd * per + MLA_NOPE, MLA_HEADS * per)])
                             for hd in range(MLA_HEADS)])
    wuq = jnp.pad(w_uq, ((0, 0), (0, 1)))[:, q_cols].astype(BF16)
    per = MLA_NOPE + MLA_V
    heads = np.arange(MLA_HEADS)[:, None] * per
    kv_cols = np.concatenate([(heads + np.arange(MLA_NOPE)).ravel(),
                              (heads + MLA_NOPE + np.arange(MLA_V)).ravel()])
    wukv = w_ukv[:, kv_cols].astype(BF16)

    tok = lambda n: pl.BlockSpec((1, tm, n), lambda b, i: (b, i, 0))
    full = lambda a: pl.BlockSpec(a.shape, lambda b, i: (0,) * a.ndim)
    out = lambda n: jax.ShapeDtypeStruct((bsz, seq, n), BF16)
    head = lambda n: pl.BlockSpec((1, MLA_HEADS, tm, n), lambda b, i: (b, 0, i, 0))
    head_out = lambda n: jax.ShapeDtypeStruct((bsz, MLA_HEADS, seq, n), BF16)
    gq, gkv, g = q_norm_g[None], kv_norm_g[None], pre_g[None]
    return pl.pallas_call(
        _ev_in_kernel,
        out_shape=(out(512), out(512), out(512), head_out(2 * LANES), head_out(2 * LANES), head_out(MLA_V)),
        grid=(bsz, seq // tm),
        in_specs=[tok(d), pl.BlockSpec((1, 1, 3 * d), lambda b, i: (b, 0, 0)), full(g), full(w),
                  full(gq), full(gkv), full(wuq), full(wukv), tok(LANES), tok(LANES)],
        out_specs=(tok(512), tok(512), tok(512), head(2 * LANES), head(2 * LANES), head(MLA_V)),
        compiler_params=_params(("parallel", "parallel")),
        name="even_in_proj",
    )(x, mod[:, None, :], g, w, gq, gkv, wuq, wukv, cos_m, sin_m)


def _s5_kernel(u_ref, z_ref, perm_ref, permt_ref, wb_ref, wc_ref, a_ref, at_ref, pow_ref,
               d_ref, wglu_ref, bglu_ref, o_ref, bu_ref, xb_ref, state_ref, carry_ref):
    t_len = u_ref.shape[1]
    n = a_ref.shape[1] // 2
    steps = t_len // SUBLANES

    @pl.when(pl.program_id(1) == 0)
    def _():
        state_ref[...] = jnp.zeros_like(state_ref)

    perm = perm_ref[...]
    u_p = _dot(perm, u_ref[0])
    z_p = _dot(perm, z_ref[0])
    bu_ref[...] = _dot(u_p.astype(BF16), wb_ref[...])

    for cb in range(n // S5_COLS):
        re = pl.ds(cb * S5_COLS, S5_COLS)
        im = pl.ds(n + cb * S5_COLS, S5_COLS)
        ar = jnp.broadcast_to(a_ref[:, re], (SUBLANES, S5_COLS))
        ai = jnp.broadcast_to(a_ref[:, im], (SUBLANES, S5_COLS))

        def step(t, carry):
            xr, xi = carry
            rows = pl.ds(pl.multiple_of(t * SUBLANES, SUBLANES), SUBLANES)
            nr = ar * xr - ai * xi + bu_ref[rows, re]
            ni = ar * xi + ai * xr + bu_ref[rows, im]
            bu_ref[rows, re] = nr
            bu_ref[rows, im] = ni
            return nr, ni

        zero = jnp.zeros((SUBLANES, S5_COLS), F32)
        er, ei = lax.fori_loop(0, steps, step, (zero, zero), unroll=4)

        sr, si = state_ref[:, re], state_ref[:, im]
        tr, ti = at_ref[:, re], at_ref[:, im]
        for j in range(SUBLANES):
            carry_ref[j:j + 1, re] = sr
            carry_ref[j:j + 1, im] = si
            sr, si = (tr * sr - ti * si + er[j:j + 1], tr * si + ti * sr + ei[j:j + 1])
        state_ref[:, re] = sr
        state_ref[:, im] = si
        cr, ci = carry_ref[:, re], carry_ref[:, im]

        def fix(t2, _):
            xr, xi = [], []
            for k in range(2):
                t = 2 * t2 + k
                rows = pl.ds(pl.multiple_of(t * SUBLANES, SUBLANES), SUBLANES)
                pr, pi = pow_ref[pl.ds(t, 1), re], pow_ref[pl.ds(t, 1), im]
                xr.append(bu_ref[rows, re] + pr * cr - pi * ci)
                xi.append(bu_ref[rows, im] + pr * ci + pi * cr)
            rows = pl.ds(pl.multiple_of(t2 * 2 * SUBLANES, 2 * SUBLANES), 2 * SUBLANES)
            xb_ref[rows, re] = jnp.concatenate(xr, axis=0).astype(BF16)
            xb_ref[rows, im] = jnp.concatenate(xi, axis=0).astype(BF16)
            return 0

        lax.fori_loop(0, steps // 2, fix, 0, unroll=2)

    y = _dot(xb_ref[...], wc_ref[...]) + d_ref[...] * u_p
    g = _gelu_tanh(y)
    gate = jax.nn.sigmoid(_dot(g.astype(BF16), wglu_ref[...]) + bglu_ref[...])
    out = (g * gate * _silu(z_p)).astype(BF16)
    o_ref[0] = _dot(permt_ref[...], out).astype(BF16)


def _s5(u, z_a, lam_re, lam_im, log_dt, b_re, b_im, c_re, c_im, d_skip, w_glu, b_glu):
    bsz, seq, width = u.shape
    groups, state = lam_re.shape
    t_len = S5_CHUNK
    steps = t_len // SUBLANES
    n = groups * state
    dt = jnp.exp(log_dt)[:, None]
    lam_dt_re, lam_dt_im = lam_re * dt, lam_im * dt
    decay = jnp.exp(lam_dt_re)
    ab_re, ab_im = decay * jnp.cos(lam_dt_im), decay * jnp.sin(lam_dt_im)
    den = lam_re * lam_re + lam_im * lam_im
    nr, ni = ab_re - 1.0, ab_im
    f_re = (nr * lam_re + ni * lam_im) / den
    f_im = (ni * lam_re - nr * lam_im) / den
    bb_re = f_re[..., None] * b_re - f_im[..., None] * b_im
    bb_im = f_re[..., None] * b_im + f_im[..., None] * b_re
    eye = jnp.eye(groups, dtype=F32)
    wb = jnp.concatenate([jnp.einsum('gph,gk->ghkp', bb_re, eye).reshape(width, n),
                          jnp.einsum('gph,gk->ghkp', bb_im, eye).reshape(width, n)], axis=1).astype(BF16)
    wc = jnp.concatenate([jnp.einsum('ghp,gk->gpkh', c_re, eye).reshape(n, width),
                          jnp.einsum('ghp,gk->gpkh', -c_im, eye).reshape(n, width)], axis=0).astype(BF16)
    a_vec = jnp.concatenate([ab_re.reshape(1, n), ab_im.reshape(1, n)], axis=1)
    ks = jnp.arange(1, steps + 1, dtype=F32)[:, None, None]
    pw_mag = jnp.exp(lam_dt_re[None] * ks)
    pw_re, pw_im = pw_mag * jnp.cos(lam_dt_im[None] * ks), pw_mag * jnp.sin(lam_dt_im[None] * ks)
    pow_tab = jnp.concatenate([pw_re.reshape(steps, n), pw_im.reshape(steps, n)], axis=1)
    at_vec = pow_tab[steps - 1:steps]
    r = np.arange(t_len)
    perm_np = np.zeros((t_len, t_len), np.float32)
    perm_np[r, (r % SUBLANES) * steps + r // SUBLANES] = 1.0
    perm = jnp.asarray(perm_np, BF16)
    permt = jnp.asarray(perm_np.T, BF16)
    d_vec = d_skip.reshape(1, width)
    wglu = w_glu.astype(BF16)
    bglu = b_glu[None]

    tok = pl.BlockSpec((1, t_len, width), lambda b, i: (b, i, 0))
    full = lambda a: pl.BlockSpec(a.shape, lambda b, i: (0,) * a.ndim)
    consts = (perm, permt, wb, wc, a_vec, at_vec, pow_tab, d_vec, wglu, bglu)
    return pl.pallas_call(
        _s5_kernel,
        out_shape=jax.ShapeDtypeStruct((bsz, seq, width), BF16),
        grid=(bsz, seq // t_len),
        in_specs=[tok, tok] + [full(a) for a in consts],
        out_specs=tok,
        scratch_shapes=[pltpu.VMEM((t_len, 2 * n), F32), pltpu.VMEM((t_len, 2 * n), BF16),
                        pltpu.VMEM((1, 2 * n), F32), pltpu.VMEM((SUBLANES, 2 * n), F32)],
        compiler_params=_params(("parallel", "arbitrary")),
        name="s5_mixer",
    )(u, z_a, *consts)


def _softmax_init(m_ref, acc_ref):
    m_ref[...] = jnp.full_like(m_ref, NEG_INF)
    acc_ref[...] = jnp.zeros_like(acc_ref)


def _softmax_tile(s, mask, v_ext, m_ref, acc_ref):
    rows, tk = s.shape
    if mask is not None:
        heads = rows // mask.shape[1]
        s = jnp.where(mask, s.reshape(heads, *mask.shape[1:]), NEG_INF).reshape(rows, tk)
    m_old = m_ref[...]
    m_new = jnp.maximum(m_old, jnp.max(s, axis=-1, keepdims=True))
    e = jnp.exp2(s - jnp.concatenate([m_new] * (tk // LANES), axis=1))
    if mask is not None:
        e = jnp.where(mask, e.reshape(heads, *mask.shape[1:]), 0.0).reshape(rows, tk)
    alpha = jnp.exp2(m_old - m_new)
    acc_ref[...] = (jnp.concatenate([alpha, alpha], axis=1) * acc_ref[...]
                    + _dot(e.astype(BF16), v_ext))
    m_ref[...] = m_new


def _softmax_finish(acc_ref):
    acc = acc_ref[...]
    return acc[:, :LANES] / jnp.maximum(acc[:, LANES:], TINY)


def _with_ones(v):
    return jnp.concatenate([v, jnp.ones(v.shape, v.dtype)], axis=1)


def _mla_kernel(q_ref, k_ref, v_ref, o_ref, m_ref, acc_ref):
    qi = pl.program_id(2)
    tq, tk = q_ref.shape[2], ATT_TK
    q = q_ref[0, 0]
    _softmax_init(m_ref, acc_ref)
    n_full = qi * tq // tk

    def tile(ki, mask):
        rows = pl.ds(pl.multiple_of(ki * tk, tk), tk)
        s = _dot_nt(q, k_ref[0, 0, rows, :])
        _softmax_tile(s, mask, _with_ones(v_ref[0, 0, rows, :]), m_ref, acc_ref)

    def body(ki, carry):
        tile(ki, None)
        return carry

    lax.fori_loop(0, n_full, body, 0)
    qpos = qi * tq + lax.broadcasted_iota(jnp.int32, (1, tq, tk), 1)
    kpos = n_full * tk + lax.broadcasted_iota(jnp.int32, (1, tq, tk), 2)
    tile(n_full, kpos <= qpos)
    o_ref[0] = _softmax_finish(acc_ref).astype(o_ref.dtype)


def _mla(q, k, v):
    bsz, heads, seq, dk = q.shape
    tq = ATT_TQ
    assert tq == ATT_TK
    return pl.pallas_call(
        _mla_kernel,
        out_shape=jax.ShapeDtypeStruct((bsz, seq, heads * MLA_V), BF16),
        grid=(bsz, heads, seq // tq),
        in_specs=[pl.BlockSpec((1, 1, tq, dk), lambda b, h, qi: (b, h, qi, 0)),
                  pl.BlockSpec((1, 1, seq, dk), lambda b, h, qi: (b, h, 0, 0)),
                  pl.BlockSpec((1, 1, seq, MLA_V), lambda b, h, qi: (b, h, 0, 0))],
        out_specs=pl.BlockSpec((1, tq, MLA_V), lambda b, h, qi: (b, qi, h)),
        scratch_shapes=[pltpu.VMEM((tq, LANES), F32), pltpu.VMEM((tq, 2 * LANES), F32)],
        compiler_params=_params(("parallel", "parallel", "arbitrary")),
        name="mla_attention",
    )(q, k, v)


def _ev_out_kernel(ya_ref, o_ref, zb_ref, x_ref, mod_ref, g_ref, wa_ref, wb_ref, out_ref):
    d = x_ref.shape[-1]
    yb = (o_ref[0].astype(F32) * _silu(zb_ref[0].astype(F32))).astype(BF16)
    y = _dot(ya_ref[0], wa_ref[...]) + _dot(yb, wb_ref[...])
    gate = mod_ref[0][:, 2 * d:]
    out_ref[0] = x_ref[0] + gate * _rms(y, g_ref[...])


def _ev_out(y_a, o_mla, z_b, x, mod, post_g, w_out):
    bsz, seq, d = x.shape
    tm = TOKEN_TILE
    wa = w_out[:y_a.shape[-1]].astype(BF16)
    wb = w_out[y_a.shape[-1]:].astype(BF16)
    g = post_g[None]
    tok = lambda n: pl.BlockSpec((1, tm, n), lambda b, i: (b, i, 0))
    full = lambda a: pl.BlockSpec(a.shape, lambda b, i: (0,) * a.ndim)
    return pl.pallas_call(
        _ev_out_kernel,
        out_shape=jax.ShapeDtypeStruct(x.shape, F32),
        grid=(bsz, seq // tm),
        in_specs=[tok(y_a.shape[-1]), tok(o_mla.shape[-1]), tok(z_b.shape[-1]), tok(d),
                  pl.BlockSpec((1, 1, 3 * d), lambda b, i: (b, 0, 0)), full(g), full(wa), full(wb)],
        out_specs=tok(d),
        compiler_params=_params(("parallel", "parallel")),
        name="even_out_proj",
    )(y_a, o_mla, z_b, x, mod[:, None, :], g, wa, wb)


OD_Q, OD_KV, OD_GATE, OD_Z = 1024, 256, 128, 1024
OD_OFF = np.cumsum([0, OD_Q] + [OD_KV] * 6 + [OD_GATE, OD_Z])


def _od_in_kernel(x_ref, mod_ref, g_ref, w_ref, cn_ref, sn_ref,
                  q_ref, kc_ref, vc_ref, ks_ref, vs_ref, kw_ref, vw_ref, gate_ref, z_ref):
    h = _modulated_input(x_ref, mod_ref, g_ref)
    p = _dot(h, w_ref[...])
    o = OD_OFF
    cos, sin = cn_ref[0], sn_ref[0]

    def rope(t):
        return t * cos + pltpu.roll(t, NSA_DIM // 2, axis=1) * sin

    scale = NSA_DIM ** -0.5 * LOG2E
    for hd in range(NSA_HEADS):
        t = p[:, hd * NSA_DIM:(hd + 1) * NSA_DIM]
        q_ref[0, hd // NSA_REP, hd % NSA_REP] = (rope(t) * scale).astype(BF16)
    plain = (kc_ref, vc_ref, None, vs_ref, None, vw_ref)
    roped = (None, None, ks_ref, None, kw_ref, None)
    for j in range(6):
        for g in range(NSA_GROUPS):
            lo = o[1 + j] + g * NSA_DIM
            t = p[:, lo:lo + NSA_DIM]
            if plain[j] is not None:
                plain[j][0, g] = t.astype(BF16)
            else:
                roped[j][0, g] = rope(t).astype(BF16)
    gate_ref[0] = jax.nn.sigmoid(p[:, o[7]:o[8]])
    z_ref[0] = p[:, o[8]:o[9]].astype(BF16)


def _od_in(x, mod, pre_g, w_in, cos_n, sin_n):
    bsz, seq, d = x.shape
    tm = TOKEN_TILE
    n_gate = 3 * NSA_HEADS
    c0 = OD_Q + 6 * OD_KV
    w = jnp.concatenate([w_in[:, :c0 + n_gate], jnp.zeros((d, OD_GATE - n_gate), F32),
                         w_in[:, c0 + n_gate:]], axis=1).astype(BF16)
    g = pre_g[None]
    tok = lambda n: pl.BlockSpec((1, tm, n), lambda b, i: (b, i, 0))
    full = lambda a: pl.BlockSpec(a.shape, lambda b, i: (0,) * a.ndim)
    q_spec = pl.BlockSpec((1, NSA_GROUPS, NSA_REP, tm, NSA_DIM), lambda b, i: (b, 0, 0, i, 0))
    kv_spec = pl.BlockSpec((1, NSA_GROUPS, tm, NSA_DIM), lambda b, i: (b, 0, i, 0))
    kv_shape = jax.ShapeDtypeStruct((bsz, NSA_GROUPS, seq, NSA_DIM), BF16)
    return pl.pallas_call(
        _od_in_kernel,
        out_shape=(jax.ShapeDtypeStruct((bsz, NSA_GROUPS, NSA_REP, seq, NSA_DIM), BF16),)
        + (kv_shape,) * 6
        + (jax.ShapeDtypeStruct((bsz, seq, OD_GATE), F32), jax.ShapeDtypeStruct((bsz, seq, OD_Z), BF16)),
        grid=(bsz, seq // tm),
        in_specs=[tok(d), pl.BlockSpec((1, 1, 3 * d), lambda b, i: (b, 0, 0)), full(g), full(w),
                  tok(LANES), tok(LANES)],
        out_specs=(q_spec,) + (kv_spec,) * 6 + (tok(OD_GATE), tok(OD_Z)),
        compiler_params=_params(("parallel", "parallel")),
        name="odd_in_proj",
    )(x, mod[:, None, :], g, w, cos_n, sin_n)


def _compress_kernel(x_ref, shift_ref, pe_ref, w1_ref, w2_ref, cos_ref, sin_ref, o_ref, *, use_rope):
    x = x_ref[0, 0]
    half = x.shape[1]
    x_next = _dot(shift_ref[...], x).astype(BF16)
    pre = (_dot(x, w1_ref[:half]) + _dot(x_next, w1_ref[half:])
           + _dot(pe_ref[...], w1_ref[...])[0:1])
    out = _dot(_gelu_tanh(pre).astype(BF16), w2_ref[...])
    if use_rope:
        out = out * cos_ref[0] + pltpu.roll(out, NSA_DIM // 2, axis=1) * sin_ref[0]
    o_ref[0, 0] = out.astype(BF16)


def _compress(kv, pe, w1, w2, cos_end, sin_end, use_rope):
    bsz, groups, seq, d = kv.shape
    nb = seq // CMP_STRIDE
    x = kv.reshape(bsz, groups, nb, CMP_STRIDE * d)
    shift = jnp.asarray(np.eye(nb, k=1, dtype=np.float32), BF16)
    pe_rows = jnp.broadcast_to(pe.reshape(1, CMP_LEN * d), (SUBLANES, CMP_LEN * d)).astype(BF16)
    w1b, w2b = w1.astype(BF16), w2.astype(BF16)
    full = lambda a: pl.BlockSpec(a.shape, lambda b, g: (0,) * a.ndim)
    end = pl.BlockSpec((1, nb, d), lambda b, g: (b, 0, 0))
    return pl.pallas_call(
        functools.partial(_compress_kernel, use_rope=use_rope),
        out_shape=jax.ShapeDtypeStruct((bsz, groups, nb, d), BF16),
        grid=(bsz, groups),
        in_specs=[pl.BlockSpec((1, 1, nb, CMP_STRIDE * d), lambda b, g: (b, g, 0, 0)),
                  full(shift), full(pe_rows), full(w1b), full(w2b), end, end],
        out_specs=pl.BlockSpec((1, 1, nb, d), lambda b, g: (b, g, 0, 0)),
        compiler_params=_params(("parallel", "parallel")),
        name="nsa_compress",
    )(x, shift, pe_rows, w1b, w2b, cos_end, sin_end)


def _cmp_sel_kernel(q_ref, kc_ref, vc_ref, pool_ref, o_ref, sel_ref, *, n_cmp, n_sel, n_top):
    qi = pl.program_id(2)
    rep, tq, d = q_ref.shape[2:]
    nb = kc_ref.shape[2]
    q = q_ref[0, 0].reshape(rep * tq, d)
    s = _dot_nt(q, kc_ref[0, 0]).reshape(rep, tq, nb)
    qpos = qi * tq + lax.broadcasted_iota(jnp.int32, (tq, nb), 0)
    blk = lax.broadcasted_iota(jnp.int32, (tq, nb), 1)
    mask = ((blk * CMP_STRIDE + (CMP_LEN - 1) <= qpos) & (blk < n_cmp))[None]
    s = jnp.where(mask, s, NEG_INF)
    m = jnp.max(s, axis=-1, keepdims=True)
    e = jnp.where(mask, jnp.exp2(s - m), 0.0)
    p = e / jnp.maximum(jnp.sum(e, axis=-1, keepdims=True), TINY)
    o = _dot(p.reshape(rep * tq, nb).astype(BF16), vc_ref[0, 0])
    o_ref[0, 0] = o.reshape(rep, tq, d).astype(o_ref.dtype)

    rows = pool_ref.shape[0]
    imp = _dot_nt(pool_ref[...], jnp.sum(p, axis=0), precision=HIGHEST)
    bid = lax.broadcasted_iota(jnp.int32, (rows, tq), 0)
    cur = (qi * tq + lax.broadcasted_iota(jnp.int32, (rows, tq), 1)) // SEL_BLOCK
    forced = (bid == 0) | (bid == cur) | (bid == cur - 1)
    imp = jnp.where(forced, FORCE_SCORE, jnp.where(bid <= cur, imp, -1.0))
    imp = jnp.where(bid < n_sel, imp, -2.0)
    rank = jnp.zeros((rows, tq), jnp.int32)
    for j in range(n_sel):
        vj = imp[j:j + 1, :]
        first = (vj > imp) | ((vj == imp) & (j < bid))
        rank = rank + first.astype(jnp.int32)
    bias = jnp.where((rank < n_top) & (bid <= cur) & (bid < n_sel), 0.0, NEG_INF)
    sel_ref[0, 0] = bias.T.astype(sel_ref.dtype)


def _cmp_sel(q, kc, vc):
    bsz, groups, rep, seq, d = q.shape
    nb = kc.shape[2]
    tq = NSA_TQ
    n_cmp = (seq - CMP_LEN) // CMP_STRIDE + 1
    n_sel = seq // SEL_BLOCK
    n_top = min(SEL_TOP, n_sel)
    ratio = SEL_BLOCK // CMP_STRIDE
    assert n_sel <= LANES and n_sel * ratio == nb
    pool_np = np.zeros((LANES, nb), np.float32)
    pool_np[np.arange(nb) // ratio, np.arange(nb)] = 1.0
    pool = jnp.asarray(pool_np)
    kv_spec = pl.BlockSpec((1, 1, nb, d), lambda b, g, i: (b, g, 0, 0))
    return pl.pallas_call(
        functools.partial(_cmp_sel_kernel, n_cmp=n_cmp, n_sel=n_sel, n_top=n_top),
        out_shape=(jax.ShapeDtypeStruct(q.shape, BF16),
                   jax.ShapeDtypeStruct((bsz, groups, seq, LANES), BF16)),
        grid=(bsz, groups, seq // tq),
        in_specs=[pl.BlockSpec((1, 1, rep, tq, d), lambda b, g, i: (b, g, 0, i, 0)), kv_spec, kv_spec,
                  pl.BlockSpec(pool.shape, lambda b, g, i: (0, 0))],
        out_specs=(pl.BlockSpec((1, 1, rep, tq, d), lambda b, g, i: (b, g, 0, i, 0)),
                   pl.BlockSpec((1, 1, tq, LANES), lambda b, g, i: (b, g, i, 0))),
        compiler_params=_params(("parallel", "parallel", "parallel")),
        name="nsa_cmp_select",
    )(q, kc, vc, pool)


def _sel_kernel(q_ref, bias_ref, k_ref, blk_ref, v_ref, o_ref, m_ref, acc_ref):
    qi = pl.program_id(2)
    rep, tq, d = q_ref.shape[2:]
    tk = NSA_TK
    q = q_ref[0, 0].reshape(rep * tq, d)
    q_ext = jnp.concatenate([q, jnp.concatenate([bias_ref[0, 0]] * rep, axis=0)], axis=1)
    _softmax_init(m_ref, acc_ref)
    n_full = qi * tq // tk

    def tile(ki, mask):
        rows = pl.ds(pl.multiple_of(ki * tk, tk), tk)
        k_ext = jnp.concatenate([k_ref[0, 0, rows, :], blk_ref[rows, :]], axis=1)
        _softmax_tile(_dot_nt(q_ext, k_ext), mask, _with_ones(v_ref[0, 0, rows, :]), m_ref, acc_ref)

    def body(ki, carry):
        tile(ki, None)
        return carry

    lax.fori_loop(0, n_full, body, 0)
    qpos = qi * tq + lax.broadcasted_iota(jnp.int32, (1, tq, tk), 1)
    kpos = n_full * tk + lax.broadcasted_iota(jnp.int32, (1, tq, tk), 2)
    tile(n_full, kpos <= qpos)
    o_ref[0, 0] = _softmax_finish(acc_ref).reshape(rep, tq, d).astype(o_ref.dtype)


def _sel_attention(q, bias, k, v):
    bsz, groups, rep, seq, d = q.shape
    tq = NSA_TQ
    assert NSA_TK % tq == 0
    onehot_np = np.zeros((seq, LANES), np.float32)
    onehot_np[np.arange(seq), np.arange(seq) // SEL_BLOCK] = 1.0
    onehot = jnp.asarray(onehot_np, BF16)
    q_spec = pl.BlockSpec((1, 1, rep, tq, d), lambda b, g, qi: (b, g, 0, qi, 0))
    kv_spec = pl.BlockSpec((1, 1, seq, d), lambda b, g, qi: (b, g, 0, 0))
    return pl.pallas_call(
        _sel_kernel,
        out_shape=jax.ShapeDtypeStruct(q.shape, BF16),
        grid=(bsz, groups, seq // tq),
        in_specs=[q_spec, pl.BlockSpec((1, 1, tq, LANES), lambda b, g, qi: (b, g, qi, 0)),
                  kv_spec, pl.BlockSpec((seq, LANES), lambda b, g, qi: (0, 0)), kv_spec],
        out_specs=q_spec,
        scratch_shapes=[pltpu.VMEM((rep * tq, LANES), F32), pltpu.VMEM((rep * tq, 2 * LANES), F32)],
        compiler_params=_params(("parallel", "parallel", "arbitrary")),
        name="nsa_selected_attention",
    )(q, bias, k, onehot, v)


def _win_kernel(q_ref, k_ref, v_ref, o_ref, *, span):
    qi = pl.program_id(2)
    rep, tq, d = q_ref.shape[2:]
    q = q_ref[0, 0].reshape(rep * tq, d)
    start = pl.multiple_of(jnp.maximum(qi * tq + tq - span, 0), tq)
    rows = pl.ds(start, span)
    s = _dot_nt(q, k_ref[0, 0, rows, :]).reshape(rep, tq, span)
    qpos = qi * tq + lax.broadcasted_iota(jnp.int32, (1, tq, span), 1)
    kpos = start + lax.broadcasted_iota(jnp.int32, (1, tq, span), 2)
    diff = qpos - kpos
    mask = (diff >= 0) & (diff < WINDOW)
    s = jnp.where(mask, s, NEG_INF)
    m = jnp.max(s, axis=-1, keepdims=True)
    e = jnp.where(mask, jnp.exp2(s - m), 0.0).reshape(rep * tq, span)
    acc = _dot(e.astype(BF16), _with_ones(v_ref[0, 0, rows, :]))
    o = acc[:, :LANES] / jnp.maximum(acc[:, LANES:], TINY)
    o_ref[0, 0] = o.reshape(rep, tq, d).astype(o_ref.dtype)


def _win_attention(q, k, v):
    bsz, groups, rep, seq, d = q.shape
    tq = NSA_TQ
    span = (-(-(WINDOW - 1) // tq) + 1) * tq
    assert span <= seq
    q_spec = pl.BlockSpec((1, 1, rep, tq, d), lambda b, g, qi: (b, g, 0, qi, 0))
    kv_spec = pl.BlockSpec((1, 1, seq, d), lambda b, g, qi: (b, g, 0, 0))
    return pl.pallas_call(
        functools.partial(_win_kernel, span=span),
        out_shape=jax.ShapeDtypeStruct(q.shape, BF16),
        grid=(bsz, groups, seq // tq),
        in_specs=[q_spec, kv_spec, kv_spec],
        out_specs=q_spec,
        compiler_params=_params(("parallel", "parallel", "arbitrary")),
        name="nsa_window_attention",
    )(q, k, v)


def _od_out_kernel(oc_ref, os_ref, ow_ref, gate_ref, z_ref, x_ref, mod_ref, g_ref, w_ref, out_ref):
    d = x_ref.shape[-1]
    gates = gate_ref[0]
    z = z_ref[0].astype(F32)
    parts = []
    for hd in range(NSA_HEADS):
        g, r = hd // NSA_REP, hd % NSA_REP
        o = (gates[:, 3 * hd:3 * hd + 1] * oc_ref[0, g, r].astype(F32)
             + gates[:, 3 * hd + 1:3 * hd + 2] * os_ref[0, g, r].astype(F32)
             + gates[:, 3 * hd + 2:3 * hd + 3] * ow_ref[0, g, r].astype(F32))
        parts.append((o * _silu(z[:, hd * NSA_DIM:(hd + 1) * NSA_DIM])).astype(BF16))
    y = _dot(jnp.concatenate(parts, axis=-1), w_ref[...])
    gate = mod_ref[0][:, 2 * d:]
    out_ref[0] = x_ref[0] + gate * _rms(y, g_ref[...])


def _od_out(o_cmp, o_sel, o_win, gates, z, x, mod, post_g, w_out):
    bsz, seq, d = x.shape
    tm = TOKEN_TILE
    w = w_out.astype(BF16)
    g = post_g[None]
    tok = lambda n: pl.BlockSpec((1, tm, n), lambda b, i: (b, i, 0))
    full = lambda a: pl.BlockSpec(a.shape, lambda b, i: (0,) * a.ndim)
    o_spec = pl.BlockSpec((1, NSA_GROUPS, NSA_REP, tm, NSA_DIM), lambda b, i: (b, 0, 0, i, 0))
    return pl.pallas_call(
        _od_out_kernel,
        out_shape=jax.ShapeDtypeStruct(x.shape, F32),
        grid=(bsz, seq // tm),
        in_specs=[o_spec, o_spec, o_spec, tok(OD_GATE), tok(OD_Z), tok(d),
                  pl.BlockSpec((1, 1, 3 * d), lambda b, i: (b, 0, 0)), full(g), full(w)],
        out_specs=tok(d),
        compiler_params=_params(("parallel", "parallel")),
        name="odd_out_proj",
    )(o_cmp, o_sel, o_win, gates, z, x, mod[:, None, :], g, w)


def _even_layer(x, mod, pre_g, post_g, tabs, w_in, lam_re, lam_im, log_dt, b_re, b_im, c_re, c_im,
                d_skip, w_glu, b_glu, q_norm_g, kv_norm_g, w_uq, w_ukv, w_out):
    cos_m, sin_m = tabs[0], tabs[1]
    u, z_a, z_b, q, k, v = _ev_in(x, mod, pre_g, w_in, q_norm_g, kv_norm_g, w_uq, w_ukv, cos_m, sin_m)
    y_a = _s5(u, z_a, lam_re, lam_im, log_dt, b_re, b_im, c_re, c_im, d_skip, w_glu, b_glu)
    o_mla = _mla(q, k, v)
    return _ev_out(y_a, o_mla, z_b, x, mod, post_g, w_out)


def _odd_layer(x, mod, pre_g, post_g, tabs, w_in, k_pe, k_w1, k_w2, v_pe, v_w1, v_w2, w_out):
    bsz, seq, _ = x.shape
    cos_n, sin_n = tabs[2], tabs[3]
    q, k_c, v_c, k_s, v_s, k_w, v_w, gates, z = _od_in(x, mod, pre_g, w_in, cos_n, sin_n)
    nb = seq // CMP_STRIDE

    def at_block_end(t):
        t = t.reshape(bsz, nb, CMP_STRIDE, LANES)[:, 1:, CMP_STRIDE - 1]
        return jnp.pad(t, ((0, 0), (0, 1), (0, 0)))

    kc = _compress(k_c, k_pe, k_w1, k_w2, at_block_end(cos_n), at_block_end(sin_n), True)
    vc = _compress(v_c, v_pe, v_w1, v_w2, at_block_end(cos_n), at_block_end(sin_n), False)
    o_cmp, sel = _cmp_sel(q, kc, vc)
    o_sel = _sel_attention(q, sel, k_s, v_s)
    o_win = _win_attention(q, k_w, v_w)
    return _od_out(o_cmp, o_sel, o_win, gates, z, x, mod, post_g, w_out)


def kernel(x, c, positions, pre_norm_g, post_norm_g, w_ada, b_ada, ev_w_in, ev_lam_re, ev_lam_im, ev_log_dt, ev_b_re, ev_b_im, ev_c_re, ev_c_im, ev_d_skip, ev_w_glu, ev_b_glu, ev_q_norm_g, ev_kv_norm_g, ev_w_uq, ev_w_ukv, ev_w_out, od_w_in, od_cmp_k_pe, od_cmp_k_w1, od_cmp_k_w2, od_cmp_v_pe, od_cmp_v_w1, od_cmp_v_w2, od_w_out):
    depth = pre_norm_g.shape[0]
    tabs = _rope_tables(positions)
    mods = _modulation(c, w_ada, b_ada)
    for layer in range(depth):
        i = layer // 2
        if layer % 2 == 0:
            x = _even_layer(x, mods[layer], pre_norm_g[layer], post_norm_g[layer], tabs,
                            ev_w_in[i], ev_lam_re[i], ev_lam_im[i], ev_log_dt[i], ev_b_re[i], ev_b_im[i],
                            ev_c_re[i], ev_c_im[i], ev_d_skip[i], ev_w_glu[i], ev_b_glu[i],
                            ev_q_norm_g[i], ev_kv_norm_g[i], ev_w_uq[i], ev_w_ukv[i], ev_w_out[i])
        else:
            x = _odd_layer(x, mods[layer], pre_norm_g[layer], post_norm_g[layer], tabs,
                           od_w_in[i], od_cmp_k_pe[i], od_cmp_k_w1[i], od_cmp_k_w2[i],
                           od_cmp_v_pe[i], od_cmp_v_w1[i], od_cmp_v_w2[i], od_w_out[i])
    return x
```

```python
import functools
import math

import numpy as np
import jax
import jax.numpy as jnp
from jax import lax
from jax.experimental import pallas as pl
from jax.experimental.pallas import tpu as pltpu

F32 = jnp.float32
BF16 = jnp.bfloat16
HIGHEST = lax.Precision.HIGHEST

EPS = 1e-6
ROPE_THETA = 10000.0
NEG_INF = -1e30
TINY = 1e-30
FORCE_SCORE = 1e9
LOG2E = 1.0 / math.log(2.0)

S5_GROUP = 16
S5_STATE = 64
MLA_HEADS = 4
MLA_NOPE = 128
MLA_ROPE = 64
MLA_V = 128
NSA_HEADS = 8
NSA_GROUPS = 2
NSA_REP = NSA_HEADS // NSA_GROUPS
NSA_DIM = 128
CMP_LEN = 32
CMP_STRIDE = 16
CMP_HIDDEN = 256
SEL_BLOCK = 64
SEL_TOP = 16
WINDOW = 512

LANES = 128
SUBLANES = 8
VMEM_LIMIT = 48 * 1024 * 1024

TOKEN_TILE = 512
S5_CHUNK = 256
S5_COLS = 512
ATT_TQ = 512
ATT_TK = 512
NSA_TQ = 256
NSA_TK = 512


def _params(sem):
    return pltpu.CompilerParams(dimension_semantics=sem, vmem_limit_bytes=VMEM_LIMIT)


def _dot(a, b, precision=None):
    return jnp.dot(a, b, preferred_element_type=F32, precision=precision)


def _dot_nt(a, b, precision=None):
    return lax.dot_general(a, b, (((1,), (1,)), ((), ())), preferred_element_type=F32,
                           precision=precision)


def _silu(v):
    return v * jax.nn.sigmoid(v)


def _gelu_tanh(v):
    return 0.5 * v * (1.0 + jnp.tanh(math.sqrt(2.0 / math.pi) * (v + 0.044715 * (v * v * v))))


def _rms(v, g):
    return v * lax.rsqrt(jnp.mean(v * v, axis=-1, keepdims=True) + EPS) * g


def _rope_tab_kernel(pos_ref, fm_ref, fn_ref, sgm_ref, sgn_ref, cm_ref, sm_ref, cn_ref, sn_ref):
    p = pos_ref[0]
    am = p * fm_ref[...]
    cm_ref[0] = jnp.cos(am)
    sm_ref[0] = jnp.sin(am) * sgm_ref[...]
    an = p * fn_ref[...]
    cn_ref[0] = jnp.cos(an)
    sn_ref[0] = jnp.sin(an) * sgn_ref[...]


def _rope_tables(positions):
    bsz, seq = positions.shape
    ts = TOKEN_TILE
    pos = positions.astype(F32)[..., None]
    half_m, half_n = MLA_ROPE // 2, NSA_DIM // 2
    fm = ROPE_THETA ** (-jnp.arange(half_m, dtype=F32) / half_m)
    fn = ROPE_THETA ** (-jnp.arange(half_n, dtype=F32) / half_n)
    fm = jnp.tile(fm, LANES // half_m)[None]
    fn = jnp.tile(fn, LANES // half_n)[None]
    sgm = jnp.tile(jnp.concatenate([-jnp.ones((half_m,), F32), jnp.ones((half_m,), F32)]),
                   LANES // MLA_ROPE)[None]
    sgn = jnp.concatenate([-jnp.ones((half_n,), F32), jnp.ones((half_n,), F32)])[None]
    row = pl.BlockSpec((1, LANES), lambda b, i: (0, 0))
    tab = pl.BlockSpec((1, ts, LANES), lambda b, i: (b, i, 0))
    shp = jax.ShapeDtypeStruct((bsz, seq, LANES), F32)
    return pl.pallas_call(
        _rope_tab_kernel,
        out_shape=(shp, shp, shp, shp),
        grid=(bsz, seq // ts),
        in_specs=[pl.BlockSpec((1, ts, 1), lambda b, i: (b, i, 0)), row, row, row, row],
        out_specs=(tab, tab, tab, tab),
        compiler_params=_params(("parallel", "parallel")),
        name="rope_tables",
    )(pos, fm, fn, sgm, sgn)


def _mod_kernel(c_ref, w_ref, b_ref, o_ref):
    s = _silu(c_ref[...])
    o_ref[0] = _dot(s, w_ref[0], precision=HIGHEST) + b_ref[0]


def _modulation(c, w_ada, b_ada):
    depth, d, _ = w_ada.shape
    bsz = c.shape[0]
    rows = -(-bsz // SUBLANES) * SUBLANES
    c_pad = jnp.pad(c, ((0, rows - bsz), (0, 0)))
    out = pl.pallas_call(
        _mod_kernel,
        out_shape=jax.ShapeDtypeStruct((depth, rows, 3 * d), F32),
        grid=(depth, 3),
        in_specs=[pl.BlockSpec((rows, d), lambda l, j: (0, 0)),
                  pl.BlockSpec((1, d, d), lambda l, j: (l, 0, j)),
                  pl.BlockSpec((1, 1, d), lambda l, j: (l, 0, j))],
        out_specs=pl.BlockSpec((1, rows, d), lambda l, j: (l, 0, j)),
        compiler_params=_params(("parallel", "parallel")),
        name="modulation",
    )(c_pad, w_ada, b_ada[:, None, :])
    return out[:, :bsz]


def _modulated_input(x_ref, mod_ref, g_ref):
    d = x_ref.shape[-1]
    x = x_ref[0]
    mod = mod_ref[0]
    shift, scale = mod[:, :d], mod[:, d:2 * d]
    return (_rms(x, g_ref[...]) * (1.0 + scale) + shift).astype(BF16)


EV_U, EV_ZA, EV_CQ, EV_CKV, EV_KPE, EV_ZB = 512, 512, 768, 256, 256, 512
EV_OFF = np.cumsum([0, EV_U, EV_ZA, EV_CQ, EV_CKV, EV_KPE, EV_ZB])


def _ev_in_kernel(x_ref, mod_ref, g_ref, w_ref, gq_ref, gkv_ref, wuq_ref, wukv_ref, cm_ref, sm_ref,
                  u_ref, za_ref, zb_ref, q_ref, k_ref, v_ref):
    h = _modulated_input(x_ref, mod_ref, g_ref)
    p = _dot(h, w_ref[...])
    o = EV_OFF
    u_ref[0] = p[:, o[0]:o[1]].astype(BF16)
    za_ref[0] = p[:, o[1]:o[2]].astype(BF16)
    zb_ref[0] = p[:, o[5]:o[6]].astype(BF16)
    cos, sin = cm_ref[0], sm_ref[0]

    cq = _rms(p[:, o[2]:o[3]], gq_ref[...]).astype(BF16)
    q = _dot(cq, wuq_ref[...]) * ((MLA_NOPE + MLA_ROPE) ** -0.5 * LOG2E)
    ckv = _rms(p[:, o[3]:o[4]], gkv_ref[...]).astype(BF16)
    kv = _dot(ckv, wukv_ref[...])
    kpe = (p[:, o[4]:o[4] + LANES] * cos + p[:, o[4] + LANES:o[5]] * sin).astype(BF16)
    nope = MLA_HEADS * MLA_NOPE
    for hd in range(MLA_HEADS):
        b0 = hd * (MLA_NOPE + 2 * LANES)
        qpe = q[:, b0 + MLA_NOPE:b0 + MLA_NOPE + LANES] * cos + q[:, b0 + MLA_NOPE + LANES:b0 + MLA_NOPE + 2 * LANES] * sin
        q_ref[0, hd] = jnp.concatenate([q[:, b0:b0 + MLA_NOPE], qpe], axis=-1).astype(BF16)
        k_ref[0, hd] = jnp.concatenate([kv[:, hd * MLA_NOPE:(hd + 1) * MLA_NOPE].astype(BF16), kpe], axis=-1)
        v_ref[0, hd] = kv[:, nope + hd * MLA_V:nope + (hd + 1) * MLA_V].astype(BF16)


def _ev_in(x, mod, pre_g, w_in, q_norm_g, kv_norm_g, w_uq, w_ukv, cos_m, sin_m):
    bsz, seq, d = x.shape
    tm = TOKEN_TILE
    hq = MLA_ROPE // 2
    pad = LANES - MLA_ROPE

    def rot_cols(base, zero):
        x1, x2, z = base + np.arange(hq), base + hq + np.arange(hq), np.full(pad, zero)
        return np.concatenate([x1, x2, z, x2, x1, z])

    c0 = np.cumsum([0, 512, 512, 768, 256, 64, 512])
    cols = np.concatenate([np.arange(c0[0], c0[4]), rot_cols(c0[4], c0[6]), np.arange(c0[5], c0[6])])
    w = jnp.pad(w_in, ((0, 0), (0, 1)))[:, cols].astype(BF16)
    per = MLA_NOPE + MLA_ROPE
    q_cols = np.concatenate([np.concatenate([hd * per + np.arange(MLA_NOPE),
                                             rot_cols(hd * per + MLA_NOPE, MLA_HEADS * per)])
                             for hd in range(MLA_HEADS)])
    wuq = jnp.pad(w_uq, ((0, 0), (0, 1)))[:, q_cols].astype(BF16)
    per = MLA_NOPE + MLA_V
    heads = np.arange(MLA_HEADS)[:, None] * per
    kv_cols = np.concatenate([(heads + np.arange(MLA_NOPE)).ravel(),
                              (heads + MLA_NOPE + np.arange(MLA_V)).ravel()])
    wukv = w_ukv[:, kv_cols].astype(BF16)

    tok = lambda n: pl.BlockSpec((1, tm, n), lambda b, i: (b, i, 0))
    full = lambda a: pl.BlockSpec(a.shape, lambda b, i: (0,) * a.ndim)
    out = lambda n: jax.ShapeDtypeStruct((bsz, seq, n), BF16)
    head = lambda n: pl.BlockSpec((1, MLA_HEADS, tm, n), lambda b, i: (b, 0, i, 0))
    head_out = lambda n: jax.ShapeDtypeStruct((bsz, MLA_HEADS, seq, n), BF16)
    gq, gkv, g = q_norm_g[None], kv_norm_g[None], pre_g[None]
    return pl.pallas_call(
        _ev_in_kernel,
        out_shape=(out(512), out(512), out(512), head_out(2 * LANES), head_out(2 * LANES), head_out(MLA_V)),
        grid=(bsz, seq // tm),
        in_specs=[tok(d), pl.BlockSpec((1, 1, 3 * d), lambda b, i: (b, 0, 0)), full(g), full(w),
                  full(gq), full(gkv), full(wuq), full(wukv), tok(LANES), tok(LANES)],
        out_specs=(tok(512), tok(512), tok(512), head(2 * LANES), head(2 * LANES), head(MLA_V)),
        compiler_params=_params(("parallel", "parallel")),
        name="even_in_proj",
    )(x, mod[:, None, :], g, w, gq, gkv, wuq, wukv, cos_m, sin_m)


def _s5_kernel(u_ref, z_ref, perm_ref, permt_ref, wb_ref, wc_ref, a_ref, at_ref, pow_ref,
               d_ref, wglu_ref, bglu_ref, o_ref, bu_ref, xb_ref, state_ref, carry_ref):
    t_len = u_ref.shape[1]
    n = a_ref.shape[1] // 2
    steps = t_len // SUBLANES

    @pl.when(pl.program_id(1) == 0)
    def _():
        state_ref[...] = jnp.zeros_like(state_ref)

    perm = perm_ref[...]
    u_p = _dot(perm, u_ref[0])
    z_p = _dot(perm, z_ref[0])
    u_pb = u_p.astype(BF16)
    bre, bim = pl.ds(0, S5_COLS), pl.ds(S5_COLS, S5_COLS)
    y_parts = []

    for cb in range(n // S5_COLS):
        u_cb = u_pb[:, cb * LANES:(cb + 1) * LANES]
        bu_ref[...] = _dot(u_cb, wb_ref[cb])
        re = pl.ds(cb * S5_COLS, S5_COLS)
        im = pl.ds(n + cb * S5_COLS, S5_COLS)
        ar = jnp.broadcast_to(a_ref[:, re], (SUBLANES, S5_COLS))
        ai = jnp.broadcast_to(a_ref[:, im], (SUBLANES, S5_COLS))

        def step(t, carry):
            xr, xi = carry
            rows = pl.ds(pl.multiple_of(t * SUBLANES, SUBLANES), SUBLANES)
            nr = ar * xr - ai * xi + bu_ref[rows, bre]
            ni = ar * xi + ai * xr + bu_ref[rows, bim]
            bu_ref[rows, bre] = nr
            bu_ref[rows, bim] = ni
            return nr, ni

        zero = jnp.zeros((SUBLANES, S5_COLS), F32)
        er, ei = lax.fori_loop(0, steps, step, (zero, zero), unroll=4)

        sr, si = state_ref[:, re], state_ref[:, im]
        tr, ti = at_ref[:, re], at_ref[:, im]
        for j in range(SUBLANES):
            carry_ref[j:j + 1, re] = sr
            carry_ref[j:j + 1, im] = si
            sr, si = (tr * sr - ti * si + er[j:j + 1], tr * si + ti * sr + ei[j:j + 1])
        state_ref[:, re] = sr
        state_ref[:, im] = si
        cr, ci = carry_ref[:, re], carry_ref[:, im]

        def fix(t2, _):
            xr, xi = [], []
            for k in range(2):
                t = 2 * t2 + k
                rows = pl.ds(pl.multiple_of(t * SUBLANES, SUBLANES), SUBLANES)
                pr, pi = pow_ref[pl.ds(t, 1), re], pow_ref[pl.ds(t, 1), im]
                xr.append(bu_ref[rows, bre] + pr * cr - pi * ci)
                xi.append(bu_ref[rows, bim] + pr * ci + pi * cr)
            rows = pl.ds(pl.multiple_of(t2 * 2 * SUBLANES, 2 * SUBLANES), 2 * SUBLANES)
            xb_ref[rows, bre] = jnp.concatenate(xr, axis=0).astype(BF16)
            xb_ref[rows, bim] = jnp.concatenate(xi, axis=0).astype(BF16)
            return 0

        lax.fori_loop(0, steps // 2, fix, 0, unroll=2)
        y_parts.append(_dot(xb_ref[...], wc_ref[cb]))

    y = jnp.concatenate(y_parts, axis=1) + d_ref[...] * u_p
    g = _gelu_tanh(y)
    gate = jax.nn.sigmoid(_dot(g.astype(BF16), wglu_ref[...]) + bglu_ref[...])
    out = (g * gate * _silu(z_p)).astype(BF16)
    o_ref[0] = _dot(permt_ref[...], out).astype(BF16)


def _s5(u, z_a, lam_re, lam_im, log_dt, b_re, b_im, c_re, c_im, d_skip, w_glu, b_glu):
    bsz, seq, width = u.shape
    groups, state = lam_re.shape
    t_len = S5_CHUNK
    steps = t_len // SUBLANES
    n = groups * state
    dt = jnp.exp(log_dt)[:, None]
    lam_dt_re, lam_dt_im = lam_re * dt, lam_im * dt
    decay = jnp.exp(lam_dt_re)
    ab_re, ab_im = decay * jnp.cos(lam_dt_im), decay * jnp.sin(lam_dt_im)
    den = lam_re * lam_re + lam_im * lam_im
    nr, ni = ab_re - 1.0, ab_im
    f_re = (nr * lam_re + ni * lam_im) / den
    f_im = (ni * lam_re - nr * lam_im) / den
    bb_re = f_re[..., None] * b_re - f_im[..., None] * b_im
    bb_im = f_re[..., None] * b_im + f_im[..., None] * b_re
    gpb = S5_COLS // state
    nblk = groups // gpb
    hdim = width // groups
    assert gpb * hdim == LANES and nblk * S5_COLS == n
    eye = jnp.eye(gpb, dtype=F32)

    def in_blocks(bb):
        bb = bb.reshape(nblk, gpb, state, hdim)
        return jnp.einsum('mgph,gk->mghkp', bb, eye).reshape(nblk, gpb * hdim, gpb * state)

    def out_blocks(cc):
        cc = cc.reshape(nblk, gpb, hdim, state)
        return jnp.einsum('mghp,gk->mgpkh', cc, eye).reshape(nblk, gpb * state, gpb * hdim)

    wb = jnp.concatenate([in_blocks(bb_re), in_blocks(bb_im)], axis=2).astype(BF16)
    wc = jnp.concatenate([out_blocks(c_re), out_blocks(-c_im)], axis=1).astype(BF16)
    a_vec = jnp.concatenate([ab_re.reshape(1, n), ab_im.reshape(1, n)], axis=1)
    ks = jnp.arange(1, steps + 1, dtype=F32)[:, None, None]
    pw_mag = jnp.exp(lam_dt_re[None] * ks)
    pw_re, pw_im = pw_mag * jnp.cos(lam_dt_im[None] * ks), pw_mag * jnp.sin(lam_dt_im[None] * ks)
    pow_tab = jnp.concatenate([pw_re.reshape(steps, n), pw_im.reshape(steps, n)], axis=1)
    at_vec = pow_tab[steps - 1:steps]
    r = np.arange(t_len)
    perm_np = np.zeros((t_len, t_len), np.float32)
    perm_np[r, (r % SUBLANES) * steps + r // SUBLANES] = 1.0
    perm = jnp.asarray(perm_np, BF16)
    permt = jnp.asarray(perm_np.T, BF16)
    d_vec = d_skip.reshape(1, width)
    wglu = w_glu.astype(BF16)
    bglu = b_glu[None]

    tok = pl.BlockSpec((1, t_len, width), lambda b, i: (b, i, 0))
    full = lambda a: pl.BlockSpec(a.shape, lambda b, i: (0,) * a.ndim)
    consts = (perm, permt, wb, wc, a_vec, at_vec, pow_tab, d_vec, wglu, bglu)
    return pl.pallas_call(
        _s5_kernel,
        out_shape=jax.ShapeDtypeStruct((bsz, seq, width), BF16),
        grid=(bsz, seq // t_len),
        in_specs=[tok, tok] + [full(a) for a in consts],
        out_specs=tok,
        scratch_shapes=[pltpu.VMEM((t_len, 2 * S5_COLS), F32), pltpu.VMEM((t_len, 2 * S5_COLS), BF16),
                        pltpu.VMEM((1, 2 * n), F32), pltpu.VMEM((SUBLANES, 2 * n), F32)],
        compiler_params=_params(("parallel", "arbitrary")),
        name="s5_mixer",
    )(u, z_a, *consts)


def _softmax_init(m_ref, acc_ref):
    m_ref[...] = jnp.full_like(m_ref, NEG_INF)
    acc_ref[...] = jnp.zeros_like(acc_ref)


def _softmax_tile(s, mask, v_ext, m_ref, acc_ref):
    rows, tk = s.shape
    if mask is not None:
        heads = rows // mask.shape[1]
        s = jnp.where(mask, s.reshape(heads, *mask.shape[1:]), NEG_INF).reshape(rows, tk)
    m_old = m_ref[...]
    m_new = jnp.maximum(m_old, jnp.max(s, axis=-1, keepdims=True))
    e = jnp.exp2(s - jnp.concatenate([m_new] * (tk // LANES), axis=1))
    if mask is not None:
        e = jnp.where(mask, e.reshape(heads, *mask.shape[1:]), 0.0).reshape(rows, tk)
    alpha = jnp.exp2(m_old - m_new)
    acc_ref[...] = (jnp.concatenate([alpha, alpha], axis=1) * acc_ref[...]
                    + _dot(e.astype(BF16), v_ext))
    m_ref[...] = m_new


def _softmax_finish(acc_ref):
    acc = acc_ref[...]
    return acc[:, :LANES] / jnp.maximum(acc[:, LANES:], TINY)


def _with_ones(v):
    return jnp.concatenate([v, jnp.ones(v.shape, v.dtype)], axis=1)


def _mla_kernel(q_ref, k_ref, v_ref, o_ref, m_ref, acc_ref):
    qi = pl.program_id(1)
    heads, tq = q_ref.shape[1:3]
    tk = ATT_TK
    _softmax_init(m_ref, acc_ref)
    n_full = qi * tq // tk

    def tile(ki, mask):
        rows = pl.ds(pl.multiple_of(ki * tk, tk), tk)
        for hd in range(heads):
            own = slice(hd * tq, (hd + 1) * tq)
            s = _dot_nt(q_ref[0, hd], k_ref[0, hd, rows, :])
            _softmax_tile(s, mask, _with_ones(v_ref[0, hd, rows, :]), m_ref.at[own], acc_ref.at[own])

    def body(ki, carry):
        tile(ki, None)
        return carry

    lax.fori_loop(0, n_full, body, 0)
    qpos = qi * tq + lax.broadcasted_iota(jnp.int32, (1, tq, tk), 1)
    kpos = n_full * tk + lax.broadcasted_iota(jnp.int32, (1, tq, tk), 2)
    tile(n_full, kpos <= qpos)
    for hd in range(heads):
        o_ref[0, :, hd * MLA_V:(hd + 1) * MLA_V] = _softmax_finish(
            acc_ref.at[hd * tq:(hd + 1) * tq]).astype(o_ref.dtype)


def _mla(q, k, v):
    bsz, heads, seq, dk = q.shape
    tq = ATT_TQ
    assert tq == ATT_TK
    return pl.pallas_call(
        _mla_kernel,
        out_shape=jax.ShapeDtypeStruct((bsz, seq, heads * MLA_V), BF16),
        grid=(bsz, seq // tq),
        in_specs=[pl.BlockSpec((1, heads, tq, dk), lambda b, qi: (b, 0, qi, 0)),
                  pl.BlockSpec((1, heads, seq, dk), lambda b, qi: (b, 0, 0, 0)),
                  pl.BlockSpec((1, heads, seq, MLA_V), lambda b, qi: (b, 0, 0, 0))],
        out_specs=pl.BlockSpec((1, tq, heads * MLA_V), lambda b, qi: (b, qi, 0)),
        scratch_shapes=[pltpu.VMEM((heads * tq, LANES), F32), pltpu.VMEM((heads * tq, 2 * LANES), F32)],
        compiler_params=_params(("parallel", "arbitrary")),
        name="mla_attention",
    )(q, k, v)


def _ev_out_kernel(ya_ref, o_ref, zb_ref, x_ref, mod_ref, g_ref, wa_ref, wb_ref, out_ref):
    d = x_ref.shape[-1]
    yb = (o_ref[0].astype(F32) * _silu(zb_ref[0].astype(F32))).astype(BF16)
    y = _dot(ya_ref[0], wa_ref[...]) + _dot(yb, wb_ref[...])
    gate = mod_ref[0][:, 2 * d:]
    out_ref[0] = x_ref[0] + gate * _rms(y, g_ref[...])


def _ev_out(y_a, o_mla, z_b, x, mod, post_g, w_out):
    bsz, seq, d = x.shape
    tm = TOKEN_TILE
    wa = w_out[:y_a.shape[-1]].astype(BF16)
    wb = w_out[y_a.shape[-1]:].astype(BF16)
    g = post_g[None]
    tok = lambda n: pl.BlockSpec((1, tm, n), lambda b, i: (b, i, 0))
    full = lambda a: pl.BlockSpec(a.shape, lambda b, i: (0,) * a.ndim)
    return pl.pallas_call(
        _ev_out_kernel,
        out_shape=jax.ShapeDtypeStruct(x.shape, F32),
        grid=(bsz, seq // tm),
        in_specs=[tok(y_a.shape[-1]), tok(o_mla.shape[-1]), tok(z_b.shape[-1]), tok(d),
                  pl.BlockSpec((1, 1, 3 * d), lambda b, i: (b, 0, 0)), full(g), full(wa), full(wb)],
        out_specs=tok(d),
        compiler_params=_params(("parallel", "parallel")),
        name="even_out_proj",
    )(y_a, o_mla, z_b, x, mod[:, None, :], g, wa, wb)


OD_Q, OD_KV, OD_GATE, OD_Z = 1024, 256, 128, 1024
OD_OFF = np.cumsum([0, OD_Q] + [OD_KV] * 6 + [OD_GATE, OD_Z])


def _od_in_kernel(x_ref, mod_ref, g_ref, w_ref, cn_ref, sn_ref,
                  q_ref, kc_ref, vc_ref, ks_ref, vs_ref, kw_ref, vw_ref, gate_ref, z_ref):
    h = _modulated_input(x_ref, mod_ref, g_ref)
    p = _dot(h, w_ref[...])
    o = OD_OFF
    cos, sin = cn_ref[0], sn_ref[0]

    def rope(t):
        return t * cos + pltpu.roll(t, NSA_DIM // 2, axis=1) * sin

    scale = NSA_DIM ** -0.5 * LOG2E
    for hd in range(NSA_HEADS):
        t = p[:, hd * NSA_DIM:(hd + 1) * NSA_DIM]
        q_ref[0, hd // NSA_REP, hd % NSA_REP] = (rope(t) * scale).astype(BF16)
    plain = (kc_ref, vc_ref, None, vs_ref, None, vw_ref)
    roped = (None, None, ks_ref, None, kw_ref, None)
    for j in range(6):
        for g in range(NSA_GROUPS):
            lo = o[1 + j] + g * NSA_DIM
            t = p[:, lo:lo + NSA_DIM]
            if plain[j] is not None:
                plain[j][0, g] = t.astype(BF16)
            else:
                roped[j][0, g] = rope(t).astype(BF16)
    gate_ref[0] = jax.nn.sigmoid(p[:, o[7]:o[8]])
    z_ref[0] = p[:, o[8]:o[9]].astype(BF16)


def _od_in(x, mod, pre_g, w_in, cos_n, sin_n):
    bsz, seq, d = x.shape
    tm = TOKEN_TILE
    n_gate = 3 * NSA_HEADS
    c0 = OD_Q + 6 * OD_KV
    w = jnp.concatenate([w_in[:, :c0 + n_gate], jnp.zeros((d, OD_GATE - n_gate), F32),
                         w_in[:, c0 + n_gate:]], axis=1).astype(BF16)
    g = pre_g[None]
    tok = lambda n: pl.BlockSpec((1, tm, n), lambda b, i: (b, i, 0))
    full = lambda a: pl.BlockSpec(a.shape, lambda b, i: (0,) * a.ndim)
    q_spec = pl.BlockSpec((1, NSA_GROUPS, NSA_REP, tm, NSA_DIM), lambda b, i: (b, 0, 0, i, 0))
    kv_spec = pl.BlockSpec((1, NSA_GROUPS, tm, NSA_DIM), lambda b, i: (b, 0, i, 0))
    kv_shape = jax.ShapeDtypeStruct((bsz, NSA_GROUPS, seq, NSA_DIM), BF16)
    return pl.pallas_call(
        _od_in_kernel,
        out_shape=(jax.ShapeDtypeStruct((bsz, NSA_GROUPS, NSA_REP, seq, NSA_DIM), BF16),)
        + (kv_shape,) * 6
        + (jax.ShapeDtypeStruct((bsz, seq, OD_GATE), F32), jax.ShapeDtypeStruct((bsz, seq, OD_Z), BF16)),
        grid=(bsz, seq // tm),
        in_specs=[tok(d), pl.BlockSpec((1, 1, 3 * d), lambda b, i: (b, 0, 0)), full(g), full(w),
                  tok(LANES), tok(LANES)],
        out_specs=(q_spec,) + (kv_spec,) * 6 + (tok(OD_GATE), tok(OD_Z)),
        compiler_params=_params(("parallel", "parallel")),
        name="odd_in_proj",
    )(x, mod[:, None, :], g, w, cos_n, sin_n)


def _compress_kernel(x_ref, shift_ref, pe_ref, w1_ref, w2_ref, cos_ref, sin_ref, o_ref, *, use_rope):
    x = x_ref[0, 0]
    half = x.shape[1]
    x_next = _dot(shift_ref[...], x).astype(BF16)
    pre = (_dot(x, w1_ref[:half]) + _dot(x_next, w1_ref[half:])
           + _dot(pe_ref[...], w1_ref[...])[0:1])
    out = _dot(_gelu_tanh(pre).astype(BF16), w2_ref[...])
    if use_rope:
        out = out * cos_ref[0] + pltpu.roll(out, NSA_DIM // 2, axis=1) * sin_ref[0]
    o_ref[0, 0] = out.astype(BF16)


def _compress(kv, pe, w1, w2, cos_end, sin_end, use_rope):
    bsz, groups, seq, d = kv.shape
    nb = seq // CMP_STRIDE
    x = kv.reshape(bsz, groups, nb, CMP_STRIDE * d)
    shift = jnp.asarray(np.eye(nb, k=1, dtype=np.float32), BF16)
    pe_rows = jnp.broadcast_to(pe.reshape(1, CMP_LEN * d), (SUBLANES, CMP_LEN * d)).astype(BF16)
    w1b, w2b = w1.astype(BF16), w2.astype(BF16)
    full = lambda a: pl.BlockSpec(a.shape, lambda b, g: (0,) * a.ndim)
    end = pl.BlockSpec((1, nb, d), lambda b, g: (b, 0, 0))
    return pl.pallas_call(
        functools.partial(_compress_kernel, use_rope=use_rope),
        out_shape=jax.ShapeDtypeStruct((bsz, groups, nb, d), BF16),
        grid=(bsz, groups),
        in_specs=[pl.BlockSpec((1, 1, nb, CMP_STRIDE * d), lambda b, g: (b, g, 0, 0)),
                  full(shift), full(pe_rows), full(w1b), full(w2b), end, end],
        out_specs=pl.BlockSpec((1, 1, nb, d), lambda b, g: (b, g, 0, 0)),
        compiler_params=_params(("parallel", "parallel")),
        name="nsa_compress",
    )(x, shift, pe_rows, w1b, w2b, cos_end, sin_end)


def _cmp_sel_kernel(q_ref, kc_ref, vc_ref, pool_ref, o_ref, sel_ref, *, n_cmp, n_sel, n_top):
    qi = pl.program_id(2)
    rep, tq, d = q_ref.shape[2:]
    nb = kc_ref.shape[2]
    q = q_ref[0, 0].reshape(rep * tq, d)
    s = _dot_nt(q, kc_ref[0, 0]).reshape(rep, tq, nb)
    qpos = qi * tq + lax.broadcasted_iota(jnp.int32, (tq, nb), 0)
    blk = lax.broadcasted_iota(jnp.int32, (tq, nb), 1)
    mask = ((blk * CMP_STRIDE + (CMP_LEN - 1) <= qpos) & (blk < n_cmp))[None]
    s = jnp.where(mask, s, NEG_INF)
    m = jnp.max(s, axis=-1, keepdims=True)
    e = jnp.where(mask, jnp.exp2(s - m), 0.0)
    p = e / jnp.maximum(jnp.sum(e, axis=-1, keepdims=True), TINY)
    o = _dot(p.reshape(rep * tq, nb).astype(BF16), vc_ref[0, 0])
    o_ref[0, 0] = o.reshape(rep, tq, d).astype(o_ref.dtype)

    rows = pool_ref.shape[0]
    imp = _dot_nt(pool_ref[...], jnp.sum(p, axis=0), precision=HIGHEST)[:n_sel]
    bid = lax.broadcasted_iota(jnp.int32, (n_sel, tq), 0)
    cur = (qi * tq + lax.broadcasted_iota(jnp.int32, (n_sel, tq), 1)) // SEL_BLOCK
    forced = (bid == 0) | (bid == cur) | (bid == cur - 1)
    imp = jnp.where(forced, FORCE_SCORE, jnp.where(bid <= cur, imp, -1.0))
    groups = [imp[g:g + SUBLANES] for g in range(0, n_sel, SUBLANES)]
    sub = lax.broadcasted_iota(jnp.int32, (SUBLANES, tq), 0)
    ranks = [jnp.zeros((SUBLANES, tq), F32) for _ in groups]
    for j in range(n_sel):
        vj = jnp.broadcast_to(imp[j:j + 1, :], (SUBLANES, tq))
        for gi, grp in enumerate(groups):
            lo = gi * SUBLANES
            if lo > j:
                first = vj >= grp
            elif lo + SUBLANES - 1 <= j:
                first = vj > grp
            else:
                first = (vj > grp) | ((vj == grp) & (sub > j - lo))
            ranks[gi] = ranks[gi] + jnp.where(first, 1.0, 0.0)
    rank = jnp.concatenate(ranks, axis=0)
    bias = jnp.where((rank < n_top) & (bid <= cur), 0.0, NEG_INF)
    bias = jnp.concatenate([bias, jnp.full((rows - n_sel, tq), NEG_INF, F32)], axis=0)
    sel_ref[0, 0] = bias.T.astype(sel_ref.dtype)


def _cmp_sel(q, kc, vc):
    bsz, groups, rep, seq, d = q.shape
    nb = kc.shape[2]
    tq = NSA_TQ
    n_cmp = (seq - CMP_LEN) // CMP_STRIDE + 1
    n_sel = seq // SEL_BLOCK
    n_top = min(SEL_TOP, n_sel)
    ratio = SEL_BLOCK // CMP_STRIDE
    assert n_sel <= LANES and n_sel * ratio == nb
    pool_np = np.zeros((LANES, nb), np.float32)
    pool_np[np.arange(nb) // ratio, np.arange(nb)] = 1.0
    pool = jnp.asarray(pool_np)
    kv_spec = pl.BlockSpec((1, 1, nb, d), lambda b, g, i: (b, g, 0, 0))
    return pl.pallas_call(
        functools.partial(_cmp_sel_kernel, n_cmp=n_cmp, n_sel=n_sel, n_top=n_top),
        out_shape=(jax.ShapeDtypeStruct(q.shape, BF16),
                   jax.ShapeDtypeStruct((bsz, groups, seq, LANES), BF16)),
        grid=(bsz, groups, seq // tq),
        in_specs=[pl.BlockSpec((1, 1, rep, tq, d), lambda b, g, i: (b, g, 0, i, 0)), kv_spec, kv_spec,
                  pl.BlockSpec(pool.shape, lambda b, g, i: (0, 0))],
        out_specs=(pl.BlockSpec((1, 1, rep, tq, d), lambda b, g, i: (b, g, 0, i, 0)),
                   pl.BlockSpec((1, 1, tq, LANES), lambda b, g, i: (b, g, i, 0))),
        compiler_params=_params(("parallel", "parallel", "parallel")),
        name="nsa_cmp_select",
    )(q, kc, vc, pool)


def _sel_kernel(q_ref, bias_ref, k_ref, blk_ref, v_ref, o_ref, m_ref, acc_ref):
    qi = pl.program_id(2)
    rep, tq, d = q_ref.shape[2:]
    tk = NSA_TK
    q = q_ref[0, 0].reshape(rep * tq, d)
    q_ext = jnp.concatenate([q, jnp.concatenate([bias_ref[0, 0]] * rep, axis=0)], axis=1)
    _softmax_init(m_ref, acc_ref)
    n_full = qi * tq // tk

    def tile(ki, mask):
        rows = pl.ds(pl.multiple_of(ki * tk, tk), tk)
        k_ext = jnp.concatenate([k_ref[0, 0, rows, :], blk_ref[rows, :]], axis=1)
        _softmax_tile(_dot_nt(q_ext, k_ext), mask, _with_ones(v_ref[0, 0, rows, :]), m_ref, acc_ref)

    def body(ki, carry):
        tile(ki, None)
        return carry

    lax.fori_loop(0, n_full, body, 0)
    qpos = qi * tq + lax.broadcasted_iota(jnp.int32, (1, tq, tk), 1)
    kpos = n_full * tk + lax.broadcasted_iota(jnp.int32, (1, tq, tk), 2)
    tile(n_full, kpos <= qpos)
    o_ref[0, 0] = _softmax_finish(acc_ref).reshape(rep, tq, d).astype(o_ref.dtype)


def _sel_attention(q, bias, k, v):
    bsz, groups, rep, seq, d = q.shape
    tq = NSA_TQ
    assert NSA_TK % tq == 0
    onehot_np = np.zeros((seq, LANES), np.float32)
    onehot_np[np.arange(seq), np.arange(seq) // SEL_BLOCK] = 1.0
    onehot = jnp.asarray(onehot_np, BF16)
    q_spec = pl.BlockSpec((1, 1, rep, tq, d), lambda b, g, qi: (b, g, 0, qi, 0))
    kv_spec = pl.BlockSpec((1, 1, seq, d), lambda b, g, qi: (b, g, 0, 0))
    return pl.pallas_call(
        _sel_kernel,
        out_shape=jax.ShapeDtypeStruct(q.shape, BF16),
        grid=(bsz, groups, seq // tq),
        in_specs=[q_spec, pl.BlockSpec((1, 1, tq, LANES), lambda b, g, qi: (b, g, qi, 0)),
                  kv_spec, pl.BlockSpec((seq, LANES), lambda b, g, qi: (0, 0)), kv_spec],
        out_specs=q_spec,
        scratch_shapes=[pltpu.VMEM((rep * tq, LANES), F32), pltpu.VMEM((rep * tq, 2 * LANES), F32)],
        compiler_params=_params(("parallel", "parallel", "arbitrary")),
        name="nsa_selected_attention",
    )(q, bias, k, onehot, v)


def _win_kernel(q_ref, k_ref, v_ref, o_ref, *, span):
    qi = pl.program_id(2)
    rep, tq, d = q_ref.shape[2:]
    start = pl.multiple_of(jnp.maximum(qi * tq + tq - span, 0), tq)
    rows = pl.ds(start, span)
    qpos = qi * tq + lax.broadcasted_iota(jnp.int32, (1, tq, span), 1)
    kpos = start + lax.broadcasted_iota(jnp.int32, (1, tq, span), 2)
    diff = qpos - kpos
    mask = (diff >= 0) & (diff < WINDOW)
    k, v_ext = k_ref[0, 0, rows, :], _with_ones(v_ref[0, 0, rows, :])
    part = rep // 2
    for h0 in range(0, rep, part):
        q = q_ref[0, 0, h0:h0 + part].reshape(part * tq, d)
        s = jnp.where(mask, _dot_nt(q, k).reshape(part, tq, span), NEG_INF)
        m = jnp.max(s, axis=-1, keepdims=True)
        e = jnp.where(mask, jnp.exp2(s - m), 0.0).reshape(part * tq, span)
        acc = _dot(e.astype(BF16), v_ext)
        o = acc[:, :LANES] / jnp.maximum(acc[:, LANES:], TINY)
        o_ref[0, 0, h0:h0 + part] = o.reshape(part, tq, d).astype(o_ref.dtype)


def _win_attention(q, k, v):
    bsz, groups, rep, seq, d = q.shape
    tq = NSA_TQ
    span = (-(-(WINDOW - 1) // tq) + 1) * tq
    assert span <= seq
    q_spec = pl.BlockSpec((1, 1, rep, tq, d), lambda b, g, qi: (b, g, 0, qi, 0))
    kv_spec = pl.BlockSpec((1, 1, seq, d), lambda b, g, qi: (b, g, 0, 0))
    return pl.pallas_call(
        functools.partial(_win_kernel, span=span),
        out_shape=jax.ShapeDtypeStruct(q.shape, BF16),
        grid=(bsz, groups, seq // tq),
        in_specs=[q_spec, kv_spec, kv_spec],
        out_specs=q_spec,
        compiler_params=_params(("parallel", "parallel", "arbitrary")),
        name="nsa_window_attention",
    )(q, k, v)


def _od_out_kernel(oc_ref, os_ref, ow_ref, gate_ref, z_ref, x_ref, mod_ref, g_ref, w_ref, out_ref):
    d = x_ref.shape[-1]
    gates = gate_ref[0]
    z = z_ref[0].astype(F32)
    parts = []
    for hd in range(NSA_HEADS):
        g, r = hd // NSA_REP, hd % NSA_REP
        o = (gates[:, 3 * hd:3 * hd + 1] * oc_ref[0, g, r].astype(F32)
             + gates[:, 3 * hd + 1:3 * hd + 2] * os_ref[0, g, r].astype(F32)
             + gates[:, 3 * hd + 2:3 * hd + 3] * ow_ref[0, g, r].astype(F32))
        parts.append((o * _silu(z[:, hd * NSA_DIM:(hd + 1) * NSA_DIM])).astype(BF16))
    y = _dot(jnp.concatenate(parts, axis=-1), w_ref[...])
    gate = mod_ref[0][:, 2 * d:]
    out_ref[0] = x_ref[0] + gate * _rms(y, g_ref[...])


def _od_out(o_cmp, o_sel, o_win, gates, z, x, mod, post_g, w_out):
    bsz, seq, d = x.shape
    tm = TOKEN_TILE
    w = w_out.astype(BF16)
    g = post_g[None]
    tok = lambda n: pl.BlockSpec((1, tm, n), lambda b, i: (b, i, 0))
    full = lambda a: pl.BlockSpec(a.shape, lambda b, i: (0,) * a.ndim)
    o_spec = pl.BlockSpec((1, NSA_GROUPS, NSA_REP, tm, NSA_DIM), lambda b, i: (b, 0, 0, i, 0))
    return pl.pallas_call(
        _od_out_kernel,
        out_shape=jax.ShapeDtypeStruct(x.shape, F32),
        grid=(bsz, seq // tm),
        in_specs=[o_spec, o_spec, o_spec, tok(OD_GATE), tok(OD_Z), tok(d),
                  pl.BlockSpec((1, 1, 3 * d), lambda b, i: (b, 0, 0)), full(g), full(w)],
        out_specs=tok(d),
        compiler_params=_params(("parallel", "parallel")),
        name="odd_out_proj",
    )(o_cmp, o_sel, o_win, gates, z, x, mod[:, None, :], g, w)


def _even_layer(x, mod, pre_g, post_g, tabs, w_in, lam_re, lam_im, log_dt, b_re, b_im, c_re, c_im,
                d_skip, w_glu, b_glu, q_norm_g, kv_norm_g, w_uq, w_ukv, w_out):
    cos_m, sin_m = tabs[0], tabs[1]
    u, z_a, z_b, q, k, v = _ev_in(x, mod, pre_g, w_in, q_norm_g, kv_norm_g, w_uq, w_ukv, cos_m, sin_m)
    y_a = _s5(u, z_a, lam_re, lam_im, log_dt, b_re, b_im, c_re, c_im, d_skip, w_glu, b_glu)
    o_mla = _mla(q, k, v)
    return _ev_out(y_a, o_mla, z_b, x, mod, post_g, w_out)


def _odd_layer(x, mod, pre_g, post_g, tabs, w_in, k_pe, k_w1, k_w2, v_pe, v_w1, v_w2, w_out):
    bsz, seq, _ = x.shape
    cos_n, sin_n = tabs[2], tabs[3]
    q, k_c, v_c, k_s, v_s, k_w, v_w, gates, z = _od_in(x, mod, pre_g, w_in, cos_n, sin_n)
    nb = seq // CMP_STRIDE

    def at_block_end(t):
        t = t.reshape(bsz, nb, CMP_STRIDE, LANES)[:, 1:, CMP_STRIDE - 1]
        return jnp.pad(t, ((0, 0), (0, 1), (0, 0)))

    kc = _compress(k_c, k_pe, k_w1, k_w2, at_block_end(cos_n), at_block_end(sin_n), True)
    vc = _compress(v_c, v_pe, v_w1, v_w2, at_block_end(cos_n), at_block_end(sin_n), False)
    o_cmp, sel = _cmp_sel(q, kc, vc)
    o_sel = _sel_attention(q, sel, k_s, v_s)
    o_win = _win_attention(q, k_w, v_w)
    return _od_out(o_cmp, o_sel, o_win, gates, z, x, mod, post_g, w_out)


def kernel(x, c, positions, pre_norm_g, post_norm_g, w_ada, b_ada, ev_w_in, ev_lam_re, ev_lam_im, ev_log_dt, ev_b_re, ev_b_im, ev_c_re, ev_c_im, ev_d_skip, ev_w_glu, ev_b_glu, ev_q_norm_g, ev_kv_norm_g, ev_w_uq, ev_w_ukv, ev_w_out, od_w_in, od_cmp_k_pe, od_cmp_k_w1, od_cmp_k_w2, od_cmp_v_pe, od_cmp_v_w1, od_cmp_v_w2, od_w_out):
    depth = pre_norm_g.shape[0]
    tabs = _rope_tables(positions)
    mods = _modulation(c, w_ada, b_ada)
    for layer in range(depth):
        i = layer // 2
        if layer % 2 == 0:
            x = _even_layer(x, mods[layer], pre_norm_g[layer], post_norm_g[layer], tabs,
                            ev_w_in[i], ev_lam_re[i], ev_lam_im[i], ev_log_dt[i], ev_b_re[i], ev_b_im[i],
                            ev_c_re[i], ev_c_im[i], ev_d_skip[i], ev_w_glu[i], ev_b_glu[i],
                            ev_q_norm_g[i], ev_kv_norm_g[i], ev_w_uq[i], ev_w_ukv[i], ev_w_out[i])
        else:
            x = _odd_layer(x, mods[layer], pre_norm_g[layer], post_norm_g[layer], tabs,
                           od_w_in[i], od_cmp_k_pe[i], od_cmp_k_w1[i], od_cmp_k_w2[i],
                           od_cmp_v_pe[i], od_cmp_v_w1[i], od_cmp_v_w2[i], od_w_out[i])
    return x
```

```python
import functools
import math

import numpy as np
import jax
import jax.numpy as jnp
from jax import lax
from jax.experimental import pallas as pl
from jax.experimental.pallas import tpu as pltpu

F32 = jnp.float32
BF16 = jnp.bfloat16
HIGHEST = lax.Precision.HIGHEST

EPS = 1e-6
ROPE_THETA = 10000.0
NEG_INF = -1e30
TINY = 1e-30
FORCE_SCORE = 1e9
LOG2E = 1.0 / math.log(2.0)

S5_GROUP = 16
S5_STATE = 64
MLA_HEADS = 4
MLA_NOPE = 128
MLA_ROPE = 64
MLA_V = 128
NSA_HEADS = 8
NSA_GROUPS = 2
NSA_REP = NSA_HEADS // NSA_GROUPS
NSA_DIM = 128
CMP_LEN = 32
CMP_STRIDE = 16
CMP_HIDDEN = 256
SEL_BLOCK = 64
SEL_TOP = 16
WINDOW = 512

LANES = 128
SUBLANES = 8
VMEM_LIMIT = 48 * 1024 * 1024

TOKEN_TILE = 512
S5_CHUNK = 256
S5_COLS = 512
ATT_TQ = 512
ATT_TK = 512
NSA_TQ = 256
NSA_TK = 512
SEL_CHAINS = 4
WIN_CHAINS = 4


def _params(sem):
    return pltpu.CompilerParams(dimension_semantics=sem, vmem_limit_bytes=VMEM_LIMIT)


def _dot(a, b, precision=None):
    return jnp.dot(a, b, preferred_element_type=F32, precision=precision)


def _dot_nt(a, b, precision=None):
    return lax.dot_general(a, b, (((1,), (1,)), ((), ())), preferred_element_type=F32,
                           precision=precision)


def _silu(v):
    return v * jax.nn.sigmoid(v)


def _gelu_tanh(v):
    return 0.5 * v * (1.0 + jnp.tanh(math.sqrt(2.0 / math.pi) * (v + 0.044715 * (v * v * v))))


def _rms(v, g):
    return v * lax.rsqrt(jnp.mean(v * v, axis=-1, keepdims=True) + EPS) * g


def _rope_tab_kernel(pos_ref, fm_ref, fn_ref, sgm_ref, sgn_ref, cm_ref, sm_ref, cn_ref, sn_ref):
    p = pos_ref[0]
    am = p * fm_ref[...]
    cm_ref[0] = jnp.cos(am)
    sm_ref[0] = jnp.sin(am) * sgm_ref[...]
    an = p * fn_ref[...]
    cn_ref[0] = jnp.cos(an)
    sn_ref[0] = jnp.sin(an) * sgn_ref[...]


def _rope_tables(positions):
    bsz, seq = positions.shape
    ts = TOKEN_TILE
    pos = positions.astype(F32)[..., None]
    half_m, half_n = MLA_ROPE // 2, NSA_DIM // 2
    fm = ROPE_THETA ** (-jnp.arange(half_m, dtype=F32) / half_m)
    fn = ROPE_THETA ** (-jnp.arange(half_n, dtype=F32) / half_n)
    fm = jnp.tile(fm, LANES // half_m)[None]
    fn = jnp.tile(fn, LANES // half_n)[None]
    sgm = jnp.tile(jnp.concatenate([-jnp.ones((half_m,), F32), jnp.ones((half_m,), F32)]),
                   LANES // MLA_ROPE)[None]
    sgn = jnp.concatenate([-jnp.ones((half_n,), F32), jnp.ones((half_n,), F32)])[None]
    row = pl.BlockSpec((1, LANES), lambda b, i: (0, 0))
    tab = pl.BlockSpec((1, ts, LANES), lambda b, i: (b, i, 0))
    shp = jax.ShapeDtypeStruct((bsz, seq, LANES), F32)
    return pl.pallas_call(
        _rope_tab_kernel,
        out_shape=(shp, shp, shp, shp),
        grid=(bsz, seq // ts),
        in_specs=[pl.BlockSpec((1, ts, 1), lambda b, i: (b, i, 0)), row, row, row, row],
        out_specs=(tab, tab, tab, tab),
        compiler_params=_params(("parallel", "parallel")),
        name="rope_tables",
    )(pos, fm, fn, sgm, sgn)


def _mod_kernel(c_ref, w_ref, b_ref, o_ref):
    s = _silu(c_ref[...])
    o_ref[0] = _dot(s, w_ref[0], precision=HIGHEST) + b_ref[0]


def _modulation(c, w_ada, b_ada):
    depth, d, _ = w_ada.shape
    bsz = c.shape[0]
    rows = -(-bsz // SUBLANES) * SUBLANES
    c_pad = jnp.pad(c, ((0, rows - bsz), (0, 0)))
    out = pl.pallas_call(
        _mod_kernel,
        out_shape=jax.ShapeDtypeStruct((depth, rows, 3 * d), F32),
        grid=(depth, 3),
        in_specs=[pl.BlockSpec((rows, d), lambda l, j: (0, 0)),
                  pl.BlockSpec((1, d, d), lambda l, j: (l, 0, j)),
                  pl.BlockSpec((1, 1, d), lambda l, j: (l, 0, j))],
        out_specs=pl.BlockSpec((1, rows, d), lambda l, j: (l, 0, j)),
        compiler_params=_params(("parallel", "parallel")),
        name="modulation",
    )(c_pad, w_ada, b_ada[:, None, :])
    return out[:, :bsz]


def _modulated_input(x_ref, mod_ref, g_ref):
    d = x_ref.shape[-1]
    x = x_ref[0]
    mod = mod_ref[0]
    shift, scale = mod[:, :d], mod[:, d:2 * d]
    return (_rms(x, g_ref[...]) * (1.0 + scale) + shift).astype(BF16)


EV_U, EV_ZA, EV_CQ, EV_CKV, EV_KPE, EV_ZB = 512, 512, 768, 256, 256, 512
EV_OFF = np.cumsum([0, EV_U, EV_ZA, EV_CQ, EV_CKV, EV_KPE, EV_ZB])


def _ev_in_kernel(x_ref, mod_ref, g_ref, w_ref, gq_ref, gkv_ref, wuq_ref, wukv_ref, cm_ref, sm_ref,
                  u_ref, za_ref, zb_ref, q_ref, k_ref, v_ref):
    h = _modulated_input(x_ref, mod_ref, g_ref)
    p = _dot(h, w_ref[...])
    o = EV_OFF
    u_ref[0] = p[:, o[0]:o[1]].astype(BF16)
    za_ref[0] = p[:, o[1]:o[2]].astype(BF16)
    zb_ref[0] = p[:, o[5]:o[6]].astype(BF16)
    cos, sin = cm_ref[0], sm_ref[0]

    cq = _rms(p[:, o[2]:o[3]], gq_ref[...]).astype(BF16)
    q = _dot(cq, wuq_ref[...]) * ((MLA_NOPE + MLA_ROPE) ** -0.5 * LOG2E)
    ckv = _rms(p[:, o[3]:o[4]], gkv_ref[...]).astype(BF16)
    kv = _dot(ckv, wukv_ref[...])
    kpe = (p[:, o[4]:o[4] + LANES] * cos + p[:, o[4] + LANES:o[5]] * sin).astype(BF16)
    nope = MLA_HEADS * MLA_NOPE
    for hd in range(MLA_HEADS):
        b0 = hd * (MLA_NOPE + 2 * LANES)
        qpe = q[:, b0 + MLA_NOPE:b0 + MLA_NOPE + LANES] * cos + q[:, b0 + MLA_NOPE + LANES:b0 + MLA_NOPE + 2 * LANES] * sin
        q_ref[0, hd] = jnp.concatenate([q[:, b0:b0 + MLA_NOPE], qpe], axis=-1).astype(BF16)
        k_ref[0, hd] = jnp.concatenate([kv[:, hd * MLA_NOPE:(hd + 1) * MLA_NOPE].astype(BF16), kpe], axis=-1)
        v_ref[0, hd] = kv[:, nope + hd * MLA_V:nope + (hd + 1) * MLA_V].astype(BF16)


def _ev_in(x, mod, pre_g, w_in, q_norm_g, kv_norm_g, w_uq, w_ukv, cos_m, sin_m):
    bsz, seq, d = x.shape
    tm = TOKEN_TILE
    hq = MLA_ROPE // 2
    pad = LANES - MLA_ROPE

    def rot_cols(base, zero):
        x1, x2, z = base + np.arange(hq), base + hq + np.arange(hq), np.full(pad, zero)
        return np.concatenate([x1, x2, z, x2, x1, z])

    c0 = np.cumsum([0, 512, 512, 768, 256, 64, 512])
    cols = np.concatenate([np.arange(c0[0], c0[4]), rot_cols(c0[4], c0[6]), np.arange(c0[5], c0[6])])
    w = jnp.pad(w_in, ((0, 0), (0, 1)))[:, cols].astype(BF16)
    per = MLA_NOPE + MLA_ROPE
    q_cols = np.concatenate([np.concatenate([hd * per + np.arange(MLA_NOPE),
                                             rot_cols(hd * per + MLA_NOPE, MLA_HEADS * per)])
                             for hd in range(MLA_HEADS)])
    wuq = jnp.pad(w_uq, ((0, 0), (0, 1)))[:, q_cols].astype(BF16)
    per = MLA_NOPE + MLA_V
    heads = np.arange(MLA_HEADS)[:, None] * per
    kv_cols = np.concatenate([(heads + np.arange(MLA_NOPE)).ravel(),
                              (heads + MLA_NOPE + np.arange(MLA_V)).ravel()])
    wukv = w_ukv[:, kv_cols].astype(BF16)

    tok = lambda n: pl.BlockSpec((1, tm, n), lambda b, i: (b, i, 0))
    full = lambda a: pl.BlockSpec(a.shape, lambda b, i: (0,) * a.ndim)
    out = lambda n: jax.ShapeDtypeStruct((bsz, seq, n), BF16)
    head = lambda n: pl.BlockSpec((1, MLA_HEADS, tm, n), lambda b, i: (b, 0, i, 0))
    head_out = lambda n: jax.ShapeDtypeStruct((bsz, MLA_HEADS, seq, n), BF16)
    gq, gkv, g = q_norm_g[None], kv_norm_g[None], pre_g[None]
    return pl.pallas_call(
        _ev_in_kernel,
        out_shape=(out(512), out(512), out(512), head_out(2 * LANES), head_out(2 * LANES), head_out(MLA_V)),
        grid=(bsz, seq // tm),
        in_specs=[tok(d), pl.BlockSpec((1, 1, 3 * d), lambda b, i: (b, 0, 0)), full(g), full(w),
                  full(gq), full(gkv), full(wuq), full(wukv), tok(LANES), tok(LANES)],
        out_specs=(tok(512), tok(512), tok(512), head(2 * LANES), head(2 * LANES), head(MLA_V)),
        compiler_params=_params(("parallel", "parallel")),
        name="even_in_proj",
    )(x, mod[:, None, :], g, w, gq, gkv, wuq, wukv, cos_m, sin_m)


def _s5_kernel(u_ref, z_ref, perm_ref, permt_ref, wb_ref, wc_ref, a_ref, at_ref, pow_ref,
               d_ref, wglu_ref, bglu_ref, o_ref, bu_ref, xb_ref, state_ref, carry_ref):
    t_len = u_ref.shape[1]
    n = a_ref.shape[1] // 2
    steps = t_len // SUBLANES

    @pl.when(pl.program_id(1) == 0)
    def _():
        state_ref[...] = jnp.zeros_like(state_ref)

    perm = perm_ref[...]
    u_p = _dot(perm, u_ref[0])
    z_p = _dot(perm, z_ref[0])
    u_pb = u_p.astype(BF16)
    bre, bim = pl.ds(0, S5_COLS), pl.ds(S5_COLS, S5_COLS)
    y_parts = []

    for cb in range(n // S5_COLS):
        u_cb = u_pb[:, cb * LANES:(cb + 1) * LANES]
        bu_ref[...] = _dot(u_cb, wb_ref[cb])
        re = pl.ds(cb * S5_COLS, S5_COLS)
        im = pl.ds(n + cb * S5_COLS, S5_COLS)
        ar = jnp.broadcast_to(a_ref[:, re], (SUBLANES, S5_COLS))
        ai = jnp.broadcast_to(a_ref[:, im], (SUBLANES, S5_COLS))

        def step(t, carry):
            xr, xi = carry
            rows = pl.ds(pl.multiple_of(t * SUBLANES, SUBLANES), SUBLANES)
            nr = ar * xr - ai * xi + bu_ref[rows, bre]
            ni = ar * xi + ai * xr + bu_ref[rows, bim]
            bu_ref[rows, bre] = nr
            bu_ref[rows, bim] = ni
            return nr, ni

        zero = jnp.zeros((SUBLANES, S5_COLS), F32)
        er, ei = lax.fori_loop(0, steps, step, (zero, zero), unroll=4)

        sr, si = state_ref[:, re], state_ref[:, im]
        tr, ti = at_ref[:, re], at_ref[:, im]
        for j in range(SUBLANES):
            carry_ref[j:j + 1, re] = sr
            carry_ref[j:j + 1, im] = si
            sr, si = (tr * sr - ti * si + er[j:j + 1], tr * si + ti * sr + ei[j:j + 1])
        state_ref[:, re] = sr
        state_ref[:, im] = si
        cr, ci = carry_ref[:, re], carry_ref[:, im]

        def fix(t2, _):
            xr, xi = [], []
            for k in range(2):
                t = 2 * t2 + k
                rows = pl.ds(pl.multiple_of(t * SUBLANES, SUBLANES), SUBLANES)
                pr, pi = pow_ref[pl.ds(t, 1), re], pow_ref[pl.ds(t, 1), im]
                xr.append(bu_ref[rows, bre] + pr * cr - pi * ci)
                xi.append(bu_ref[rows, bim] + pr * ci + pi * cr)
            rows = pl.ds(pl.multiple_of(t2 * 2 * SUBLANES, 2 * SUBLANES), 2 * SUBLANES)
            xb_ref[rows, bre] = jnp.concatenate(xr, axis=0).astype(BF16)
            xb_ref[rows, bim] = jnp.concatenate(xi, axis=0).astype(BF16)
            return 0

        lax.fori_loop(0, steps // 2, fix, 0, unroll=2)
        y_parts.append(_dot(xb_ref[...], wc_ref[cb]))

    y = jnp.concatenate(y_parts, axis=1) + d_ref[...] * u_p
    g = _gelu_tanh(y)
    gate = jax.nn.sigmoid(_dot(g.astype(BF16), wglu_ref[...]) + bglu_ref[...])
    out = (g * gate * _silu(z_p)).astype(BF16)
    o_ref[0] = _dot(permt_ref[...], out).astype(BF16)


def _s5(u, z_a, lam_re, lam_im, log_dt, b_re, b_im, c_re, c_im, d_skip, w_glu, b_glu):
    bsz, seq, width = u.shape
    groups, state = lam_re.shape
    t_len = S5_CHUNK
    steps = t_len // SUBLANES
    n = groups * state
    dt = jnp.exp(log_dt)[:, None]
    lam_dt_re, lam_dt_im = lam_re * dt, lam_im * dt
    decay = jnp.exp(lam_dt_re)
    ab_re, ab_im = decay * jnp.cos(lam_dt_im), decay * jnp.sin(lam_dt_im)
    den = lam_re * lam_re + lam_im * lam_im
    nr, ni = ab_re - 1.0, ab_im
    f_re = (nr * lam_re + ni * lam_im) / den
    f_im = (ni * lam_re - nr * lam_im) / den
    bb_re = f_re[..., None] * b_re - f_im[..., None] * b_im
    bb_im = f_re[..., None] * b_im + f_im[..., None] * b_re
    gpb = S5_COLS // state
    nblk = groups // gpb
    hdim = width // groups
    assert gpb * hdim == LANES and nblk * S5_COLS == n
    eye = jnp.eye(gpb, dtype=F32)

    def in_blocks(bb):
        bb = bb.reshape(nblk, gpb, state, hdim)
        return jnp.einsum('mgph,gk->mghkp', bb, eye).reshape(nblk, gpb * hdim, gpb * state)

    def out_blocks(cc):
        cc = cc.reshape(nblk, gpb, hdim, state)
        return jnp.einsum('mghp,gk->mgpkh', cc, eye).reshape(nblk, gpb * state, gpb * hdim)

    wb = jnp.concatenate([in_blocks(bb_re), in_blocks(bb_im)], axis=2).astype(BF16)
    wc = jnp.concatenate([out_blocks(c_re), out_blocks(-c_im)], axis=1).astype(BF16)
    a_vec = jnp.concatenate([ab_re.reshape(1, n), ab_im.reshape(1, n)], axis=1)
    ks = jnp.arange(1, steps + 1, dtype=F32)[:, None, None]
    pw_mag = jnp.exp(lam_dt_re[None] * ks)
    pw_re, pw_im = pw_mag * jnp.cos(lam_dt_im[None] * ks), pw_mag * jnp.sin(lam_dt_im[None] * ks)
    pow_tab = jnp.concatenate([pw_re.reshape(steps, n), pw_im.reshape(steps, n)], axis=1)
    at_vec = pow_tab[steps - 1:steps]
    r = np.arange(t_len)
    perm_np = np.zeros((t_len, t_len), np.float32)
    perm_np[r, (r % SUBLANES) * steps + r // SUBLANES] = 1.0
    perm = jnp.asarray(perm_np, BF16)
    permt = jnp.asarray(perm_np.T, BF16)
    d_vec = d_skip.reshape(1, width)
    wglu = w_glu.astype(BF16)
    bglu = b_glu[None]

    tok = pl.BlockSpec((1, t_len, width), lambda b, i: (b, i, 0))
    full = lambda a: pl.BlockSpec(a.shape, lambda b, i: (0,) * a.ndim)
    consts = (perm, permt, wb, wc, a_vec, at_vec, pow_tab, d_vec, wglu, bglu)
    return pl.pallas_call(
        _s5_kernel,
        out_shape=jax.ShapeDtypeStruct((bsz, seq, width), BF16),
        grid=(bsz, seq // t_len),
        in_specs=[tok, tok] + [full(a) for a in consts],
        out_specs=tok,
        scratch_shapes=[pltpu.VMEM((t_len, 2 * S5_COLS), F32), pltpu.VMEM((t_len, 2 * S5_COLS), BF16),
                        pltpu.VMEM((1, 2 * n), F32), pltpu.VMEM((SUBLANES, 2 * n), F32)],
        compiler_params=_params(("parallel", "arbitrary")),
        name="s5_mixer",
    )(u, z_a, *consts)


def _softmax_init(m_ref, acc_ref):
    m_ref[...] = jnp.full_like(m_ref, NEG_INF)
    acc_ref[...] = jnp.zeros_like(acc_ref)


def _softmax_tile(s, mask, v_ext, m_ref, acc_ref):
    rows, tk = s.shape
    if mask is not None:
        heads = rows // mask.shape[1]
        s = jnp.where(mask, s.reshape(heads, *mask.shape[1:]), NEG_INF).reshape(rows, tk)
    m_old = m_ref[...]
    m_new = jnp.maximum(m_old, jnp.max(s, axis=-1, keepdims=True))
    e = jnp.exp2(s - jnp.concatenate([m_new] * (tk // LANES), axis=1))
    if mask is not None:
        e = jnp.where(mask, e.reshape(heads, *mask.shape[1:]), 0.0).reshape(rows, tk)
    alpha = jnp.exp2(m_old - m_new)
    acc_ref[...] = (jnp.concatenate([alpha, alpha], axis=1) * acc_ref[...]
                    + _dot(e.astype(BF16), v_ext))
    m_ref[...] = m_new


def _softmax_finish(acc_ref):
    acc = acc_ref[...]
    return acc[:, :LANES] / jnp.maximum(acc[:, LANES:], TINY)


def _with_ones(v):
    return jnp.concatenate([v, jnp.ones(v.shape, v.dtype)], axis=1)


def _pipelined_causal_tiles(n_full, scores, update, sa_ref, sb_ref, causal):
    def pair(i, carry):
        scores(2 * i + 1, sb_ref)
        update(sa_ref, 2 * i, None)
        scores(2 * i + 2, sa_ref)
        update(sb_ref, 2 * i + 1, None)
        return carry

    scores(0, sa_ref)
    lax.fori_loop(0, n_full // 2, pair, 0)

    @pl.when(n_full % 2 == 1)
    def _():
        scores(n_full, sb_ref)
        update(sa_ref, n_full - 1, None)
        update(sb_ref, n_full, causal)

    @pl.when(n_full % 2 == 0)
    def _():
        update(sa_ref, n_full, causal)


def _mla_kernel(q_ref, k_ref, v_ref, o_ref, m_ref, acc_ref, sa_ref, sb_ref):
    qi = pl.program_id(1)
    heads, tq = q_ref.shape[1:3]
    tk = ATT_TK
    _softmax_init(m_ref, acc_ref)
    n_full = qi * tq // tk
    owns = [slice(hd * tq, (hd + 1) * tq) for hd in range(heads)]

    def scores(ki, s_ref):
        rows = pl.ds(pl.multiple_of(ki * tk, tk), tk)
        for hd, own in enumerate(owns):
            s_ref[own] = _dot_nt(q_ref[0, hd], k_ref[0, hd, rows, :])

    def update(s_ref, ki, mask):
        rows = pl.ds(pl.multiple_of(ki * tk, tk), tk)
        for hd, own in enumerate(owns):
            _softmax_tile(s_ref[own], mask, _with_ones(v_ref[0, hd, rows, :]), m_ref.at[own], acc_ref.at[own])

    qpos = qi * tq + lax.broadcasted_iota(jnp.int32, (1, tq, tk), 1)
    kpos = n_full * tk + lax.broadcasted_iota(jnp.int32, (1, tq, tk), 2)
    _pipelined_causal_tiles(n_full, scores, update, sa_ref, sb_ref, kpos <= qpos)
    for hd in range(heads):
        o_ref[0, :, hd * MLA_V:(hd + 1) * MLA_V] = _softmax_finish(
            acc_ref.at[hd * tq:(hd + 1) * tq]).astype(o_ref.dtype)


def _mla(q, k, v):
    bsz, heads, seq, dk = q.shape
    tq = ATT_TQ
    assert tq == ATT_TK
    return pl.pallas_call(
        _mla_kernel,
        out_shape=jax.ShapeDtypeStruct((bsz, seq, heads * MLA_V), BF16),
        grid=(bsz, seq // tq),
        in_specs=[pl.BlockSpec((1, heads, tq, dk), lambda b, qi: (b, 0, qi, 0)),
                  pl.BlockSpec((1, heads, seq, dk), lambda b, qi: (b, 0, 0, 0)),
                  pl.BlockSpec((1, heads, seq, MLA_V), lambda b, qi: (b, 0, 0, 0))],
        out_specs=pl.BlockSpec((1, tq, heads * MLA_V), lambda b, qi: (b, qi, 0)),
        scratch_shapes=[pltpu.VMEM((heads * tq, LANES), F32), pltpu.VMEM((heads * tq, 2 * LANES), F32),
                        pltpu.VMEM((heads * tq, ATT_TK), F32), pltpu.VMEM((heads * tq, ATT_TK), F32)],
        compiler_params=_params(("parallel", "arbitrary")),
        name="mla_attention",
    )(q, k, v)


def _ev_out_kernel(ya_ref, o_ref, zb_ref, x_ref, mod_ref, g_ref, wa_ref, wb_ref, out_ref):
    d = x_ref.shape[-1]
    yb = (o_ref[0].astype(F32) * _silu(zb_ref[0].astype(F32))).astype(BF16)
    y = _dot(ya_ref[0], wa_ref[...]) + _dot(yb, wb_ref[...])
    gate = mod_ref[0][:, 2 * d:]
    out_ref[0] = x_ref[0] + gate * _rms(y, g_ref[...])


def _ev_out(y_a, o_mla, z_b, x, mod, post_g, w_out):
    bsz, seq, d = x.shape
    tm = TOKEN_TILE
    wa = w_out[:y_a.shape[-1]].astype(BF16)
    wb = w_out[y_a.shape[-1]:].astype(BF16)
    g = post_g[None]
    tok = lambda n: pl.BlockSpec((1, tm, n), lambda b, i: (b, i, 0))
    full = lambda a: pl.BlockSpec(a.shape, lambda b, i: (0,) * a.ndim)
    return pl.pallas_call(
        _ev_out_kernel,
        out_shape=jax.ShapeDtypeStruct(x.shape, F32),
        grid=(bsz, seq // tm),
        in_specs=[tok(y_a.shape[-1]), tok(o_mla.shape[-1]), tok(z_b.shape[-1]), tok(d),
                  pl.BlockSpec((1, 1, 3 * d), lambda b, i: (b, 0, 0)), full(g), full(wa), full(wb)],
        out_specs=tok(d),
        compiler_params=_params(("parallel", "parallel")),
        name="even_out_proj",
    )(y_a, o_mla, z_b, x, mod[:, None, :], g, wa, wb)


OD_Q, OD_KV, OD_GATE, OD_Z = 1024, 256, 128, 1024
OD_OFF = np.cumsum([0, OD_Q] + [OD_KV] * 6 + [OD_GATE, OD_Z])


def _od_in_kernel(x_ref, mod_ref, g_ref, w_ref, cn_ref, sn_ref,
                  q_ref, kc_ref, vc_ref, ks_ref, vs_ref, kw_ref, vw_ref, gate_ref, z_ref):
    h = _modulated_input(x_ref, mod_ref, g_ref)
    p = _dot(h, w_ref[...])
    o = OD_OFF
    cos, sin = cn_ref[0], sn_ref[0]

    def rope(t):
        return t * cos + pltpu.roll(t, NSA_DIM // 2, axis=1) * sin

    scale = NSA_DIM ** -0.5 * LOG2E
    for hd in range(NSA_HEADS):
        t = p[:, hd * NSA_DIM:(hd + 1) * NSA_DIM]
        q_ref[0, hd // NSA_REP, hd % NSA_REP] = (rope(t) * scale).astype(BF16)
    plain = (kc_ref, vc_ref, None, vs_ref, None, vw_ref)
    roped = (None, None, ks_ref, None, kw_ref, None)
    for j in range(6):
        for g in range(NSA_GROUPS):
            lo = o[1 + j] + g * NSA_DIM
            t = p[:, lo:lo + NSA_DIM]
            if plain[j] is not None:
                plain[j][0, g] = t.astype(BF16)
            else:
                roped[j][0, g] = rope(t).astype(BF16)
    gate_ref[0] = jax.nn.sigmoid(p[:, o[7]:o[8]])
    z_ref[0] = p[:, o[8]:o[9]].astype(BF16)


def _od_in(x, mod, pre_g, w_in, cos_n, sin_n):
    bsz, seq, d = x.shape
    tm = TOKEN_TILE
    n_gate = 3 * NSA_HEADS
    c0 = OD_Q + 6 * OD_KV
    w = jnp.concatenate([w_in[:, :c0 + n_gate], jnp.zeros((d, OD_GATE - n_gate), F32),
                         w_in[:, c0 + n_gate:]], axis=1).astype(BF16)
    g = pre_g[None]
    tok = lambda n: pl.BlockSpec((1, tm, n), lambda b, i: (b, i, 0))
    full = lambda a: pl.BlockSpec(a.shape, lambda b, i: (0,) * a.ndim)
    q_spec = pl.BlockSpec((1, NSA_GROUPS, NSA_REP, tm, NSA_DIM), lambda b, i: (b, 0, 0, i, 0))
    kv_spec = pl.BlockSpec((1, NSA_GROUPS, tm, NSA_DIM), lambda b, i: (b, 0, i, 0))
    kv_shape = jax.ShapeDtypeStruct((bsz, NSA_GROUPS, seq, NSA_DIM), BF16)
    return pl.pallas_call(
        _od_in_kernel,
        out_shape=(jax.ShapeDtypeStruct((bsz, NSA_GROUPS, NSA_REP, seq, NSA_DIM), BF16),)
        + (kv_shape,) * 6
        + (jax.ShapeDtypeStruct((bsz, seq, OD_GATE), F32), jax.ShapeDtypeStruct((bsz, seq, OD_Z), BF16)),
        grid=(bsz, seq // tm),
        in_specs=[tok(d), pl.BlockSpec((1, 1, 3 * d), lambda b, i: (b, 0, 0)), full(g), full(w),
                  tok(LANES), tok(LANES)],
        out_specs=(q_spec,) + (kv_spec,) * 6 + (tok(OD_GATE), tok(OD_Z)),
        compiler_params=_params(("parallel", "parallel")),
        name="odd_in_proj",
    )(x, mod[:, None, :], g, w, cos_n, sin_n)


def _compress_kernel(x_ref, shift_ref, pe_ref, w1_ref, w2_ref, cos_ref, sin_ref, o_ref, *, use_rope):
    x = x_ref[0, 0]
    half = x.shape[1]
    x_next = _dot(shift_ref[...], x).astype(BF16)
    pre = (_dot(x, w1_ref[:half]) + _dot(x_next, w1_ref[half:])
           + _dot(pe_ref[...], w1_ref[...])[0:1])
    out = _dot(_gelu_tanh(pre).astype(BF16), w2_ref[...])
    if use_rope:
        out = out * cos_ref[0] + pltpu.roll(out, NSA_DIM // 2, axis=1) * sin_ref[0]
    o_ref[0, 0] = out.astype(BF16)


def _compress(kv, pe, w1, w2, cos_end, sin_end, use_rope):
    bsz, groups, seq, d = kv.shape
    nb = seq // CMP_STRIDE
    x = kv.reshape(bsz, groups, nb, CMP_STRIDE * d)
    shift = jnp.asarray(np.eye(nb, k=1, dtype=np.float32), BF16)
    pe_rows = jnp.broadcast_to(pe.reshape(1, CMP_LEN * d), (SUBLANES, CMP_LEN * d)).astype(BF16)
    w1b, w2b = w1.astype(BF16), w2.astype(BF16)
    full = lambda a: pl.BlockSpec(a.shape, lambda b, g: (0,) * a.ndim)
    end = pl.BlockSpec((1, nb, d), lambda b, g: (b, 0, 0))
    return pl.pallas_call(
        functools.partial(_compress_kernel, use_rope=use_rope),
        out_shape=jax.ShapeDtypeStruct((bsz, groups, nb, d), BF16),
        grid=(bsz, groups),
        in_specs=[pl.BlockSpec((1, 1, nb, CMP_STRIDE * d), lambda b, g: (b, g, 0, 0)),
                  full(shift), full(pe_rows), full(w1b), full(w2b), end, end],
        out_specs=pl.BlockSpec((1, 1, nb, d), lambda b, g: (b, g, 0, 0)),
        compiler_params=_params(("parallel", "parallel")),
        name="nsa_compress",
    )(x, shift, pe_rows, w1b, w2b, cos_end, sin_end)


def _cmp_sel_kernel(q_ref, kc_ref, vc_ref, pool_ref, o_ref, sel_ref, *, n_cmp, n_sel, n_top):
    qi = pl.program_id(2)
    rep, tq, d = q_ref.shape[2:]
    nb = kc_ref.shape[2]
    q = q_ref[0, 0].reshape(rep * tq, d)
    s = _dot_nt(q, kc_ref[0, 0]).reshape(rep, tq, nb)
    qpos = qi * tq + lax.broadcasted_iota(jnp.int32, (tq, nb), 0)
    blk = lax.broadcasted_iota(jnp.int32, (tq, nb), 1)
    mask = ((blk * CMP_STRIDE + (CMP_LEN - 1) <= qpos) & (blk < n_cmp))[None]
    s = jnp.where(mask, s, NEG_INF)
    m = jnp.max(s, axis=-1, keepdims=True)
    e = jnp.where(mask, jnp.exp2(s - m), 0.0)
    p = e / jnp.maximum(jnp.sum(e, axis=-1, keepdims=True), TINY)
    o = _dot(p.reshape(rep * tq, nb).astype(BF16), vc_ref[0, 0])
    o_ref[0, 0] = o.reshape(rep, tq, d).astype(o_ref.dtype)

    rows = pool_ref.shape[0]
    imp = _dot_nt(pool_ref[...], jnp.sum(p, axis=0), precision=HIGHEST)[:n_sel]
    bid = lax.broadcasted_iota(jnp.int32, (n_sel, tq), 0)
    cur = (qi * tq + lax.broadcasted_iota(jnp.int32, (n_sel, tq), 1)) // SEL_BLOCK
    forced = (bid == 0) | (bid == cur) | (bid == cur - 1)
    imp = jnp.where(forced, FORCE_SCORE, jnp.where(bid <= cur, imp, -1.0))
    groups = [imp[g:g + SUBLANES] for g in range(0, n_sel, SUBLANES)]
    sub = lax.broadcasted_iota(jnp.int32, (SUBLANES, tq), 0)
    ranks = [jnp.zeros((SUBLANES, tq), F32) for _ in groups]
    for j in range(n_sel):
        vj = jnp.broadcast_to(imp[j:j + 1, :], (SUBLANES, tq))
        for gi, grp in enumerate(groups):
            lo = gi * SUBLANES
            if lo > j:
                first = vj >= grp
            elif lo + SUBLANES - 1 <= j:
                first = vj > grp
            else:
                first = (vj > grp) | ((vj == grp) & (sub > j - lo))
            ranks[gi] = ranks[gi] + jnp.where(first, 1.0, 0.0)
    rank = jnp.concatenate(ranks, axis=0)
    bias = jnp.where((rank < n_top) & (bid <= cur), 0.0, NEG_INF)
    bias = jnp.concatenate([bias, jnp.full((rows - n_sel, tq), NEG_INF, F32)], axis=0)
    sel_ref[0, 0] = bias.T.astype(sel_ref.dtype)


def _cmp_sel(q, kc, vc):
    bsz, groups, rep, seq, d = q.shape
    nb = kc.shape[2]
    tq = NSA_TQ
    n_cmp = (seq - CMP_LEN) // CMP_STRIDE + 1
    n_sel = seq // SEL_BLOCK
    n_top = min(SEL_TOP, n_sel)
    ratio = SEL_BLOCK // CMP_STRIDE
    assert n_sel <= LANES and n_sel * ratio == nb
    pool_np = np.zeros((LANES, nb), np.float32)
    pool_np[np.arange(nb) // ratio, np.arange(nb)] = 1.0
    pool = jnp.asarray(pool_np)
    kv_spec = pl.BlockSpec((1, 1, nb, d), lambda b, g, i: (b, g, 0, 0))
    return pl.pallas_call(
        functools.partial(_cmp_sel_kernel, n_cmp=n_cmp, n_sel=n_sel, n_top=n_top),
        out_shape=(jax.ShapeDtypeStruct(q.shape, BF16),
                   jax.ShapeDtypeStruct((bsz, groups, seq, LANES), BF16)),
        grid=(bsz, groups, seq // tq),
        in_specs=[pl.BlockSpec((1, 1, rep, tq, d), lambda b, g, i: (b, g, 0, i, 0)), kv_spec, kv_spec,
                  pl.BlockSpec(pool.shape, lambda b, g, i: (0, 0))],
        out_specs=(pl.BlockSpec((1, 1, rep, tq, d), lambda b, g, i: (b, g, 0, i, 0)),
                   pl.BlockSpec((1, 1, tq, LANES), lambda b, g, i: (b, g, i, 0))),
        compiler_params=_params(("parallel", "parallel", "parallel")),
        name="nsa_cmp_select",
    )(q, kc, vc, pool)


def _sel_kernel(q_ref, bias_ref, k_ref, blk_ref, v_ref, o_ref, m_ref, acc_ref, sa_ref, sb_ref):
    qi = pl.program_id(2)
    rep, tq, d = q_ref.shape[2:]
    tk = NSA_TK
    part = rep // SEL_CHAINS
    q_ext = [jnp.concatenate([q_ref[0, 0, c * part:(c + 1) * part].reshape(part * tq, d),
                              jnp.concatenate([bias_ref[0, 0]] * part, axis=0)], axis=1)
             for c in range(SEL_CHAINS)]
    _softmax_init(m_ref, acc_ref)
    n_full = qi * tq // tk

    owns = [slice(c * part * tq, (c + 1) * part * tq) for c in range(SEL_CHAINS)]

    def scores(ki, s_ref):
        rows = pl.ds(pl.multiple_of(ki * tk, tk), tk)
        k_ext = jnp.concatenate([k_ref[0, 0, rows, :], blk_ref[rows, :]], axis=1)
        for c, own in enumerate(owns):
            s_ref[own] = _dot_nt(q_ext[c], k_ext)

    def update(s_ref, ki, mask):
        rows = pl.ds(pl.multiple_of(ki * tk, tk), tk)
        v_ext = _with_ones(v_ref[0, 0, rows, :])
        for own in owns:
            _softmax_tile(s_ref[own], mask, v_ext, m_ref.at[own], acc_ref.at[own])

    qpos = qi * tq + lax.broadcasted_iota(jnp.int32, (1, tq, tk), 1)
    kpos = n_full * tk + lax.broadcasted_iota(jnp.int32, (1, tq, tk), 2)
    _pipelined_causal_tiles(n_full, scores, update, sa_ref, sb_ref, kpos <= qpos)
    o_ref[0, 0] = _softmax_finish(acc_ref).reshape(rep, tq, d).astype(o_ref.dtype)


def _sel_attention(q, bias, k, v):
    bsz, groups, rep, seq, d = q.shape
    tq = NSA_TQ
    assert NSA_TK % tq == 0
    onehot_np = np.zeros((seq, LANES), np.float32)
    onehot_np[np.arange(seq), np.arange(seq) // SEL_BLOCK] = 1.0
    onehot = jnp.asarray(onehot_np, BF16)
    q_spec = pl.BlockSpec((1, 1, rep, tq, d), lambda b, g, qi: (b, g, 0, qi, 0))
    kv_spec = pl.BlockSpec((1, 1, seq, d), lambda b, g, qi: (b, g, 0, 0))
    return pl.pallas_call(
        _sel_kernel,
        out_shape=jax.ShapeDtypeStruct(q.shape, BF16),
        grid=(bsz, groups, seq // tq),
        in_specs=[q_spec, pl.BlockSpec((1, 1, tq, LANES), lambda b, g, qi: (b, g, qi, 0)),
                  kv_spec, pl.BlockSpec((seq, LANES), lambda b, g, qi: (0, 0)), kv_spec],
        out_specs=q_spec,
        scratch_shapes=[pltpu.VMEM((rep * tq, LANES), F32), pltpu.VMEM((rep * tq, 2 * LANES), F32),
                        pltpu.VMEM((rep * tq, NSA_TK), F32), pltpu.VMEM((rep * tq, NSA_TK), F32)],
        compiler_params=_params(("parallel", "parallel", "arbitrary")),
        name="nsa_selected_attention",
    )(q, bias, k, onehot, v)


def _win_kernel(q_ref, k_ref, v_ref, o_ref, *, span):
    qi = pl.program_id(2)
    rep, tq, d = q_ref.shape[2:]
    start = pl.multiple_of(jnp.maximum(qi * tq + tq - span, 0), tq)
    rows = pl.ds(start, span)
    qpos = qi * tq + lax.broadcasted_iota(jnp.int32, (1, tq, span), 1)
    kpos = start + lax.broadcasted_iota(jnp.int32, (1, tq, span), 2)
    diff = qpos - kpos
    mask = (diff >= 0) & (diff < WINDOW)
    k, v_ext = k_ref[0, 0, rows, :], _with_ones(v_ref[0, 0, rows, :])
    part = rep // WIN_CHAINS
    for h0 in range(0, rep, part):
        q = q_ref[0, 0, h0:h0 + part].reshape(part * tq, d)
        s = jnp.where(mask, _dot_nt(q, k).reshape(part, tq, span), NEG_INF)
        m = jnp.max(s, axis=-1, keepdims=True)
        e = jnp.where(mask, jnp.exp2(s - m), 0.0).reshape(part * tq, span)
        acc = _dot(e.astype(BF16), v_ext)
        o = acc[:, :LANES] / jnp.maximum(acc[:, LANES:], TINY)
        o_ref[0, 0, h0:h0 + part] = o.reshape(part, tq, d).astype(o_ref.dtype)


def _win_attention(q, k, v):
    bsz, groups, rep, seq, d = q.shape
    tq = NSA_TQ
    span = (-(-(WINDOW - 1) // tq) + 1) * tq
    assert span <= seq
    q_spec = pl.BlockSpec((1, 1, rep, tq, d), lambda b, g, qi: (b, g, 0, qi, 0))
    kv_spec = pl.BlockSpec((1, 1, seq, d), lambda b, g, qi: (b, g, 0, 0))
    return pl.pallas_call(
        functools.partial(_win_kernel, span=span),
        out_shape=jax.ShapeDtypeStruct(q.shape, BF16),
        grid=(bsz, groups, seq // tq),
        in_specs=[q_spec, kv_spec, kv_spec],
        out_specs=q_spec,
        compiler_params=_params(("parallel", "parallel", "arbitrary")),
        name="nsa_window_attention",
    )(q, k, v)


def _od_out_kernel(oc_ref, os_ref, ow_ref, gate_ref, z_ref, x_ref, mod_ref, g_ref, w_ref, out_ref):
    d = x_ref.shape[-1]
    gates = gate_ref[0]
    z = z_ref[0].astype(F32)
    parts = []
    for hd in range(NSA_HEADS):
        g, r = hd // NSA_REP, hd % NSA_REP
        o = (gates[:, 3 * hd:3 * hd + 1] * oc_ref[0, g, r].astype(F32)
             + gates[:, 3 * hd + 1:3 * hd + 2] * os_ref[0, g, r].astype(F32)
             + gates[:, 3 * hd + 2:3 * hd + 3] * ow_ref[0, g, r].astype(F32))
        parts.append((o * _silu(z[:, hd * NSA_DIM:(hd + 1) * NSA_DIM])).astype(BF16))
    y = _dot(jnp.concatenate(parts, axis=-1), w_ref[...])
    gate = mod_ref[0][:, 2 * d:]
    out_ref[0] = x_ref[0] + gate * _rms(y, g_ref[...])


def _od_out(o_cmp, o_sel, o_win, gates, z, x, mod, post_g, w_out):
    bsz, seq, d = x.shape
    tm = TOKEN_TILE
    w = w_out.astype(BF16)
    g = post_g[None]
    tok = lambda n: pl.BlockSpec((1, tm, n), lambda b, i: (b, i, 0))
    full = lambda a: pl.BlockSpec(a.shape, lambda b, i: (0,) * a.ndim)
    o_spec = pl.BlockSpec((1, NSA_GROUPS, NSA_REP, tm, NSA_DIM), lambda b, i: (b, 0, 0, i, 0))
    return pl.pallas_call(
        _od_out_kernel,
        out_shape=jax.ShapeDtypeStruct(x.shape, F32),
        grid=(bsz, seq // tm),
        in_specs=[o_spec, o_spec, o_spec, tok(OD_GATE), tok(OD_Z), tok(d),
                  pl.BlockSpec((1, 1, 3 * d), lambda b, i: (b, 0, 0)), full(g), full(w)],
        out_specs=tok(d),
        compiler_params=_params(("parallel", "parallel")),
        name="odd_out_proj",
    )(o_cmp, o_sel, o_win, gates, z, x, mod[:, None, :], g, w)


def _even_layer(x, mod, pre_g, post_g, tabs, w_in, lam_re, lam_im, log_dt, b_re, b_im, c_re, c_im,
                d_skip, w_glu, b_glu, q_norm_g, kv_norm_g, w_uq, w_ukv, w_out):
    cos_m, sin_m = tabs[0], tabs[1]
    u, z_a, z_b, q, k, v = _ev_in(x, mod, pre_g, w_in, q_norm_g, kv_norm_g, w_uq, w_ukv, cos_m, sin_m)
    y_a = _s5(u, z_a, lam_re, lam_im, log_dt, b_re, b_im, c_re, c_im, d_skip, w_glu, b_glu)
    o_mla = _mla(q, k, v)
    return _ev_out(y_a, o_mla, z_b, x, mod, post_g, w_out)


def _odd_layer(x, mod, pre_g, post_g, tabs, w_in, k_pe, k_w1, k_w2, v_pe, v_w1, v_w2, w_out):
    bsz, seq, _ = x.shape
    cos_n, sin_n = tabs[2], tabs[3]
    q, k_c, v_c, k_s, v_s, k_w, v_w, gates, z = _od_in(x, mod, pre_g, w_in, cos_n, sin_n)
    nb = seq // CMP_STRIDE

    def at_block_end(t):
        t = t.reshape(bsz, nb, CMP_STRIDE, LANES)[:, 1:, CMP_STRIDE - 1]
        return jnp.pad(t, ((0, 0), (0, 1), (0, 0)))

    kc = _compress(k_c, k_pe, k_w1, k_w2, at_block_end(cos_n), at_block_end(sin_n), True)
    vc = _compress(v_c, v_pe, v_w1, v_w2, at_block_end(cos_n), at_block_end(sin_n), False)
    o_cmp, sel = _cmp_sel(q, kc, vc)
    o_sel = _sel_attention(q, sel, k_s, v_s)
    o_win = _win_attention(q, k_w, v_w)
    return _od_out(o_cmp, o_sel, o_win, gates, z, x, mod, post_g, w_out)


def kernel(x, c, positions, pre_norm_g, post_norm_g, w_ada, b_ada, ev_w_in, ev_lam_re, ev_lam_im, ev_log_dt, ev_b_re, ev_b_im, ev_c_re, ev_c_im, ev_d_skip, ev_w_glu, ev_b_glu, ev_q_norm_g, ev_kv_norm_g, ev_w_uq, ev_w_ukv, ev_w_out, od_w_in, od_cmp_k_pe, od_cmp_k_w1, od_cmp_k_w2, od_cmp_v_pe, od_cmp_v_w1, od_cmp_v_w2, od_w_out):
    depth = pre_norm_g.shape[0]
    tabs = _rope_tables(positions)
    mods = _modulation(c, w_ada, b_ada)
    for layer in range(depth):
        i = layer // 2
        if layer % 2 == 0:
            x = _even_layer(x, mods[layer], pre_norm_g[layer], post_norm_g[layer], tabs,
                            ev_w_in[i], ev_lam_re[i], ev_lam_im[i], ev_log_dt[i], ev_b_re[i], ev_b_im[i],
                            ev_c_re[i], ev_c_im[i], ev_d_skip[i], ev_w_glu[i], ev_b_glu[i],
                            ev_q_norm_g[i], ev_kv_norm_g[i], ev_w_uq[i], ev_w_ukv[i], ev_w_out[i])
        else:
            x = _odd_layer(x, mods[layer], pre_norm_g[layer], post_norm_g[layer], tabs,
                           od_w_in[i], od_cmp_k_pe[i], od_cmp_k_w1[i], od_cmp_k_w2[i],
                           od_cmp_v_pe[i], od_cmp_v_w1[i], od_cmp_v_w2[i], od_w_out[i])
    return x
```

```python
import functools
import math

import numpy as np
import jax
import jax.numpy as jnp
from jax import lax
from jax.experimental import pallas as pl
from jax.experimental.pallas import tpu as pltpu

F32 = jnp.float32
BF16 = jnp.bfloat16
HIGHEST = lax.Precision.HIGHEST

EPS = 1e-6
ROPE_THETA = 10000.0
NEG_INF = -1e30
TINY = 1e-30
FORCE_SCORE = 1e9
LOG2E = 1.0 / math.log(2.0)

S5_GROUP = 16
S5_STATE = 64
MLA_HEADS = 4
MLA_NOPE = 128
MLA_ROPE = 64
MLA_V = 128
NSA_HEADS = 8
NSA_GROUPS = 2
NSA_REP = NSA_HEADS // NSA_GROUPS
NSA_DIM = 128
CMP_LEN = 32
CMP_STRIDE = 16
CMP_HIDDEN = 256
SEL_BLOCK = 64
SEL_TOP = 16
WINDOW = 512

LANES = 128
SUBLANES = 8
VMEM_LIMIT = 48 * 1024 * 1024

TOKEN_TILE = 512
S5_CHUNK = 256
S5_COLS = 512
ATT_TQ = 512
ATT_TK = 512
NSA_TQ = 256
NSA_TK = 512
SEL_CHAINS = 4
WIN_CHAINS = 4


def _params(sem):
    return pltpu.CompilerParams(dimension_semantics=sem, vmem_limit_bytes=VMEM_LIMIT)


def _dot(a, b, precision=None):
    return jnp.dot(a, b, preferred_element_type=F32, precision=precision)


def _dot_nt(a, b, precision=None):
    return lax.dot_general(a, b, (((1,), (1,)), ((), ())), preferred_element_type=F32,
                           precision=precision)


def _silu(v):
    return v * jax.nn.sigmoid(v)


def _gelu_tanh(v):
    return 0.5 * v * (1.0 + jnp.tanh(math.sqrt(2.0 / math.pi) * (v + 0.044715 * (v * v * v))))


def _rms(v, g):
    return v * lax.rsqrt(jnp.mean(v * v, axis=-1, keepdims=True) + EPS) * g


def _rope_tab_kernel(pos_ref, fm_ref, fn_ref, sgm_ref, sgn_ref, cm_ref, sm_ref, cn_ref, sn_ref):
    p = pos_ref[0]
    am = p * fm_ref[...]
    cm_ref[0] = jnp.cos(am)
    sm_ref[0] = jnp.sin(am) * sgm_ref[...]
    an = p * fn_ref[...]
    cn_ref[0] = jnp.cos(an)
    sn_ref[0] = jnp.sin(an) * sgn_ref[...]


def _rope_tables(positions):
    bsz, seq = positions.shape
    ts = min(TOKEN_TILE, seq)
    pos = positions.astype(F32)[..., None]
    half_m, half_n = MLA_ROPE // 2, NSA_DIM // 2
    fm = ROPE_THETA ** (-jnp.arange(half_m, dtype=F32) / half_m)
    fn = ROPE_THETA ** (-jnp.arange(half_n, dtype=F32) / half_n)
    fm = jnp.tile(fm, LANES // half_m)[None]
    fn = jnp.tile(fn, LANES // half_n)[None]
    sgm = jnp.tile(jnp.concatenate([-jnp.ones((half_m,), F32), jnp.ones((half_m,), F32)]),
                   LANES // MLA_ROPE)[None]
    sgn = jnp.concatenate([-jnp.ones((half_n,), F32), jnp.ones((half_n,), F32)])[None]
    row = pl.BlockSpec((1, LANES), lambda b, i: (0, 0))
    tab = pl.BlockSpec((1, ts, LANES), lambda b, i: (b, i, 0))
    shp = jax.ShapeDtypeStruct((bsz, seq, LANES), F32)
    return pl.pallas_call(
        _rope_tab_kernel,
        out_shape=(shp, shp, shp, shp),
        grid=(bsz, seq // ts),
        in_specs=[pl.BlockSpec((1, ts, 1), lambda b, i: (b, i, 0)), row, row, row, row],
        out_specs=(tab, tab, tab, tab),
        compiler_params=_params(("parallel", "parallel")),
        name="rope_tables",
    )(pos, fm, fn, sgm, sgn)


def _mod_kernel(c_ref, w_ref, b_ref, o_ref):
    s = _silu(c_ref[...])
    o_ref[0] = _dot(s, w_ref[0], precision=HIGHEST) + b_ref[0]


def _modulation(c, w_ada, b_ada):
    depth, d, _ = w_ada.shape
    bsz = c.shape[0]
    rows = -(-bsz // SUBLANES) * SUBLANES
    c_pad = jnp.pad(c, ((0, rows - bsz), (0, 0)))
    out = pl.pallas_call(
        _mod_kernel,
        out_shape=jax.ShapeDtypeStruct((depth, rows, 3 * d), F32),
        grid=(depth, 3),
        in_specs=[pl.BlockSpec((rows, d), lambda l, j: (0, 0)),
                  pl.BlockSpec((1, d, d), lambda l, j: (l, 0, j)),
                  pl.BlockSpec((1, 1, d), lambda l, j: (l, 0, j))],
        out_specs=pl.BlockSpec((1, rows, d), lambda l, j: (l, 0, j)),
        compiler_params=_params(("parallel", "parallel")),
        name="modulation",
    )(c_pad, w_ada, b_ada[:, None, :])
    return out[:, :bsz]


def _modulated_input(x_ref, mod_ref, g_ref):
    d = x_ref.shape[-1]
    x = x_ref[0]
    mod = mod_ref[0]
    shift, scale = mod[:, :d], mod[:, d:2 * d]
    return (_rms(x, g_ref[...]) * (1.0 + scale) + shift).astype(BF16)


EV_U, EV_ZA, EV_CQ, EV_CKV, EV_KPE, EV_ZB = 512, 512, 768, 256, 256, 512
EV_OFF = np.cumsum([0, EV_U, EV_ZA, EV_CQ, EV_CKV, EV_KPE, EV_ZB])


def _ev_in_kernel(x_ref, mod_ref, g_ref, w_ref, gq_ref, gkv_ref, wuq_ref, wukv_ref, cm_ref, sm_ref,
                  u_ref, za_ref, zb_ref, q_ref, k_ref, v_ref):
    h = _modulated_input(x_ref, mod_ref, g_ref)
    p = _dot(h, w_ref[...])
    o = EV_OFF
    u_ref[0] = p[:, o[0]:o[1]].astype(BF16)
    za_ref[0] = p[:, o[1]:o[2]].astype(BF16)
    zb_ref[0] = p[:, o[5]:o[6]].astype(BF16)
    cos, sin = cm_ref[0], sm_ref[0]

    cq = _rms(p[:, o[2]:o[3]], gq_ref[...]).astype(BF16)
    q = _dot(cq, wuq_ref[...]) * ((MLA_NOPE + MLA_ROPE) ** -0.5 * LOG2E)
    ckv = _rms(p[:, o[3]:o[4]], gkv_ref[...]).astype(BF16)
    kv = _dot(ckv, wukv_ref[...])
    kpe = (p[:, o[4]:o[4] + LANES] * cos + p[:, o[4] + LANES:o[5]] * sin).astype(BF16)
    nope = MLA_HEADS * MLA_NOPE
    for hd in range(MLA_HEADS):
        b0 = hd * (MLA_NOPE + 2 * LANES)
        qpe = q[:, b0 + MLA_NOPE:b0 + MLA_NOPE + LANES] * cos + q[:, b0 + MLA_NOPE + LANES:b0 + MLA_NOPE + 2 * LANES] * sin
        q_ref[0, hd] = jnp.concatenate([q[:, b0:b0 + MLA_NOPE], qpe], axis=-1).astype(BF16)
        k_ref[0, hd] = jnp.concatenate([kv[:, hd * MLA_NOPE:(hd + 1) * MLA_NOPE].astype(BF16), kpe], axis=-1)
        v_ref[0, hd] = kv[:, nope + hd * MLA_V:nope + (hd + 1) * MLA_V].astype(BF16)


def _select_columns(w, cols):
    n = w.shape[1]
    cols = [int(c) for c in cols]
    pieces, i = [], 0
    while i < len(cols):
        j = i + 1
        if cols[i] == n:
            while j < len(cols) and cols[j] == n:
                j += 1
            pieces.append(jnp.zeros((w.shape[0], j - i), w.dtype))
        else:
            while j < len(cols) and cols[j] == cols[j - 1] + 1 and cols[j] != n:
                j += 1
            pieces.append(w[:, cols[i]:cols[j - 1] + 1])
        i = j
    return jnp.concatenate(pieces, axis=1)


def _ev_in(x, mod, pre_g, w_in, q_norm_g, kv_norm_g, w_uq, w_ukv, cos_m, sin_m):
    bsz, seq, d = x.shape
    tm = TOKEN_TILE
    hq = MLA_ROPE // 2
    pad = LANES - MLA_ROPE

    def rot_cols(base, zero):
        x1, x2, z = base + np.arange(hq), base + hq + np.arange(hq), np.full(pad, zero)
        return np.concatenate([x1, x2, z, x2, x1, z])

    c0 = np.cumsum([0, 512, 512, 768, 256, 64, 512])
    cols = np.concatenate([np.arange(c0[0], c0[4]), rot_cols(c0[4], c0[6]), np.arange(c0[5], c0[6])])
    w = _select_columns(w_in, cols).astype(BF16)
    per = MLA_NOPE + MLA_ROPE
    q_cols = np.concatenate([np.concatenate([hd * per + np.arange(MLA_NOPE),
                                             rot_cols(hd * per + MLA_NOPE, MLA_HEADS * per)])
                             for hd in range(MLA_HEADS)])
    wuq = _select_columns(w_uq, q_cols).astype(BF16)
    per = MLA_NOPE + MLA_V
    heads = np.arange(MLA_HEADS)[:, None] * per
    kv_cols = np.concatenate([(heads + np.arange(MLA_NOPE)).ravel(),
                              (heads + MLA_NOPE + np.arange(MLA_V)).ravel()])
    wukv = _select_columns(w_ukv, kv_cols).astype(BF16)

    tok = lambda n: pl.BlockSpec((1, tm, n), lambda b, i: (b, i, 0))
    full = lambda a: pl.BlockSpec(a.shape, lambda b, i: (0,) * a.ndim)
    out = lambda n: jax.ShapeDtypeStruct((bsz, seq, n), BF16)
    head = lambda n: pl.BlockSpec((1, MLA_HEADS, tm, n), lambda b, i: (b, 0, i, 0))
    head_out = lambda n: jax.ShapeDtypeStruct((bsz, MLA_HEADS, seq, n), BF16)
    gq, gkv, g = q_norm_g[None], kv_norm_g[None], pre_g[None]
    return pl.pallas_call(
        _ev_in_kernel,
        out_shape=(out(512), out(512), out(512), head_out(2 * LANES), head_out(2 * LANES), head_out(MLA_V)),
        grid=(bsz, seq // tm),
        in_specs=[tok(d), pl.BlockSpec((1, 1, 3 * d), lambda b, i: (b, 0, 0)), full(g), full(w),
                  full(gq), full(gkv), full(wuq), full(wukv), tok(LANES), tok(LANES)],
        out_specs=(tok(512), tok(512), tok(512), head(2 * LANES), head(2 * LANES), head(MLA_V)),
        compiler_params=_params(("parallel", "parallel")),
        name="even_in_proj",
    )(x, mod[:, None, :], g, w, gq, gkv, wuq, wukv, cos_m, sin_m)


def _s5_kernel(u_ref, z_ref, perm_ref, permt_ref, wb_ref, wc_ref, a_ref, at_ref, pow_ref,
               d_ref, wglu_ref, bglu_ref, o_ref, bu_ref, xb_ref, state_ref, carry_ref):
    t_len = u_ref.shape[1]
    n = a_ref.shape[1] // 2
    steps = t_len // SUBLANES

    @pl.when(pl.program_id(1) == 0)
    def _():
        state_ref[...] = jnp.zeros_like(state_ref)

    perm = perm_ref[...]
    u_p = _dot(perm, u_ref[0])
    z_p = _dot(perm, z_ref[0])
    u_pb = u_p.astype(BF16)
    bre, bim = pl.ds(0, S5_COLS), pl.ds(S5_COLS, S5_COLS)
    y_parts = []

    for cb in range(n // S5_COLS):
        u_cb = u_pb[:, cb * LANES:(cb + 1) * LANES]
        bu_ref[...] = _dot(u_cb, wb_ref[cb])
        re = pl.ds(cb * S5_COLS, S5_COLS)
        im = pl.ds(n + cb * S5_COLS, S5_COLS)
        ar = jnp.broadcast_to(a_ref[:, re], (SUBLANES, S5_COLS))
        ai = jnp.broadcast_to(a_ref[:, im], (SUBLANES, S5_COLS))

        def step(t, carry):
            xr, xi = carry
            rows = pl.ds(pl.multiple_of(t * SUBLANES, SUBLANES), SUBLANES)
            nr = ar * xr - ai * xi + bu_ref[rows, bre]
            ni = ar * xi + ai * xr + bu_ref[rows, bim]
            bu_ref[rows, bre] = nr
            bu_ref[rows, bim] = ni
            return nr, ni

        zero = jnp.zeros((SUBLANES, S5_COLS), F32)
        er, ei = lax.fori_loop(0, steps, step, (zero, zero), unroll=True)

        sr, si = state_ref[:, re], state_ref[:, im]
        tr, ti = at_ref[:, re], at_ref[:, im]
        for j in range(SUBLANES):
            carry_ref[j:j + 1, re] = sr
            carry_ref[j:j + 1, im] = si
            sr, si = (tr * sr - ti * si + er[j:j + 1], tr * si + ti * sr + ei[j:j + 1])
        state_ref[:, re] = sr
        state_ref[:, im] = si
        cr, ci = carry_ref[:, re], carry_ref[:, im]

        def fix(t2, _):
            xr, xi = [], []
            for k in range(2):
                t = 2 * t2 + k
                rows = pl.ds(pl.multiple_of(t * SUBLANES, SUBLANES), SUBLANES)
                pr, pi = pow_ref[pl.ds(t, 1), re], pow_ref[pl.ds(t, 1), im]
                xr.append(bu_ref[rows, bre] + pr * cr - pi * ci)
                xi.append(bu_ref[rows, bim] + pr * ci + pi * cr)
            rows = pl.ds(pl.multiple_of(t2 * 2 * SUBLANES, 2 * SUBLANES), 2 * SUBLANES)
            xb_ref[rows, bre] = jnp.concatenate(xr, axis=0).astype(BF16)
            xb_ref[rows, bim] = jnp.concatenate(xi, axis=0).astype(BF16)
            return 0

        lax.fori_loop(0, steps // 2, fix, 0, unroll=True)
        y_parts.append(_dot(xb_ref[...], wc_ref[cb]))

    y = jnp.concatenate(y_parts, axis=1) + d_ref[...] * u_p
    g = _gelu_tanh(y)
    gate = jax.nn.sigmoid(_dot(g.astype(BF16), wglu_ref[...]) + bglu_ref[...])
    out = (g * gate * _silu(z_p)).astype(BF16)
    o_ref[0] = _dot(permt_ref[...], out).astype(BF16)


def _s5(u, z_a, lam_re, lam_im, log_dt, b_re, b_im, c_re, c_im, d_skip, w_glu, b_glu):
    bsz, seq, width = u.shape
    groups, state = lam_re.shape
    t_len = S5_CHUNK
    steps = t_len // SUBLANES
    n = groups * state
    dt = jnp.exp(log_dt)[:, None]
    lam_dt_re, lam_dt_im = lam_re * dt, lam_im * dt
    decay = jnp.exp(lam_dt_re)
    ab_re, ab_im = decay * jnp.cos(lam_dt_im), decay * jnp.sin(lam_dt_im)
    den = lam_re * lam_re + lam_im * lam_im
    nr, ni = ab_re - 1.0, ab_im
    f_re = (nr * lam_re + ni * lam_im) / den
    f_im = (ni * lam_re - nr * lam_im) / den
    bb_re = f_re[..., None] * b_re - f_im[..., None] * b_im
    bb_im = f_re[..., None] * b_im + f_im[..., None] * b_re
    gpb = S5_COLS // state
    nblk = groups // gpb
    hdim = width // groups
    assert gpb * hdim == LANES and nblk * S5_COLS == n
    eye = jnp.eye(gpb, dtype=F32)

    def in_blocks(bb):
        bb = bb.reshape(nblk, gpb, state, hdim)
        return jnp.einsum('mgph,gk->mghkp', bb, eye).reshape(nblk, gpb * hdim, gpb * state)

    def out_blocks(cc):
        cc = cc.reshape(nblk, gpb, hdim, state)
        return jnp.einsum('mghp,gk->mgpkh', cc, eye).reshape(nblk, gpb * state, gpb * hdim)

    wb = jnp.concatenate([in_blocks(bb_re), in_blocks(bb_im)], axis=2).astype(BF16)
    wc = jnp.concatenate([out_blocks(c_re), out_blocks(-c_im)], axis=1).astype(BF16)
    a_vec = jnp.concatenate([ab_re.reshape(1, n), ab_im.reshape(1, n)], axis=1)
    ks = jnp.arange(1, steps + 1, dtype=F32)[:, None, None]
    pw_mag = jnp.exp(lam_dt_re[None] * ks)
    pw_re, pw_im = pw_mag * jnp.cos(lam_dt_im[None] * ks), pw_mag * jnp.sin(lam_dt_im[None] * ks)
    pow_tab = jnp.concatenate([pw_re.reshape(steps, n), pw_im.reshape(steps, n)], axis=1)
    at_vec = pow_tab[steps - 1:steps]
    r = np.arange(t_len)
    perm_np = np.zeros((t_len, t_len), np.float32)
    perm_np[r, (r % SUBLANES) * steps + r // SUBLANES] = 1.0
    perm = jnp.asarray(perm_np, BF16)
    permt = jnp.asarray(perm_np.T, BF16)
    d_vec = d_skip.reshape(1, width)
    wglu = w_glu.astype(BF16)
    bglu = b_glu[None]

    tok = pl.BlockSpec((1, t_len, width), lambda b, i: (b, i, 0))
    full = lambda a: pl.BlockSpec(a.shape, lambda b, i: (0,) * a.ndim)
    consts = (perm, permt, wb, wc, a_vec, at_vec, pow_tab, d_vec, wglu, bglu)
    return pl.pallas_call(
        _s5_kernel,
        out_shape=jax.ShapeDtypeStruct((bsz, seq, width), BF16),
        grid=(bsz, seq // t_len),
        in_specs=[tok, tok] + [full(a) for a in consts],
        out_specs=tok,
        scratch_shapes=[pltpu.VMEM((t_len, 2 * S5_COLS), F32), pltpu.VMEM((t_len, 2 * S5_COLS), BF16),
                        pltpu.VMEM((1, 2 * n), F32), pltpu.VMEM((SUBLANES, 2 * n), F32)],
        compiler_params=_params(("parallel", "arbitrary")),
        name="s5_mixer",
    )(u, z_a, *consts)


def _softmax_init(m_ref, acc_ref):
    m_ref[...] = jnp.full_like(m_ref, NEG_INF)
    acc_ref[...] = jnp.zeros_like(acc_ref)


def _softmax_tile(s, mask, v_ext, m_ref, acc_ref):
    rows, tk = s.shape
    if mask is not None:
        heads = rows // mask.shape[1]
        s = jnp.where(mask, s.reshape(heads, *mask.shape[1:]), NEG_INF).reshape(rows, tk)
    m_old = m_ref[...]
    m_new = jnp.maximum(m_old, jnp.max(s, axis=-1, keepdims=True))
    e = jnp.exp2(s - jnp.concatenate([m_new] * (tk // LANES), axis=1))
    if mask is not None:
        e = jnp.where(mask, e.reshape(heads, *mask.shape[1:]), 0.0).reshape(rows, tk)
    alpha = jnp.exp2(m_old - m_new)
    acc_ref[...] = (jnp.concatenate([alpha, alpha], axis=1) * acc_ref[...]
                    + _dot(e.astype(BF16), v_ext))
    m_ref[...] = m_new


def _softmax_finish(acc_ref):
    acc = acc_ref[...]
    return acc[:, :LANES] / jnp.maximum(acc[:, LANES:], TINY)


def _with_ones(v):
    return jnp.concatenate([v, jnp.ones(v.shape, v.dtype)], axis=1)


def _pipelined_causal_tiles(n_full, scores, update, sa_ref, sb_ref, causal):
    def pair(i, carry):
        scores(2 * i + 1, sb_ref)
        update(sa_ref, 2 * i, None)
        scores(2 * i + 2, sa_ref)
        update(sb_ref, 2 * i + 1, None)
        return carry

    scores(0, sa_ref)
    lax.fori_loop(0, n_full // 2, pair, 0)

    @pl.when(n_full % 2 == 1)
    def _():
        scores(n_full, sb_ref)
        update(sa_ref, n_full - 1, None)
        update(sb_ref, n_full, causal)

    @pl.when(n_full % 2 == 0)
    def _():
        update(sa_ref, n_full, causal)


def _mla_kernel(q_ref, k_ref, v_ref, o_ref, m_ref, acc_ref, sa_ref, sb_ref):
    qi = pl.program_id(1)
    heads, tq = q_ref.shape[1:3]
    tk = ATT_TK
    _softmax_init(m_ref, acc_ref)
    n_full = qi * tq // tk
    owns = [slice(hd * tq, (hd + 1) * tq) for hd in range(heads)]

    def scores(ki, s_ref):
        rows = pl.ds(pl.multiple_of(ki * tk, tk), tk)
        for hd, own in enumerate(owns):
            s_ref[own] = _dot_nt(q_ref[0, hd], k_ref[0, hd, rows, :])

    def update(s_ref, ki, mask):
        rows = pl.ds(pl.multiple_of(ki * tk, tk), tk)
        for hd, own in enumerate(owns):
            _softmax_tile(s_ref[own], mask, _with_ones(v_ref[0, hd, rows, :]), m_ref.at[own], acc_ref.at[own])

    qpos = qi * tq + lax.broadcasted_iota(jnp.int32, (1, tq, tk), 1)
    kpos = n_full * tk + lax.broadcasted_iota(jnp.int32, (1, tq, tk), 2)
    _pipelined_causal_tiles(n_full, scores, update, sa_ref, sb_ref, kpos <= qpos)
    for hd in range(heads):
        o_ref[0, :, hd * MLA_V:(hd + 1) * MLA_V] = _softmax_finish(
            acc_ref.at[hd * tq:(hd + 1) * tq]).astype(o_ref.dtype)


def _mla(q, k, v):
    bsz, heads, seq, dk = q.shape
    tq = ATT_TQ
    assert tq == ATT_TK
    return pl.pallas_call(
        _mla_kernel,
        out_shape=jax.ShapeDtypeStruct((bsz, seq, heads * MLA_V), BF16),
        grid=(bsz, seq // tq),
        in_specs=[pl.BlockSpec((1, heads, tq, dk), lambda b, qi: (b, 0, qi, 0)),
                  pl.BlockSpec((1, heads, seq, dk), lambda b, qi: (b, 0, 0, 0)),
                  pl.BlockSpec((1, heads, seq, MLA_V), lambda b, qi: (b, 0, 0, 0))],
        out_specs=pl.BlockSpec((1, tq, heads * MLA_V), lambda b, qi: (b, qi, 0)),
        scratch_shapes=[pltpu.VMEM((heads * tq, LANES), F32), pltpu.VMEM((heads * tq, 2 * LANES), F32),
                        pltpu.VMEM((heads * tq, ATT_TK), F32), pltpu.VMEM((heads * tq, ATT_TK), F32)],
        compiler_params=_params(("parallel", "arbitrary")),
        name="mla_attention",
    )(q, k, v)


def _ev_out_kernel(ya_ref, o_ref, zb_ref, x_ref, mod_ref, g_ref, wa_ref, wb_ref, out_ref):
    d = x_ref.shape[-1]
    yb = (o_ref[0].astype(F32) * _silu(zb_ref[0].astype(F32))).astype(BF16)
    y = _dot(ya_ref[0], wa_ref[...]) + _dot(yb, wb_ref[...])
    gate = mod_ref[0][:, 2 * d:]
    out_ref[0] = x_ref[0] + gate * _rms(y, g_ref[...])


def _ev_out(y_a, o_mla, z_b, x, mod, post_g, w_out):
    bsz, seq, d = x.shape
    tm = TOKEN_TILE
    wa = w_out[:y_a.shape[-1]].astype(BF16)
    wb = w_out[y_a.shape[-1]:].astype(BF16)
    g = post_g[None]
    tok = lambda n: pl.BlockSpec((1, tm, n), lambda b, i: (b, i, 0))
    full = lambda a: pl.BlockSpec(a.shape, lambda b, i: (0,) * a.ndim)
    return pl.pallas_call(
        _ev_out_kernel,
        out_shape=jax.ShapeDtypeStruct(x.shape, F32),
        grid=(bsz, seq // tm),
        in_specs=[tok(y_a.shape[-1]), tok(o_mla.shape[-1]), tok(z_b.shape[-1]), tok(d),
                  pl.BlockSpec((1, 1, 3 * d), lambda b, i: (b, 0, 0)), full(g), full(wa), full(wb)],
        out_specs=tok(d),
        compiler_params=_params(("parallel", "parallel")),
        name="even_out_proj",
    )(y_a, o_mla, z_b, x, mod[:, None, :], g, wa, wb)


OD_Q, OD_KV, OD_GATE, OD_Z = 1024, 256, 128, 1024
OD_OFF = np.cumsum([0, OD_Q] + [OD_KV] * 6 + [OD_GATE, OD_Z])


def _od_in_kernel(x_ref, mod_ref, g_ref, w_ref, cn_ref, sn_ref,
                  q_ref, kc_ref, vc_ref, ks_ref, vs_ref, kw_ref, vw_ref, gate_ref, z_ref):
    h = _modulated_input(x_ref, mod_ref, g_ref)
    p = _dot(h, w_ref[...])
    o = OD_OFF
    cos, sin = cn_ref[0], sn_ref[0]

    def rope(t):
        return t * cos + pltpu.roll(t, NSA_DIM // 2, axis=1) * sin

    scale = NSA_DIM ** -0.5 * LOG2E
    for hd in range(NSA_HEADS):
        t = p[:, hd * NSA_DIM:(hd + 1) * NSA_DIM]
        q_ref[0, hd // NSA_REP, hd % NSA_REP] = (rope(t) * scale).astype(BF16)
    plain = (kc_ref, vc_ref, None, vs_ref, None, vw_ref)
    roped = (None, None, ks_ref, None, kw_ref, None)
    for j in range(6):
        for g in range(NSA_GROUPS):
            lo = o[1 + j] + g * NSA_DIM
            t = p[:, lo:lo + NSA_DIM]
            if plain[j] is not None:
                plain[j][0, g] = t.astype(BF16)
            else:
                roped[j][0, g] = rope(t).astype(BF16)
    gate_ref[0] = jax.nn.sigmoid(p[:, o[7]:o[8]])
    z_ref[0] = p[:, o[8]:o[9]].astype(BF16)


def _od_in(x, mod, pre_g, w_in, cos_n, sin_n):
    bsz, seq, d = x.shape
    tm = TOKEN_TILE
    n_gate = 3 * NSA_HEADS
    c0 = OD_Q + 6 * OD_KV
    w = jnp.concatenate([w_in[:, :c0 + n_gate], jnp.zeros((d, OD_GATE - n_gate), F32),
                         w_in[:, c0 + n_gate:]], axis=1).astype(BF16)
    g = pre_g[None]
    tok = lambda n: pl.BlockSpec((1, tm, n), lambda b, i: (b, i, 0))
    full = lambda a: pl.BlockSpec(a.shape, lambda b, i: (0,) * a.ndim)
    q_spec = pl.BlockSpec((1, NSA_GROUPS, NSA_REP, tm, NSA_DIM), lambda b, i: (b, 0, 0, i, 0))
    kv_spec = pl.BlockSpec((1, NSA_GROUPS, tm, NSA_DIM), lambda b, i: (b, 0, i, 0))
    kv_shape = jax.ShapeDtypeStruct((bsz, NSA_GROUPS, seq, NSA_DIM), BF16)
    return pl.pallas_call(
        _od_in_kernel,
        out_shape=(jax.ShapeDtypeStruct((bsz, NSA_GROUPS, NSA_REP, seq, NSA_DIM), BF16),)
        + (kv_shape,) * 6
        + (jax.ShapeDtypeStruct((bsz, seq, OD_GATE), F32), jax.ShapeDtypeStruct((bsz, seq, OD_Z), BF16)),
        grid=(bsz, seq // tm),
        in_specs=[tok(d), pl.BlockSpec((1, 1, 3 * d), lambda b, i: (b, 0, 0)), full(g), full(w),
                  tok(LANES), tok(LANES)],
        out_specs=(q_spec,) + (kv_spec,) * 6 + (tok(OD_GATE), tok(OD_Z)),
        compiler_params=_params(("parallel", "parallel")),
        name="odd_in_proj",
    )(x, mod[:, None, :], g, w, cos_n, sin_n)


def _compress_kernel(x_ref, shift_ref, pe_ref, w1_ref, w2_ref, cos_ref, sin_ref, o_ref, *, use_rope):
    x = x_ref[0, 0]
    half = x.shape[1]
    x_next = _dot(shift_ref[...], x).astype(BF16)
    pre = (_dot(x, w1_ref[:half]) + _dot(x_next, w1_ref[half:])
           + _dot(pe_ref[...], w1_ref[...])[0:1])
    out = _dot(_gelu_tanh(pre).astype(BF16), w2_ref[...])
    if use_rope:
        out = out * cos_ref[0] + pltpu.roll(out, NSA_DIM // 2, axis=1) * sin_ref[0]
    o_ref[0, 0] = out.astype(BF16)


def _compress(kv, pe, w1, w2, cos_end, sin_end, use_rope):
    bsz, groups, seq, d = kv.shape
    nb = seq // CMP_STRIDE
    x = kv.reshape(bsz, groups, nb, CMP_STRIDE * d)
    shift = jnp.asarray(np.eye(nb, k=1, dtype=np.float32), BF16)
    pe_rows = jnp.broadcast_to(pe.reshape(1, CMP_LEN * d), (SUBLANES, CMP_LEN * d)).astype(BF16)
    w1b, w2b = w1.astype(BF16), w2.astype(BF16)
    full = lambda a: pl.BlockSpec(a.shape, lambda b, g: (0,) * a.ndim)
    end = pl.BlockSpec((1, nb, d), lambda b, g: (b, 0, 0))
    return pl.pallas_call(
        functools.partial(_compress_kernel, use_rope=use_rope),
        out_shape=jax.ShapeDtypeStruct((bsz, groups, nb, d), BF16),
        grid=(bsz, groups),
        in_specs=[pl.BlockSpec((1, 1, nb, CMP_STRIDE * d), lambda b, g: (b, g, 0, 0)),
                  full(shift), full(pe_rows), full(w1b), full(w2b), end, end],
        out_specs=pl.BlockSpec((1, 1, nb, d), lambda b, g: (b, g, 0, 0)),
        compiler_params=_params(("parallel", "parallel")),
        name="nsa_compress",
    )(x, shift, pe_rows, w1b, w2b, cos_end, sin_end)


def _cmp_sel_kernel(q_ref, kc_ref, vc_ref, pool_ref, o_ref, sel_ref, **static):
    for g in range(q_ref.shape[1]):
        _cmp_sel_group(g, q_ref, kc_ref, vc_ref, pool_ref, o_ref, sel_ref, **static)


def _cmp_sel_group(g, q_ref, kc_ref, vc_ref, pool_ref, o_ref, sel_ref, *, n_cmp, n_sel, n_top):
    qi = pl.program_id(1)
    rep, tq, d = q_ref.shape[2:]
    nb = kc_ref.shape[2]
    q = q_ref[0, g].reshape(rep * tq, d)
    s = _dot_nt(q, kc_ref[0, g]).reshape(rep, tq, nb)
    qpos = qi * tq + lax.broadcasted_iota(jnp.int32, (tq, nb), 0)
    blk = lax.broadcasted_iota(jnp.int32, (tq, nb), 1)
    mask = ((blk * CMP_STRIDE + (CMP_LEN - 1) <= qpos) & (blk < n_cmp))[None]
    s = jnp.where(mask, s, NEG_INF)
    m = jnp.max(s, axis=-1, keepdims=True)
    e = jnp.where(mask, jnp.exp2(s - m), 0.0)
    p = e / jnp.maximum(jnp.sum(e, axis=-1, keepdims=True), TINY)
    o = _dot(p.reshape(rep * tq, nb).astype(BF16), vc_ref[0, g])
    o_ref[0, g] = o.reshape(rep, tq, d).astype(o_ref.dtype)

    rows = pool_ref.shape[0]
    imp = _dot_nt(pool_ref[...], jnp.sum(p, axis=0), precision=HIGHEST)[:n_sel]
    bid = lax.broadcasted_iota(jnp.int32, (n_sel, tq), 0)
    cur = (qi * tq + lax.broadcasted_iota(jnp.int32, (n_sel, tq), 1)) // SEL_BLOCK
    forced = (bid == 0) | (bid == cur) | (bid == cur - 1)
    imp = jnp.where(forced, FORCE_SCORE, jnp.where(bid <= cur, imp, -1.0))
    groups = [imp[g:g + SUBLANES] for g in range(0, n_sel, SUBLANES)]
    sub = lax.broadcasted_iota(jnp.int32, (SUBLANES, tq), 0)
    ranks = [jnp.zeros((SUBLANES, tq), F32) for _ in groups]
    for j in range(n_sel):
        vj = jnp.broadcast_to(imp[j:j + 1, :], (SUBLANES, tq))
        for gi, grp in enumerate(groups):
            lo = gi * SUBLANES
            if lo > j:
                first = vj >= grp
            elif lo + SUBLANES - 1 <= j:
                first = vj > grp
            else:
                first = (vj > grp) | ((vj == grp) & (sub > j - lo))
            ranks[gi] = ranks[gi] + jnp.where(first, 1.0, 0.0)
    rank = jnp.concatenate(ranks, axis=0)
    bias = jnp.where((rank < n_top) & (bid <= cur), 0.0, NEG_INF)
    bias = jnp.concatenate([bias, jnp.full((rows - n_sel, tq), NEG_INF, F32)], axis=0)
    sel_ref[0, g] = bias.T.astype(sel_ref.dtype)


def _cmp_sel(q, kc, vc):
    bsz, groups, rep, seq, d = q.shape
    nb = kc.shape[2]
    tq = NSA_TQ
    n_cmp = (seq - CMP_LEN) // CMP_STRIDE + 1
    n_sel = seq // SEL_BLOCK
    n_top = min(SEL_TOP, n_sel)
    ratio = SEL_BLOCK // CMP_STRIDE
    assert n_sel <= LANES and n_sel * ratio == nb
    pool_np = np.zeros((LANES, nb), np.float32)
    pool_np[np.arange(nb) // ratio, np.arange(nb)] = 1.0
    pool = jnp.asarray(pool_np)
    kv_spec = pl.BlockSpec((1, groups, nb, d), lambda b, i: (b, 0, 0, 0))
    q_spec = pl.BlockSpec((1, groups, rep, tq, d), lambda b, i: (b, 0, 0, i, 0))
    return pl.pallas_call(
        functools.partial(_cmp_sel_kernel, n_cmp=n_cmp, n_sel=n_sel, n_top=n_top),
        out_shape=(jax.ShapeDtypeStruct(q.shape, BF16),
                   jax.ShapeDtypeStruct((bsz, groups, seq, LANES), BF16)),
        grid=(bsz, seq // tq),
        in_specs=[q_spec, kv_spec, kv_spec, pl.BlockSpec(pool.shape, lambda b, i: (0, 0))],
        out_specs=(q_spec, pl.BlockSpec((1, groups, tq, LANES), lambda b, i: (b, 0, i, 0))),
        compiler_params=_params(("parallel", "parallel")),
        name="nsa_cmp_select",
    )(q, kc, vc, pool)


def _sel_kernel(q_ref, bias_ref, k_ref, blk_ref, v_ref, o_ref, m_ref, acc_ref, sa_ref, sb_ref):
    qi = pl.program_id(2)
    rep, tq, d = q_ref.shape[2:]
    tk = NSA_TK
    part = rep // SEL_CHAINS
    q_ext = [jnp.concatenate([q_ref[0, 0, c * part:(c + 1) * part].reshape(part * tq, d),
                              jnp.concatenate([bias_ref[0, 0]] * part, axis=0)], axis=1)
             for c in range(SEL_CHAINS)]
    _softmax_init(m_ref, acc_ref)
    n_full = qi * tq // tk

    owns = [slice(c * part * tq, (c + 1) * part * tq) for c in range(SEL_CHAINS)]

    def scores(ki, s_ref):
        rows = pl.ds(pl.multiple_of(ki * tk, tk), tk)
        k_ext = jnp.concatenate([k_ref[0, 0, rows, :], blk_ref[rows, :]], axis=1)
        for c, own in enumerate(owns):
            s_ref[own] = _dot_nt(q_ext[c], k_ext)

    def update(s_ref, ki, mask):
        rows = pl.ds(pl.multiple_of(ki * tk, tk), tk)
        v_ext = _with_ones(v_ref[0, 0, rows, :])
        for own in owns:
            _softmax_tile(s_ref[own], mask, v_ext, m_ref.at[own], acc_ref.at[own])

    qpos = qi * tq + lax.broadcasted_iota(jnp.int32, (1, tq, tk), 1)
    kpos = n_full * tk + lax.broadcasted_iota(jnp.int32, (1, tq, tk), 2)
    _pipelined_causal_tiles(n_full, scores, update, sa_ref, sb_ref, kpos <= qpos)
    o_ref[0, 0] = _softmax_finish(acc_ref).reshape(rep, tq, d).astype(o_ref.dtype)


def _sel_attention(q, bias, k, v):
    bsz, groups, rep, seq, d = q.shape
    tq = NSA_TQ
    assert NSA_TK % tq == 0
    onehot_np = np.zeros((seq, LANES), np.float32)
    onehot_np[np.arange(seq), np.arange(seq) // SEL_BLOCK] = 1.0
    onehot = jnp.asarray(onehot_np, BF16)
    q_spec = pl.BlockSpec((1, 1, rep, tq, d), lambda b, g, qi: (b, g, 0, qi, 0))
    kv_spec = pl.BlockSpec((1, 1, seq, d), lambda b, g, qi: (b, g, 0, 0))
    return pl.pallas_call(
        _sel_kernel,
        out_shape=jax.ShapeDtypeStruct(q.shape, BF16),
        grid=(bsz, groups, seq // tq),
        in_specs=[q_spec, pl.BlockSpec((1, 1, tq, LANES), lambda b, g, qi: (b, g, qi, 0)),
                  kv_spec, pl.BlockSpec((seq, LANES), lambda b, g, qi: (0, 0)), kv_spec],
        out_specs=q_spec,
        scratch_shapes=[pltpu.VMEM((rep * tq, LANES), F32), pltpu.VMEM((rep * tq, 2 * LANES), F32),
                        pltpu.VMEM((rep * tq, NSA_TK), F32), pltpu.VMEM((rep * tq, NSA_TK), F32)],
        compiler_params=_params(("parallel", "parallel", "arbitrary")),
        name="nsa_selected_attention",
    )(q, bias, k, onehot, v)


def _win_kernel(q_ref, k_ref, v_ref, o_ref, *, span):
    qi = pl.program_id(1)
    groups, rep, tq, d = q_ref.shape[1:]
    start = pl.multiple_of(jnp.maximum(qi * tq + tq - span, 0), tq)
    rows = pl.ds(start, span)
    qpos = qi * tq + lax.broadcasted_iota(jnp.int32, (1, tq, span), 1)
    kpos = start + lax.broadcasted_iota(jnp.int32, (1, tq, span), 2)
    diff = qpos - kpos
    mask = (diff >= 0) & (diff < WINDOW)
    part = rep // WIN_CHAINS
    for g in range(groups):
        k, v_ext = k_ref[0, g, rows, :], _with_ones(v_ref[0, g, rows, :])
        for h0 in range(0, rep, part):
            q = q_ref[0, g, h0:h0 + part].reshape(part * tq, d)
            s = jnp.where(mask, _dot_nt(q, k).reshape(part, tq, span), NEG_INF)
            m = jnp.max(s, axis=-1, keepdims=True)
            e = jnp.where(mask, jnp.exp2(s - m), 0.0).reshape(part * tq, span)
            acc = _dot(e.astype(BF16), v_ext)
            o = acc[:, :LANES] / jnp.maximum(acc[:, LANES:], TINY)
            o_ref[0, g, h0:h0 + part] = o.reshape(part, tq, d).astype(o_ref.dtype)


def _win_attention(q, k, v):
    bsz, groups, rep, seq, d = q.shape
    tq = NSA_TQ
    span = (-(-(WINDOW - 1) // tq) + 1) * tq
    assert span <= seq
    q_spec = pl.BlockSpec((1, groups, rep, tq, d), lambda b, qi: (b, 0, 0, qi, 0))
    kv_spec = pl.BlockSpec((1, groups, seq, d), lambda b, qi: (b, 0, 0, 0))
    return pl.pallas_call(
        functools.partial(_win_kernel, span=span),
        out_shape=jax.ShapeDtypeStruct(q.shape, BF16),
        grid=(bsz, seq // tq),
        in_specs=[q_spec, kv_spec, kv_spec],
        out_specs=q_spec,
        compiler_params=_params(("parallel", "arbitrary")),
        name="nsa_window_attention",
    )(q, k, v)


def _od_out_kernel(oc_ref, os_ref, ow_ref, gate_ref, z_ref, x_ref, mod_ref, g_ref, w_ref, out_ref):
    d = x_ref.shape[-1]
    gates = gate_ref[0]
    z = z_ref[0].astype(F32)
    parts = []
    for hd in range(NSA_HEADS):
        g, r = hd // NSA_REP, hd % NSA_REP
        o = (gates[:, 3 * hd:3 * hd + 1] * oc_ref[0, g, r].astype(F32)
             + gates[:, 3 * hd + 1:3 * hd + 2] * os_ref[0, g, r].astype(F32)
             + gates[:, 3 * hd + 2:3 * hd + 3] * ow_ref[0, g, r].astype(F32))
        parts.append((o * _silu(z[:, hd * NSA_DIM:(hd + 1) * NSA_DIM])).astype(BF16))
    y = _dot(jnp.concatenate(parts, axis=-1), w_ref[...])
    gate = mod_ref[0][:, 2 * d:]
    out_ref[0] = x_ref[0] + gate * _rms(y, g_ref[...])


def _od_out(o_cmp, o_sel, o_win, gates, z, x, mod, post_g, w_out):
    bsz, seq, d = x.shape
    tm = TOKEN_TILE
    w = w_out.astype(BF16)
    g = post_g[None]
    tok = lambda n: pl.BlockSpec((1, tm, n), lambda b, i: (b, i, 0))
    full = lambda a: pl.BlockSpec(a.shape, lambda b, i: (0,) * a.ndim)
    o_spec = pl.BlockSpec((1, NSA_GROUPS, NSA_REP, tm, NSA_DIM), lambda b, i: (b, 0, 0, i, 0))
    return pl.pallas_call(
        _od_out_kernel,
        out_shape=jax.ShapeDtypeStruct(x.shape, F32),
        grid=(bsz, seq // tm),
        in_specs=[o_spec, o_spec, o_spec, tok(OD_GATE), tok(OD_Z), tok(d),
                  pl.BlockSpec((1, 1, 3 * d), lambda b, i: (b, 0, 0)), full(g), full(w)],
        out_specs=tok(d),
        compiler_params=_params(("parallel", "parallel")),
        name="odd_out_proj",
    )(o_cmp, o_sel, o_win, gates, z, x, mod[:, None, :], g, w)


def _even_layer(x, mod, pre_g, post_g, tabs, w_in, lam_re, lam_im, log_dt, b_re, b_im, c_re, c_im,
                d_skip, w_glu, b_glu, q_norm_g, kv_norm_g, w_uq, w_ukv, w_out):
    cos_m, sin_m = tabs[0], tabs[1]
    u, z_a, z_b, q, k, v = _ev_in(x, mod, pre_g, w_in, q_norm_g, kv_norm_g, w_uq, w_ukv, cos_m, sin_m)
    y_a = _s5(u, z_a, lam_re, lam_im, log_dt, b_re, b_im, c_re, c_im, d_skip, w_glu, b_glu)
    o_mla = _mla(q, k, v)
    return _ev_out(y_a, o_mla, z_b, x, mod, post_g, w_out)


def _odd_layer(x, mod, pre_g, post_g, tabs, end_tabs, w_in, k_pe, k_w1, k_w2, v_pe, v_w1, v_w2, w_out):
    cos_n, sin_n = tabs[2], tabs[3]
    cos_end, sin_end = end_tabs[2], end_tabs[3]
    q, k_c, v_c, k_s, v_s, k_w, v_w, gates, z = _od_in(x, mod, pre_g, w_in, cos_n, sin_n)
    kc = _compress(k_c, k_pe, k_w1, k_w2, cos_end, sin_end, True)
    vc = _compress(v_c, v_pe, v_w1, v_w2, cos_end, sin_end, False)
    o_cmp, sel = _cmp_sel(q, kc, vc)
    o_sel = _sel_attention(q, sel, k_s, v_s)
    o_win = _win_attention(q, k_w, v_w)
    return _od_out(o_cmp, o_sel, o_win, gates, z, x, mod, post_g, w_out)


def kernel(x, c, positions, pre_norm_g, post_norm_g, w_ada, b_ada, ev_w_in, ev_lam_re, ev_lam_im, ev_log_dt, ev_b_re, ev_b_im, ev_c_re, ev_c_im, ev_d_skip, ev_w_glu, ev_b_glu, ev_q_norm_g, ev_kv_norm_g, ev_w_uq, ev_w_ukv, ev_w_out, od_w_in, od_cmp_k_pe, od_cmp_k_w1, od_cmp_k_w2, od_cmp_v_pe, od_cmp_v_w1, od_cmp_v_w2, od_w_out):
    depth = pre_norm_g.shape[0]
    tabs = _rope_tables(positions)
    seq = positions.shape[1]
    pos_end = positions[:, CMP_LEN - 1::CMP_STRIDE]
    pos_end = jnp.pad(pos_end, ((0, 0), (0, seq // CMP_STRIDE - pos_end.shape[1])))
    end_tabs = _rope_tables(pos_end)
    mods = _modulation(c, w_ada, b_ada)
    for layer in range(depth):
        i = layer // 2
        if layer % 2 == 0:
            x = _even_layer(x, mods[layer], pre_norm_g[layer], post_norm_g[layer], tabs,
                            ev_w_in[i], ev_lam_re[i], ev_lam_im[i], ev_log_dt[i], ev_b_re[i], ev_b_im[i],
                            ev_c_re[i], ev_c_im[i], ev_d_skip[i], ev_w_glu[i], ev_b_glu[i],
                            ev_q_norm_g[i], ev_kv_norm_g[i], ev_w_uq[i], ev_w_ukv[i], ev_w_out[i])
        else:
            x = _odd_layer(x, mods[layer], pre_norm_g[layer], post_norm_g[layer], tabs, end_tabs,
                           od_w_in[i], od_cmp_k_pe[i], od_cmp_k_w1[i], od_cmp_k_w2[i],
                           od_cmp_v_pe[i], od_cmp_v_w1[i], od_cmp_v_w2[i], od_w_out[i])
    return x
```

```python
import functools
import math

import numpy as np
import jax
import jax.numpy as jnp
from jax import lax
from jax.experimental import pallas as pl
from jax.experimental.pallas import tpu as pltpu

F32 = jnp.float32
BF16 = jnp.bfloat16
HIGHEST = lax.Precision.HIGHEST

EPS = 1e-6
ROPE_THETA = 10000.0
NEG_INF = -1e30
TINY = 1e-30
FORCE_SCORE = 1e9
LOG2E = 1.0 / math.log(2.0)

S5_GROUP = 16
S5_STATE = 64
MLA_HEADS = 4
MLA_NOPE = 128
MLA_ROPE = 64
MLA_V = 128
NSA_HEADS = 8
NSA_GROUPS = 2
NSA_REP = NSA_HEADS // NSA_GROUPS
NSA_DIM = 128
CMP_LEN = 32
CMP_STRIDE = 16
CMP_HIDDEN = 256
SEL_BLOCK = 64
SEL_TOP = 16
WINDOW = 512

LANES = 128
SUBLANES = 8
VMEM_LIMIT = 48 * 1024 * 1024

TOKEN_TILE = 512
S5_CHUNK = 256
S5_COLS = 512
ATT_TQ = 512
ATT_TK = 512
NSA_TQ = 256
NSA_TK = 512
SEL_CHAINS = 4
WIN_CHAINS = 4


def _params(sem):
    return pltpu.CompilerParams(dimension_semantics=sem, vmem_limit_bytes=VMEM_LIMIT)


def _dot(a, b, precision=None):
    return jnp.dot(a, b, preferred_element_type=F32, precision=precision)


def _dot_nt(a, b, precision=None):
    return lax.dot_general(a, b, (((1,), (1,)), ((), ())), preferred_element_type=F32,
                           precision=precision)


def _silu(v):
    return v * jax.nn.sigmoid(v)


def _gelu_tanh(v):
    return 0.5 * v * (1.0 + jnp.tanh(math.sqrt(2.0 / math.pi) * (v + 0.044715 * (v * v * v))))


def _rms(v, g):
    return v * lax.rsqrt(jnp.mean(v * v, axis=-1, keepdims=True) + EPS) * g


def _rope_tab_kernel(pos_ref, f_ref, sgm_ref, sgn_ref, cm_ref, sm_ref, cn_ref, sn_ref):
    hn, hm = NSA_DIM // 2, MLA_ROPE // 2
    ang = pos_ref[0] * f_ref[...]
    c, s = jnp.cos(ang), jnp.sin(ang)
    cn_ref[0] = jnp.concatenate([c[:, :hn]] * (LANES // hn), axis=1)
    sn_ref[0] = jnp.concatenate([s[:, :hn]] * (LANES // hn), axis=1) * sgn_ref[...]
    cm_ref[0] = jnp.concatenate([c[:, hn:hn + hm]] * (LANES // hm), axis=1)
    sm_ref[0] = jnp.concatenate([s[:, hn:hn + hm]] * (LANES // hm), axis=1) * sgm_ref[...]


def _rope_tables(positions):
    bsz, seq = positions.shape
    ts = min(TOKEN_TILE, seq)
    pos = positions.astype(F32)[..., None]
    half_m, half_n = MLA_ROPE // 2, NSA_DIM // 2
    fm = ROPE_THETA ** (-jnp.arange(half_m, dtype=F32) / half_m)
    fn = ROPE_THETA ** (-jnp.arange(half_n, dtype=F32) / half_n)
    freqs = jnp.concatenate([fn, fm, jnp.zeros((LANES - half_n - half_m,), F32)])[None]
    sgm = jnp.tile(jnp.concatenate([-jnp.ones((half_m,), F32), jnp.ones((half_m,), F32)]),
                   LANES // MLA_ROPE)[None]
    sgn = jnp.concatenate([-jnp.ones((half_n,), F32), jnp.ones((half_n,), F32)])[None]
    row = pl.BlockSpec((1, LANES), lambda b, i: (0, 0))
    tab = pl.BlockSpec((1, ts, LANES), lambda b, i: (b, i, 0))
    shp = jax.ShapeDtypeStruct((bsz, seq, LANES), F32)
    return pl.pallas_call(
        _rope_tab_kernel,
        out_shape=(shp, shp, shp, shp),
        grid=(bsz, seq // ts),
        in_specs=[pl.BlockSpec((1, ts, 1), lambda b, i: (b, i, 0)), row, row, row],
        out_specs=(tab, tab, tab, tab),
        compiler_params=_params(("parallel", "parallel")),
        name="rope_tables",
    )(pos, freqs, sgm, sgn)


def _mod_kernel(c_ref, w_ref, b_ref, o_ref):
    s = _silu(c_ref[...])
    o_ref[0] = _dot(s, w_ref[0], precision=HIGHEST) + b_ref[0]


def _modulation(c, w_ada, b_ada):
    depth, d, _ = w_ada.shape
    bsz = c.shape[0]
    rows = -(-bsz // SUBLANES) * SUBLANES
    c_pad = jnp.pad(c, ((0, rows - bsz), (0, 0)))
    out = pl.pallas_call(
        _mod_kernel,
        out_shape=jax.ShapeDtypeStruct((depth, rows, 3 * d), F32),
        grid=(depth, 3),
        in_specs=[pl.BlockSpec((rows, d), lambda l, j: (0, 0)),
                  pl.BlockSpec((1, d, d), lambda l, j: (l, 0, j)),
                  pl.BlockSpec((1, 1, d), lambda l, j: (l, 0, j))],
        out_specs=pl.BlockSpec((1, rows, d), lambda l, j: (l, 0, j)),
        compiler_params=_params(("parallel", "parallel")),
        name="modulation",
    )(c_pad, w_ada, b_ada[:, None, :])
    return out[:, :bsz]


def _modulated_input(x_ref, mod_ref, g_ref):
    d = x_ref.shape[-1]
    x = x_ref[0]
    mod = mod_ref[0]
    shift, scale = mod[:, :d], mod[:, d:2 * d]
    return (_rms(x, g_ref[...]) * (1.0 + scale) + shift).astype(BF16)


EV_U, EV_ZA, EV_CQ, EV_CKV, EV_KPE, EV_ZB = 512, 512, 768, 256, 256, 512
EV_OFF = np.cumsum([0, EV_U, EV_ZA, EV_CQ, EV_CKV, EV_KPE, EV_ZB])


def _ev_in_kernel(x_ref, mod_ref, g_ref, w_ref, gq_ref, gkv_ref, wuq_ref, wukv_ref, cm_ref, sm_ref,
                  u_ref, za_ref, zb_ref, q_ref, k_ref, v_ref):
    h = _modulated_input(x_ref, mod_ref, g_ref)
    p = _dot(h, w_ref[...])
    o = EV_OFF
    u_ref[0] = p[:, o[0]:o[1]].astype(BF16)
    za_ref[0] = p[:, o[1]:o[2]].astype(BF16)
    zb_ref[0] = p[:, o[5]:o[6]].astype(BF16)
    cos, sin = cm_ref[0], sm_ref[0]

    cq = _rms(p[:, o[2]:o[3]], gq_ref[...]).astype(BF16)
    q = _dot(cq, wuq_ref[...]) * ((MLA_NOPE + MLA_ROPE) ** -0.5 * LOG2E)
    ckv = _rms(p[:, o[3]:o[4]], gkv_ref[...]).astype(BF16)
    kv = _dot(ckv, wukv_ref[...])
    kpe = (p[:, o[4]:o[4] + LANES] * cos + p[:, o[4] + LANES:o[5]] * sin).astype(BF16)
    nope = MLA_HEADS * MLA_NOPE
    for hd in range(MLA_HEADS):
        b0 = hd * (MLA_NOPE + 2 * LANES)
        qpe = q[:, b0 + MLA_NOPE:b0 + MLA_NOPE + LANES] * cos + q[:, b0 + MLA_NOPE + LANES:b0 + MLA_NOPE + 2 * LANES] * sin
        q_ref[0, hd] = jnp.concatenate([q[:, b0:b0 + MLA_NOPE], qpe], axis=-1).astype(BF16)
        k_ref[0, hd] = jnp.concatenate([kv[:, hd * MLA_NOPE:(hd + 1) * MLA_NOPE].astype(BF16), kpe], axis=-1)
        v_ref[0, hd] = kv[:, nope + hd * MLA_V:nope + (hd + 1) * MLA_V].astype(BF16)


def _select_columns(w, cols):
    n = w.shape[1]
    cols = [int(c) for c in cols]
    pieces, i = [], 0
    while i < len(cols):
        j = i + 1
        if cols[i] == n:
            while j < len(cols) and cols[j] == n:
                j += 1
            pieces.append(jnp.zeros((w.shape[0], j - i), w.dtype))
        else:
            while j < len(cols) and cols[j] == cols[j - 1] + 1 and cols[j] != n:
                j += 1
            pieces.append(w[:, cols[i]:cols[j - 1] + 1])
        i = j
    return jnp.concatenate(pieces, axis=1)


def _ev_in(x, mod, pre_g, w_in, q_norm_g, kv_norm_g, w_uq, w_ukv, cos_m, sin_m):
    bsz, seq, d = x.shape
    tm = TOKEN_TILE
    hq = MLA_ROPE // 2
    pad = LANES - MLA_ROPE

    def rot_cols(base, zero):
        x1, x2, z = base + np.arange(hq), base + hq + np.arange(hq), np.full(pad, zero)
        return np.concatenate([x1, x2, z, x2, x1, z])

    c0 = np.cumsum([0, 512, 512, 768, 256, 64, 512])
    cols = np.concatenate([np.arange(c0[0], c0[4]), rot_cols(c0[4], c0[6]), np.arange(c0[5], c0[6])])
    w = _select_columns(w_in, cols).astype(BF16)
    per = MLA_NOPE + MLA_ROPE
    q_cols = np.concatenate([np.concatenate([hd * per + np.arange(MLA_NOPE),
                                             rot_cols(hd * per + MLA_NOPE, MLA_HEADS * per)])
                             for hd in range(MLA_HEADS)])
    wuq = _select_columns(w_uq, q_cols).astype(BF16)
    per = MLA_NOPE + MLA_V
    heads = np.arange(MLA_HEADS)[:, None] * per
    kv_cols = np.concatenate([(heads + np.arange(MLA_NOPE)).ravel(),
                              (heads + MLA_NOPE + np.arange(MLA_V)).ravel()])
    wukv = _select_columns(w_ukv, kv_cols).astype(BF16)

    tok = lambda n: pl.BlockSpec((1, tm, n), lambda b, i: (b, i, 0))
    full = lambda a: pl.BlockSpec(a.shape, lambda b, i: (0,) * a.ndim)
    out = lambda n: jax.ShapeDtypeStruct((bsz, seq, n), BF16)
    head = lambda n: pl.BlockSpec((1, MLA_HEADS, tm, n), lambda b, i: (b, 0, i, 0))
    head_out = lambda n: jax.ShapeDtypeStruct((bsz, MLA_HEADS, seq, n), BF16)
    gq, gkv, g = q_norm_g[None], kv_norm_g[None], pre_g[None]
    return pl.pallas_call(
        _ev_in_kernel,
        out_shape=(out(512), out(512), out(512), head_out(2 * LANES), head_out(2 * LANES), head_out(MLA_V)),
        grid=(bsz, seq // tm),
        in_specs=[tok(d), pl.BlockSpec((1, 1, 3 * d), lambda b, i: (b, 0, 0)), full(g), full(w),
                  full(gq), full(gkv), full(wuq), full(wukv), tok(LANES), tok(LANES)],
        out_specs=(tok(512), tok(512), tok(512), head(2 * LANES), head(2 * LANES), head(MLA_V)),
        compiler_params=_params(("parallel", "parallel")),
        name="even_in_proj",
    )(x, mod[:, None, :], g, w, gq, gkv, wuq, wukv, cos_m, sin_m)


def _s5_kernel(u_ref, z_ref, perm_ref, permt_ref, wb_ref, wc_ref, a_ref, at_ref, pow_ref,
               d_ref, wglu_ref, bglu_ref, o_ref, bu_ref, xb_ref, state_ref, carry_ref):
    t_len = u_ref.shape[1]
    n = a_ref.shape[1] // 2
    steps = t_len // SUBLANES

    @pl.when(pl.program_id(1) == 0)
    def _():
        state_ref[...] = jnp.zeros_like(state_ref)

    perm = perm_ref[...]
    u_p = _dot(perm, u_ref[0])
    z_p = _dot(perm, z_ref[0])
    u_pb = u_p.astype(BF16)
    bre, bim = pl.ds(0, S5_COLS), pl.ds(S5_COLS, S5_COLS)
    y_parts = []

    for cb in range(n // S5_COLS):
        u_cb = u_pb[:, cb * LANES:(cb + 1) * LANES]
        bu_ref[...] = _dot(u_cb, wb_ref[cb])
        re = pl.ds(cb * S5_COLS, S5_COLS)
        im = pl.ds(n + cb * S5_COLS, S5_COLS)
        ar = jnp.broadcast_to(a_ref[:, re], (SUBLANES, S5_COLS))
        ai = jnp.broadcast_to(a_ref[:, im], (SUBLANES, S5_COLS))

        def step(t, carry):
            xr, xi = carry
            rows = pl.ds(pl.multiple_of(t * SUBLANES, SUBLANES), SUBLANES)
            nr = ar * xr - ai * xi + bu_ref[rows, bre]
            ni = ar * xi + ai * xr + bu_ref[rows, bim]
            bu_ref[rows, bre] = nr
            bu_ref[rows, bim] = ni
            return nr, ni

        zero = jnp.zeros((SUBLANES, S5_COLS), F32)
        er, ei = lax.fori_loop(0, steps, step, (zero, zero), unroll=True)

        sr, si = state_ref[:, re], state_ref[:, im]
        tr, ti = at_ref[:, re], at_ref[:, im]
        for j in range(SUBLANES):
            carry_ref[j:j + 1, re] = sr
            carry_ref[j:j + 1, im] = si
            sr, si = (tr * sr - ti * si + er[j:j + 1], tr * si + ti * sr + ei[j:j + 1])
        state_ref[:, re] = sr
        state_ref[:, im] = si
        cr, ci = carry_ref[:, re], carry_ref[:, im]

        def fix(t2, _):
            xr, xi = [], []
            for k in range(2):
                t = 2 * t2 + k
                rows = pl.ds(pl.multiple_of(t * SUBLANES, SUBLANES), SUBLANES)
                pr, pi = pow_ref[pl.ds(t, 1), re], pow_ref[pl.ds(t, 1), im]
                xr.append(bu_ref[rows, bre] + pr * cr - pi * ci)
                xi.append(bu_ref[rows, bim] + pr * ci + pi * cr)
            rows = pl.ds(pl.multiple_of(t2 * 2 * SUBLANES, 2 * SUBLANES), 2 * SUBLANES)
            xb_ref[rows, bre] = jnp.concatenate(xr, axis=0).astype(BF16)
            xb_ref[rows, bim] = jnp.concatenate(xi, axis=0).astype(BF16)
            return 0

        lax.fori_loop(0, steps // 2, fix, 0, unroll=True)
        y_parts.append(_dot(xb_ref[...], wc_ref[cb]))

    y = jnp.concatenate(y_parts, axis=1) + d_ref[...] * u_p
    g = _gelu_tanh(y)
    gate = jax.nn.sigmoid(_dot(g.astype(BF16), wglu_ref[...]) + bglu_ref[...])
    out = (g * gate * _silu(z_p)).astype(BF16)
    o_ref[0] = _dot(permt_ref[...], out).astype(BF16)


def _s5(u, z_a, lam_re, lam_im, log_dt, b_re, b_im, c_re, c_im, d_skip, w_glu, b_glu):
    bsz, seq, width = u.shape
    groups, state = lam_re.shape
    t_len = S5_CHUNK
    steps = t_len // SUBLANES
    n = groups * state
    dt = jnp.exp(log_dt)[:, None]
    lam_dt_re, lam_dt_im = lam_re * dt, lam_im * dt
    decay = jnp.exp(lam_dt_re)
    ab_re, ab_im = decay * jnp.cos(lam_dt_im), decay * jnp.sin(lam_dt_im)
    den = lam_re * lam_re + lam_im * lam_im
    nr, ni = ab_re - 1.0, ab_im
    f_re = (nr * lam_re + ni * lam_im) / den
    f_im = (ni * lam_re - nr * lam_im) / den
    bb_re = f_re[..., None] * b_re - f_im[..., None] * b_im
    bb_im = f_re[..., None] * b_im + f_im[..., None] * b_re
    gpb = S5_COLS // state
    nblk = groups // gpb
    hdim = width // groups
    assert gpb * hdim == LANES and nblk * S5_COLS == n
    eye = jnp.eye(gpb, dtype=F32)

    def in_blocks(bb):
        bb = bb.reshape(nblk, gpb, state, hdim)
        return jnp.einsum('mgph,gk->mghkp', bb, eye).reshape(nblk, gpb * hdim, gpb * state)

    def out_blocks(cc):
        cc = cc.reshape(nblk, gpb, hdim, state)
        return jnp.einsum('mghp,gk->mgpkh', cc, eye).reshape(nblk, gpb * state, gpb * hdim)

    wb = jnp.concatenate([in_blocks(bb_re), in_blocks(bb_im)], axis=2).astype(BF16)
    wc = jnp.concatenate([out_blocks(c_re), out_blocks(-c_im)], axis=1).astype(BF16)
    a_vec = jnp.concatenate([ab_re.reshape(1, n), ab_im.reshape(1, n)], axis=1)
    ks = jnp.arange(1, steps + 1, dtype=F32)[:, None, None]
    pw_mag = jnp.exp(lam_dt_re[None] * ks)
    pw_re, pw_im = pw_mag * jnp.cos(lam_dt_im[None] * ks), pw_mag * jnp.sin(lam_dt_im[None] * ks)
    pow_tab = jnp.concatenate([pw_re.reshape(steps, n), pw_im.reshape(steps, n)], axis=1)
    at_vec = pow_tab[steps - 1:steps]
    r = np.arange(t_len)
    perm_np = np.zeros((t_len, t_len), np.float32)
    perm_np[r, (r % SUBLANES) * steps + r // SUBLANES] = 1.0
    perm = jnp.asarray(perm_np, BF16)
    permt = jnp.asarray(perm_np.T, BF16)
    d_vec = d_skip.reshape(1, width)
    wglu = w_glu.astype(BF16)
    bglu = b_glu[None]

    tok = pl.BlockSpec((1, t_len, width), lambda b, i: (b, i, 0))
    full = lambda a: pl.BlockSpec(a.shape, lambda b, i: (0,) * a.ndim)
    consts = (perm, permt, wb, wc, a_vec, at_vec, pow_tab, d_vec, wglu, bglu)
    return pl.pallas_call(
        _s5_kernel,
        out_shape=jax.ShapeDtypeStruct((bsz, seq, width), BF16),
        grid=(bsz, seq // t_len),
        in_specs=[tok, tok] + [full(a) for a in consts],
        out_specs=tok,
        scratch_shapes=[pltpu.VMEM((t_len, 2 * S5_COLS), F32), pltpu.VMEM((t_len, 2 * S5_COLS), BF16),
                        pltpu.VMEM((1, 2 * n), F32), pltpu.VMEM((SUBLANES, 2 * n), F32)],
        compiler_params=_params(("parallel", "arbitrary")),
        name="s5_mixer",
    )(u, z_a, *consts)


def _softmax_init(m_ref, acc_ref):
    m_ref[...] = jnp.full_like(m_ref, NEG_INF)
    acc_ref[...] = jnp.zeros_like(acc_ref)


def _softmax_tile(s, mask, v_ext, m_ref, acc_ref):
    rows, tk = s.shape
    if mask is not None:
        heads = rows // mask.shape[1]
        s = jnp.where(mask, s.reshape(heads, *mask.shape[1:]), NEG_INF).reshape(rows, tk)
    m_old = m_ref[...]
    m_new = jnp.maximum(m_old, jnp.max(s, axis=-1, keepdims=True))
    e = jnp.exp2(s - jnp.concatenate([m_new] * (tk // LANES), axis=1))
    if mask is not None:
        e = jnp.where(mask, e.reshape(heads, *mask.shape[1:]), 0.0).reshape(rows, tk)
    alpha = jnp.exp2(m_old - m_new)
    acc_ref[...] = (jnp.concatenate([alpha, alpha], axis=1) * acc_ref[...]
                    + _dot(e.astype(BF16), v_ext))
    m_ref[...] = m_new


def _softmax_finish(acc_ref):
    acc = acc_ref[...]
    return acc[:, :LANES] / jnp.maximum(acc[:, LANES:], TINY)


def _with_ones(v):
    return jnp.concatenate([v, jnp.ones(v.shape, v.dtype)], axis=1)


def _pipelined_causal_tiles(n_full, scores, update, sa_ref, sb_ref, causal, filler=None):
    def pair(i, carry):
        scores(2 * i + 1, sb_ref)
        update(sa_ref, 2 * i, None)
        scores(2 * i + 2, sa_ref)
        update(sb_ref, 2 * i + 1, None)
        return carry

    scores(0, sa_ref)
    if filler is not None:
        filler()
    lax.fori_loop(0, n_full // 2, pair, 0)

    @pl.when(n_full % 2 == 1)
    def _():
        scores(n_full, sb_ref)
        update(sa_ref, n_full - 1, None)
        update(sb_ref, n_full, causal)

    @pl.when(n_full % 2 == 0)
    def _():
        update(sa_ref, n_full, causal)


def _mla_kernel(q_ref, k_ref, v_ref, o_ref, m_ref, acc_ref, sa_ref, sb_ref):
    qi = pl.program_id(1)
    heads, tq = q_ref.shape[1:3]
    tk = ATT_TK
    _softmax_init(m_ref, acc_ref)
    n_full = qi * tq // tk
    owns = [slice(hd * tq, (hd + 1) * tq) for hd in range(heads)]

    def scores(ki, s_ref):
        rows = pl.ds(pl.multiple_of(ki * tk, tk), tk)
        for hd, own in enumerate(owns):
            s_ref[own] = _dot_nt(q_ref[0, hd], k_ref[0, hd, rows, :])

    def update(s_ref, ki, mask):
        rows = pl.ds(pl.multiple_of(ki * tk, tk), tk)
        for hd, own in enumerate(owns):
            _softmax_tile(s_ref[own], mask, _with_ones(v_ref[0, hd, rows, :]), m_ref.at[own], acc_ref.at[own])

    qpos = qi * tq + lax.broadcasted_iota(jnp.int32, (1, tq, tk), 1)
    kpos = n_full * tk + lax.broadcasted_iota(jnp.int32, (1, tq, tk), 2)
    _pipelined_causal_tiles(n_full, scores, update, sa_ref, sb_ref, kpos <= qpos)
    for hd in range(heads):
        o_ref[0, :, hd * MLA_V:(hd + 1) * MLA_V] = _softmax_finish(
            acc_ref.at[hd * tq:(hd + 1) * tq]).astype(o_ref.dtype)


def _mla(q, k, v):
    bsz, heads, seq, dk = q.shape
    tq = ATT_TQ
    assert tq == ATT_TK
    return pl.pallas_call(
        _mla_kernel,
        out_shape=jax.ShapeDtypeStruct((bsz, seq, heads * MLA_V), BF16),
        grid=(bsz, seq // tq),
        in_specs=[pl.BlockSpec((1, heads, tq, dk), lambda b, qi: (b, 0, qi, 0)),
                  pl.BlockSpec((1, heads, seq, dk), lambda b, qi: (b, 0, 0, 0)),
                  pl.BlockSpec((1, heads, seq, MLA_V), lambda b, qi: (b, 0, 0, 0))],
        out_specs=pl.BlockSpec((1, tq, heads * MLA_V), lambda b, qi: (b, qi, 0)),
        scratch_shapes=[pltpu.VMEM((heads * tq, LANES), F32), pltpu.VMEM((heads * tq, 2 * LANES), F32),
                        pltpu.VMEM((heads * tq, ATT_TK), F32), pltpu.VMEM((heads * tq, ATT_TK), F32)],
        compiler_params=_params(("parallel", "arbitrary")),
        name="mla_attention",
    )(q, k, v)


def _ev_out_kernel(ya_ref, o_ref, zb_ref, x_ref, mod_ref, g_ref, wa_ref, wb_ref, out_ref):
    d = x_ref.shape[-1]
    yb = (o_ref[0].astype(F32) * _silu(zb_ref[0].astype(F32))).astype(BF16)
    y = _dot(ya_ref[0], wa_ref[...]) + _dot(yb, wb_ref[...])
    gate = mod_ref[0][:, 2 * d:]
    out_ref[0] = x_ref[0] + gate * _rms(y, g_ref[...])


def _ev_out(y_a, o_mla, z_b, x, mod, post_g, w_out):
    bsz, seq, d = x.shape
    tm = TOKEN_TILE
    wa = w_out[:y_a.shape[-1]].astype(BF16)
    wb = w_out[y_a.shape[-1]:].astype(BF16)
    g = post_g[None]
    tok = lambda n: pl.BlockSpec((1, tm, n), lambda b, i: (b, i, 0))
    full = lambda a: pl.BlockSpec(a.shape, lambda b, i: (0,) * a.ndim)
    return pl.pallas_call(
        _ev_out_kernel,
        out_shape=jax.ShapeDtypeStruct(x.shape, F32),
        grid=(bsz, seq // tm),
        in_specs=[tok(y_a.shape[-1]), tok(o_mla.shape[-1]), tok(z_b.shape[-1]), tok(d),
                  pl.BlockSpec((1, 1, 3 * d), lambda b, i: (b, 0, 0)), full(g), full(wa), full(wb)],
        out_specs=tok(d),
        compiler_params=_params(("parallel", "parallel")),
        name="even_out_proj",
    )(y_a, o_mla, z_b, x, mod[:, None, :], g, wa, wb)


OD_Q, OD_KV, OD_GATE, OD_Z = 1024, 256, 128, 1024
OD_OFF = np.cumsum([0, OD_Q] + [OD_KV] * 6 + [OD_GATE, OD_Z])


def _od_in_kernel(x_ref, mod_ref, g_ref, w_ref, cn_ref, sn_ref,
                  q_ref, kc_ref, vc_ref, ks_ref, vs_ref, kw_ref, vw_ref, gate_ref, z_ref):
    h = _modulated_input(x_ref, mod_ref, g_ref)
    p = _dot(h, w_ref[...])
    o = OD_OFF
    cos, sin = cn_ref[0], sn_ref[0]

    def rope(t):
        return t * cos + pltpu.roll(t, NSA_DIM // 2, axis=1) * sin

    scale = NSA_DIM ** -0.5 * LOG2E
    for hd in range(NSA_HEADS):
        t = p[:, hd * NSA_DIM:(hd + 1) * NSA_DIM]
        q_ref[0, hd // NSA_REP, hd % NSA_REP] = (rope(t) * scale).astype(BF16)
    plain = (kc_ref, vc_ref, None, vs_ref, None, vw_ref)
    roped = (None, None, ks_ref, None, kw_ref, None)
    for j in range(6):
        for g in range(NSA_GROUPS):
            lo = o[1 + j] + g * NSA_DIM
            t = p[:, lo:lo + NSA_DIM]
            if plain[j] is not None:
                plain[j][0, g] = t.astype(plain[j].dtype)
            else:
                roped[j][0, g] = rope(t).astype(BF16)
    gate_ref[0] = jax.nn.sigmoid(p[:, o[7]:o[8]])
    z_ref[0] = p[:, o[8]:o[9]].astype(BF16)


def _od_in(x, mod, pre_g, w_in, cos_n, sin_n):
    bsz, seq, d = x.shape
    tm = TOKEN_TILE
    n_gate = 3 * NSA_HEADS
    c0 = OD_Q + 6 * OD_KV
    w = jnp.concatenate([w_in[:, :c0 + n_gate], jnp.zeros((d, OD_GATE - n_gate), F32),
                         w_in[:, c0 + n_gate:]], axis=1).astype(BF16)
    g = pre_g[None]
    tok = lambda n: pl.BlockSpec((1, tm, n), lambda b, i: (b, i, 0))
    full = lambda a: pl.BlockSpec(a.shape, lambda b, i: (0,) * a.ndim)
    q_spec = pl.BlockSpec((1, NSA_GROUPS, NSA_REP, tm, NSA_DIM), lambda b, i: (b, 0, 0, i, 0))
    kv_spec = pl.BlockSpec((1, NSA_GROUPS, tm, NSA_DIM), lambda b, i: (b, 0, i, 0))
    kv_shape = jax.ShapeDtypeStruct((bsz, NSA_GROUPS, seq, NSA_DIM), BF16)
    cmp_shape = jax.ShapeDtypeStruct(kv_shape.shape, F32)
    return pl.pallas_call(
        _od_in_kernel,
        out_shape=(jax.ShapeDtypeStruct((bsz, NSA_GROUPS, NSA_REP, seq, NSA_DIM), BF16),)
        + (cmp_shape,) * 2 + (kv_shape,) * 4
        + (jax.ShapeDtypeStruct((bsz, seq, OD_GATE), F32), jax.ShapeDtypeStruct((bsz, seq, OD_Z), BF16)),
        grid=(bsz, seq // tm),
        in_specs=[tok(d), pl.BlockSpec((1, 1, 3 * d), lambda b, i: (b, 0, 0)), full(g), full(w),
                  tok(LANES), tok(LANES)],
        out_specs=(q_spec,) + (kv_spec,) * 6 + (tok(OD_GATE), tok(OD_Z)),
        compiler_params=_params(("parallel", "parallel")),
        name="odd_in_proj",
    )(x, mod[:, None, :], g, w, cos_n, sin_n)


def _compress_kernel(x_ref, pe_ref, w1_ref, w2_ref, cos_ref, sin_ref, o_ref, *, use_rope):
    seq, d = x_ref.shape[2:]
    nb = seq // CMP_STRIDE
    lo = _dot(pe_ref[...], w1_ref[...])[0:1]
    hi = jnp.zeros((nb, w1_ref.shape[1]), F32)
    for l in range(CMP_STRIDE):
        xl = x_ref[0, 0, pl.ds(l, nb, stride=CMP_STRIDE), :].astype(BF16)
        lo = lo + _dot(xl, w1_ref[l * d:(l + 1) * d])
        hi = hi + _dot(xl, w1_ref[(CMP_STRIDE + l) * d:(CMP_STRIDE + l + 1) * d])
    row = lax.broadcasted_iota(jnp.int32, hi.shape, 0)
    pre = lo + jnp.where(row < nb - 1, pltpu.roll(hi, nb - 1, axis=0), 0.0)
    out = _dot(_gelu_tanh(pre).astype(BF16), w2_ref[...])
    if use_rope:
        out = out * cos_ref[0] + pltpu.roll(out, NSA_DIM // 2, axis=1) * sin_ref[0]
    o_ref[0, 0] = out.astype(BF16)


def _compress(kv, pe, w1, w2, cos_end, sin_end, use_rope):
    bsz, groups, seq, d = kv.shape
    nb = seq // CMP_STRIDE
    pe_rows = jnp.broadcast_to(pe.reshape(1, CMP_LEN * d), (SUBLANES, CMP_LEN * d)).astype(BF16)
    w1b, w2b = w1.astype(BF16), w2.astype(BF16)
    full = lambda a: pl.BlockSpec(a.shape, lambda b, g: (0,) * a.ndim)
    end = pl.BlockSpec((1, nb, d), lambda b, g: (b, 0, 0))
    return pl.pallas_call(
        functools.partial(_compress_kernel, use_rope=use_rope),
        out_shape=jax.ShapeDtypeStruct((bsz, groups, nb, d), BF16),
        grid=(bsz, groups),
        in_specs=[pl.BlockSpec((1, 1, seq, d), lambda b, g: (b, g, 0, 0)),
                  full(pe_rows), full(w1b), full(w2b), end, end],
        out_specs=pl.BlockSpec((1, 1, nb, d), lambda b, g: (b, g, 0, 0)),
        compiler_params=_params(("parallel", "parallel")),
        name="nsa_compress",
    )(kv, pe_rows, w1b, w2b, cos_end, sin_end)


def _cmp_sel_kernel(q_ref, kc_ref, vc_ref, pool_ref, o_ref, sel_ref, **static):
    for g in range(q_ref.shape[1]):
        _cmp_sel_group(g, q_ref, kc_ref, vc_ref, pool_ref, o_ref, sel_ref, **static)


def _cmp_sel_group(g, q_ref, kc_ref, vc_ref, pool_ref, o_ref, sel_ref, *, n_cmp, n_sel, n_top):
    qi = pl.program_id(1)
    rep, tq, d = q_ref.shape[2:]
    nb = kc_ref.shape[2]
    q = q_ref[0, g].reshape(rep * tq, d)
    s = _dot_nt(q, kc_ref[0, g]).reshape(rep, tq, nb)
    qpos = qi * tq + lax.broadcasted_iota(jnp.int32, (tq, nb), 0)
    blk = lax.broadcasted_iota(jnp.int32, (tq, nb), 1)
    mask = ((blk * CMP_STRIDE + (CMP_LEN - 1) <= qpos) & (blk < n_cmp))[None]
    s = jnp.where(mask, s, NEG_INF)
    m = jnp.max(s, axis=-1, keepdims=True)
    e = jnp.where(mask, jnp.exp2(s - m), 0.0)
    p = e / jnp.maximum(jnp.sum(e, axis=-1, keepdims=True), TINY)
    o = _dot(p.reshape(rep * tq, nb).astype(BF16), vc_ref[0, g])
    o_ref[0, g] = o.reshape(rep, tq, d).astype(o_ref.dtype)

    rows = pool_ref.shape[0]
    imp = _dot_nt(pool_ref[...], jnp.sum(p, axis=0), precision=HIGHEST)[:n_sel]
    bid = lax.broadcasted_iota(jnp.int32, (n_sel, tq), 0)
    cur = (qi * tq + lax.broadcasted_iota(jnp.int32, (n_sel, tq), 1)) // SEL_BLOCK
    forced = (bid == 0) | (bid == cur) | (bid == cur - 1)
    imp = jnp.where(forced, FORCE_SCORE, jnp.where(bid <= cur, imp, -1.0))
    groups = [imp[g:g + SUBLANES] for g in range(0, n_sel, SUBLANES)]
    sub = lax.broadcasted_iota(jnp.int32, (SUBLANES, tq), 0)
    ranks = [jnp.zeros((SUBLANES, tq), F32) for _ in groups]
    for j in range(n_sel):
        vj = jnp.broadcast_to(imp[j:j + 1, :], (SUBLANES, tq))
        for gi, grp in enumerate(groups):
            lo = gi * SUBLANES
            if lo > j:
                first = vj >= grp
            elif lo + SUBLANES - 1 <= j:
                first = vj > grp
            else:
                first = (vj > grp) | ((vj == grp) & (sub > j - lo))
            ranks[gi] = ranks[gi] + jnp.where(first, 1.0, 0.0)
    rank = jnp.concatenate(ranks, axis=0)
    bias = jnp.where((rank < n_top) & (bid <= cur), 0.0, NEG_INF)
    bias = jnp.concatenate([bias, jnp.full((rows - n_sel, tq), NEG_INF, F32)], axis=0)
    sel_ref[0, g] = bias.T.astype(sel_ref.dtype)


def _cmp_sel(q, kc, vc):
    bsz, groups, rep, seq, d = q.shape
    nb = kc.shape[2]
    tq = NSA_TQ
    n_cmp = (seq - CMP_LEN) // CMP_STRIDE + 1
    n_sel = seq // SEL_BLOCK
    n_top = min(SEL_TOP, n_sel)
    ratio = SEL_BLOCK // CMP_STRIDE
    assert n_sel <= LANES and n_sel * ratio == nb
    pool_np = np.zeros((LANES, nb), np.float32)
    pool_np[np.arange(nb) // ratio, np.arange(nb)] = 1.0
    pool = jnp.asarray(pool_np)
    kv_spec = pl.BlockSpec((1, groups, nb, d), lambda b, i: (b, 0, 0, 0))
    q_spec = pl.BlockSpec((1, groups, rep, tq, d), lambda b, i: (b, 0, 0, i, 0))
    return pl.pallas_call(
        functools.partial(_cmp_sel_kernel, n_cmp=n_cmp, n_sel=n_sel, n_top=n_top),
        out_shape=(jax.ShapeDtypeStruct(q.shape, BF16),
                   jax.ShapeDtypeStruct((bsz, groups, seq, LANES), BF16)),
        grid=(bsz, seq // tq),
        in_specs=[q_spec, kv_spec, kv_spec, pl.BlockSpec(pool.shape, lambda b, i: (0, 0))],
        out_specs=(q_spec, pl.BlockSpec((1, groups, tq, LANES), lambda b, i: (b, 0, i, 0))),
        compiler_params=_params(("parallel", "parallel")),
        name="nsa_cmp_select",
    )(q, kc, vc, pool)


def _sel_win_kernel(q_ref, bias_ref, k_ref, blk_ref, v_ref, kw_ref, vw_ref, o_ref, ow_ref,
                    m_ref, acc_ref, sa_ref, sb_ref, *, span):
    qi = pl.program_id(2)
    rep, tq, d = q_ref.shape[2:]
    tk = NSA_TK
    part = rep // SEL_CHAINS
    q_ext = [jnp.concatenate([q_ref[0, 0, c * part:(c + 1) * part].reshape(part * tq, d),
                              jnp.concatenate([bias_ref[0, 0]] * part, axis=0)], axis=1)
             for c in range(SEL_CHAINS)]
    _softmax_init(m_ref, acc_ref)
    n_full = qi * tq // tk

    owns = [slice(c * part * tq, (c + 1) * part * tq) for c in range(SEL_CHAINS)]

    def scores(ki, s_ref):
        rows = pl.ds(pl.multiple_of(ki * tk, tk), tk)
        k_ext = jnp.concatenate([k_ref[0, 0, rows, :], blk_ref[rows, :]], axis=1)
        for c, own in enumerate(owns):
            s_ref[own] = _dot_nt(q_ext[c], k_ext)

    def update(s_ref, ki, mask):
        rows = pl.ds(pl.multiple_of(ki * tk, tk), tk)
        v_ext = _with_ones(v_ref[0, 0, rows, :])
        for own in owns:
            _softmax_tile(s_ref[own], mask, v_ext, m_ref.at[own], acc_ref.at[own])

    def window():
        start = pl.multiple_of(jnp.maximum(qi * tq + tq - span, 0), tq)
        rows = pl.ds(start, span)
        wq = qi * tq + lax.broadcasted_iota(jnp.int32, (1, tq, span), 1)
        wk = start + lax.broadcasted_iota(jnp.int32, (1, tq, span), 2)
        mask = (wq - wk >= 0) & (wq - wk < WINDOW)
        k, v_ext = kw_ref[0, 0, rows, :], _with_ones(vw_ref[0, 0, rows, :])
        wpart = rep // WIN_CHAINS
        for h0 in range(0, rep, wpart):
            q = q_ref[0, 0, h0:h0 + wpart].reshape(wpart * tq, d)
            s = jnp.where(mask, _dot_nt(q, k).reshape(wpart, tq, span), NEG_INF)
            m = jnp.max(s, axis=-1, keepdims=True)
            e = jnp.where(mask, jnp.exp2(s - m), 0.0).reshape(wpart * tq, span)
            acc = _dot(e.astype(BF16), v_ext)
            o = acc[:, :LANES] / jnp.maximum(acc[:, LANES:], TINY)
            ow_ref[0, 0, h0:h0 + wpart] = o.reshape(wpart, tq, d).astype(ow_ref.dtype)

    qpos = qi * tq + lax.broadcasted_iota(jnp.int32, (1, tq, tk), 1)
    kpos = n_full * tk + lax.broadcasted_iota(jnp.int32, (1, tq, tk), 2)
    _pipelined_causal_tiles(n_full, scores, update, sa_ref, sb_ref, kpos <= qpos, filler=window)
    o_ref[0, 0] = _softmax_finish(acc_ref).reshape(rep, tq, d).astype(o_ref.dtype)


def _sel_win_attention(q, bias, k, v, kw, vw):
    bsz, groups, rep, seq, d = q.shape
    tq = NSA_TQ
    assert NSA_TK % tq == 0
    span = (-(-(WINDOW - 1) // tq) + 1) * tq
    assert span <= seq
    onehot_np = np.zeros((seq, LANES), np.float32)
    onehot_np[np.arange(seq), np.arange(seq) // SEL_BLOCK] = 1.0
    onehot = jnp.asarray(onehot_np, BF16)
    q_spec = pl.BlockSpec((1, 1, rep, tq, d), lambda b, g, qi: (b, g, 0, qi, 0))
    kv_spec = pl.BlockSpec((1, 1, seq, d), lambda b, g, qi: (b, g, 0, 0))
    out = jax.ShapeDtypeStruct(q.shape, BF16)
    return pl.pallas_call(
        functools.partial(_sel_win_kernel, span=span),
        out_shape=(out, out),
        grid=(bsz, groups, seq // tq),
        in_specs=[q_spec, pl.BlockSpec((1, 1, tq, LANES), lambda b, g, qi: (b, g, qi, 0)),
                  kv_spec, pl.BlockSpec((seq, LANES), lambda b, g, qi: (0, 0)), kv_spec, kv_spec, kv_spec],
        out_specs=(q_spec, q_spec),
        scratch_shapes=[pltpu.VMEM((rep * tq, LANES), F32), pltpu.VMEM((rep * tq, 2 * LANES), F32),
                        pltpu.VMEM((rep * tq, NSA_TK), F32), pltpu.VMEM((rep * tq, NSA_TK), F32)],
        compiler_params=_params(("parallel", "parallel", "arbitrary")),
        name="nsa_selected_window_attention",
    )(q, bias, k, onehot, v, kw, vw)


def _od_out_kernel(oc_ref, os_ref, ow_ref, gate_ref, z_ref, x_ref, mod_ref, g_ref, w_ref, out_ref):
    d = x_ref.shape[-1]
    gates = gate_ref[0]
    z = z_ref[0].astype(F32)
    parts = []
    for hd in range(NSA_HEADS):
        g, r = hd // NSA_REP, hd % NSA_REP
        o = (gates[:, 3 * hd:3 * hd + 1] * oc_ref[0, g, r].astype(F32)
             + gates[:, 3 * hd + 1:3 * hd + 2] * os_ref[0, g, r].astype(F32)
             + gates[:, 3 * hd + 2:3 * hd + 3] * ow_ref[0, g, r].astype(F32))
        parts.append((o * _silu(z[:, hd * NSA_DIM:(hd + 1) * NSA_DIM])).astype(BF16))
    y = _dot(jnp.concatenate(parts, axis=-1), w_ref[...])
    gate = mod_ref[0][:, 2 * d:]
    out_ref[0] = x_ref[0] + gate * _rms(y, g_ref[...])


def _od_out(o_cmp, o_sel, o_win, gates, z, x, mod, post_g, w_out):
    bsz, seq, d = x.shape
    tm = TOKEN_TILE
    w = w_out.astype(BF16)
    g = post_g[None]
    tok = lambda n: pl.BlockSpec((1, tm, n), lambda b, i: (b, i, 0))
    full = lambda a: pl.BlockSpec(a.shape, lambda b, i: (0,) * a.ndim)
    o_spec = pl.BlockSpec((1, NSA_GROUPS, NSA_REP, tm, NSA_DIM), lambda b, i: (b, 0, 0, i, 0))
    return pl.pallas_call(
        _od_out_kernel,
        out_shape=jax.ShapeDtypeStruct(x.shape, F32),
        grid=(bsz, seq // tm),
        in_specs=[o_spec, o_spec, o_spec, tok(OD_GATE), tok(OD_Z), tok(d),
                  pl.BlockSpec((1, 1, 3 * d), lambda b, i: (b, 0, 0)), full(g), full(w)],
        out_specs=tok(d),
        compiler_params=_params(("parallel", "parallel")),
        name="odd_out_proj",
    )(o_cmp, o_sel, o_win, gates, z, x, mod[:, None, :], g, w)


def _even_layer(x, mod, pre_g, post_g, tabs, w_in, lam_re, lam_im, log_dt, b_re, b_im, c_re, c_im,
                d_skip, w_glu, b_glu, q_norm_g, kv_norm_g, w_uq, w_ukv, w_out):
    cos_m, sin_m = tabs[0], tabs[1]
    u, z_a, z_b, q, k, v = _ev_in(x, mod, pre_g, w_in, q_norm_g, kv_norm_g, w_uq, w_ukv, cos_m, sin_m)
    y_a = _s5(u, z_a, lam_re, lam_im, log_dt, b_re, b_im, c_re, c_im, d_skip, w_glu, b_glu)
    o_mla = _mla(q, k, v)
    return _ev_out(y_a, o_mla, z_b, x, mod, post_g, w_out)


def _odd_layer(x, mod, pre_g, post_g, tabs, end_tabs, w_in, k_pe, k_w1, k_w2, v_pe, v_w1, v_w2, w_out):
    cos_n, sin_n = tabs[2], tabs[3]
    cos_end, sin_end = end_tabs[2], end_tabs[3]
    q, k_c, v_c, k_s, v_s, k_w, v_w, gates, z = _od_in(x, mod, pre_g, w_in, cos_n, sin_n)
    kc = _compress(k_c, k_pe, k_w1, k_w2, cos_end, sin_end, True)
    vc = _compress(v_c, v_pe, v_w1, v_w2, cos_end, sin_end, False)
    o_cmp, sel = _cmp_sel(q, kc, vc)
    o_sel, o_win = _sel_win_attention(q, sel, k_s, v_s, k_w, v_w)
    return _od_out(o_cmp, o_sel, o_win, gates, z, x, mod, post_g, w_out)


def kernel(x, c, positions, pre_norm_g, post_norm_g, w_ada, b_ada, ev_w_in, ev_lam_re, ev_lam_im, ev_log_dt, ev_b_re, ev_b_im, ev_c_re, ev_c_im, ev_d_skip, ev_w_glu, ev_b_glu, ev_q_norm_g, ev_kv_norm_g, ev_w_uq, ev_w_ukv, ev_w_out, od_w_in, od_cmp_k_pe, od_cmp_k_w1, od_cmp_k_w2, od_cmp_v_pe, od_cmp_v_w1, od_cmp_v_w2, od_w_out):
    depth = pre_norm_g.shape[0]
    tabs = _rope_tables(positions)
    seq = positions.shape[1]
    pos_end = positions[:, CMP_LEN - 1::CMP_STRIDE]
    pos_end = jnp.pad(pos_end, ((0, 0), (0, seq // CMP_STRIDE - pos_end.shape[1])))
    end_tabs = _rope_tables(pos_end)
    mods = _modulation(c, w_ada, b_ada)
    for layer in range(depth):
        i = layer // 2
        if layer % 2 == 0:
            x = _even_layer(x, mods[layer], pre_norm_g[layer], post_norm_g[layer], tabs,
                            ev_w_in[i], ev_lam_re[i], ev_lam_im[i], ev_log_dt[i], ev_b_re[i], ev_b_im[i],
                            ev_c_re[i], ev_c_im[i], ev_d_skip[i], ev_w_glu[i], ev_b_glu[i],
                            ev_q_norm_g[i], ev_kv_norm_g[i], ev_w_uq[i], ev_w_ukv[i], ev_w_out[i])
        else:
            x = _odd_layer(x, mods[layer], pre_norm_g[layer], post_norm_g[layer], tabs, end_tabs,
                           od_w_in[i], od_cmp_k_pe[i], od_cmp_k_w1[i], od_cmp_k_w2[i],
                           od_cmp_v_pe[i], od_cmp_v_w1[i], od_cmp_v_w2[i], od_w_out[i])
    return x
```

```python
import functools
import math

import numpy as np
import jax
import jax.numpy as jnp
from jax import lax
from jax.experimental import pallas as pl
from jax.experimental.pallas import tpu as pltpu

F32 = jnp.float32
BF16 = jnp.bfloat16
HIGHEST = lax.Precision.HIGHEST

EPS = 1e-6
ROPE_THETA = 10000.0
NEG_INF = -1e30
TINY = 1e-30
FORCE_SCORE = 1e9
LOG2E = 1.0 / math.log(2.0)

S5_GROUP = 16
S5_STATE = 64
MLA_HEADS = 4
MLA_NOPE = 128
MLA_ROPE = 64
MLA_V = 128
NSA_HEADS = 8
NSA_GROUPS = 2
NSA_REP = NSA_HEADS // NSA_GROUPS
NSA_DIM = 128
CMP_LEN = 32
CMP_STRIDE = 16
CMP_HIDDEN = 256
SEL_BLOCK = 64
SEL_TOP = 16
WINDOW = 512

LANES = 128
SUBLANES = 8
VMEM_LIMIT = 48 * 1024 * 1024

TOKEN_TILE = 1024
S5_CHUNK = 256
S5_COLS = 512
ATT_TQ = 512
ATT_TK = 512
NSA_TQ = 256
NSA_TK = 512
SEL_CHAINS = 4
WIN_CHAINS = 4


def _params(sem):
    return pltpu.CompilerParams(dimension_semantics=sem, vmem_limit_bytes=VMEM_LIMIT)


def _dot(a, b, precision=None):
    return jnp.dot(a, b, preferred_element_type=F32, precision=precision)


def _dot_nt(a, b, precision=None):
    return lax.dot_general(a, b, (((1,), (1,)), ((), ())), preferred_element_type=F32,
                           precision=precision)


def _silu(v):
    return v * jax.nn.sigmoid(v)


def _gelu_tanh(v):
    return 0.5 * v * (1.0 + jnp.tanh(math.sqrt(2.0 / math.pi) * (v + 0.044715 * (v * v * v))))


def _rms(v, g):
    return v * lax.rsqrt(jnp.mean(v * v, axis=-1, keepdims=True) + EPS) * g


def _rope_tab_kernel(pos_ref, f_ref, sgm_ref, sgn_ref, cm_ref, sm_ref, cn_ref, sn_ref):
    hn, hm = NSA_DIM // 2, MLA_ROPE // 2
    ang = pos_ref[0] * f_ref[...]
    c, s = jnp.cos(ang), jnp.sin(ang)
    cn_ref[0] = jnp.concatenate([c[:, :hn]] * (LANES // hn), axis=1)
    sn_ref[0] = jnp.concatenate([s[:, :hn]] * (LANES // hn), axis=1) * sgn_ref[...]
    cm_ref[0] = jnp.concatenate([c[:, hn:hn + hm]] * (LANES // hm), axis=1)
    sm_ref[0] = jnp.concatenate([s[:, hn:hn + hm]] * (LANES // hm), axis=1) * sgm_ref[...]


def _rope_tables(positions):
    bsz, seq = positions.shape
    ts = min(TOKEN_TILE, seq)
    pos = positions.astype(F32)[..., None]
    half_m, half_n = MLA_ROPE // 2, NSA_DIM // 2
    fm = ROPE_THETA ** (-jnp.arange(half_m, dtype=F32) / half_m)
    fn = ROPE_THETA ** (-jnp.arange(half_n, dtype=F32) / half_n)
    freqs = jnp.concatenate([fn, fm, jnp.zeros((LANES - half_n - half_m,), F32)])[None]
    sgm = jnp.tile(jnp.concatenate([-jnp.ones((half_m,), F32), jnp.ones((half_m,), F32)]),
                   LANES // MLA_ROPE)[None]
    sgn = jnp.concatenate([-jnp.ones((half_n,), F32), jnp.ones((half_n,), F32)])[None]
    row = pl.BlockSpec((1, LANES), lambda b, i: (0, 0))
    tab = pl.BlockSpec((1, ts, LANES), lambda b, i: (b, i, 0))
    shp = jax.ShapeDtypeStruct((bsz, seq, LANES), F32)
    return pl.pallas_call(
        _rope_tab_kernel,
        out_shape=(shp, shp, shp, shp),
        grid=(bsz, seq // ts),
        in_specs=[pl.BlockSpec((1, ts, 1), lambda b, i: (b, i, 0)), row, row, row],
        out_specs=(tab, tab, tab, tab),
        compiler_params=_params(("parallel", "parallel")),
        name="rope_tables",
    )(pos, freqs, sgm, sgn)


def _mod_kernel(c_ref, w_ref, b_ref, o_ref):
    s = _silu(c_ref[...])
    o_ref[0] = _dot(s, w_ref[0], precision=HIGHEST) + b_ref[0]


def _modulation(c, w_ada, b_ada):
    depth, d, _ = w_ada.shape
    bsz = c.shape[0]
    rows = -(-bsz // SUBLANES) * SUBLANES
    c_pad = jnp.pad(c, ((0, rows - bsz), (0, 0)))
    out = pl.pallas_call(
        _mod_kernel,
        out_shape=jax.ShapeDtypeStruct((depth, rows, 3 * d), F32),
        grid=(depth, 3),
        in_specs=[pl.BlockSpec((rows, d), lambda l, j: (0, 0)),
                  pl.BlockSpec((1, d, d), lambda l, j: (l, 0, j)),
                  pl.BlockSpec((1, 1, d), lambda l, j: (l, 0, j))],
        out_specs=pl.BlockSpec((1, rows, d), lambda l, j: (l, 0, j)),
        compiler_params=_params(("parallel", "parallel")),
        name="modulation",
    )(c_pad, w_ada, b_ada[:, None, :])
    return out[:, :bsz]


def _modulated_input(x_ref, mod_ref, g_ref):
    d = x_ref.shape[-1]
    x = x_ref[0]
    mod = mod_ref[0]
    shift, scale = mod[:, :d], mod[:, d:2 * d]
    return (_rms(x, g_ref[...]) * (1.0 + scale) + shift).astype(BF16)


EV_U, EV_ZA, EV_CQ, EV_CKV, EV_KPE, EV_ZB = 512, 512, 768, 256, 256, 512
EV_OFF = np.cumsum([0, EV_U, EV_ZA, EV_CQ, EV_CKV, EV_KPE, EV_ZB])


def _ev_in_kernel(x_ref, mod_ref, g_ref, w_ref, gq_ref, gkv_ref, wuq_ref, wukv_ref, cm_ref, sm_ref,
                  u_ref, za_ref, zb_ref, q_ref, k_ref, v_ref):
    h = _modulated_input(x_ref, mod_ref, g_ref)
    p = _dot(h, w_ref[...])
    o = EV_OFF
    u_ref[0] = p[:, o[0]:o[1]].astype(BF16)
    za_ref[0] = p[:, o[1]:o[2]].astype(BF16)
    zb_ref[0] = p[:, o[5]:o[6]].astype(BF16)
    cos, sin = cm_ref[0], sm_ref[0]

    cq = _rms(p[:, o[2]:o[3]], gq_ref[...]).astype(BF16)
    q = _dot(cq, wuq_ref[...]) * ((MLA_NOPE + MLA_ROPE) ** -0.5 * LOG2E)
    ckv = _rms(p[:, o[3]:o[4]], gkv_ref[...]).astype(BF16)
    kv = _dot(ckv, wukv_ref[...])
    kpe = (p[:, o[4]:o[4] + LANES] * cos + p[:, o[4] + LANES:o[5]] * sin).astype(BF16)
    nope = MLA_HEADS * MLA_NOPE
    for hd in range(MLA_HEADS):
        b0 = hd * (MLA_NOPE + 2 * LANES)
        qpe = q[:, b0 + MLA_NOPE:b0 + MLA_NOPE + LANES] * cos + q[:, b0 + MLA_NOPE + LANES:b0 + MLA_NOPE + 2 * LANES] * sin
        q_ref[0, hd] = jnp.concatenate([q[:, b0:b0 + MLA_NOPE], qpe], axis=-1).astype(BF16)
        k_ref[0, hd] = jnp.concatenate([kv[:, hd * MLA_NOPE:(hd + 1) * MLA_NOPE].astype(BF16), kpe], axis=-1)
        v_ref[0, hd] = kv[:, nope + hd * MLA_V:nope + (hd + 1) * MLA_V].astype(BF16)


def _select_columns(w, cols):
    n = w.shape[1]
    cols = [int(c) for c in cols]
    pieces, i = [], 0
    while i < len(cols):
        j = i + 1
        if cols[i] == n:
            while j < len(cols) and cols[j] == n:
                j += 1
            pieces.append(jnp.zeros((w.shape[0], j - i), w.dtype))
        else:
            while j < len(cols) and cols[j] == cols[j - 1] + 1 and cols[j] != n:
                j += 1
            pieces.append(w[:, cols[i]:cols[j - 1] + 1])
        i = j
    return jnp.concatenate(pieces, axis=1)


def _ev_in(x, mod, pre_g, w_in, q_norm_g, kv_norm_g, w_uq, w_ukv, cos_m, sin_m):
    bsz, seq, d = x.shape
    tm = TOKEN_TILE
    hq = MLA_ROPE // 2
    pad = LANES - MLA_ROPE

    def rot_cols(base, zero):
        x1, x2, z = base + np.arange(hq), base + hq + np.arange(hq), np.full(pad, zero)
        return np.concatenate([x1, x2, z, x2, x1, z])

    c0 = np.cumsum([0, 512, 512, 768, 256, 64, 512])
    cols = np.concatenate([np.arange(c0[0], c0[4]), rot_cols(c0[4], c0[6]), np.arange(c0[5], c0[6])])
    w = _select_columns(w_in, cols).astype(BF16)
    per = MLA_NOPE + MLA_ROPE
    q_cols = np.concatenate([np.concatenate([hd * per + np.arange(MLA_NOPE),
                                             rot_cols(hd * per + MLA_NOPE, MLA_HEADS * per)])
                             for hd in range(MLA_HEADS)])
    wuq = _select_columns(w_uq, q_cols).astype(BF16)
    per = MLA_NOPE + MLA_V
    heads = np.arange(MLA_HEADS)[:, None] * per
    kv_cols = np.concatenate([(heads + np.arange(MLA_NOPE)).ravel(),
                              (heads + MLA_NOPE + np.arange(MLA_V)).ravel()])
    wukv = _select_columns(w_ukv, kv_cols).astype(BF16)

    tok = lambda n: pl.BlockSpec((1, tm, n), lambda b, i: (b, i, 0))
    full = lambda a: pl.BlockSpec(a.shape, lambda b, i: (0,) * a.ndim)
    out = lambda n: jax.ShapeDtypeStruct((bsz, seq, n), BF16)
    head = lambda n: pl.BlockSpec((1, MLA_HEADS, tm, n), lambda b, i: (b, 0, i, 0))
    head_out = lambda n: jax.ShapeDtypeStruct((bsz, MLA_HEADS, seq, n), BF16)
    gq, gkv, g = q_norm_g[None], kv_norm_g[None], pre_g[None]
    return pl.pallas_call(
        _ev_in_kernel,
        out_shape=(out(512), out(512), out(512), head_out(2 * LANES), head_out(2 * LANES), head_out(MLA_V)),
        grid=(bsz, seq // tm),
        in_specs=[tok(d), pl.BlockSpec((1, 1, 3 * d), lambda b, i: (b, 0, 0)), full(g), full(w),
                  full(gq), full(gkv), full(wuq), full(wukv), tok(LANES), tok(LANES)],
        out_specs=(tok(512), tok(512), tok(512), head(2 * LANES), head(2 * LANES), head(MLA_V)),
        compiler_params=_params(("parallel", "parallel")),
        name="even_in_proj",
    )(x, mod[:, None, :], g, w, gq, gkv, wuq, wukv, cos_m, sin_m)


def _s5_kernel(u_ref, z_ref, perm_ref, permt_ref, wb_ref, wc_ref, a_ref, at_ref, pow_ref,
               d_ref, wglu_ref, bglu_ref, o_ref, bu_ref, xb_ref, state_ref, carry_ref):
    t_len = u_ref.shape[1]
    n = a_ref.shape[1] // 2
    steps = t_len // SUBLANES

    @pl.when(pl.program_id(1) == 0)
    def _():
        state_ref[...] = jnp.zeros_like(state_ref)

    perm = perm_ref[...]
    u_p = _dot(perm, u_ref[0])
    z_p = _dot(perm, z_ref[0])
    u_pb = u_p.astype(BF16)
    bre, bim = pl.ds(0, S5_COLS), pl.ds(S5_COLS, S5_COLS)
    y_parts = []

    for cb in range(n // S5_COLS):
        u_cb = u_pb[:, cb * LANES:(cb + 1) * LANES]
        bu_ref[...] = _dot(u_cb, wb_ref[cb])
        re = pl.ds(cb * S5_COLS, S5_COLS)
        im = pl.ds(n + cb * S5_COLS, S5_COLS)
        ar = jnp.broadcast_to(a_ref[:, re], (SUBLANES, S5_COLS))
        ai = jnp.broadcast_to(a_ref[:, im], (SUBLANES, S5_COLS))

        def step(t, carry):
            xr, xi = carry
            rows = pl.ds(pl.multiple_of(t * SUBLANES, SUBLANES), SUBLANES)
            nr = ar * xr - ai * xi + bu_ref[rows, bre]
            ni = ar * xi + ai * xr + bu_ref[rows, bim]
            bu_ref[rows, bre] = nr
            bu_ref[rows, bim] = ni
            return nr, ni

        zero = jnp.zeros((SUBLANES, S5_COLS), F32)
        er, ei = lax.fori_loop(0, steps, step, (zero, zero), unroll=True)

        sr, si = state_ref[:, re], state_ref[:, im]
        tr, ti = at_ref[:, re], at_ref[:, im]
        for j in range(SUBLANES):
            carry_ref[j:j + 1, re] = sr
            carry_ref[j:j + 1, im] = si
            sr, si = (tr * sr - ti * si + er[j:j + 1], tr * si + ti * sr + ei[j:j + 1])
        state_ref[:, re] = sr
        state_ref[:, im] = si
        cr, ci = carry_ref[:, re], carry_ref[:, im]

        def fix(t2, _):
            xr, xi = [], []
            for k in range(2):
                t = 2 * t2 + k
                rows = pl.ds(pl.multiple_of(t * SUBLANES, SUBLANES), SUBLANES)
                pr, pi = pow_ref[pl.ds(t, 1), re], pow_ref[pl.ds(t, 1), im]
                xr.append(bu_ref[rows, bre] + pr * cr - pi * ci)
                xi.append(bu_ref[rows, bim] + pr * ci + pi * cr)
            rows = pl.ds(pl.multiple_of(t2 * 2 * SUBLANES, 2 * SUBLANES), 2 * SUBLANES)
            xb_ref[rows, bre] = jnp.concatenate(xr, axis=0).astype(BF16)
            xb_ref[rows, bim] = jnp.concatenate(xi, axis=0).astype(BF16)
            return 0

        lax.fori_loop(0, steps // 2, fix, 0, unroll=True)
        y_parts.append(_dot(xb_ref[...], wc_ref[cb]))

    y = jnp.concatenate(y_parts, axis=1) + d_ref[...] * u_p
    g = _gelu_tanh(y)
    gate = jax.nn.sigmoid(_dot(g.astype(BF16), wglu_ref[...]) + bglu_ref[...])
    out = (g * gate * _silu(z_p)).astype(BF16)
    o_ref[0] = _dot(permt_ref[...], out).astype(BF16)


def _s5(u, z_a, lam_re, lam_im, log_dt, b_re, b_im, c_re, c_im, d_skip, w_glu, b_glu):
    bsz, seq, width = u.shape
    groups, state = lam_re.shape
    t_len = S5_CHUNK
    steps = t_len // SUBLANES
    n = groups * state
    dt = jnp.exp(log_dt)[:, None]
    lam_dt_re, lam_dt_im = lam_re * dt, lam_im * dt
    decay = jnp.exp(lam_dt_re)
    ab_re, ab_im = decay * jnp.cos(lam_dt_im), decay * jnp.sin(lam_dt_im)
    den = lam_re * lam_re + lam_im * lam_im
    nr, ni = ab_re - 1.0, ab_im
    f_re = (nr * lam_re + ni * lam_im) / den
    f_im = (ni * lam_re - nr * lam_im) / den
    bb_re = f_re[..., None] * b_re - f_im[..., None] * b_im
    bb_im = f_re[..., None] * b_im + f_im[..., None] * b_re
    gpb = S5_COLS // state
    nblk = groups // gpb
    hdim = width // groups
    assert gpb * hdim == LANES and nblk * S5_COLS == n
    eye = jnp.eye(gpb, dtype=F32)

    def in_blocks(bb):
        bb = bb.reshape(nblk, gpb, state, hdim)
        return jnp.einsum('mgph,gk->mghkp', bb, eye).reshape(nblk, gpb * hdim, gpb * state)

    def out_blocks(cc):
        cc = cc.reshape(nblk, gpb, hdim, state)
        return jnp.einsum('mghp,gk->mgpkh', cc, eye).reshape(nblk, gpb * state, gpb * hdim)

    wb = jnp.concatenate([in_blocks(bb_re), in_blocks(bb_im)], axis=2).astype(BF16)
    wc = jnp.concatenate([out_blocks(c_re), out_blocks(-c_im)], axis=1).astype(BF16)
    a_vec = jnp.concatenate([ab_re.reshape(1, n), ab_im.reshape(1, n)], axis=1)
    ks = jnp.arange(1, steps + 1, dtype=F32)[:, None, None]
    pw_mag = jnp.exp(lam_dt_re[None] * ks)
    pw_re, pw_im = pw_mag * jnp.cos(lam_dt_im[None] * ks), pw_mag * jnp.sin(lam_dt_im[None] * ks)
    pow_tab = jnp.concatenate([pw_re.reshape(steps, n), pw_im.reshape(steps, n)], axis=1)
    at_vec = pow_tab[steps - 1:steps]
    r = np.arange(t_len)
    perm_np = np.zeros((t_len, t_len), np.float32)
    perm_np[r, (r % SUBLANES) * steps + r // SUBLANES] = 1.0
    perm = jnp.asarray(perm_np, BF16)
    permt = jnp.asarray(perm_np.T, BF16)
    d_vec = d_skip.reshape(1, width)
    wglu = w_glu.astype(BF16)
    bglu = b_glu[None]

    tok = pl.BlockSpec((1, t_len, width), lambda b, i: (b, i, 0))
    full = lambda a: pl.BlockSpec(a.shape, lambda b, i: (0,) * a.ndim)
    consts = (perm, permt, wb, wc, a_vec, at_vec, pow_tab, d_vec, wglu, bglu)
    return pl.pallas_call(
        _s5_kernel,
        out_shape=jax.ShapeDtypeStruct((bsz, seq, width), BF16),
        grid=(bsz, seq // t_len),
        in_specs=[tok, tok] + [full(a) for a in consts],
        out_specs=tok,
        scratch_shapes=[pltpu.VMEM((t_len, 2 * S5_COLS), F32), pltpu.VMEM((t_len, 2 * S5_COLS), BF16),
                        pltpu.VMEM((1, 2 * n), F32), pltpu.VMEM((SUBLANES, 2 * n), F32)],
        compiler_params=_params(("parallel", "arbitrary")),
        name="s5_mixer",
    )(u, z_a, *consts)


def _softmax_init(m_ref, acc_ref):
    m_ref[...] = jnp.full_like(m_ref, NEG_INF)
    acc_ref[...] = jnp.zeros_like(acc_ref)


def _softmax_tile(s, mask, v_ext, m_ref, acc_ref):
    rows, tk = s.shape
    if mask is not None:
        heads = rows // mask.shape[1]
        s = jnp.where(mask, s.reshape(heads, *mask.shape[1:]), NEG_INF).reshape(rows, tk)
    m_old = m_ref[...]
    m_new = jnp.maximum(m_old, jnp.max(s, axis=-1, keepdims=True))
    e = jnp.exp2(s - jnp.concatenate([m_new] * (tk // LANES), axis=1))
    if mask is not None:
        e = jnp.where(mask, e.reshape(heads, *mask.shape[1:]), 0.0).reshape(rows, tk)
    alpha = jnp.exp2(m_old - m_new)
    acc_ref[...] = (jnp.concatenate([alpha, alpha], axis=1) * acc_ref[...]
                    + _dot(e.astype(BF16), v_ext))
    m_ref[...] = m_new


def _softmax_finish(acc_ref):
    acc = acc_ref[...]
    return acc[:, :LANES] / jnp.maximum(acc[:, LANES:], TINY)


def _with_ones(v):
    return jnp.concatenate([v, jnp.ones(v.shape, v.dtype)], axis=1)


def _pipelined_causal_tiles(n_full, scores, update, sa_ref, sb_ref, causal, filler=None):
    def pair(i, carry):
        scores(2 * i + 1, sb_ref)
        update(sa_ref, 2 * i, None)
        scores(2 * i + 2, sa_ref)
        update(sb_ref, 2 * i + 1, None)
        return carry

    scores(0, sa_ref)
    if filler is not None:
        filler()
    lax.fori_loop(0, n_full // 2, pair, 0)

    @pl.when(n_full % 2 == 1)
    def _():
        scores(n_full, sb_ref)
        update(sa_ref, n_full - 1, None)
        update(sb_ref, n_full, causal)

    @pl.when(n_full % 2 == 0)
    def _():
        update(sa_ref, n_full, causal)


def _mla_kernel(q_ref, k_ref, v_ref, o_ref, m_ref, acc_ref, sa_ref, sb_ref):
    qi = pl.program_id(1)
    heads, tq = q_ref.shape[1:3]
    tk = ATT_TK
    _softmax_init(m_ref, acc_ref)
    n_full = qi * tq // tk
    owns = [slice(hd * tq, (hd + 1) * tq) for hd in range(heads)]

    def scores(ki, s_ref):
        rows = pl.ds(pl.multiple_of(ki * tk, tk), tk)
        for hd, own in enumerate(owns):
            s_ref[own] = _dot_nt(q_ref[0, hd], k_ref[0, hd, rows, :])

    def update(s_ref, ki, mask):
        rows = pl.ds(pl.multiple_of(ki * tk, tk), tk)
        for hd, own in enumerate(owns):
            _softmax_tile(s_ref[own], mask, _with_ones(v_ref[0, hd, rows, :]), m_ref.at[own], acc_ref.at[own])

    qpos = qi * tq + lax.broadcasted_iota(jnp.int32, (1, tq, tk), 1)
    kpos = n_full * tk + lax.broadcasted_iota(jnp.int32, (1, tq, tk), 2)
    _pipelined_causal_tiles(n_full, scores, update, sa_ref, sb_ref, kpos <= qpos)
    for hd in range(heads):
        o_ref[0, :, hd * MLA_V:(hd + 1) * MLA_V] = _softmax_finish(
            acc_ref.at[hd * tq:(hd + 1) * tq]).astype(o_ref.dtype)


def _mla(q, k, v):
    bsz, heads, seq, dk = q.shape
    tq = ATT_TQ
    assert tq == ATT_TK
    return pl.pallas_call(
        _mla_kernel,
        out_shape=jax.ShapeDtypeStruct((bsz, seq, heads * MLA_V), BF16),
        grid=(bsz, seq // tq),
        in_specs=[pl.BlockSpec((1, heads, tq, dk), lambda b, qi: (b, 0, qi, 0)),
                  pl.BlockSpec((1, heads, seq, dk), lambda b, qi: (b, 0, 0, 0)),
                  pl.BlockSpec((1, heads, seq, MLA_V), lambda b, qi: (b, 0, 0, 0))],
        out_specs=pl.BlockSpec((1, tq, heads * MLA_V), lambda b, qi: (b, qi, 0)),
        scratch_shapes=[pltpu.VMEM((heads * tq, LANES), F32), pltpu.VMEM((heads * tq, 2 * LANES), F32),
                        pltpu.VMEM((heads * tq, ATT_TK), F32), pltpu.VMEM((heads * tq, ATT_TK), F32)],
        compiler_params=_params(("parallel", "arbitrary")),
        name="mla_attention",
    )(q, k, v)


def _ev_out_kernel(ya_ref, o_ref, zb_ref, x_ref, mod_ref, g_ref, wa_ref, wb_ref, out_ref):
    d = x_ref.shape[-1]
    yb = (o_ref[0].astype(F32) * _silu(zb_ref[0].astype(F32))).astype(BF16)
    y = _dot(ya_ref[0], wa_ref[...]) + _dot(yb, wb_ref[...])
    gate = mod_ref[0][:, 2 * d:]
    out_ref[0] = x_ref[0] + gate * _rms(y, g_ref[...])


def _ev_out(y_a, o_mla, z_b, x, mod, post_g, w_out):
    bsz, seq, d = x.shape
    tm = TOKEN_TILE
    wa = w_out[:y_a.shape[-1]].astype(BF16)
    wb = w_out[y_a.shape[-1]:].astype(BF16)
    g = post_g[None]
    tok = lambda n: pl.BlockSpec((1, tm, n), lambda b, i: (b, i, 0))
    full = lambda a: pl.BlockSpec(a.shape, lambda b, i: (0,) * a.ndim)
    return pl.pallas_call(
        _ev_out_kernel,
        out_shape=jax.ShapeDtypeStruct(x.shape, F32),
        grid=(bsz, seq // tm),
        in_specs=[tok(y_a.shape[-1]), tok(o_mla.shape[-1]), tok(z_b.shape[-1]), tok(d),
                  pl.BlockSpec((1, 1, 3 * d), lambda b, i: (b, 0, 0)), full(g), full(wa), full(wb)],
        out_specs=tok(d),
        compiler_params=_params(("parallel", "parallel")),
        name="even_out_proj",
    )(y_a, o_mla, z_b, x, mod[:, None, :], g, wa, wb)


OD_Q, OD_KV, OD_GATE, OD_Z = 1024, 256, NSA_GROUPS * LANES, 1024
OD_OFF = np.cumsum([0, OD_Q] + [OD_KV] * 6 + [OD_GATE, OD_Z])


def _od_in_kernel(x_ref, mod_ref, g_ref, w_ref, cn_ref, sn_ref,
                  q_ref, kc_ref, vc_ref, ks_ref, vs_ref, kw_ref, vw_ref, gate_ref, z_ref):
    h = _modulated_input(x_ref, mod_ref, g_ref)
    p = _dot(h, w_ref[...])
    o = OD_OFF
    cos, sin = cn_ref[0], sn_ref[0]

    def rope(t):
        return t * cos + pltpu.roll(t, NSA_DIM // 2, axis=1) * sin

    scale = NSA_DIM ** -0.5 * LOG2E
    for hd in range(NSA_HEADS):
        t = p[:, hd * NSA_DIM:(hd + 1) * NSA_DIM]
        q_ref[0, hd // NSA_REP, hd % NSA_REP] = (rope(t) * scale).astype(BF16)
    plain = (kc_ref, vc_ref, None, vs_ref, None, vw_ref)
    roped = (None, None, ks_ref, None, kw_ref, None)
    for j in range(6):
        for g in range(NSA_GROUPS):
            lo = o[1 + j] + g * NSA_DIM
            t = p[:, lo:lo + NSA_DIM]
            if plain[j] is not None:
                plain[j][0, g] = t.astype(plain[j].dtype)
            else:
                roped[j][0, g] = rope(t).astype(BF16)
    for g in range(NSA_GROUPS):
        gate_ref[0, g] = jax.nn.sigmoid(p[:, o[7] + g * LANES:o[7] + (g + 1) * LANES])
    z_ref[0] = p[:, o[8]:o[9]].astype(BF16)


def _od_in(x, mod, pre_g, w_in, cos_n, sin_n):
    bsz, seq, d = x.shape
    tm = TOKEN_TILE
    per = 3 * NSA_REP
    c0 = OD_Q + 6 * OD_KV
    pad = jnp.zeros((d, LANES - per), F32)
    gate_cols = [piece for g in range(NSA_GROUPS)
                 for piece in (w_in[:, c0 + g * per:c0 + (g + 1) * per], pad)]
    w = jnp.concatenate([w_in[:, :c0]] + gate_cols + [w_in[:, c0 + NSA_GROUPS * per:]], axis=1).astype(BF16)
    g = pre_g[None]
    tok = lambda n: pl.BlockSpec((1, tm, n), lambda b, i: (b, i, 0))
    full = lambda a: pl.BlockSpec(a.shape, lambda b, i: (0,) * a.ndim)
    q_spec = pl.BlockSpec((1, NSA_GROUPS, NSA_REP, tm, NSA_DIM), lambda b, i: (b, 0, 0, i, 0))
    kv_spec = pl.BlockSpec((1, NSA_GROUPS, tm, NSA_DIM), lambda b, i: (b, 0, i, 0))
    kv_shape = jax.ShapeDtypeStruct((bsz, NSA_GROUPS, seq, NSA_DIM), BF16)
    cmp_shape = jax.ShapeDtypeStruct(kv_shape.shape, F32)
    return pl.pallas_call(
        _od_in_kernel,
        out_shape=(jax.ShapeDtypeStruct((bsz, NSA_GROUPS, NSA_REP, seq, NSA_DIM), BF16),)
        + (cmp_shape,) * 2 + (kv_shape,) * 4
        + (jax.ShapeDtypeStruct((bsz, NSA_GROUPS, seq, LANES), F32), jax.ShapeDtypeStruct((bsz, seq, OD_Z), BF16)),
        grid=(bsz, seq // tm),
        in_specs=[tok(d), pl.BlockSpec((1, 1, 3 * d), lambda b, i: (b, 0, 0)), full(g), full(w),
                  tok(LANES), tok(LANES)],
        out_specs=(q_spec,) + (kv_spec,) * 7 + (tok(OD_Z),),
        compiler_params=_params(("parallel", "parallel")),
        name="odd_in_proj",
    )(x, mod[:, None, :], g, w, cos_n, sin_n)


def _compress_kernel(x_ref, pe_ref, w1_ref, w2_ref, cos_ref, sin_ref, o_ref, *, use_rope):
    seq, d = x_ref.shape[2:]
    nb = seq // CMP_STRIDE
    lo = _dot(pe_ref[...], w1_ref[...])[0:1]
    hi = jnp.zeros((nb, w1_ref.shape[1]), F32)
    for l in range(CMP_STRIDE):
        xl = x_ref[0, 0, pl.ds(l, nb, stride=CMP_STRIDE), :].astype(BF16)
        lo = lo + _dot(xl, w1_ref[l * d:(l + 1) * d])
        hi = hi + _dot(xl, w1_ref[(CMP_STRIDE + l) * d:(CMP_STRIDE + l + 1) * d])
    row = lax.broadcasted_iota(jnp.int32, hi.shape, 0)
    pre = lo + jnp.where(row < nb - 1, pltpu.roll(hi, nb - 1, axis=0), 0.0)
    out = _dot(_gelu_tanh(pre).astype(BF16), w2_ref[...])
    if use_rope:
        out = out * cos_ref[0] + pltpu.roll(out, NSA_DIM // 2, axis=1) * sin_ref[0]
    o_ref[0, 0] = out.astype(BF16)


def _compress(kv, pe, w1, w2, cos_end, sin_end, use_rope):
    bsz, groups, seq, d = kv.shape
    nb = seq // CMP_STRIDE
    pe_rows = jnp.broadcast_to(pe.reshape(1, CMP_LEN * d), (SUBLANES, CMP_LEN * d)).astype(BF16)
    w1b, w2b = w1.astype(BF16), w2.astype(BF16)
    full = lambda a: pl.BlockSpec(a.shape, lambda b, g: (0,) * a.ndim)
    end = pl.BlockSpec((1, nb, d), lambda b, g: (b, 0, 0))
    return pl.pallas_call(
        functools.partial(_compress_kernel, use_rope=use_rope),
        out_shape=jax.ShapeDtypeStruct((bsz, groups, nb, d), BF16),
        grid=(bsz, groups),
        in_specs=[pl.BlockSpec((1, 1, seq, d), lambda b, g: (b, g, 0, 0)),
                  full(pe_rows), full(w1b), full(w2b), end, end],
        out_specs=pl.BlockSpec((1, 1, nb, d), lambda b, g: (b, g, 0, 0)),
        compiler_params=_params(("parallel", "parallel")),
        name="nsa_compress",
    )(kv, pe_rows, w1b, w2b, cos_end, sin_end)


def _cmp_sel_kernel(q_ref, kc_ref, vc_ref, pool_ref, o_ref, sel_ref, **static):
    for g in range(q_ref.shape[1]):
        _cmp_sel_group(g, q_ref, kc_ref, vc_ref, pool_ref, o_ref, sel_ref, **static)


def _cmp_sel_group(g, q_ref, kc_ref, vc_ref, pool_ref, o_ref, sel_ref, *, n_cmp, n_sel, n_top):
    qi = pl.program_id(1)
    rep, tq, d = q_ref.shape[2:]
    nb = kc_ref.shape[2]
    q = q_ref[0, g].reshape(rep * tq, d)
    s = _dot_nt(q, kc_ref[0, g]).reshape(rep, tq, nb)
    qpos = qi * tq + lax.broadcasted_iota(jnp.int32, (tq, nb), 0)
    blk = lax.broadcasted_iota(jnp.int32, (tq, nb), 1)
    mask = ((blk * CMP_STRIDE + (CMP_LEN - 1) <= qpos) & (blk < n_cmp))[None]
    s = jnp.where(mask, s, NEG_INF)
    m = jnp.max(s, axis=-1, keepdims=True)
    e = jnp.where(mask, jnp.exp2(s - m), 0.0)
    p = e / jnp.maximum(jnp.sum(e, axis=-1, keepdims=True), TINY)
    o = _dot(p.reshape(rep * tq, nb).astype(BF16), vc_ref[0, g])
    o_ref[0, g] = o.reshape(rep, tq, d).astype(o_ref.dtype)

    rows = pool_ref.shape[0]
    imp = _dot_nt(pool_ref[...], jnp.sum(p, axis=0), precision=HIGHEST)[:n_sel]
    bid = lax.broadcasted_iota(jnp.int32, (n_sel, tq), 0)
    cur = (qi * tq + lax.broadcasted_iota(jnp.int32, (n_sel, tq), 1)) // SEL_BLOCK
    forced = (bid == 0) | (bid == cur) | (bid == cur - 1)
    imp = jnp.where(forced, FORCE_SCORE, jnp.where(bid <= cur, imp, -1.0))
    groups = [imp[g:g + SUBLANES] for g in range(0, n_sel, SUBLANES)]
    sub = lax.broadcasted_iota(jnp.int32, (SUBLANES, tq), 0)
    ranks = [jnp.zeros((SUBLANES, tq), F32) for _ in groups]
    for j in range(n_sel):
        vj = jnp.broadcast_to(imp[j:j + 1, :], (SUBLANES, tq))
        for gi, grp in enumerate(groups):
            lo = gi * SUBLANES
            if lo > j:
                first = vj >= grp
            elif lo + SUBLANES - 1 <= j:
                first = vj > grp
            else:
                first = (vj > grp) | ((vj == grp) & (sub > j - lo))
            ranks[gi] = ranks[gi] + jnp.where(first, 1.0, 0.0)
    rank = jnp.concatenate(ranks, axis=0)
    bias = jnp.where((rank < n_top) & (bid <= cur), 0.0, NEG_INF)
    bias = jnp.concatenate([bias, jnp.full((rows - n_sel, tq), NEG_INF, F32)], axis=0)
    sel_ref[0, g] = bias.T.astype(sel_ref.dtype)


def _cmp_sel(q, kc, vc):
    bsz, groups, rep, seq, d = q.shape
    nb = kc.shape[2]
    tq = NSA_TQ
    n_cmp = (seq - CMP_LEN) // CMP_STRIDE + 1
    n_sel = seq // SEL_BLOCK
    n_top = min(SEL_TOP, n_sel)
    ratio = SEL_BLOCK // CMP_STRIDE
    assert n_sel <= LANES and n_sel * ratio == nb
    pool_np = np.zeros((LANES, nb), np.float32)
    pool_np[np.arange(nb) // ratio, np.arange(nb)] = 1.0
    pool = jnp.asarray(pool_np)
    kv_spec = pl.BlockSpec((1, groups, nb, d), lambda b, i: (b, 0, 0, 0))
    q_spec = pl.BlockSpec((1, groups, rep, tq, d), lambda b, i: (b, 0, 0, i, 0))
    return pl.pallas_call(
        functools.partial(_cmp_sel_kernel, n_cmp=n_cmp, n_sel=n_sel, n_top=n_top),
        out_shape=(jax.ShapeDtypeStruct(q.shape, BF16),
                   jax.ShapeDtypeStruct((bsz, groups, seq, LANES), BF16)),
        grid=(bsz, seq // tq),
        in_specs=[q_spec, kv_spec, kv_spec, pl.BlockSpec(pool.shape, lambda b, i: (0, 0))],
        out_specs=(q_spec, pl.BlockSpec((1, groups, tq, LANES), lambda b, i: (b, 0, i, 0))),
        compiler_params=_params(("parallel", "parallel")),
        name="nsa_cmp_select",
    )(q, kc, vc, pool)


def _sel_win_kernel(q_ref, bias_ref, k_ref, blk_ref, v_ref, kw_ref, vw_ref, oc_ref, gate_ref, z_ref,
                    o_ref, m_ref, acc_ref, sa_ref, sb_ref, ow_ref, *, span):
    qi = pl.program_id(2)
    rep, tq, d = q_ref.shape[2:]
    tk = NSA_TK
    part = rep // SEL_CHAINS
    q_ext = [jnp.concatenate([q_ref[0, 0, c * part:(c + 1) * part].reshape(part * tq, d),
                              jnp.concatenate([bias_ref[0, 0]] * part, axis=0)], axis=1)
             for c in range(SEL_CHAINS)]
    _softmax_init(m_ref, acc_ref)
    n_full = qi * tq // tk

    owns = [slice(c * part * tq, (c + 1) * part * tq) for c in range(SEL_CHAINS)]

    def scores(ki, s_ref):
        rows = pl.ds(pl.multiple_of(ki * tk, tk), tk)
        k_ext = jnp.concatenate([k_ref[0, 0, rows, :], blk_ref[rows, :]], axis=1)
        for c, own in enumerate(owns):
            s_ref[own] = _dot_nt(q_ext[c], k_ext)

    def update(s_ref, ki, mask):
        rows = pl.ds(pl.multiple_of(ki * tk, tk), tk)
        v_ext = _with_ones(v_ref[0, 0, rows, :])
        for own in owns:
            _softmax_tile(s_ref[own], mask, v_ext, m_ref.at[own], acc_ref.at[own])

    def window():
        start = pl.multiple_of(jnp.maximum(qi * tq + tq - span, 0), tq)
        rows = pl.ds(start, span)
        wq = qi * tq + lax.broadcasted_iota(jnp.int32, (1, tq, span), 1)
        wk = start + lax.broadcasted_iota(jnp.int32, (1, tq, span), 2)
        mask = (wq - wk >= 0) & (wq - wk < WINDOW)
        k, v_ext = kw_ref[0, 0, rows, :], _with_ones(vw_ref[0, 0, rows, :])
        wpart = rep // WIN_CHAINS
        for h0 in range(0, rep, wpart):
            q = q_ref[0, 0, h0:h0 + wpart].reshape(wpart * tq, d)
            s = jnp.where(mask, _dot_nt(q, k).reshape(wpart, tq, span), NEG_INF)
            m = jnp.max(s, axis=-1, keepdims=True)
            e = jnp.where(mask, jnp.exp2(s - m), 0.0).reshape(wpart * tq, span)
            acc = _dot(e.astype(BF16), v_ext)
            o = acc[:, :LANES] / jnp.maximum(acc[:, LANES:], TINY)
            ow_ref[h0 * tq:(h0 + wpart) * tq] = o

    qpos = qi * tq + lax.broadcasted_iota(jnp.int32, (1, tq, tk), 1)
    kpos = n_full * tk + lax.broadcasted_iota(jnp.int32, (1, tq, tk), 2)
    _pipelined_causal_tiles(n_full, scores, update, sa_ref, sb_ref, kpos <= qpos, filler=window)
    gates = gate_ref[0, 0]
    for r in range(rep):
        own = slice(r * tq, (r + 1) * tq)
        o = (gates[:, 3 * r:3 * r + 1] * oc_ref[0, 0, r].astype(F32)
             + gates[:, 3 * r + 1:3 * r + 2] * _softmax_finish(acc_ref.at[own])
             + gates[:, 3 * r + 2:3 * r + 3] * ow_ref[own])
        z = z_ref[0, :, r * d:(r + 1) * d].astype(F32)
        o_ref[0, 0, r] = (o * _silu(z)).astype(o_ref.dtype)


def _sel_win_attention(q, bias, k, v, kw, vw, o_cmp, gates, z):
    bsz, groups, rep, seq, d = q.shape
    tq = NSA_TQ
    assert NSA_TK % tq == 0
    span = (-(-(WINDOW - 1) // tq) + 1) * tq
    assert span <= seq
    onehot_np = np.zeros((seq, LANES), np.float32)
    onehot_np[np.arange(seq), np.arange(seq) // SEL_BLOCK] = 1.0
    onehot = jnp.asarray(onehot_np, BF16)
    q_spec = pl.BlockSpec((1, 1, rep, tq, d), lambda b, g, qi: (b, g, 0, qi, 0))
    kv_spec = pl.BlockSpec((1, 1, seq, d), lambda b, g, qi: (b, g, 0, 0))
    lane_spec = pl.BlockSpec((1, 1, tq, LANES), lambda b, g, qi: (b, g, qi, 0))
    return pl.pallas_call(
        functools.partial(_sel_win_kernel, span=span),
        out_shape=jax.ShapeDtypeStruct(q.shape, BF16),
        grid=(bsz, groups, seq // tq),
        in_specs=[q_spec, lane_spec, kv_spec, pl.BlockSpec((seq, LANES), lambda b, g, qi: (0, 0)),
                  kv_spec, kv_spec, kv_spec, q_spec, lane_spec,
                  pl.BlockSpec((1, tq, rep * d), lambda b, g, qi: (b, qi, g))],
        out_specs=q_spec,
        scratch_shapes=[pltpu.VMEM((rep * tq, LANES), F32), pltpu.VMEM((rep * tq, 2 * LANES), F32),
                        pltpu.VMEM((rep * tq, NSA_TK), F32), pltpu.VMEM((rep * tq, NSA_TK), F32),
                        pltpu.VMEM((rep * tq, d), F32)],
        compiler_params=_params(("parallel", "parallel", "arbitrary")),
        name="nsa_selected_window_attention",
    )(q, bias, k, onehot, v, kw, vw, o_cmp, gates, z)


def _od_out_kernel(o_ref, x_ref, mod_ref, g_ref, w_ref, out_ref):
    d = x_ref.shape[-1]
    heads = [o_ref[0, hd // NSA_REP, hd % NSA_REP] for hd in range(NSA_HEADS)]
    y = _dot(jnp.concatenate(heads, axis=-1), w_ref[...])
    gate = mod_ref[0][:, 2 * d:]
    out_ref[0] = x_ref[0] + gate * _rms(y, g_ref[...])


def _od_out(o, x, mod, post_g, w_out):
    bsz, seq, d = x.shape
    tm = TOKEN_TILE
    w = w_out.astype(BF16)
    g = post_g[None]
    tok = lambda n: pl.BlockSpec((1, tm, n), lambda b, i: (b, i, 0))
    full = lambda a: pl.BlockSpec(a.shape, lambda b, i: (0,) * a.ndim)
    o_spec = pl.BlockSpec((1, NSA_GROUPS, NSA_REP, tm, NSA_DIM), lambda b, i: (b, 0, 0, i, 0))
    return pl.pallas_call(
        _od_out_kernel,
        out_shape=jax.ShapeDtypeStruct(x.shape, F32),
        grid=(bsz, seq // tm),
        in_specs=[o_spec, tok(d), pl.BlockSpec((1, 1, 3 * d), lambda b, i: (b, 0, 0)), full(g), full(w)],
        out_specs=tok(d),
        compiler_params=_params(("parallel", "parallel")),
        name="odd_out_proj",
    )(o, x, mod[:, None, :], g, w)


def _even_layer(x, mod, pre_g, post_g, tabs, w_in, lam_re, lam_im, log_dt, b_re, b_im, c_re, c_im,
                d_skip, w_glu, b_glu, q_norm_g, kv_norm_g, w_uq, w_ukv, w_out):
    cos_m, sin_m = tabs[0], tabs[1]
    u, z_a, z_b, q, k, v = _ev_in(x, mod, pre_g, w_in, q_norm_g, kv_norm_g, w_uq, w_ukv, cos_m, sin_m)
    y_a = _s5(u, z_a, lam_re, lam_im, log_dt, b_re, b_im, c_re, c_im, d_skip, w_glu, b_glu)
    o_mla = _mla(q, k, v)
    return _ev_out(y_a, o_mla, z_b, x, mod, post_g, w_out)


def _odd_layer(x, mod, pre_g, post_g, tabs, end_tabs, w_in, k_pe, k_w1, k_w2, v_pe, v_w1, v_w2, w_out):
    cos_n, sin_n = tabs[2], tabs[3]
    cos_end, sin_end = end_tabs[2], end_tabs[3]
    q, k_c, v_c, k_s, v_s, k_w, v_w, gates, z = _od_in(x, mod, pre_g, w_in, cos_n, sin_n)
    kc = _compress(k_c, k_pe, k_w1, k_w2, cos_end, sin_end, True)
    vc = _compress(v_c, v_pe, v_w1, v_w2, cos_end, sin_end, False)
    o_cmp, sel = _cmp_sel(q, kc, vc)
    o = _sel_win_attention(q, sel, k_s, v_s, k_w, v_w, o_cmp, gates, z)
    return _od_out(o, x, mod, post_g, w_out)


def kernel(x, c, positions, pre_norm_g, post_norm_g, w_ada, b_ada, ev_w_in, ev_lam_re, ev_lam_im, ev_log_dt, ev_b_re, ev_b_im, ev_c_re, ev_c_im, ev_d_skip, ev_w_glu, ev_b_glu, ev_q_norm_g, ev_kv_norm_g, ev_w_uq, ev_w_ukv, ev_w_out, od_w_in, od_cmp_k_pe, od_cmp_k_w1, od_cmp_k_w2, od_cmp_v_pe, od_cmp_v_w1, od_cmp_v_w2, od_w_out):
    depth = pre_norm_g.shape[0]
    tabs = _rope_tables(positions)
    seq = positions.shape[1]
    pos_end = positions[:, CMP_LEN - 1::CMP_STRIDE]
    pos_end = jnp.pad(pos_end, ((0, 0), (0, seq // CMP_STRIDE - pos_end.shape[1])))
    end_tabs = _rope_tables(pos_end)
    mods = _modulation(c, w_ada, b_ada)
    for layer in range(depth):
        i = layer // 2
        if layer % 2 == 0:
            x = _even_layer(x, mods[layer], pre_norm_g[layer], post_norm_g[layer], tabs,
                            ev_w_in[i], ev_lam_re[i], ev_lam_im[i], ev_log_dt[i], ev_b_re[i], ev_b_im[i],
                            ev_c_re[i], ev_c_im[i], ev_d_skip[i], ev_w_glu[i], ev_b_glu[i],
                            ev_q_norm_g[i], ev_kv_norm_g[i], ev_w_uq[i], ev_w_ukv[i], ev_w_out[i])
        else:
            x = _odd_layer(x, mods[layer], pre_norm_g[layer], post_norm_g[layer], tabs, end_tabs,
                           od_w_in[i], od_cmp_k_pe[i], od_cmp_k_w1[i], od_cmp_k_w2[i],
                           od_cmp_v_pe[i], od_cmp_v_w1[i], od_cmp_v_w2[i], od_w_out[i])
    return x
```

```python
import functools
import math

import numpy as np
import jax
import jax.numpy as jnp
from jax import lax
from jax.experimental import pallas as pl
from jax.experimental.pallas import tpu as pltpu

F32 = jnp.float32
BF16 = jnp.bfloat16
HIGHEST = lax.Precision.HIGHEST

EPS = 1e-6
ROPE_THETA = 10000.0
NEG_INF = -1e30
TINY = 1e-30
FORCE_SCORE = 1e9
LOG2E = 1.0 / math.log(2.0)

S5_GROUP = 16
S5_STATE = 64
MLA_HEADS = 4
MLA_NOPE = 128
MLA_ROPE = 64
MLA_V = 128
NSA_HEADS = 8
NSA_GROUPS = 2
NSA_REP = NSA_HEADS // NSA_GROUPS
NSA_DIM = 128
CMP_LEN = 32
CMP_STRIDE = 16
CMP_HIDDEN = 256
SEL_BLOCK = 64
SEL_TOP = 16
WINDOW = 512

LANES = 128
SUBLANES = 8
VMEM_LIMIT = 48 * 1024 * 1024

TOKEN_TILE = 1024
S5_CHUNK = 256
S5_COLS = 512
ATT_TQ = 512
ATT_TK = 512
NSA_TQ = 256
NSA_TK = 512
SEL_CHAINS = 4
WIN_CHAINS = 4


def _params(sem):
    return pltpu.CompilerParams(dimension_semantics=sem, vmem_limit_bytes=VMEM_LIMIT)


def _dot(a, b, precision=None):
    return jnp.dot(a, b, preferred_element_type=F32, precision=precision)


def _dot_nt(a, b, precision=None):
    return lax.dot_general(a, b, (((1,), (1,)), ((), ())), preferred_element_type=F32,
                           precision=precision)


def _silu(v):
    return v * jax.nn.sigmoid(v)


def _gelu_tanh(v):
    return 0.5 * v * (1.0 + jnp.tanh(math.sqrt(2.0 / math.pi) * (v + 0.044715 * (v * v * v))))


def _rms(v, g):
    return v * lax.rsqrt(jnp.mean(v * v, axis=-1, keepdims=True) + EPS) * g


def _rope_tab_kernel(pos_ref, f_ref, sgm_ref, sgn_ref, cm_ref, sm_ref, cn_ref, sn_ref):
    hn, hm = NSA_DIM // 2, MLA_ROPE // 2
    ang = pos_ref[0] * f_ref[...]
    c, s = jnp.cos(ang), jnp.sin(ang)
    cn_ref[0] = jnp.concatenate([c[:, :hn]] * (LANES // hn), axis=1)
    sn_ref[0] = jnp.concatenate([s[:, :hn]] * (LANES // hn), axis=1) * sgn_ref[...]
    cm_ref[0] = jnp.concatenate([c[:, hn:hn + hm]] * (LANES // hm), axis=1)
    sm_ref[0] = jnp.concatenate([s[:, hn:hn + hm]] * (LANES // hm), axis=1) * sgm_ref[...]


def _rope_tables(positions):
    bsz, seq = positions.shape
    ts = min(TOKEN_TILE, seq)
    pos = positions.astype(F32)[..., None]
    half_m, half_n = MLA_ROPE // 2, NSA_DIM // 2
    fm = ROPE_THETA ** (-jnp.arange(half_m, dtype=F32) / half_m)
    fn = ROPE_THETA ** (-jnp.arange(half_n, dtype=F32) / half_n)
    freqs = jnp.concatenate([fn, fm, jnp.zeros((LANES - half_n - half_m,), F32)])[None]
    sgm = jnp.tile(jnp.concatenate([-jnp.ones((half_m,), F32), jnp.ones((half_m,), F32)]),
                   LANES // MLA_ROPE)[None]
    sgn = jnp.concatenate([-jnp.ones((half_n,), F32), jnp.ones((half_n,), F32)])[None]
    row = pl.BlockSpec((1, LANES), lambda b, i: (0, 0))
    tab = pl.BlockSpec((1, ts, LANES), lambda b, i: (b, i, 0))
    shp = jax.ShapeDtypeStruct((bsz, seq, LANES), F32)
    return pl.pallas_call(
        _rope_tab_kernel,
        out_shape=(shp, shp, shp, shp),
        grid=(bsz, seq // ts),
        in_specs=[pl.BlockSpec((1, ts, 1), lambda b, i: (b, i, 0)), row, row, row],
        out_specs=(tab, tab, tab, tab),
        compiler_params=_params(("parallel", "parallel")),
        name="rope_tables",
    )(pos, freqs, sgm, sgn)


def _mod_kernel(c_ref, w_ref, b_ref, o_ref):
    s = _silu(c_ref[...])
    o_ref[0] = _dot(s, w_ref[0], precision=HIGHEST) + b_ref[0]


def _modulation(c, w_ada, b_ada):
    depth, d, _ = w_ada.shape
    bsz = c.shape[0]
    rows = -(-bsz // SUBLANES) * SUBLANES
    c_pad = jnp.pad(c, ((0, rows - bsz), (0, 0)))
    out = pl.pallas_call(
        _mod_kernel,
        out_shape=jax.ShapeDtypeStruct((depth, rows, 3 * d), F32),
        grid=(depth, 3),
        in_specs=[pl.BlockSpec((rows, d), lambda l, j: (0, 0)),
                  pl.BlockSpec((1, d, d), lambda l, j: (l, 0, j)),
                  pl.BlockSpec((1, 1, d), lambda l, j: (l, 0, j))],
        out_specs=pl.BlockSpec((1, rows, d), lambda l, j: (l, 0, j)),
        compiler_params=_params(("parallel", "parallel")),
        name="modulation",
    )(c_pad, w_ada, b_ada[:, None, :])
    return out[:, :bsz]


def _modulated_input(x_ref, mod_ref, g_ref):
    d = x_ref.shape[-1]
    x = x_ref[0]
    mod = mod_ref[0]
    shift, scale = mod[:, :d], mod[:, d:2 * d]
    return (_rms(x, g_ref[...]) * (1.0 + scale) + shift).astype(BF16)


EV_U, EV_ZA, EV_CQ, EV_CKV, EV_KPE, EV_ZB = 512, 512, 768, 256, 256, 512
EV_OFF = np.cumsum([0, EV_U, EV_ZA, EV_CQ, EV_CKV, EV_KPE, EV_ZB])


def _ev_in_kernel(x_ref, mod_ref, g_ref, w_ref, gq_ref, gkv_ref, wuq_ref, wukv_ref, cm_ref, sm_ref,
                  u_ref, za_ref, zb_ref, q_ref, k_ref, v_ref):
    h = _modulated_input(x_ref, mod_ref, g_ref)
    p = _dot(h, w_ref[...])
    o = EV_OFF
    u_ref[0] = p[:, o[0]:o[1]].astype(BF16)
    za_ref[0] = p[:, o[1]:o[2]].astype(BF16)
    zb_ref[0] = p[:, o[5]:o[6]].astype(BF16)
    cos, sin = cm_ref[0], sm_ref[0]

    cq = _rms(p[:, o[2]:o[3]], gq_ref[...]).astype(BF16)
    q = _dot(cq, wuq_ref[...]) * ((MLA_NOPE + MLA_ROPE) ** -0.5 * LOG2E)
    ckv = _rms(p[:, o[3]:o[4]], gkv_ref[...]).astype(BF16)
    kv = _dot(ckv, wukv_ref[...])
    kpe = (p[:, o[4]:o[4] + LANES] * cos + p[:, o[4] + LANES:o[5]] * sin).astype(BF16)
    nope = MLA_HEADS * MLA_NOPE
    for hd in range(MLA_HEADS):
        b0 = hd * (MLA_NOPE + 2 * LANES)
        qpe = q[:, b0 + MLA_NOPE:b0 + MLA_NOPE + LANES] * cos + q[:, b0 + MLA_NOPE + LANES:b0 + MLA_NOPE + 2 * LANES] * sin
        q_ref[0, hd] = jnp.concatenate([q[:, b0:b0 + MLA_NOPE], qpe], axis=-1).astype(BF16)
        k_ref[0, hd] = jnp.concatenate([kv[:, hd * MLA_NOPE:(hd + 1) * MLA_NOPE].astype(BF16), kpe], axis=-1)
        v_ref[0, hd] = kv[:, nope + hd * MLA_V:nope + (hd + 1) * MLA_V].astype(BF16)


def _select_columns(w, cols):
    n = w.shape[1]
    cols = [int(c) for c in cols]
    pieces, i = [], 0
    while i < len(cols):
        j = i + 1
        if cols[i] == n:
            while j < len(cols) and cols[j] == n:
                j += 1
            pieces.append(jnp.zeros((w.shape[0], j - i), w.dtype))
        else:
            while j < len(cols) and cols[j] == cols[j - 1] + 1 and cols[j] != n:
                j += 1
            pieces.append(w[:, cols[i]:cols[j - 1] + 1])
        i = j
    return jnp.concatenate(pieces, axis=1)


def _ev_in(x, mod, pre_g, w_in, q_norm_g, kv_norm_g, w_uq, w_ukv, cos_m, sin_m):
    bsz, seq, d = x.shape
    tm = TOKEN_TILE
    hq = MLA_ROPE // 2
    pad = LANES - MLA_ROPE

    def rot_cols(base, zero):
        x1, x2, z = base + np.arange(hq), base + hq + np.arange(hq), np.full(pad, zero)
        return np.concatenate([x1, x2, z, x2, x1, z])

    c0 = np.cumsum([0, 512, 512, 768, 256, 64, 512])
    cols = np.concatenate([np.arange(c0[0], c0[4]), rot_cols(c0[4], c0[6]), np.arange(c0[5], c0[6])])
    w = _select_columns(w_in, cols).astype(BF16)
    per = MLA_NOPE + MLA_ROPE
    q_cols = np.concatenate([np.concatenate([hd * per + np.arange(MLA_NOPE),
                                             rot_cols(hd * per + MLA_NOPE, MLA_HEADS * per)])
                             for hd in range(MLA_HEADS)])
    wuq = _select_columns(w_uq, q_cols).astype(BF16)
    per = MLA_NOPE + MLA_V
    heads = np.arange(MLA_HEADS)[:, None] * per
    kv_cols = np.concatenate([(heads + np.arange(MLA_NOPE)).ravel(),
                              (heads + MLA_NOPE + np.arange(MLA_V)).ravel()])
    wukv = _select_columns(w_ukv, kv_cols).astype(BF16)

    tok = lambda n: pl.BlockSpec((1, tm, n), lambda b, i: (b, i, 0))
    full = lambda a: pl.BlockSpec(a.shape, lambda b, i: (0,) * a.ndim)
    out = lambda n: jax.ShapeDtypeStruct((bsz, seq, n), BF16)
    head = lambda n: pl.BlockSpec((1, MLA_HEADS, tm, n), lambda b, i: (b, 0, i, 0))
    head_out = lambda n: jax.ShapeDtypeStruct((bsz, MLA_HEADS, seq, n), BF16)
    gq, gkv, g = q_norm_g[None], kv_norm_g[None], pre_g[None]
    return pl.pallas_call(
        _ev_in_kernel,
        out_shape=(out(512), out(512), out(512), head_out(2 * LANES), head_out(2 * LANES), head_out(MLA_V)),
        grid=(bsz, seq // tm),
        in_specs=[tok(d), pl.BlockSpec((1, 1, 3 * d), lambda b, i: (b, 0, 0)), full(g), full(w),
                  full(gq), full(gkv), full(wuq), full(wukv), tok(LANES), tok(LANES)],
        out_specs=(tok(512), tok(512), tok(512), head(2 * LANES), head(2 * LANES), head(MLA_V)),
        compiler_params=_params(("parallel", "parallel")),
        name="even_in_proj",
    )(x, mod[:, None, :], g, w, gq, gkv, wuq, wukv, cos_m, sin_m)


def _s5_kernel(u_ref, z_ref, perm_ref, permt_ref, wb_ref, wc_ref, a_ref, at_ref, pow_ref,
               d_ref, wglu_ref, bglu_ref, o_ref, bu_ref, xb_ref, state_ref, carry_ref):
    t_len = u_ref.shape[1]
    n = a_ref.shape[1] // 2
    steps = t_len // SUBLANES

    @pl.when(pl.program_id(1) == 0)
    def _():
        state_ref[...] = jnp.zeros_like(state_ref)

    perm = perm_ref[...]
    u_p = _dot(perm, u_ref[0])
    z_p = _dot(perm, z_ref[0])
    u_pb = u_p.astype(BF16)
    bre, bim = pl.ds(0, S5_COLS), pl.ds(S5_COLS, S5_COLS)
    y_parts = []

    for cb in range(n // S5_COLS):
        u_cb = u_pb[:, cb * LANES:(cb + 1) * LANES]
        bu_ref[...] = _dot(u_cb, wb_ref[cb])
        re = pl.ds(cb * S5_COLS, S5_COLS)
        im = pl.ds(n + cb * S5_COLS, S5_COLS)
        ar = jnp.broadcast_to(a_ref[:, re], (SUBLANES, S5_COLS))
        ai = jnp.broadcast_to(a_ref[:, im], (SUBLANES, S5_COLS))

        def step(t, carry):
            xr, xi = carry
            rows = pl.ds(pl.multiple_of(t * SUBLANES, SUBLANES), SUBLANES)
            nr = ar * xr - ai * xi + bu_ref[rows, bre]
            ni = ar * xi + ai * xr + bu_ref[rows, bim]
            bu_ref[rows, bre] = nr
            bu_ref[rows, bim] = ni
            return nr, ni

        zero = jnp.zeros((SUBLANES, S5_COLS), F32)
        er, ei = lax.fori_loop(0, steps, step, (zero, zero), unroll=True)

        sr, si = state_ref[:, re], state_ref[:, im]
        tr, ti = at_ref[:, re], at_ref[:, im]
        for j in range(SUBLANES):
            carry_ref[j:j + 1, re] = sr
            carry_ref[j:j + 1, im] = si
            sr, si = (tr * sr - ti * si + er[j:j + 1], tr * si + ti * sr + ei[j:j + 1])
        state_ref[:, re] = sr
        state_ref[:, im] = si
        cr, ci = carry_ref[:, re], carry_ref[:, im]

        def fix(t2, _):
            xr, xi = [], []
            for k in range(2):
                t = 2 * t2 + k
                rows = pl.ds(pl.multiple_of(t * SUBLANES, SUBLANES), SUBLANES)
                pr, pi = pow_ref[pl.ds(t, 1), re], pow_ref[pl.ds(t, 1), im]
                xr.append(bu_ref[rows, bre] + pr * cr - pi * ci)
                xi.append(bu_ref[rows, bim] + pr * ci + pi * cr)
            rows = pl.ds(pl.multiple_of(t2 * 2 * SUBLANES, 2 * SUBLANES), 2 * SUBLANES)
            xb_ref[rows, bre] = jnp.concatenate(xr, axis=0).astype(BF16)
            xb_ref[rows, bim] = jnp.concatenate(xi, axis=0).astype(BF16)
            return 0

        lax.fori_loop(0, steps // 2, fix, 0, unroll=True)
        y_parts.append(_dot(xb_ref[...], wc_ref[cb]))

    y = jnp.concatenate(y_parts, axis=1) + d_ref[...] * u_p
    g = _gelu_tanh(y)
    gate = jax.nn.sigmoid(_dot(g.astype(BF16), wglu_ref[...]) + bglu_ref[...])
    out = (g * gate * _silu(z_p)).astype(BF16)
    o_ref[0] = _dot(permt_ref[...], out).astype(BF16)


def _s5(u, z_a, lam_re, lam_im, log_dt, b_re, b_im, c_re, c_im, d_skip, w_glu, b_glu):
    bsz, seq, width = u.shape
    groups, state = lam_re.shape
    t_len = S5_CHUNK
    steps = t_len // SUBLANES
    n = groups * state
    dt = jnp.exp(log_dt)[:, None]
    lam_dt_re, lam_dt_im = lam_re * dt, lam_im * dt
    decay = jnp.exp(lam_dt_re)
    ab_re, ab_im = decay * jnp.cos(lam_dt_im), decay * jnp.sin(lam_dt_im)
    den = lam_re * lam_re + lam_im * lam_im
    nr, ni = ab_re - 1.0, ab_im
    f_re = (nr * lam_re + ni * lam_im) / den
    f_im = (ni * lam_re - nr * lam_im) / den
    bb_re = f_re[..., None] * b_re - f_im[..., None] * b_im
    bb_im = f_re[..., None] * b_im + f_im[..., None] * b_re
    gpb = S5_COLS // state
    nblk = groups // gpb
    hdim = width // groups
    assert gpb * hdim == LANES and nblk * S5_COLS == n
    eye = jnp.eye(gpb, dtype=F32)

    def in_blocks(bb):
        bb = bb.reshape(nblk, gpb, state, hdim)
        return jnp.einsum('mgph,gk->mghkp', bb, eye).reshape(nblk, gpb * hdim, gpb * state)

    def out_blocks(cc):
        cc = cc.reshape(nblk, gpb, hdim, state)
        return jnp.einsum('mghp,gk->mgpkh', cc, eye).reshape(nblk, gpb * state, gpb * hdim)

    wb = jnp.concatenate([in_blocks(bb_re), in_blocks(bb_im)], axis=2).astype(BF16)
    wc = jnp.concatenate([out_blocks(c_re), out_blocks(-c_im)], axis=1).astype(BF16)
    a_vec = jnp.concatenate([ab_re.reshape(1, n), ab_im.reshape(1, n)], axis=1)
    ks = jnp.arange(1, steps + 1, dtype=F32)[:, None, None]
    pw_mag = jnp.exp(lam_dt_re[None] * ks)
    pw_re, pw_im = pw_mag * jnp.cos(lam_dt_im[None] * ks), pw_mag * jnp.sin(lam_dt_im[None] * ks)
    pow_tab = jnp.concatenate([pw_re.reshape(steps, n), pw_im.reshape(steps, n)], axis=1)
    at_vec = pow_tab[steps - 1:steps]
    r = np.arange(t_len)
    perm_np = np.zeros((t_len, t_len), np.float32)
    perm_np[r, (r % SUBLANES) * steps + r // SUBLANES] = 1.0
    perm = jnp.asarray(perm_np, BF16)
    permt = jnp.asarray(perm_np.T, BF16)
    d_vec = d_skip.reshape(1, width)
    wglu = w_glu.astype(BF16)
    bglu = b_glu[None]

    tok = pl.BlockSpec((1, t_len, width), lambda b, i: (b, i, 0))
    full = lambda a: pl.BlockSpec(a.shape, lambda b, i: (0,) * a.ndim)
    consts = (perm, permt, wb, wc, a_vec, at_vec, pow_tab, d_vec, wglu, bglu)
    return pl.pallas_call(
        _s5_kernel,
        out_shape=jax.ShapeDtypeStruct((bsz, seq, width), BF16),
        grid=(bsz, seq // t_len),
        in_specs=[tok, tok] + [full(a) for a in consts],
        out_specs=tok,
        scratch_shapes=[pltpu.VMEM((t_len, 2 * S5_COLS), F32), pltpu.VMEM((t_len, 2 * S5_COLS), BF16),
                        pltpu.VMEM((1, 2 * n), F32), pltpu.VMEM((SUBLANES, 2 * n), F32)],
        compiler_params=_params(("parallel", "arbitrary")),
        name="s5_mixer",
    )(u, z_a, *consts)


def _softmax_init(m_ref, acc_ref):
    m_ref[...] = jnp.full_like(m_ref, NEG_INF)
    acc_ref[...] = jnp.zeros_like(acc_ref)


def _softmax_tile(s, mask, v_ext, m_ref, acc_ref):
    rows, tk = s.shape
    if mask is not None:
        heads = rows // mask.shape[1]
        s = jnp.where(mask, s.reshape(heads, *mask.shape[1:]), NEG_INF).reshape(rows, tk)
    m_old = m_ref[...]
    m_new = jnp.maximum(m_old, jnp.max(s, axis=-1, keepdims=True))
    e = jnp.exp2(s - jnp.concatenate([m_new] * (tk // LANES), axis=1))
    if mask is not None:
        e = jnp.where(mask, e.reshape(heads, *mask.shape[1:]), 0.0).reshape(rows, tk)
    alpha = jnp.exp2(m_old - m_new)
    acc_ref[...] = (jnp.concatenate([alpha, alpha], axis=1) * acc_ref[...]
                    + _dot(e.astype(BF16), v_ext))
    m_ref[...] = m_new


def _softmax_finish(acc_ref):
    acc = acc_ref[...]
    return acc[:, :LANES] / jnp.maximum(acc[:, LANES:], TINY)


def _with_ones(v):
    return jnp.concatenate([v, jnp.ones(v.shape, v.dtype)], axis=1)


def _pipelined_causal_tiles(n_full, scores, update, sa_ref, sb_ref, causal, filler=None):
    def pair(i, carry):
        scores(2 * i + 1, sb_ref)
        update(sa_ref, 2 * i, None)
        scores(2 * i + 2, sa_ref)
        update(sb_ref, 2 * i + 1, None)
        return carry

    scores(0, sa_ref)
    if filler is not None:
        filler()
    lax.fori_loop(0, n_full // 2, pair, 0)

    @pl.when(n_full % 2 == 1)
    def _():
        scores(n_full, sb_ref)
        update(sa_ref, n_full - 1, None)
        update(sb_ref, n_full, causal)

    @pl.when(n_full % 2 == 0)
    def _():
        update(sa_ref, n_full, causal)


def _mla_kernel(q_ref, k_ref, v_ref, o_ref, m_ref, acc_ref, sa_ref, sb_ref):
    qi = pl.program_id(1)
    heads, tq = q_ref.shape[1:3]
    tk = ATT_TK
    _softmax_init(m_ref, acc_ref)
    n_full = qi * tq // tk
    owns = [slice(hd * tq, (hd + 1) * tq) for hd in range(heads)]

    def scores(ki, s_ref):
        rows = pl.ds(pl.multiple_of(ki * tk, tk), tk)
        for hd, own in enumerate(owns):
            s_ref[own] = _dot_nt(q_ref[0, hd], k_ref[0, hd, rows, :])

    def update(s_ref, ki, mask):
        rows = pl.ds(pl.multiple_of(ki * tk, tk), tk)
        for hd, own in enumerate(owns):
            v_ext = _with_ones(v_ref[0, hd, rows, :])
            if mask is None:
                _softmax_tile(s_ref[own], None, v_ext, m_ref.at[own], acc_ref.at[own])
            else:
                half = tq // 2
                top = slice(own.start, own.start + half)
                low = slice(own.start + half, own.stop)
                _softmax_tile(s_ref[top, :half], mask[:, :half, :half], v_ext[:half],
                              m_ref.at[top], acc_ref.at[top])
                _softmax_tile(s_ref[low], mask[:, half:, :], v_ext, m_ref.at[low], acc_ref.at[low])

    qpos = qi * tq + lax.broadcasted_iota(jnp.int32, (1, tq, tk), 1)
    kpos = n_full * tk + lax.broadcasted_iota(jnp.int32, (1, tq, tk), 2)
    _pipelined_causal_tiles(n_full, scores, update, sa_ref, sb_ref, kpos <= qpos)
    for hd in range(heads):
        o_ref[0, :, hd * MLA_V:(hd + 1) * MLA_V] = _softmax_finish(
            acc_ref.at[hd * tq:(hd + 1) * tq]).astype(o_ref.dtype)


def _mla(q, k, v):
    bsz, heads, seq, dk = q.shape
    tq = ATT_TQ
    assert tq == ATT_TK
    return pl.pallas_call(
        _mla_kernel,
        out_shape=jax.ShapeDtypeStruct((bsz, seq, heads * MLA_V), BF16),
        grid=(bsz, seq // tq),
        in_specs=[pl.BlockSpec((1, heads, tq, dk), lambda b, qi: (b, 0, qi, 0)),
                  pl.BlockSpec((1, heads, seq, dk), lambda b, qi: (b, 0, 0, 0)),
                  pl.BlockSpec((1, heads, seq, MLA_V), lambda b, qi: (b, 0, 0, 0))],
        out_specs=pl.BlockSpec((1, tq, heads * MLA_V), lambda b, qi: (b, qi, 0)),
        scratch_shapes=[pltpu.VMEM((heads * tq, LANES), F32), pltpu.VMEM((heads * tq, 2 * LANES), F32),
                        pltpu.VMEM((heads * tq, ATT_TK), F32), pltpu.VMEM((heads * tq, ATT_TK), F32)],
        compiler_params=_params(("parallel", "arbitrary")),
        name="mla_attention",
    )(q, k, v)


def _ev_out_kernel(ya_ref, o_ref, zb_ref, x_ref, mod_ref, g_ref, wa_ref, wb_ref, out_ref):
    d = x_ref.shape[-1]
    yb = (o_ref[0].astype(F32) * _silu(zb_ref[0].astype(F32))).astype(BF16)
    y = _dot(ya_ref[0], wa_ref[...]) + _dot(yb, wb_ref[...])
    gate = mod_ref[0][:, 2 * d:]
    out_ref[0] = x_ref[0] + gate * _rms(y, g_ref[...])


def _ev_out(y_a, o_mla, z_b, x, mod, post_g, w_out):
    bsz, seq, d = x.shape
    tm = TOKEN_TILE
    wa = w_out[:y_a.shape[-1]].astype(BF16)
    wb = w_out[y_a.shape[-1]:].astype(BF16)
    g = post_g[None]
    tok = lambda n: pl.BlockSpec((1, tm, n), lambda b, i: (b, i, 0))
    full = lambda a: pl.BlockSpec(a.shape, lambda b, i: (0,) * a.ndim)
    return pl.pallas_call(
        _ev_out_kernel,
        out_shape=jax.ShapeDtypeStruct(x.shape, F32),
        grid=(bsz, seq // tm),
        in_specs=[tok(y_a.shape[-1]), tok(o_mla.shape[-1]), tok(z_b.shape[-1]), tok(d),
                  pl.BlockSpec((1, 1, 3 * d), lambda b, i: (b, 0, 0)), full(g), full(wa), full(wb)],
        out_specs=tok(d),
        compiler_params=_params(("parallel", "parallel")),
        name="even_out_proj",
    )(y_a, o_mla, z_b, x, mod[:, None, :], g, wa, wb)


OD_Q, OD_KV, OD_GATE, OD_Z = 1024, 256, 128, 1024
OD_OFF = np.cumsum([0, OD_Q] + [OD_KV] * 6 + [OD_GATE, OD_Z])


def _od_in_kernel(x_ref, mod_ref, g_ref, w_ref, cn_ref, sn_ref,
                  q_ref, kc_ref, vc_ref, ks_ref, vs_ref, kw_ref, vw_ref, gate_ref, z_ref):
    h = _modulated_input(x_ref, mod_ref, g_ref)
    p = _dot(h, w_ref[...])
    o = OD_OFF
    cos, sin = cn_ref[0], sn_ref[0]

    def rope(t):
        return t * cos + pltpu.roll(t, NSA_DIM // 2, axis=1) * sin

    scale = NSA_DIM ** -0.5 * LOG2E
    for hd in range(NSA_HEADS):
        t = p[:, hd * NSA_DIM:(hd + 1) * NSA_DIM]
        q_ref[0, hd // NSA_REP, hd % NSA_REP] = (rope(t) * scale).astype(BF16)
    plain = (kc_ref, vc_ref, None, vs_ref, None, vw_ref)
    roped = (None, None, ks_ref, None, kw_ref, None)
    for j in range(6):
        for g in range(NSA_GROUPS):
            lo = o[1 + j] + g * NSA_DIM
            t = p[:, lo:lo + NSA_DIM]
            if plain[j] is not None:
                plain[j][0, g] = t.astype(plain[j].dtype)
            else:
                roped[j][0, g] = rope(t).astype(BF16)
    gate_ref[0] = jax.nn.sigmoid(p[:, o[7]:o[8]])
    z_ref[0] = p[:, o[8]:o[9]].astype(BF16)


def _od_in(x, mod, pre_g, w_in, cos_n, sin_n):
    bsz, seq, d = x.shape
    tm = TOKEN_TILE
    n_gate = 3 * NSA_HEADS
    c0 = OD_Q + 6 * OD_KV
    w = jnp.concatenate([w_in[:, :c0 + n_gate], jnp.zeros((d, OD_GATE - n_gate), F32),
                         w_in[:, c0 + n_gate:]], axis=1).astype(BF16)
    g = pre_g[None]
    tok = lambda n: pl.BlockSpec((1, tm, n), lambda b, i: (b, i, 0))
    full = lambda a: pl.BlockSpec(a.shape, lambda b, i: (0,) * a.ndim)
    q_spec = pl.BlockSpec((1, NSA_GROUPS, NSA_REP, tm, NSA_DIM), lambda b, i: (b, 0, 0, i, 0))
    kv_spec = pl.BlockSpec((1, NSA_GROUPS, tm, NSA_DIM), lambda b, i: (b, 0, i, 0))
    kv_shape = jax.ShapeDtypeStruct((bsz, NSA_GROUPS, seq, NSA_DIM), BF16)
    cmp_shape = jax.ShapeDtypeStruct(kv_shape.shape, F32)
    return pl.pallas_call(
        _od_in_kernel,
        out_shape=(jax.ShapeDtypeStruct((bsz, NSA_GROUPS, NSA_REP, seq, NSA_DIM), BF16),)
        + (cmp_shape,) * 2 + (kv_shape,) * 4
        + (jax.ShapeDtypeStruct((bsz, seq, OD_GATE), F32), jax.ShapeDtypeStruct((bsz, seq, OD_Z), BF16)),
        grid=(bsz, seq // tm),
        in_specs=[tok(d), pl.BlockSpec((1, 1, 3 * d), lambda b, i: (b, 0, 0)), full(g), full(w),
                  tok(LANES), tok(LANES)],
        out_specs=(q_spec,) + (kv_spec,) * 6 + (tok(OD_GATE), tok(OD_Z)),
        compiler_params=_params(("parallel", "parallel")),
        name="odd_in_proj",
    )(x, mod[:, None, :], g, w, cos_n, sin_n)


def _compress_kernel(x_ref, pe_ref, w1_ref, w2_ref, cos_ref, sin_ref, o_ref, *, use_rope):
    seq, d = x_ref.shape[2:]
    nb = seq // CMP_STRIDE
    lo = _dot(pe_ref[...], w1_ref[...])[0:1]
    hi = jnp.zeros((nb, w1_ref.shape[1]), F32)
    for l in range(CMP_STRIDE):
        xl = x_ref[0, 0, pl.ds(l, nb, stride=CMP_STRIDE), :].astype(BF16)
        lo = lo + _dot(xl, w1_ref[l * d:(l + 1) * d])
        hi = hi + _dot(xl, w1_ref[(CMP_STRIDE + l) * d:(CMP_STRIDE + l + 1) * d])
    row = lax.broadcasted_iota(jnp.int32, hi.shape, 0)
    pre = lo + jnp.where(row < nb - 1, pltpu.roll(hi, nb - 1, axis=0), 0.0)
    out = _dot(_gelu_tanh(pre).astype(BF16), w2_ref[...])
    if use_rope:
        out = out * cos_ref[0] + pltpu.roll(out, NSA_DIM // 2, axis=1) * sin_ref[0]
    o_ref[0, 0] = out.astype(BF16)


def _compress(kv, pe, w1, w2, cos_end, sin_end, use_rope):
    bsz, groups, seq, d = kv.shape
    nb = seq // CMP_STRIDE
    pe_rows = jnp.broadcast_to(pe.reshape(1, CMP_LEN * d), (SUBLANES, CMP_LEN * d)).astype(BF16)
    w1b, w2b = w1.astype(BF16), w2.astype(BF16)
    full = lambda a: pl.BlockSpec(a.shape, lambda b, g: (0,) * a.ndim)
    end = pl.BlockSpec((1, nb, d), lambda b, g: (b, 0, 0))
    return pl.pallas_call(
        functools.partial(_compress_kernel, use_rope=use_rope),
        out_shape=jax.ShapeDtypeStruct((bsz, groups, nb, d), BF16),
        grid=(bsz, groups),
        in_specs=[pl.BlockSpec((1, 1, seq, d), lambda b, g: (b, g, 0, 0)),
                  full(pe_rows), full(w1b), full(w2b), end, end],
        out_specs=pl.BlockSpec((1, 1, nb, d), lambda b, g: (b, g, 0, 0)),
        compiler_params=_params(("parallel", "parallel")),
        name="nsa_compress",
    )(kv, pe_rows, w1b, w2b, cos_end, sin_end)


def _cmp_sel_kernel(q_ref, kc_ref, vc_ref, pool_ref, o_ref, sel_ref, **static):
    for g in range(q_ref.shape[1]):
        _cmp_sel_group(g, q_ref, kc_ref, vc_ref, pool_ref, o_ref, sel_ref, **static)


def _cmp_sel_group(g, q_ref, kc_ref, vc_ref, pool_ref, o_ref, sel_ref, *, n_cmp, n_sel, n_top):
    qi = pl.program_id(1)
    rep, tq, d = q_ref.shape[2:]
    nb = kc_ref.shape[2]
    q = q_ref[0, g].reshape(rep * tq, d)
    s = _dot_nt(q, kc_ref[0, g]).reshape(rep, tq, nb)
    qpos = qi * tq + lax.broadcasted_iota(jnp.int32, (tq, nb), 0)
    blk = lax.broadcasted_iota(jnp.int32, (tq, nb), 1)
    mask = ((blk * CMP_STRIDE + (CMP_LEN - 1) <= qpos) & (blk < n_cmp))[None]
    s = jnp.where(mask, s, NEG_INF)
    m = jnp.max(s, axis=-1, keepdims=True)
    e = jnp.where(mask, jnp.exp2(s - m), 0.0)
    p = e / jnp.maximum(jnp.sum(e, axis=-1, keepdims=True), TINY)
    o = _dot(p.reshape(rep * tq, nb).astype(BF16), vc_ref[0, g])
    o_ref[0, g] = o.reshape(rep, tq, d).astype(o_ref.dtype)

    rows = pool_ref.shape[0]
    imp = _dot_nt(pool_ref[...], jnp.sum(p, axis=0), precision=HIGHEST)[:n_sel]
    bid = lax.broadcasted_iota(jnp.int32, (n_sel, tq), 0)
    cur = (qi * tq + lax.broadcasted_iota(jnp.int32, (n_sel, tq), 1)) // SEL_BLOCK
    forced = (bid == 0) | (bid == cur) | (bid == cur - 1)
    imp = jnp.where(forced, FORCE_SCORE, jnp.where(bid <= cur, imp, -1.0))
    groups = [imp[g:g + SUBLANES] for g in range(0, n_sel, SUBLANES)]
    sub = lax.broadcasted_iota(jnp.int32, (SUBLANES, tq), 0)
    ranks = [jnp.zeros((SUBLANES, tq), F32) for _ in groups]
    for j in range(n_sel):
        vj = jnp.broadcast_to(imp[j:j + 1, :], (SUBLANES, tq))
        for gi, grp in enumerate(groups):
            lo = gi * SUBLANES
            if lo > j:
                first = vj >= grp
            elif lo + SUBLANES - 1 <= j:
                first = vj > grp
            else:
                first = (vj > grp) | ((vj == grp) & (sub > j - lo))
            ranks[gi] = ranks[gi] + jnp.where(first, 1.0, 0.0)
    rank = jnp.concatenate(ranks, axis=0)
    bias = jnp.where((rank < n_top) & (bid <= cur), 0.0, NEG_INF)
    bias = jnp.concatenate([bias, jnp.full((rows - n_sel, tq), NEG_INF, F32)], axis=0)
    sel_ref[0, g] = bias.T.astype(sel_ref.dtype)


def _cmp_sel(q, kc, vc):
    bsz, groups, rep, seq, d = q.shape
    nb = kc.shape[2]
    tq = NSA_TQ
    n_cmp = (seq - CMP_LEN) // CMP_STRIDE + 1
    n_sel = seq // SEL_BLOCK
    n_top = min(SEL_TOP, n_sel)
    ratio = SEL_BLOCK // CMP_STRIDE
    assert n_sel <= LANES and n_sel * ratio == nb
    pool_np = np.zeros((LANES, nb), np.float32)
    pool_np[np.arange(nb) // ratio, np.arange(nb)] = 1.0
    pool = jnp.asarray(pool_np)
    kv_spec = pl.BlockSpec((1, groups, nb, d), lambda b, i: (b, 0, 0, 0))
    q_spec = pl.BlockSpec((1, groups, rep, tq, d), lambda b, i: (b, 0, 0, i, 0))
    return pl.pallas_call(
        functools.partial(_cmp_sel_kernel, n_cmp=n_cmp, n_sel=n_sel, n_top=n_top),
        out_shape=(jax.ShapeDtypeStruct(q.shape, BF16),
                   jax.ShapeDtypeStruct((bsz, groups, seq, LANES), BF16)),
        grid=(bsz, seq // tq),
        in_specs=[q_spec, kv_spec, kv_spec, pl.BlockSpec(pool.shape, lambda b, i: (0, 0))],
        out_specs=(q_spec, pl.BlockSpec((1, groups, tq, LANES), lambda b, i: (b, 0, i, 0))),
        compiler_params=_params(("parallel", "parallel")),
        name="nsa_cmp_select",
    )(q, kc, vc, pool)


def _sel_win_kernel(q_ref, bias_ref, k_ref, blk_ref, v_ref, kw_ref, vw_ref, o_ref, ow_ref,
                    m_ref, acc_ref, sa_ref, sb_ref, *, span):
    qi = pl.program_id(2)
    rep, tq, d = q_ref.shape[2:]
    tk = NSA_TK
    part = rep // SEL_CHAINS
    q_ext = [jnp.concatenate([q_ref[0, 0, c * part:(c + 1) * part].reshape(part * tq, d),
                              jnp.concatenate([bias_ref[0, 0]] * part, axis=0)], axis=1)
             for c in range(SEL_CHAINS)]
    _softmax_init(m_ref, acc_ref)
    n_full = qi * tq // tk

    owns = [slice(c * part * tq, (c + 1) * part * tq) for c in range(SEL_CHAINS)]

    def scores(ki, s_ref):
        rows = pl.ds(pl.multiple_of(ki * tk, tk), tk)
        k_ext = jnp.concatenate([k_ref[0, 0, rows, :], blk_ref[rows, :]], axis=1)
        for c, own in enumerate(owns):
            s_ref[own] = _dot_nt(q_ext[c], k_ext)

    def update(s_ref, ki, mask):
        rows = pl.ds(pl.multiple_of(ki * tk, tk), tk)
        v_ext = _with_ones(v_ref[0, 0, rows, :])
        for own in owns:
            _softmax_tile(s_ref[own], mask, v_ext, m_ref.at[own], acc_ref.at[own])

    def window():
        start = pl.multiple_of(jnp.maximum(qi * tq + tq - span, 0), tq)
        rows = pl.ds(start, span)
        wq = qi * tq + lax.broadcasted_iota(jnp.int32, (1, tq, span), 1)
        wk = start + lax.broadcasted_iota(jnp.int32, (1, tq, span), 2)
        mask = (wq - wk >= 0) & (wq - wk < WINDOW)
        k, v_ext = kw_ref[0, 0, rows, :], _with_ones(vw_ref[0, 0, rows, :])
        wpart = rep // WIN_CHAINS
        for h0 in range(0, rep, wpart):
            q = q_ref[0, 0, h0:h0 + wpart].reshape(wpart * tq, d)
            s = jnp.where(mask, _dot_nt(q, k).reshape(wpart, tq, span), NEG_INF)
            m = jnp.max(s, axis=-1, keepdims=True)
            e = jnp.where(mask, jnp.exp2(s - m), 0.0).reshape(wpart * tq, span)
            acc = _dot(e.astype(BF16), v_ext)
            o = acc[:, :LANES] / jnp.maximum(acc[:, LANES:], TINY)
            ow_ref[0, 0, h0:h0 + wpart] = o.reshape(wpart, tq, d).astype(ow_ref.dtype)

    qpos = qi * tq + lax.broadcasted_iota(jnp.int32, (1, tq, tk), 1)
    kpos = n_full * tk + lax.broadcasted_iota(jnp.int32, (1, tq, tk), 2)
    _pipelined_causal_tiles(n_full, scores, update, sa_ref, sb_ref, kpos <= qpos, filler=window)
    o_ref[0, 0] = _softmax_finish(acc_ref).reshape(rep, tq, d).astype(o_ref.dtype)


def _sel_win_attention(q, bias, k, v, kw, vw):
    bsz, groups, rep, seq, d = q.shape
    tq = NSA_TQ
    assert NSA_TK % tq == 0
    span = (-(-(WINDOW - 1) // tq) + 1) * tq
    assert span <= seq
    onehot_np = np.zeros((seq, LANES), np.float32)
    onehot_np[np.arange(seq), np.arange(seq) // SEL_BLOCK] = 1.0
    onehot = jnp.asarray(onehot_np, BF16)
    q_spec = pl.BlockSpec((1, 1, rep, tq, d), lambda b, g, qi: (b, g, 0, qi, 0))
    kv_spec = pl.BlockSpec((1, 1, seq, d), lambda b, g, qi: (b, g, 0, 0))
    out = jax.ShapeDtypeStruct(q.shape, BF16)
    return pl.pallas_call(
        functools.partial(_sel_win_kernel, span=span),
        out_shape=(out, out),
        grid=(bsz, groups, seq // tq),
        in_specs=[q_spec, pl.BlockSpec((1, 1, tq, LANES), lambda b, g, qi: (b, g, qi, 0)),
                  kv_spec, pl.BlockSpec((seq, LANES), lambda b, g, qi: (0, 0)), kv_spec, kv_spec, kv_spec],
        out_specs=(q_spec, q_spec),
        scratch_shapes=[pltpu.VMEM((rep * tq, LANES), F32), pltpu.VMEM((rep * tq, 2 * LANES), F32),
                        pltpu.VMEM((rep * tq, NSA_TK), F32), pltpu.VMEM((rep * tq, NSA_TK), F32)],
        compiler_params=_params(("parallel", "parallel", "arbitrary")),
        name="nsa_selected_window_attention",
    )(q, bias, k, onehot, v, kw, vw)


def _od_out_kernel(oc_ref, os_ref, ow_ref, gate_ref, spread_ref, z_ref, x_ref, mod_ref, g_ref, w_ref, out_ref):
    d = x_ref.shape[-1]
    gates = gate_ref[0]
    hi = gates.astype(BF16)
    lo = (gates - hi.astype(F32)).astype(BF16)
    wide = _dot(jnp.concatenate([hi, lo], axis=-1), spread_ref[...])
    z = z_ref[0].astype(F32)
    parts = []
    for hd in range(NSA_HEADS):
        g, r = hd // NSA_REP, hd % NSA_REP
        c = 3 * hd * NSA_DIM
        o = (wide[:, c:c + NSA_DIM] * oc_ref[0, g, r].astype(F32)
             + wide[:, c + NSA_DIM:c + 2 * NSA_DIM] * os_ref[0, g, r].astype(F32)
             + wide[:, c + 2 * NSA_DIM:c + 3 * NSA_DIM] * ow_ref[0, g, r].astype(F32))
        parts.append((o * _silu(z[:, hd * NSA_DIM:(hd + 1) * NSA_DIM])).astype(BF16))
    y = _dot(jnp.concatenate(parts, axis=-1), w_ref[...])
    gate = mod_ref[0][:, 2 * d:]
    out_ref[0] = x_ref[0] + gate * _rms(y, g_ref[...])


def _od_out(o_cmp, o_sel, o_win, gates, z, x, mod, post_g, w_out):
    bsz, seq, d = x.shape
    tm = TOKEN_TILE
    w = w_out.astype(BF16)
    g = post_g[None]
    n_gate = 3 * NSA_HEADS
    spread_np = np.zeros((2 * OD_GATE, n_gate * NSA_DIM), np.float32)
    for c in range(n_gate):
        spread_np[[c, OD_GATE + c], c * NSA_DIM:(c + 1) * NSA_DIM] = 1.0
    spread = jnp.asarray(spread_np, BF16)
    tok = lambda n: pl.BlockSpec((1, tm, n), lambda b, i: (b, i, 0))
    full = lambda a: pl.BlockSpec(a.shape, lambda b, i: (0,) * a.ndim)
    o_spec = pl.BlockSpec((1, NSA_GROUPS, NSA_REP, tm, NSA_DIM), lambda b, i: (b, 0, 0, i, 0))
    return pl.pallas_call(
        _od_out_kernel,
        out_shape=jax.ShapeDtypeStruct(x.shape, F32),
        grid=(bsz, seq // tm),
        in_specs=[o_spec, o_spec, o_spec, tok(OD_GATE), full(spread), tok(OD_Z), tok(d),
                  pl.BlockSpec((1, 1, 3 * d), lambda b, i: (b, 0, 0)), full(g), full(w)],
        out_specs=tok(d),
        compiler_params=_params(("parallel", "parallel")),
        name="odd_out_proj",
    )(o_cmp, o_sel, o_win, gates, spread, z, x, mod[:, None, :], g, w)


def _even_layer(x, mod, pre_g, post_g, tabs, w_in, lam_re, lam_im, log_dt, b_re, b_im, c_re, c_im,
                d_skip, w_glu, b_glu, q_norm_g, kv_norm_g, w_uq, w_ukv, w_out):
    cos_m, sin_m = tabs[0], tabs[1]
    u, z_a, z_b, q, k, v = _ev_in(x, mod, pre_g, w_in, q_norm_g, kv_norm_g, w_uq, w_ukv, cos_m, sin_m)
    y_a = _s5(u, z_a, lam_re, lam_im, log_dt, b_re, b_im, c_re, c_im, d_skip, w_glu, b_glu)
    o_mla = _mla(q, k, v)
    return _ev_out(y_a, o_mla, z_b, x, mod, post_g, w_out)


def _odd_layer(x, mod, pre_g, post_g, tabs, end_tabs, w_in, k_pe, k_w1, k_w2, v_pe, v_w1, v_w2, w_out):
    cos_n, sin_n = tabs[2], tabs[3]
    cos_end, sin_end = end_tabs[2], end_tabs[3]
    q, k_c, v_c, k_s, v_s, k_w, v_w, gates, z = _od_in(x, mod, pre_g, w_in, cos_n, sin_n)
    kc = _compress(k_c, k_pe, k_w1, k_w2, cos_end, sin_end, True)
    vc = _compress(v_c, v_pe, v_w1, v_w2, cos_end, sin_end, False)
    o_cmp, sel = _cmp_sel(q, kc, vc)
    o_sel, o_win = _sel_win_attention(q, sel, k_s, v_s, k_w, v_w)
    return _od_out(o_cmp, o_sel, o_win, gates, z, x, mod, post_g, w_out)


def kernel(x, c, positions, pre_norm_g, post_norm_g, w_ada, b_ada, ev_w_in, ev_lam_re, ev_lam_im, ev_log_dt, ev_b_re, ev_b_im, ev_c_re, ev_c_im, ev_d_skip, ev_w_glu, ev_b_glu, ev_q_norm_g, ev_kv_norm_g, ev_w_uq, ev_w_ukv, ev_w_out, od_w_in, od_cmp_k_pe, od_cmp_k_w1, od_cmp_k_w2, od_cmp_v_pe, od_cmp_v_w1, od_cmp_v_w2, od_w_out):
    depth = pre_norm_g.shape[0]
    tabs = _rope_tables(positions)
    seq = positions.shape[1]
    pos_end = positions[:, CMP_LEN - 1::CMP_STRIDE]
    pos_end = jnp.pad(pos_end, ((0, 0), (0, seq // CMP_STRIDE - pos_end.shape[1])))
    end_tabs = _rope_tables(pos_end)
    mods = _modulation(c, w_ada, b_ada)
    for layer in range(depth):
        i = layer // 2
        if layer % 2 == 0:
            x = _even_layer(x, mods[layer], pre_norm_g[layer], post_norm_g[layer], tabs,
                            ev_w_in[i], ev_lam_re[i], ev_lam_im[i], ev_log_dt[i], ev_b_re[i], ev_b_im[i],
                            ev_c_re[i], ev_c_im[i], ev_d_skip[i], ev_w_glu[i], ev_b_glu[i],
                            ev_q_norm_g[i], ev_kv_norm_g[i], ev_w_uq[i], ev_w_ukv[i], ev_w_out[i])
        else:
            x = _odd_layer(x, mods[layer], pre_norm_g[layer], post_norm_g[layer], tabs, end_tabs,
                           od_w_in[i], od_cmp_k_pe[i], od_cmp_k_w1[i], od_cmp_k_w2[i],
                           od_cmp_v_pe[i], od_cmp_v_w1[i], od_cmp_v_w2[i], od_w_out[i])
    return x
```

```python
import functools
import math

import numpy as np
import jax
import jax.numpy as jnp
from jax import lax
from jax.experimental import pallas as pl
from jax.experimental.pallas import tpu as pltpu

F32 = jnp.float32
BF16 = jnp.bfloat16
HIGHEST = lax.Precision.HIGHEST

EPS = 1e-6
ROPE_THETA = 10000.0
NEG_INF = -1e30
TINY = 1e-30
FORCE_SCORE = 1e9
LOG2E = 1.0 / math.log(2.0)

S5_GROUP = 16
S5_STATE = 64
MLA_HEADS = 4
MLA_NOPE = 128
MLA_ROPE = 64
MLA_V = 128
NSA_HEADS = 8
NSA_GROUPS = 2
NSA_REP = NSA_HEADS // NSA_GROUPS
NSA_DIM = 128
CMP_LEN = 32
CMP_STRIDE = 16
CMP_HIDDEN = 256
SEL_BLOCK = 64
SEL_TOP = 16
WINDOW = 512

LANES = 128
SUBLANES = 8
VMEM_LIMIT = 48 * 1024 * 1024

TOKEN_TILE = 1024
S5_CHUNK = 256
S5_COLS = 512
ATT_TQ = 512
ATT_TK = 512
NSA_TQ = 256
NSA_TK = 512
SEL_CHAINS = 4
WIN_CHAINS = 4


def _params(sem):
    return pltpu.CompilerParams(dimension_semantics=sem, vmem_limit_bytes=VMEM_LIMIT)


def _dot(a, b, precision=None):
    return jnp.dot(a, b, preferred_element_type=F32, precision=precision)


def _dot_nt(a, b, precision=None):
    return lax.dot_general(a, b, (((1,), (1,)), ((), ())), preferred_element_type=F32,
                           precision=precision)


def _silu(v):
    return v * jax.nn.sigmoid(v)


def _gelu_tanh(v):
    return 0.5 * v * (1.0 + jnp.tanh(math.sqrt(2.0 / math.pi) * (v + 0.044715 * (v * v * v))))


def _rms(v, g):
    return v * lax.rsqrt(jnp.mean(v * v, axis=-1, keepdims=True) + EPS) * g


def _rope_tab_kernel(pos_ref, f_ref, sgm_ref, sgn_ref, cm_ref, sm_ref, cn_ref, sn_ref):
    hn, hm = NSA_DIM // 2, MLA_ROPE // 2
    ang = pos_ref[0] * f_ref[...]
    c, s = jnp.cos(ang), jnp.sin(ang)
    cn_ref[0] = jnp.concatenate([c[:, :hn]] * (LANES // hn), axis=1)
    sn_ref[0] = jnp.concatenate([s[:, :hn]] * (LANES // hn), axis=1) * sgn_ref[...]
    cm_ref[0] = jnp.concatenate([c[:, hn:hn + hm]] * (LANES // hm), axis=1)
    sm_ref[0] = jnp.concatenate([s[:, hn:hn + hm]] * (LANES // hm), axis=1) * sgm_ref[...]


def _rope_tables(positions):
    bsz, seq = positions.shape
    ts = min(TOKEN_TILE, seq)
    pos = positions.astype(F32)[..., None]
    half_m, half_n = MLA_ROPE // 2, NSA_DIM // 2
    fm = ROPE_THETA ** (-jnp.arange(half_m, dtype=F32) / half_m)
    fn = ROPE_THETA ** (-jnp.arange(half_n, dtype=F32) / half_n)
    freqs = jnp.concatenate([fn, fm, jnp.zeros((LANES - half_n - half_m,), F32)])[None]
    sgm = jnp.concatenate([-jnp.ones((half_m,), F32), jnp.ones((half_m,), F32),
                           jnp.zeros((LANES - MLA_ROPE,), F32)])[None]
    sgn = jnp.concatenate([-jnp.ones((half_n,), F32), jnp.ones((half_n,), F32)])[None]
    row = pl.BlockSpec((1, LANES), lambda b, i: (0, 0))
    tab = pl.BlockSpec((1, ts, LANES), lambda b, i: (b, i, 0))
    shp = jax.ShapeDtypeStruct((bsz, seq, LANES), F32)
    return pl.pallas_call(
        _rope_tab_kernel,
        out_shape=(shp, shp, shp, shp),
        grid=(bsz, seq // ts),
        in_specs=[pl.BlockSpec((1, ts, 1), lambda b, i: (b, i, 0)), row, row, row],
        out_specs=(tab, tab, tab, tab),
        compiler_params=_params(("parallel", "parallel")),
        name="rope_tables",
    )(pos, freqs, sgm, sgn)


def _mod_kernel(c_ref, w_ref, b_ref, o_ref):
    s = _silu(c_ref[...])
    o_ref[0] = _dot(s, w_ref[0], precision=HIGHEST) + b_ref[0]


def _modulation(c, w_ada, b_ada):
    depth, d, _ = w_ada.shape
    bsz = c.shape[0]
    rows = -(-bsz // SUBLANES) * SUBLANES
    c_pad = jnp.pad(c, ((0, rows - bsz), (0, 0)))
    out = pl.pallas_call(
        _mod_kernel,
        out_shape=jax.ShapeDtypeStruct((depth, rows, 3 * d), F32),
        grid=(depth, 3),
        in_specs=[pl.BlockSpec((rows, d), lambda l, j: (0, 0)),
                  pl.BlockSpec((1, d, d), lambda l, j: (l, 0, j)),
                  pl.BlockSpec((1, 1, d), lambda l, j: (l, 0, j))],
        out_specs=pl.BlockSpec((1, rows, d), lambda l, j: (l, 0, j)),
        compiler_params=_params(("parallel", "parallel")),
        name="modulation",
    )(c_pad, w_ada, b_ada[:, None, :])
    return out[:, :bsz]


def _modulated_input(x_ref, mod_ref, g_ref):
    d = x_ref.shape[-1]
    x = x_ref[0]
    mod = mod_ref[0]
    shift, scale = mod[:, :d], mod[:, d:2 * d]
    return (_rms(x, g_ref[...]) * (1.0 + scale) + shift).astype(BF16)


EV_U, EV_ZA, EV_CQ, EV_CKV, EV_KPE, EV_ZB = 512, 512, 768, 256, 128, 512
EV_OFF = np.cumsum([0, EV_U, EV_ZA, EV_CQ, EV_CKV, EV_KPE, EV_ZB])


def _ev_in_kernel(x_ref, mod_ref, g_ref, w_ref, gq_ref, gkv_ref, wuq_ref, wukv_ref, cm_ref, sm_ref,
                  u_ref, za_ref, zb_ref, q_ref, k_ref, v_ref):
    h = _modulated_input(x_ref, mod_ref, g_ref)
    p = _dot(h, w_ref[...])
    o = EV_OFF
    u_ref[0] = p[:, o[0]:o[1]].astype(BF16)
    za_ref[0] = p[:, o[1]:o[2]].astype(BF16)
    zb_ref[0] = p[:, o[5]:o[6]].astype(BF16)
    cos, sin = cm_ref[0], sm_ref[0]
    hq = MLA_ROPE // 2
    first = lax.broadcasted_iota(jnp.int32, cos.shape, 1) < hq

    def rope(t):
        swapped = jnp.where(first, pltpu.roll(t, LANES - hq, axis=1), pltpu.roll(t, hq, axis=1))
        return t * cos + swapped * sin

    cq = _rms(p[:, o[2]:o[3]], gq_ref[...]).astype(BF16)
    q = _dot(cq, wuq_ref[...]) * ((MLA_NOPE + MLA_ROPE) ** -0.5 * LOG2E)
    ckv = _rms(p[:, o[3]:o[4]], gkv_ref[...]).astype(BF16)
    kv = _dot(ckv, wukv_ref[...])
    kpe = rope(p[:, o[4]:o[5]]).astype(BF16)
    nope = MLA_HEADS * MLA_NOPE
    for hd in range(MLA_HEADS):
        b0 = hd * (MLA_NOPE + LANES)
        qpe = rope(q[:, b0 + MLA_NOPE:b0 + MLA_NOPE + LANES])
        q_ref[0, hd] = jnp.concatenate([q[:, b0:b0 + MLA_NOPE], qpe], axis=-1).astype(BF16)
        k_ref[0, hd] = jnp.concatenate([kv[:, hd * MLA_NOPE:(hd + 1) * MLA_NOPE].astype(BF16), kpe], axis=-1)
        v_ref[0, hd] = kv[:, nope + hd * MLA_V:nope + (hd + 1) * MLA_V].astype(BF16)


def _select_columns(w, cols):
    n = w.shape[1]
    cols = [int(c) for c in cols]
    pieces, i = [], 0
    while i < len(cols):
        j = i + 1
        if cols[i] == n:
            while j < len(cols) and cols[j] == n:
                j += 1
            pieces.append(jnp.zeros((w.shape[0], j - i), w.dtype))
        else:
            while j < len(cols) and cols[j] == cols[j - 1] + 1 and cols[j] != n:
                j += 1
            pieces.append(w[:, cols[i]:cols[j - 1] + 1])
        i = j
    return jnp.concatenate(pieces, axis=1)


def _ev_in(x, mod, pre_g, w_in, q_norm_g, kv_norm_g, w_uq, w_ukv, cos_m, sin_m):
    bsz, seq, d = x.shape
    tm = TOKEN_TILE
    hq = MLA_ROPE // 2
    pad = LANES - MLA_ROPE

    def rot_cols(base, zero):
        return np.concatenate([base + np.arange(MLA_ROPE), np.full(pad, zero)])

    c0 = np.cumsum([0, 512, 512, 768, 256, 64, 512])
    cols = np.concatenate([np.arange(c0[0], c0[4]), rot_cols(c0[4], c0[6]), np.arange(c0[5], c0[6])])
    w = _select_columns(w_in, cols).astype(BF16)
    per = MLA_NOPE + MLA_ROPE
    q_cols = np.concatenate([np.concatenate([hd * per + np.arange(MLA_NOPE),
                                             rot_cols(hd * per + MLA_NOPE, MLA_HEADS * per)])
                             for hd in range(MLA_HEADS)])
    wuq = _select_columns(w_uq, q_cols).astype(BF16)
    per = MLA_NOPE + MLA_V
    heads = np.arange(MLA_HEADS)[:, None] * per
    kv_cols = np.concatenate([(heads + np.arange(MLA_NOPE)).ravel(),
                              (heads + MLA_NOPE + np.arange(MLA_V)).ravel()])
    wukv = _select_columns(w_ukv, kv_cols).astype(BF16)

    tok = lambda n: pl.BlockSpec((1, tm, n), lambda b, i: (b, i, 0))
    full = lambda a: pl.BlockSpec(a.shape, lambda b, i: (0,) * a.ndim)
    out = lambda n: jax.ShapeDtypeStruct((bsz, seq, n), BF16)
    head = lambda n: pl.BlockSpec((1, MLA_HEADS, tm, n), lambda b, i: (b, 0, i, 0))
    head_out = lambda n: jax.ShapeDtypeStruct((bsz, MLA_HEADS, seq, n), BF16)
    gq, gkv, g = q_norm_g[None], kv_norm_g[None], pre_g[None]
    return pl.pallas_call(
        _ev_in_kernel,
        out_shape=(out(512), out(512), out(512), head_out(2 * LANES), head_out(2 * LANES), head_out(MLA_V)),
        grid=(bsz, seq // tm),
        in_specs=[tok(d), pl.BlockSpec((1, 1, 3 * d), lambda b, i: (b, 0, 0)), full(g), full(w),
                  full(gq), full(gkv), full(wuq), full(wukv), tok(LANES), tok(LANES)],
        out_specs=(tok(512), tok(512), tok(512), head(2 * LANES), head(2 * LANES), head(MLA_V)),
        compiler_params=_params(("parallel", "parallel")),
        name="even_in_proj",
    )(x, mod[:, None, :], g, w, gq, gkv, wuq, wukv, cos_m, sin_m)


def _s5_kernel(u_ref, z_ref, perm_ref, permt_ref, wb_ref, wc_ref, a_ref, at_ref, pow_ref,
               d_ref, wglu_ref, bglu_ref, o_ref, bu_ref, xb_ref, state_ref, carry_ref):
    t_len = u_ref.shape[1]
    n = a_ref.shape[1] // 2
    steps = t_len // SUBLANES

    @pl.when(pl.program_id(1) == 0)
    def _():
        state_ref[...] = jnp.zeros_like(state_ref)

    perm = perm_ref[...]
    u_p = _dot(perm, u_ref[0])
    z_p = _dot(perm, z_ref[0])
    u_pb = u_p.astype(BF16)
    bre, bim = pl.ds(0, S5_COLS), pl.ds(S5_COLS, S5_COLS)
    y_parts = []

    for cb in range(n // S5_COLS):
        u_cb = u_pb[:, cb * LANES:(cb + 1) * LANES]
        bu_ref[...] = _dot(u_cb, wb_ref[cb])
        re = pl.ds(cb * S5_COLS, S5_COLS)
        im = pl.ds(n + cb * S5_COLS, S5_COLS)
        ar = jnp.broadcast_to(a_ref[:, re], (SUBLANES, S5_COLS))
        ai = jnp.broadcast_to(a_ref[:, im], (SUBLANES, S5_COLS))

        def step(t, carry):
            xr, xi = carry
            rows = pl.ds(pl.multiple_of(t * SUBLANES, SUBLANES), SUBLANES)
            nr = ar * xr - ai * xi + bu_ref[rows, bre]
            ni = ar * xi + ai * xr + bu_ref[rows, bim]
            bu_ref[rows, bre] = nr
            bu_ref[rows, bim] = ni
            return nr, ni

        zero = jnp.zeros((SUBLANES, S5_COLS), F32)
        er, ei = lax.fori_loop(0, steps, step, (zero, zero), unroll=True)

        sr, si = state_ref[:, re], state_ref[:, im]
        tr, ti = at_ref[:, re], at_ref[:, im]
        for j in range(SUBLANES):
            carry_ref[j:j + 1, re] = sr
            carry_ref[j:j + 1, im] = si
            sr, si = (tr * sr - ti * si + er[j:j + 1], tr * si + ti * sr + ei[j:j + 1])
        state_ref[:, re] = sr
        state_ref[:, im] = si
        cr, ci = carry_ref[:, re], carry_ref[:, im]

        def fix(t2, _):
            xr, xi = [], []
            for k in range(2):
                t = 2 * t2 + k
                rows = pl.ds(pl.multiple_of(t * SUBLANES, SUBLANES), SUBLANES)
                pr, pi = pow_ref[pl.ds(t, 1), re], pow_ref[pl.ds(t, 1), im]
                xr.append(bu_ref[rows, bre] + pr * cr - pi * ci)
                xi.append(bu_ref[rows, bim] + pr * ci + pi * cr)
            rows = pl.ds(pl.multiple_of(t2 * 2 * SUBLANES, 2 * SUBLANES), 2 * SUBLANES)
            xb_ref[rows, bre] = jnp.concatenate(xr, axis=0).astype(BF16)
            xb_ref[rows, bim] = jnp.concatenate(xi, axis=0).astype(BF16)
            return 0

        lax.fori_loop(0, steps // 2, fix, 0, unroll=True)
        y_parts.append(_dot(xb_ref[...], wc_ref[cb]))

    y = jnp.concatenate(y_parts, axis=1) + d_ref[...] * u_p
    g = _gelu_tanh(y)
    gate = jax.nn.sigmoid(_dot(g.astype(BF16), wglu_ref[...]) + bglu_ref[...])
    out = (g * gate * _silu(z_p)).astype(BF16)
    o_ref[0] = _dot(permt_ref[...], out).astype(BF16)


def _s5(u, z_a, lam_re, lam_im, log_dt, b_re, b_im, c_re, c_im, d_skip, w_glu, b_glu):
    bsz, seq, width = u.shape
    groups, state = lam_re.shape
    t_len = S5_CHUNK
    steps = t_len // SUBLANES
    n = groups * state
    dt = jnp.exp(log_dt)[:, None]
    lam_dt_re, lam_dt_im = lam_re * dt, lam_im * dt
    decay = jnp.exp(lam_dt_re)
    ab_re, ab_im = decay * jnp.cos(lam_dt_im), decay * jnp.sin(lam_dt_im)
    den = lam_re * lam_re + lam_im * lam_im
    nr, ni = ab_re - 1.0, ab_im
    f_re = (nr * lam_re + ni * lam_im) / den
    f_im = (ni * lam_re - nr * lam_im) / den
    bb_re = f_re[..., None] * b_re - f_im[..., None] * b_im
    bb_im = f_re[..., None] * b_im + f_im[..., None] * b_re
    gpb = S5_COLS // state
    nblk = groups // gpb
    hdim = width // groups
    assert gpb * hdim == LANES and nblk * S5_COLS == n
    eye = jnp.eye(gpb, dtype=F32)

    def in_blocks(bb):
        bb = bb.reshape(nblk, gpb, state, hdim)
        return jnp.einsum('mgph,gk->mghkp', bb, eye).reshape(nblk, gpb * hdim, gpb * state)

    def out_blocks(cc):
        cc = cc.reshape(nblk, gpb, hdim, state)
        return jnp.einsum('mghp,gk->mgpkh', cc, eye).reshape(nblk, gpb * state, gpb * hdim)

    wb = jnp.concatenate([in_blocks(bb_re), in_blocks(bb_im)], axis=2).astype(BF16)
    wc = jnp.concatenate([out_blocks(c_re), out_blocks(-c_im)], axis=1).astype(BF16)
    a_vec = jnp.concatenate([ab_re.reshape(1, n), ab_im.reshape(1, n)], axis=1)
    ks = jnp.arange(1, steps + 1, dtype=F32)[:, None, None]
    pw_mag = jnp.exp(lam_dt_re[None] * ks)
    pw_re, pw_im = pw_mag * jnp.cos(lam_dt_im[None] * ks), pw_mag * jnp.sin(lam_dt_im[None] * ks)
    pow_tab = jnp.concatenate([pw_re.reshape(steps, n), pw_im.reshape(steps, n)], axis=1)
    at_vec = pow_tab[steps - 1:steps]
    r = np.arange(t_len)
    perm_np = np.zeros((t_len, t_len), np.float32)
    perm_np[r, (r % SUBLANES) * steps + r // SUBLANES] = 1.0
    perm = jnp.asarray(perm_np, BF16)
    permt = jnp.asarray(perm_np.T, BF16)
    d_vec = d_skip.reshape(1, width)
    wglu = w_glu.astype(BF16)
    bglu = b_glu[None]

    tok = pl.BlockSpec((1, t_len, width), lambda b, i: (b, i, 0))
    full = lambda a: pl.BlockSpec(a.shape, lambda b, i: (0,) * a.ndim)
    consts = (perm, permt, wb, wc, a_vec, at_vec, pow_tab, d_vec, wglu, bglu)
    return pl.pallas_call(
        _s5_kernel,
        out_shape=jax.ShapeDtypeStruct((bsz, seq, width), BF16),
        grid=(bsz, seq // t_len),
        in_specs=[tok, tok] + [full(a) for a in consts],
        out_specs=tok,
        scratch_shapes=[pltpu.VMEM((t_len, 2 * S5_COLS), F32), pltpu.VMEM((t_len, 2 * S5_COLS), BF16),
                        pltpu.VMEM((1, 2 * n), F32), pltpu.VMEM((SUBLANES, 2 * n), F32)],
        compiler_params=_params(("parallel", "arbitrary")),
        name="s5_mixer",
    )(u, z_a, *consts)


def _softmax_init(m_ref, acc_ref):
    m_ref[...] = jnp.full_like(m_ref, NEG_INF)
    acc_ref[...] = jnp.zeros_like(acc_ref)


def _softmax_tile(s, mask, v_ext, m_ref, acc_ref):
    rows, tk = s.shape
    if mask is not None:
        heads = rows // mask.shape[1]
        s = jnp.where(mask, s.reshape(heads, *mask.shape[1:]), NEG_INF).reshape(rows, tk)
    m_old = m_ref[...]
    m_new = jnp.maximum(m_old, jnp.max(s, axis=-1, keepdims=True))
    e = jnp.exp2(s - jnp.concatenate([m_new] * (tk // LANES), axis=1))
    if mask is not None:
        e = jnp.where(mask, e.reshape(heads, *mask.shape[1:]), 0.0).reshape(rows, tk)
    alpha = jnp.exp2(m_old - m_new)
    acc_ref[...] = (jnp.concatenate([alpha, alpha], axis=1) * acc_ref[...]
                    + _dot(e.astype(BF16), v_ext))
    m_ref[...] = m_new


def _softmax_finish(acc_ref):
    acc = acc_ref[...]
    return acc[:, :LANES] / jnp.maximum(acc[:, LANES:], TINY)


def _with_ones(v):
    return jnp.concatenate([v, jnp.ones(v.shape, v.dtype)], axis=1)


def _pipelined_causal_tiles(n_full, scores, update, sa_ref, sb_ref, causal, filler=None):
    def pair(i, carry):
        scores(2 * i + 1, sb_ref)
        update(sa_ref, 2 * i, None)
        scores(2 * i + 2, sa_ref)
        update(sb_ref, 2 * i + 1, None)
        return carry

    scores(0, sa_ref)
    if filler is not None:
        filler()
    lax.fori_loop(0, n_full // 2, pair, 0)

    @pl.when(n_full % 2 == 1)
    def _():
        scores(n_full, sb_ref)
        update(sa_ref, n_full - 1, None)
        update(sb_ref, n_full, causal)

    @pl.when(n_full % 2 == 0)
    def _():
        update(sa_ref, n_full, causal)


def _mla_kernel(q_ref, k_ref, v_ref, o_ref, m_ref, acc_ref, sa_ref, sb_ref):
    qi = pl.program_id(1)
    heads, tq = q_ref.shape[1:3]
    tk = ATT_TK
    _softmax_init(m_ref, acc_ref)
    n_full = qi * tq // tk
    owns = [slice(hd * tq, (hd + 1) * tq) for hd in range(heads)]

    def scores(ki, s_ref):
        rows = pl.ds(pl.multiple_of(ki * tk, tk), tk)
        for hd, own in enumerate(owns):
            s_ref[own] = _dot_nt(q_ref[0, hd], k_ref[0, hd, rows, :])

    def update(s_ref, ki, mask):
        rows = pl.ds(pl.multiple_of(ki * tk, tk), tk)
        for hd, own in enumerate(owns):
            v_ext = _with_ones(v_ref[0, hd, rows, :])
            if mask is None:
                _softmax_tile(s_ref[own], None, v_ext, m_ref.at[own], acc_ref.at[own])
            else:
                half = tq // 2
                top = slice(own.start, own.start + half)
                low = slice(own.start + half, own.stop)
                _softmax_tile(s_ref[top, :half], mask[:, :half, :half], v_ext[:half],
                              m_ref.at[top], acc_ref.at[top])
                _softmax_tile(s_ref[low], mask[:, half:, :], v_ext, m_ref.at[low], acc_ref.at[low])

    qpos = qi * tq + lax.broadcasted_iota(jnp.int32, (1, tq, tk), 1)
    kpos = n_full * tk + lax.broadcasted_iota(jnp.int32, (1, tq, tk), 2)
    _pipelined_causal_tiles(n_full, scores, update, sa_ref, sb_ref, kpos <= qpos)
    for hd in range(heads):
        o_ref[0, :, hd * MLA_V:(hd + 1) * MLA_V] = _softmax_finish(
            acc_ref.at[hd * tq:(hd + 1) * tq]).astype(o_ref.dtype)


def _mla(q, k, v):
    bsz, heads, seq, dk = q.shape
    tq = ATT_TQ
    assert tq == ATT_TK
    return pl.pallas_call(
        _mla_kernel,
        out_shape=jax.ShapeDtypeStruct((bsz, seq, heads * MLA_V), BF16),
        grid=(bsz, seq // tq),
        in_specs=[pl.BlockSpec((1, heads, tq, dk), lambda b, qi: (b, 0, qi, 0)),
                  pl.BlockSpec((1, heads, seq, dk), lambda b, qi: (b, 0, 0, 0)),
                  pl.BlockSpec((1, heads, seq, MLA_V), lambda b, qi: (b, 0, 0, 0))],
        out_specs=pl.BlockSpec((1, tq, heads * MLA_V), lambda b, qi: (b, qi, 0)),
        scratch_shapes=[pltpu.VMEM((heads * tq, LANES), F32), pltpu.VMEM((heads * tq, 2 * LANES), F32),
                        pltpu.VMEM((heads * tq, ATT_TK), F32), pltpu.VMEM((heads * tq, ATT_TK), F32)],
        compiler_params=_params(("parallel", "arbitrary")),
        name="mla_attention",
    )(q, k, v)


def _ev_out_kernel(ya_ref, o_ref, zb_ref, x_ref, mod_ref, g_ref, wa_ref, wb_ref, out_ref):
    d = x_ref.shape[-1]
    yb = (o_ref[0].astype(F32) * _silu(zb_ref[0].astype(F32))).astype(BF16)
    y = _dot(ya_ref[0], wa_ref[...]) + _dot(yb, wb_ref[...])
    gate = mod_ref[0][:, 2 * d:]
    out_ref[0] = x_ref[0] + gate * _rms(y, g_ref[...])


def _ev_out(y_a, o_mla, z_b, x, mod, post_g, w_out):
    bsz, seq, d = x.shape
    tm = TOKEN_TILE
    wa = w_out[:y_a.shape[-1]].astype(BF16)
    wb = w_out[y_a.shape[-1]:].astype(BF16)
    g = post_g[None]
    tok = lambda n: pl.BlockSpec((1, tm, n), lambda b, i: (b, i, 0))
    full = lambda a: pl.BlockSpec(a.shape, lambda b, i: (0,) * a.ndim)
    return pl.pallas_call(
        _ev_out_kernel,
        out_shape=jax.ShapeDtypeStruct(x.shape, F32),
        grid=(bsz, seq // tm),
        in_specs=[tok(y_a.shape[-1]), tok(o_mla.shape[-1]), tok(z_b.shape[-1]), tok(d),
                  pl.BlockSpec((1, 1, 3 * d), lambda b, i: (b, 0, 0)), full(g), full(wa), full(wb)],
        out_specs=tok(d),
        compiler_params=_params(("parallel", "parallel")),
        name="even_out_proj",
    )(y_a, o_mla, z_b, x, mod[:, None, :], g, wa, wb)


OD_Q, OD_KV, OD_GATE, OD_Z = 1024, 256, 128, 1024
OD_OFF = np.cumsum([0, OD_Q] + [OD_KV] * 6 + [OD_GATE, OD_Z])


def _od_in_kernel(x_ref, mod_ref, g_ref, w_ref, cn_ref, sn_ref,
                  q_ref, kc_ref, vc_ref, ks_ref, vs_ref, kw_ref, vw_ref, gate_ref, z_ref):
    h = _modulated_input(x_ref, mod_ref, g_ref)
    p = _dot(h, w_ref[...])
    o = OD_OFF
    cos, sin = cn_ref[0], sn_ref[0]

    def rope(t):
        return t * cos + pltpu.roll(t, NSA_DIM // 2, axis=1) * sin

    scale = NSA_DIM ** -0.5 * LOG2E
    for hd in range(NSA_HEADS):
        t = p[:, hd * NSA_DIM:(hd + 1) * NSA_DIM]
        q_ref[0, hd // NSA_REP, hd % NSA_REP] = (rope(t) * scale).astype(BF16)
    plain = (kc_ref, vc_ref, None, vs_ref, None, vw_ref)
    roped = (None, None, ks_ref, None, kw_ref, None)
    for j in range(6):
        for g in range(NSA_GROUPS):
            lo = o[1 + j] + g * NSA_DIM
            t = p[:, lo:lo + NSA_DIM]
            if plain[j] is not None:
                plain[j][0, g] = t.astype(plain[j].dtype)
            else:
                roped[j][0, g] = rope(t).astype(BF16)
    gate_ref[0] = jax.nn.sigmoid(p[:, o[7]:o[8]])
    z_ref[0] = p[:, o[8]:o[9]].astype(BF16)


def _od_in(x, mod, pre_g, w_in, cos_n, sin_n):
    bsz, seq, d = x.shape
    tm = TOKEN_TILE
    n_gate = 3 * NSA_HEADS
    c0 = OD_Q + 6 * OD_KV
    w = jnp.concatenate([w_in[:, :c0 + n_gate], jnp.zeros((d, OD_GATE - n_gate), F32),
                         w_in[:, c0 + n_gate:]], axis=1).astype(BF16)
    g = pre_g[None]
    tok = lambda n: pl.BlockSpec((1, tm, n), lambda b, i: (b, i, 0))
    full = lambda a: pl.BlockSpec(a.shape, lambda b, i: (0,) * a.ndim)
    q_spec = pl.BlockSpec((1, NSA_GROUPS, NSA_REP, tm, NSA_DIM), lambda b, i: (b, 0, 0, i, 0))
    kv_spec = pl.BlockSpec((1, NSA_GROUPS, tm, NSA_DIM), lambda b, i: (b, 0, i, 0))
    kv_shape = jax.ShapeDtypeStruct((bsz, NSA_GROUPS, seq, NSA_DIM), BF16)
    cmp_shape = jax.ShapeDtypeStruct(kv_shape.shape, F32)
    return pl.pallas_call(
        _od_in_kernel,
        out_shape=(jax.ShapeDtypeStruct((bsz, NSA_GROUPS, NSA_REP, seq, NSA_DIM), BF16),)
        + (cmp_shape,) * 2 + (kv_shape,) * 4
        + (jax.ShapeDtypeStruct((bsz, seq, OD_GATE), F32), jax.ShapeDtypeStruct((bsz, seq, OD_Z), BF16)),
        grid=(bsz, seq // tm),
        in_specs=[tok(d), pl.BlockSpec((1, 1, 3 * d), lambda b, i: (b, 0, 0)), full(g), full(w),
                  tok(LANES), tok(LANES)],
        out_specs=(q_spec,) + (kv_spec,) * 6 + (tok(OD_GATE), tok(OD_Z)),
        compiler_params=_params(("parallel", "parallel")),
        name="odd_in_proj",
    )(x, mod[:, None, :], g, w, cos_n, sin_n)


def _compress_kernel(x_ref, pe_ref, w1_ref, w2_ref, cos_ref, sin_ref, o_ref, *, use_rope):
    seq, d = x_ref.shape[2:]
    nb = seq // CMP_STRIDE
    lo = _dot(pe_ref[...], w1_ref[...])[0:1]
    hi = jnp.zeros((nb, w1_ref.shape[1]), F32)
    for l in range(CMP_STRIDE):
        xl = x_ref[0, 0, pl.ds(l, nb, stride=CMP_STRIDE), :].astype(BF16)
        lo = lo + _dot(xl, w1_ref[l * d:(l + 1) * d])
        hi = hi + _dot(xl, w1_ref[(CMP_STRIDE + l) * d:(CMP_STRIDE + l + 1) * d])
    row = lax.broadcasted_iota(jnp.int32, hi.shape, 0)
    pre = lo + jnp.where(row < nb - 1, pltpu.roll(hi, nb - 1, axis=0), 0.0)
    out = _dot(_gelu_tanh(pre).astype(BF16), w2_ref[...])
    if use_rope:
        out = out * cos_ref[0] + pltpu.roll(out, NSA_DIM // 2, axis=1) * sin_ref[0]
    o_ref[0, 0] = out.astype(BF16)


def _compress(kv, pe, w1, w2, cos_end, sin_end, use_rope):
    bsz, groups, seq, d = kv.shape
    nb = seq // CMP_STRIDE
    pe_rows = jnp.broadcast_to(pe.reshape(1, CMP_LEN * d), (SUBLANES, CMP_LEN * d)).astype(BF16)
    w1b, w2b = w1.astype(BF16), w2.astype(BF16)
    full = lambda a: pl.BlockSpec(a.shape, lambda b, g: (0,) * a.ndim)
    end = pl.BlockSpec((1, nb, d), lambda b, g: (b, 0, 0))
    return pl.pallas_call(
        functools.partial(_compress_kernel, use_rope=use_rope),
        out_shape=jax.ShapeDtypeStruct((bsz, groups, nb, d), BF16),
        grid=(bsz, groups),
        in_specs=[pl.BlockSpec((1, 1, seq, d), lambda b, g: (b, g, 0, 0)),
                  full(pe_rows), full(w1b), full(w2b), end, end],
        out_specs=pl.BlockSpec((1, 1, nb, d), lambda b, g: (b, g, 0, 0)),
        compiler_params=_params(("parallel", "parallel")),
        name="nsa_compress",
    )(kv, pe_rows, w1b, w2b, cos_end, sin_end)


def _cmp_sel_kernel(q_ref, kc_ref, vc_ref, pool_ref, o_ref, sel_ref, **static):
    for g in range(q_ref.shape[1]):
        _cmp_sel_group(g, q_ref, kc_ref, vc_ref, pool_ref, o_ref, sel_ref, **static)


def _cmp_sel_group(g, q_ref, kc_ref, vc_ref, pool_ref, o_ref, sel_ref, *, n_cmp, n_sel, n_top):
    qi = pl.program_id(1)
    rep, tq, d = q_ref.shape[2:]
    nb = kc_ref.shape[2]
    q = q_ref[0, g].reshape(rep * tq, d)
    s = _dot_nt(q, kc_ref[0, g]).reshape(rep, tq, nb)
    qpos = qi * tq + lax.broadcasted_iota(jnp.int32, (tq, nb), 0)
    blk = lax.broadcasted_iota(jnp.int32, (tq, nb), 1)
    mask = ((blk * CMP_STRIDE + (CMP_LEN - 1) <= qpos) & (blk < n_cmp))[None]
    s = jnp.where(mask, s, NEG_INF)
    m = jnp.max(s, axis=-1, keepdims=True)
    e = jnp.where(mask, jnp.exp2(s - m), 0.0)
    p = e / jnp.maximum(jnp.sum(e, axis=-1, keepdims=True), TINY)
    o = _dot(p.reshape(rep * tq, nb).astype(BF16), vc_ref[0, g])
    o_ref[0, g] = o.reshape(rep, tq, d).astype(o_ref.dtype)

    rows = pool_ref.shape[0]
    imp = _dot_nt(pool_ref[...], jnp.sum(p, axis=0), precision=HIGHEST)[:n_sel]
    bid = lax.broadcasted_iota(jnp.int32, (n_sel, tq), 0)
    cur = (qi * tq + lax.broadcasted_iota(jnp.int32, (n_sel, tq), 1)) // SEL_BLOCK
    forced = (bid == 0) | (bid == cur) | (bid == cur - 1)
    imp = jnp.where(forced, FORCE_SCORE, jnp.where(bid <= cur, imp, -1.0))
    groups = [imp[g:g + SUBLANES] for g in range(0, n_sel, SUBLANES)]
    sub = lax.broadcasted_iota(jnp.int32, (SUBLANES, tq), 0)
    ranks = [jnp.zeros((SUBLANES, tq), F32) for _ in groups]
    for j in range(n_sel):
        vj = jnp.broadcast_to(imp[j:j + 1, :], (SUBLANES, tq))
        for gi, grp in enumerate(groups):
            lo = gi * SUBLANES
            if lo > j:
                first = vj >= grp
            elif lo + SUBLANES - 1 <= j:
                first = vj > grp
            else:
                first = (vj > grp) | ((vj == grp) & (sub > j - lo))
            ranks[gi] = ranks[gi] + jnp.where(first, 1.0, 0.0)
    rank = jnp.concatenate(ranks, axis=0)
    bias = jnp.where((rank < n_top) & (bid <= cur), 0.0, NEG_INF)
    bias = jnp.concatenate([bias, jnp.full((rows - n_sel, tq), NEG_INF, F32)], axis=0)
    sel_ref[0, g] = bias.T.astype(sel_ref.dtype)


def _cmp_sel(q, kc, vc):
    bsz, groups, rep, seq, d = q.shape
    nb = kc.shape[2]
    tq = NSA_TQ
    n_cmp = (seq - CMP_LEN) // CMP_STRIDE + 1
    n_sel = seq // SEL_BLOCK
    n_top = min(SEL_TOP, n_sel)
    ratio = SEL_BLOCK // CMP_STRIDE
    assert n_sel <= LANES and n_sel * ratio == nb
    pool_np = np.zeros((LANES, nb), np.float32)
    pool_np[np.arange(nb) // ratio, np.arange(nb)] = 1.0
    pool = jnp.asarray(pool_np)
    kv_spec = pl.BlockSpec((1, groups, nb, d), lambda b, i: (b, 0, 0, 0))
    q_spec = pl.BlockSpec((1, groups, rep, tq, d), lambda b, i: (b, 0, 0, i, 0))
    return pl.pallas_call(
        functools.partial(_cmp_sel_kernel, n_cmp=n_cmp, n_sel=n_sel, n_top=n_top),
        out_shape=(jax.ShapeDtypeStruct(q.shape, BF16),
                   jax.ShapeDtypeStruct((bsz, groups, seq, LANES), BF16)),
        grid=(bsz, seq // tq),
        in_specs=[q_spec, kv_spec, kv_spec, pl.BlockSpec(pool.shape, lambda b, i: (0, 0))],
        out_specs=(q_spec, pl.BlockSpec((1, groups, tq, LANES), lambda b, i: (b, 0, i, 0))),
        compiler_params=_params(("parallel", "parallel")),
        name="nsa_cmp_select",
    )(q, kc, vc, pool)


def _sel_win_kernel(q_ref, bias_ref, k_ref, blk_ref, v_ref, kw_ref, vw_ref, o_ref, ow_ref,
                    m_ref, acc_ref, sa_ref, sb_ref, *, span):
    qi = pl.program_id(2)
    rep, tq, d = q_ref.shape[2:]
    tk = NSA_TK
    part = rep // SEL_CHAINS
    q_ext = [jnp.concatenate([q_ref[0, 0, c * part:(c + 1) * part].reshape(part * tq, d),
                              jnp.concatenate([bias_ref[0, 0]] * part, axis=0)], axis=1)
             for c in range(SEL_CHAINS)]
    _softmax_init(m_ref, acc_ref)
    n_full = qi * tq // tk

    owns = [slice(c * part * tq, (c + 1) * part * tq) for c in range(SEL_CHAINS)]

    def scores(ki, s_ref):
        rows = pl.ds(pl.multiple_of(ki * tk, tk), tk)
        k_ext = jnp.concatenate([k_ref[0, 0, rows, :], blk_ref[rows, :]], axis=1)
        for c, own in enumerate(owns):
            s_ref[own] = _dot_nt(q_ext[c], k_ext)

    def update(s_ref, ki, mask):
        rows = pl.ds(pl.multiple_of(ki * tk, tk), tk)
        v_ext = _with_ones(v_ref[0, 0, rows, :])
        for own in owns:
            _softmax_tile(s_ref[own], mask, v_ext, m_ref.at[own], acc_ref.at[own])

    def window():
        start = pl.multiple_of(jnp.maximum(qi * tq + tq - span, 0), tq)
        rows = pl.ds(start, span)
        wq = qi * tq + lax.broadcasted_iota(jnp.int32, (1, tq, span), 1)
        wk = start + lax.broadcasted_iota(jnp.int32, (1, tq, span), 2)
        mask = (wq - wk >= 0) & (wq - wk < WINDOW)
        k, v_ext = kw_ref[0, 0, rows, :], _with_ones(vw_ref[0, 0, rows, :])
        wpart = rep // WIN_CHAINS
        for h0 in range(0, rep, wpart):
            q = q_ref[0, 0, h0:h0 + wpart].reshape(wpart * tq, d)
            s = jnp.where(mask, _dot_nt(q, k).reshape(wpart, tq, span), NEG_INF)
            m = jnp.max(s, axis=-1, keepdims=True)
            e = jnp.where(mask, jnp.exp2(s - m), 0.0).reshape(wpart * tq, span)
            acc = _dot(e.astype(BF16), v_ext)
            o = acc[:, :LANES] / jnp.maximum(acc[:, LANES:], TINY)
            ow_ref[0, 0, h0:h0 + wpart] = o.reshape(wpart, tq, d).astype(ow_ref.dtype)

    qpos = qi * tq + lax.broadcasted_iota(jnp.int32, (1, tq, tk), 1)
    kpos = n_full * tk + lax.broadcasted_iota(jnp.int32, (1, tq, tk), 2)
    _pipelined_causal_tiles(n_full, scores, update, sa_ref, sb_ref, kpos <= qpos, filler=window)
    o_ref[0, 0] = _softmax_finish(acc_ref).reshape(rep, tq, d).astype(o_ref.dtype)


def _sel_win_attention(q, bias, k, v, kw, vw):
    bsz, groups, rep, seq, d = q.shape
    tq = NSA_TQ
    assert NSA_TK % tq == 0
    span = (-(-(WINDOW - 1) // tq) + 1) * tq
    assert span <= seq
    onehot_np = np.zeros((seq, LANES), np.float32)
    onehot_np[np.arange(seq), np.arange(seq) // SEL_BLOCK] = 1.0
    onehot = jnp.asarray(onehot_np, BF16)
    q_spec = pl.BlockSpec((1, 1, rep, tq, d), lambda b, g, qi: (b, g, 0, qi, 0))
    kv_spec = pl.BlockSpec((1, 1, seq, d), lambda b, g, qi: (b, g, 0, 0))
    out = jax.ShapeDtypeStruct(q.shape, BF16)
    return pl.pallas_call(
        functools.partial(_sel_win_kernel, span=span),
        out_shape=(out, out),
        grid=(bsz, groups, seq // tq),
        in_specs=[q_spec, pl.BlockSpec((1, 1, tq, LANES), lambda b, g, qi: (b, g, qi, 0)),
                  kv_spec, pl.BlockSpec((seq, LANES), lambda b, g, qi: (0, 0)), kv_spec, kv_spec, kv_spec],
        out_specs=(q_spec, q_spec),
        scratch_shapes=[pltpu.VMEM((rep * tq, LANES), F32), pltpu.VMEM((rep * tq, 2 * LANES), F32),
                        pltpu.VMEM((rep * tq, NSA_TK), F32), pltpu.VMEM((rep * tq, NSA_TK), F32)],
        compiler_params=_params(("parallel", "parallel", "arbitrary")),
        name="nsa_selected_window_attention",
    )(q, bias, k, onehot, v, kw, vw)


def _od_out_kernel(oc_ref, os_ref, ow_ref, gate_ref, spread_ref, z_ref, x_ref, mod_ref, g_ref, w_ref, out_ref):
    d = x_ref.shape[-1]
    gates = gate_ref[0]
    hi = gates.astype(BF16)
    lo = (gates - hi.astype(F32)).astype(BF16)
    wide = _dot(jnp.concatenate([hi, lo], axis=-1), spread_ref[...])
    z = z_ref[0].astype(F32)
    parts = []
    for hd in range(NSA_HEADS):
        g, r = hd // NSA_REP, hd % NSA_REP
        c = 3 * hd * NSA_DIM
        o = (wide[:, c:c + NSA_DIM] * oc_ref[0, g, r].astype(F32)
             + wide[:, c + NSA_DIM:c + 2 * NSA_DIM] * os_ref[0, g, r].astype(F32)
             + wide[:, c + 2 * NSA_DIM:c + 3 * NSA_DIM] * ow_ref[0, g, r].astype(F32))
        parts.append((o * _silu(z[:, hd * NSA_DIM:(hd + 1) * NSA_DIM])).astype(BF16))
    y = _dot(jnp.concatenate(parts, axis=-1), w_ref[...])
    gate = mod_ref[0][:, 2 * d:]
    out_ref[0] = x_ref[0] + gate * _rms(y, g_ref[...])


def _od_out(o_cmp, o_sel, o_win, gates, z, x, mod, post_g, w_out):
    bsz, seq, d = x.shape
    tm = TOKEN_TILE
    w = w_out.astype(BF16)
    g = post_g[None]
    n_gate = 3 * NSA_HEADS
    spread_np = np.zeros((2 * OD_GATE, n_gate * NSA_DIM), np.float32)
    for c in range(n_gate):
        spread_np[[c, OD_GATE + c], c * NSA_DIM:(c + 1) * NSA_DIM] = 1.0
    spread = jnp.asarray(spread_np, BF16)
    tok = lambda n: pl.BlockSpec((1, tm, n), lambda b, i: (b, i, 0))
    full = lambda a: pl.BlockSpec(a.shape, lambda b, i: (0,) * a.ndim)
    o_spec = pl.BlockSpec((1, NSA_GROUPS, NSA_REP, tm, NSA_DIM), lambda b, i: (b, 0, 0, i, 0))
    return pl.pallas_call(
        _od_out_kernel,
        out_shape=jax.ShapeDtypeStruct(x.shape, F32),
        grid=(bsz, seq // tm),
        in_specs=[o_spec, o_spec, o_spec, tok(OD_GATE), full(spread), tok(OD_Z), tok(d),
                  pl.BlockSpec((1, 1, 3 * d), lambda b, i: (b, 0, 0)), full(g), full(w)],
        out_specs=tok(d),
        compiler_params=_params(("parallel", "parallel")),
        name="odd_out_proj",
    )(o_cmp, o_sel, o_win, gates, spread, z, x, mod[:, None, :], g, w)


def _even_layer(x, mod, pre_g, post_g, tabs, w_in, lam_re, lam_im, log_dt, b_re, b_im, c_re, c_im,
                d_skip, w_glu, b_glu, q_norm_g, kv_norm_g, w_uq, w_ukv, w_out):
    cos_m, sin_m = tabs[0], tabs[1]
    u, z_a, z_b, q, k, v = _ev_in(x, mod, pre_g, w_in, q_norm_g, kv_norm_g, w_uq, w_ukv, cos_m, sin_m)
    y_a = _s5(u, z_a, lam_re, lam_im, log_dt, b_re, b_im, c_re, c_im, d_skip, w_glu, b_glu)
    o_mla = _mla(q, k, v)
    return _ev_out(y_a, o_mla, z_b, x, mod, post_g, w_out)


def _odd_layer(x, mod, pre_g, post_g, tabs, end_tabs, w_in, k_pe, k_w1, k_w2, v_pe, v_w1, v_w2, w_out):
    cos_n, sin_n = tabs[2], tabs[3]
    cos_end, sin_end = end_tabs[2], end_tabs[3]
    q, k_c, v_c, k_s, v_s, k_w, v_w, gates, z = _od_in(x, mod, pre_g, w_in, cos_n, sin_n)
    kc = _compress(k_c, k_pe, k_w1, k_w2, cos_end, sin_end, True)
    vc = _compress(v_c, v_pe, v_w1, v_w2, cos_end, sin_end, False)
    o_cmp, sel = _cmp_sel(q, kc, vc)
    o_sel, o_win = _sel_win_attention(q, sel, k_s, v_s, k_w, v_w)
    return _od_out(o_cmp, o_sel, o_win, gates, z, x, mod, post_g, w_out)


def kernel(x, c, positions, pre_norm_g, post_norm_g, w_ada, b_ada, ev_w_in, ev_lam_re, ev_lam_im, ev_log_dt, ev_b_re, ev_b_im, ev_c_re, ev_c_im, ev_d_skip, ev_w_glu, ev_b_glu, ev_q_norm_g, ev_kv_norm_g, ev_w_uq, ev_w_ukv, ev_w_out, od_w_in, od_cmp_k_pe, od_cmp_k_w1, od_cmp_k_w2, od_cmp_v_pe, od_cmp_v_w1, od_cmp_v_w2, od_w_out):
    depth = pre_norm_g.shape[0]
    tabs = _rope_tables(positions)
    seq = positions.shape[1]
    pos_end = positions[:, CMP_LEN - 1::CMP_STRIDE]
    pos_end = jnp.pad(pos_end, ((0, 0), (0, seq // CMP_STRIDE - pos_end.shape[1])))
    end_tabs = _rope_tables(pos_end)
    mods = _modulation(c, w_ada, b_ada)
    for layer in range(depth):
        i = layer // 2
        if layer % 2 == 0:
            x = _even_layer(x, mods[layer], pre_norm_g[layer], post_norm_g[layer], tabs,
                            ev_w_in[i], ev_lam_re[i], ev_lam_im[i], ev_log_dt[i], ev_b_re[i], ev_b_im[i],
                            ev_c_re[i], ev_c_im[i], ev_d_skip[i], ev_w_glu[i], ev_b_glu[i],
                            ev_q_norm_g[i], ev_kv_norm_g[i], ev_w_uq[i], ev_w_ukv[i], ev_w_out[i])
        else:
            x = _odd_layer(x, mods[layer], pre_norm_g[layer], post_norm_g[layer], tabs, end_tabs,
                           od_w_in[i], od_cmp_k_pe[i], od_cmp_k_w1[i], od_cmp_k_w2[i],
                           od_cmp_v_pe[i], od_cmp_v_w1[i], od_cmp_v_w2[i], od_w_out[i])
    return x
```

```python
import functools
import math

import numpy as np
import jax
import jax.numpy as jnp
from jax import lax
from jax.experimental import pallas as pl
from jax.experimental.pallas import tpu as pltpu

F32 = jnp.float32
BF16 = jnp.bfloat16
HIGHEST = lax.Precision.HIGHEST

EPS = 1e-6
ROPE_THETA = 10000.0
NEG_INF = -1e30
TINY = 1e-30
FORCE_SCORE = 1e9
LOG2E = 1.0 / math.log(2.0)

S5_GROUP = 16
S5_STATE = 64
MLA_HEADS = 4
MLA_NOPE = 128
MLA_ROPE = 64
MLA_V = 128
NSA_HEADS = 8
NSA_GROUPS = 2
NSA_REP = NSA_HEADS // NSA_GROUPS
NSA_DIM = 128
CMP_LEN = 32
CMP_STRIDE = 16
CMP_HIDDEN = 256
SEL_BLOCK = 64
SEL_TOP = 16
WINDOW = 512

LANES = 128
SUBLANES = 8
VMEM_LIMIT = 48 * 1024 * 1024

TOKEN_TILE = 1024
S5_CHUNK = 256
S5_COLS = 512
ATT_TQ = 512
ATT_TK = 512
NSA_TQ = 256
NSA_TK = 512
SEL_CHAINS = 4
WIN_CHAINS = 4


def _params(sem):
    return pltpu.CompilerParams(dimension_semantics=sem, vmem_limit_bytes=VMEM_LIMIT)


def _dot(a, b, precision=None):
    return jnp.dot(a, b, preferred_element_type=F32, precision=precision)


def _dot_nt(a, b, precision=None):
    return lax.dot_general(a, b, (((1,), (1,)), ((), ())), preferred_element_type=F32,
                           precision=precision)


def _silu(v):
    return v * jax.nn.sigmoid(v)


def _gelu_tanh(v):
    return 0.5 * v * (1.0 + jnp.tanh(math.sqrt(2.0 / math.pi) * (v + 0.044715 * (v * v * v))))


def _rms(v, g):
    return v * lax.rsqrt(jnp.mean(v * v, axis=-1, keepdims=True) + EPS) * g


def _rope_tab_kernel(pos_ref, f_ref, sgm_ref, sgn_ref, cm_ref, sm_ref, cn_ref, sn_ref):
    hn, hm = NSA_DIM // 2, MLA_ROPE // 2
    ang = pos_ref[0] * f_ref[...]
    c, s = jnp.cos(ang), jnp.sin(ang)
    cn_ref[0] = jnp.concatenate([c[:, :hn]] * (LANES // hn), axis=1)
    sn_ref[0] = jnp.concatenate([s[:, :hn]] * (LANES // hn), axis=1) * sgn_ref[...]
    cm_ref[0] = jnp.concatenate([c[:, hn:hn + hm]] * (LANES // hm), axis=1)
    sm_ref[0] = jnp.concatenate([s[:, hn:hn + hm]] * (LANES // hm), axis=1) * sgm_ref[...]


def _rope_tables(positions):
    bsz, seq = positions.shape
    ts = min(TOKEN_TILE, seq)
    pos = positions.astype(F32)[..., None]
    half_m, half_n = MLA_ROPE // 2, NSA_DIM // 2
    fm = ROPE_THETA ** (-jnp.arange(half_m, dtype=F32) / half_m)
    fn = ROPE_THETA ** (-jnp.arange(half_n, dtype=F32) / half_n)
    freqs = jnp.concatenate([fn, fm, jnp.zeros((LANES - half_n - half_m,), F32)])[None]
    sgm = jnp.concatenate([-jnp.ones((half_m,), F32), jnp.ones((half_m,), F32),
                           jnp.zeros((LANES - MLA_ROPE,), F32)])[None]
    sgn = jnp.concatenate([-jnp.ones((half_n,), F32), jnp.ones((half_n,), F32)])[None]
    row = pl.BlockSpec((1, LANES), lambda b, i: (0, 0))
    tab = pl.BlockSpec((1, ts, LANES), lambda b, i: (b, i, 0))
    shp = jax.ShapeDtypeStruct((bsz, seq, LANES), F32)
    return pl.pallas_call(
        _rope_tab_kernel,
        out_shape=(shp, shp, shp, shp),
        grid=(bsz, seq // ts),
        in_specs=[pl.BlockSpec((1, ts, 1), lambda b, i: (b, i, 0)), row, row, row],
        out_specs=(tab, tab, tab, tab),
        compiler_params=_params(("parallel", "parallel")),
        name="rope_tables",
    )(pos, freqs, sgm, sgn)


def _mod_kernel(c_ref, w_ref, b_ref, o_ref):
    s = _silu(c_ref[...])
    o_ref[0] = _dot(s, w_ref[0], precision=HIGHEST) + b_ref[0]


def _modulation(c, w_ada, b_ada):
    depth, d, _ = w_ada.shape
    bsz = c.shape[0]
    rows = -(-bsz // SUBLANES) * SUBLANES
    c_pad = jnp.pad(c, ((0, rows - bsz), (0, 0)))
    out = pl.pallas_call(
        _mod_kernel,
        out_shape=jax.ShapeDtypeStruct((depth, rows, 3 * d), F32),
        grid=(depth, 3),
        in_specs=[pl.BlockSpec((rows, d), lambda l, j: (0, 0)),
                  pl.BlockSpec((1, d, d), lambda l, j: (l, 0, j)),
                  pl.BlockSpec((1, 1, d), lambda l, j: (l, 0, j))],
        out_specs=pl.BlockSpec((1, rows, d), lambda l, j: (l, 0, j)),
        compiler_params=_params(("parallel", "parallel")),
        name="modulation",
    )(c_pad, w_ada, b_ada[:, None, :])
    return out[:, :bsz]


def _modulated_input(x_ref, mod_ref, g_ref):
    d = x_ref.shape[-1]
    x = x_ref[0]
    mod = mod_ref[0]
    shift, scale = mod[:, :d], mod[:, d:2 * d]
    return (_rms(x, g_ref[...]) * (1.0 + scale) + shift).astype(BF16)


EV_U, EV_ZA, EV_CQ, EV_CKV, EV_KPE, EV_ZB = 512, 512, 768, 256, MLA_ROPE, 512
EV_OFF = np.cumsum([0, EV_U, EV_ZA, EV_CQ, EV_CKV, EV_KPE, EV_ZB])
EV_COLS = -(-EV_OFF[-1] // LANES) * LANES


def _ev_in_kernel(x_ref, mod_ref, g_ref, w_ref, gq_ref, gkv_ref, wuq_ref, wukv_ref, cm_ref, sm_ref,
                  u_ref, za_ref, zb_ref, q_ref, k_ref, v_ref):
    h = _modulated_input(x_ref, mod_ref, g_ref)
    p = _dot(h, w_ref[...])
    o = EV_OFF
    u_ref[0] = p[:, o[0]:o[1]].astype(BF16)
    za_ref[0] = p[:, o[1]:o[2]].astype(BF16)
    zb_ref[0] = p[:, o[5]:o[6]].astype(BF16)
    cos, sin = cm_ref[0], sm_ref[0]
    hq = MLA_ROPE // 2
    lane = lax.broadcasted_iota(jnp.int32, cos.shape, 1)
    first = lane < hq

    def rope(t):
        swapped = jnp.where(first, pltpu.roll(t, LANES - hq, axis=1), pltpu.roll(t, hq, axis=1))
        return t * cos + swapped * sin

    cq = _rms(p[:, o[2]:o[3]], gq_ref[...]).astype(BF16)
    q = _dot(cq, wuq_ref[...]) * ((MLA_NOPE + MLA_ROPE) ** -0.5 * LOG2E)
    ckv = _rms(p[:, o[3]:o[4]], gkv_ref[...]).astype(BF16)
    kv = _dot(ckv, wukv_ref[...])
    kpe = rope(jnp.where(lane < MLA_ROPE, p[:, o[4]:o[4] + LANES], 0.0)).astype(BF16)
    nope = MLA_HEADS * MLA_NOPE
    for hd in range(MLA_HEADS):
        b0 = hd * (MLA_NOPE + LANES)
        qpe = rope(q[:, b0 + MLA_NOPE:b0 + MLA_NOPE + LANES])
        q_ref[0, hd] = jnp.concatenate([q[:, b0:b0 + MLA_NOPE], qpe], axis=-1).astype(BF16)
        k_ref[0, hd] = jnp.concatenate([kv[:, hd * MLA_NOPE:(hd + 1) * MLA_NOPE].astype(BF16), kpe], axis=-1)
        v_ref[0, hd] = kv[:, nope + hd * MLA_V:nope + (hd + 1) * MLA_V].astype(BF16)


def _ev_in(x, mod, pre_g, w_in, q_norm_g, kv_norm_g, w_uq, w_ukv, cos_m, sin_m):
    bsz, seq, d = x.shape
    tm = TOKEN_TILE
    w = jnp.pad(w_in, ((0, 0), (0, EV_COLS - w_in.shape[1]))).astype(BF16)
    rank = w_uq.shape[0]
    wuq = jnp.pad(w_uq.reshape(rank, MLA_HEADS, MLA_NOPE + MLA_ROPE), ((0, 0), (0, 0), (0, LANES - MLA_ROPE)))
    wuq = wuq.reshape(rank, MLA_HEADS * (MLA_NOPE + LANES)).astype(BF16)
    rank = w_ukv.shape[0]
    wukv = jnp.concatenate([w_ukv.reshape(rank, MLA_HEADS, MLA_NOPE + MLA_V)[:, :, :MLA_NOPE].reshape(rank, -1),
                            w_ukv.reshape(rank, MLA_HEADS, MLA_NOPE + MLA_V)[:, :, MLA_NOPE:].reshape(rank, -1)],
                           axis=1).astype(BF16)

    tok = lambda n: pl.BlockSpec((1, tm, n), lambda b, i: (b, i, 0))
    full = lambda a: pl.BlockSpec(a.shape, lambda b, i: (0,) * a.ndim)
    out = lambda n: jax.ShapeDtypeStruct((bsz, seq, n), BF16)
    head = lambda n: pl.BlockSpec((1, MLA_HEADS, tm, n), lambda b, i: (b, 0, i, 0))
    head_out = lambda n: jax.ShapeDtypeStruct((bsz, MLA_HEADS, seq, n), BF16)
    gq, gkv, g = q_norm_g[None], kv_norm_g[None], pre_g[None]
    return pl.pallas_call(
        _ev_in_kernel,
        out_shape=(out(512), out(512), out(512), head_out(2 * LANES), head_out(2 * LANES), head_out(MLA_V)),
        grid=(bsz, seq // tm),
        in_specs=[tok(d), pl.BlockSpec((1, 1, 3 * d), lambda b, i: (b, 0, 0)), full(g), full(w),
                  full(gq), full(gkv), full(wuq), full(wukv), tok(LANES), tok(LANES)],
        out_specs=(tok(512), tok(512), tok(512), head(2 * LANES), head(2 * LANES), head(MLA_V)),
        compiler_params=_params(("parallel", "parallel")),
        name="even_in_proj",
    )(x, mod[:, None, :], g, w, gq, gkv, wuq, wukv, cos_m, sin_m)


def _s5_kernel(u_ref, z_ref, perm_ref, permt_ref, wb_ref, wc_ref, a_ref, at_ref, pow_ref,
               d_ref, wglu_ref, bglu_ref, o_ref, bu_ref, xb_ref, state_ref, carry_ref):
    t_len = u_ref.shape[1]
    n = a_ref.shape[1] // 2
    steps = t_len // SUBLANES

    @pl.when(pl.program_id(1) == 0)
    def _():
        state_ref[...] = jnp.zeros_like(state_ref)

    perm = perm_ref[...]
    u_p = _dot(perm, u_ref[0])
    z_p = _dot(perm, z_ref[0])
    u_pb = u_p.astype(BF16)
    bre, bim = pl.ds(0, S5_COLS), pl.ds(S5_COLS, S5_COLS)
    y_parts = []

    for cb in range(n // S5_COLS):
        u_cb = u_pb[:, cb * LANES:(cb + 1) * LANES]
        bu_ref[...] = _dot(u_cb, wb_ref[cb])
        re = pl.ds(cb * S5_COLS, S5_COLS)
        im = pl.ds(n + cb * S5_COLS, S5_COLS)
        ar = jnp.broadcast_to(a_ref[:, re], (SUBLANES, S5_COLS))
        ai = jnp.broadcast_to(a_ref[:, im], (SUBLANES, S5_COLS))

        def step(t, carry):
            xr, xi = carry
            rows = pl.ds(pl.multiple_of(t * SUBLANES, SUBLANES), SUBLANES)
            nr = ar * xr - ai * xi + bu_ref[rows, bre]
            ni = ar * xi + ai * xr + bu_ref[rows, bim]
            bu_ref[rows, bre] = nr
            bu_ref[rows, bim] = ni
            return nr, ni

        zero = jnp.zeros((SUBLANES, S5_COLS), F32)
        er, ei = lax.fori_loop(0, steps, step, (zero, zero), unroll=True)

        sr, si = state_ref[:, re], state_ref[:, im]
        tr, ti = at_ref[:, re], at_ref[:, im]
        for j in range(SUBLANES):
            carry_ref[j:j + 1, re] = sr
            carry_ref[j:j + 1, im] = si
            sr, si = (tr * sr - ti * si + er[j:j + 1], tr * si + ti * sr + ei[j:j + 1])
        state_ref[:, re] = sr
        state_ref[:, im] = si
        cr, ci = carry_ref[:, re], carry_ref[:, im]

        def fix(t2, _):
            xr, xi = [], []
            for k in range(2):
                t = 2 * t2 + k
                rows = pl.ds(pl.multiple_of(t * SUBLANES, SUBLANES), SUBLANES)
                pr, pi = pow_ref[pl.ds(t, 1), re], pow_ref[pl.ds(t, 1), im]
                xr.append(bu_ref[rows, bre] + pr * cr - pi * ci)
                xi.append(bu_ref[rows, bim] + pr * ci + pi * cr)
            rows = pl.ds(pl.multiple_of(t2 * 2 * SUBLANES, 2 * SUBLANES), 2 * SUBLANES)
            xb_ref[rows, bre] = jnp.concatenate(xr, axis=0).astype(BF16)
            xb_ref[rows, bim] = jnp.concatenate(xi, axis=0).astype(BF16)
            return 0

        lax.fori_loop(0, steps // 2, fix, 0, unroll=True)
        y_parts.append(_dot(xb_ref[...], wc_ref[cb]))

    y = jnp.concatenate(y_parts, axis=1) + d_ref[...] * u_p
    g = _gelu_tanh(y)
    gate = jax.nn.sigmoid(_dot(g.astype(BF16), wglu_ref[...]) + bglu_ref[...])
    out = (g * gate * _silu(z_p)).astype(BF16)
    o_ref[0] = _dot(permt_ref[...], out).astype(BF16)


def _s5(u, z_a, lam_re, lam_im, log_dt, b_re, b_im, c_re, c_im, d_skip, w_glu, b_glu):
    bsz, seq, width = u.shape
    groups, state = lam_re.shape
    t_len = S5_CHUNK
    steps = t_len // SUBLANES
    n = groups * state
    dt = jnp.exp(log_dt)[:, None]
    lam_dt_re, lam_dt_im = lam_re * dt, lam_im * dt
    decay = jnp.exp(lam_dt_re)
    ab_re, ab_im = decay * jnp.cos(lam_dt_im), decay * jnp.sin(lam_dt_im)
    den = lam_re * lam_re + lam_im * lam_im
    nr, ni = ab_re - 1.0, ab_im
    f_re = (nr * lam_re + ni * lam_im) / den
    f_im = (ni * lam_re - nr * lam_im) / den
    bb_re = f_re[..., None] * b_re - f_im[..., None] * b_im
    bb_im = f_re[..., None] * b_im + f_im[..., None] * b_re
    gpb = S5_COLS // state
    nblk = groups // gpb
    hdim = width // groups
    assert gpb * hdim == LANES and nblk * S5_COLS == n
    eye = jnp.eye(gpb, dtype=F32)

    def in_blocks(bb):
        bb = bb.reshape(nblk, gpb, state, hdim)
        return jnp.einsum('mgph,gk->mghkp', bb, eye).reshape(nblk, gpb * hdim, gpb * state)

    def out_blocks(cc):
        cc = cc.reshape(nblk, gpb, hdim, state)
        return jnp.einsum('mghp,gk->mgpkh', cc, eye).reshape(nblk, gpb * state, gpb * hdim)

    wb = jnp.concatenate([in_blocks(bb_re), in_blocks(bb_im)], axis=2).astype(BF16)
    wc = jnp.concatenate([out_blocks(c_re), out_blocks(-c_im)], axis=1).astype(BF16)
    a_vec = jnp.concatenate([ab_re.reshape(1, n), ab_im.reshape(1, n)], axis=1)
    ks = jnp.arange(1, steps + 1, dtype=F32)[:, None, None]
    pw_mag = jnp.exp(lam_dt_re[None] * ks)
    pw_re, pw_im = pw_mag * jnp.cos(lam_dt_im[None] * ks), pw_mag * jnp.sin(lam_dt_im[None] * ks)
    pow_tab = jnp.concatenate([pw_re.reshape(steps, n), pw_im.reshape(steps, n)], axis=1)
    at_vec = pow_tab[steps - 1:steps]
    r = np.arange(t_len)
    perm_np = np.zeros((t_len, t_len), np.float32)
    perm_np[r, (r % SUBLANES) * steps + r // SUBLANES] = 1.0
    perm = jnp.asarray(perm_np, BF16)
    permt = jnp.asarray(perm_np.T, BF16)
    d_vec = d_skip.reshape(1, width)
    wglu = w_glu.astype(BF16)
    bglu = b_glu[None]

    tok = pl.BlockSpec((1, t_len, width), lambda b, i: (b, i, 0))
    full = lambda a: pl.BlockSpec(a.shape, lambda b, i: (0,) * a.ndim)
    consts = (perm, permt, wb, wc, a_vec, at_vec, pow_tab, d_vec, wglu, bglu)
    return pl.pallas_call(
        _s5_kernel,
        out_shape=jax.ShapeDtypeStruct((bsz, seq, width), BF16),
        grid=(bsz, seq // t_len),
        in_specs=[tok, tok] + [full(a) for a in consts],
        out_specs=tok,
        scratch_shapes=[pltpu.VMEM((t_len, 2 * S5_COLS), F32), pltpu.VMEM((t_len, 2 * S5_COLS), BF16),
                        pltpu.VMEM((1, 2 * n), F32), pltpu.VMEM((SUBLANES, 2 * n), F32)],
        compiler_params=_params(("parallel", "arbitrary")),
        name="s5_mixer",
    )(u, z_a, *consts)


def _softmax_init(m_ref, acc_ref):
    m_ref[...] = jnp.full_like(m_ref, NEG_INF)
    acc_ref[...] = jnp.zeros_like(acc_ref)


def _softmax_tile(s, mask, v_ext, m_ref, acc_ref):
    rows, tk = s.shape
    if mask is not None:
        heads = rows // mask.shape[1]
        s = jnp.where(mask, s.reshape(heads, *mask.shape[1:]), NEG_INF).reshape(rows, tk)
    m_old = m_ref[...]
    m_new = jnp.maximum(m_old, jnp.max(s, axis=-1, keepdims=True))
    e = jnp.exp2(s - jnp.concatenate([m_new] * (tk // LANES), axis=1))
    if mask is not None:
        e = jnp.where(mask, e.reshape(heads, *mask.shape[1:]), 0.0).reshape(rows, tk)
    alpha = jnp.exp2(m_old - m_new)
    acc_ref[...] = (jnp.concatenate([alpha, alpha], axis=1) * acc_ref[...]
                    + _dot(e.astype(BF16), v_ext))
    m_ref[...] = m_new


def _softmax_finish(acc_ref):
    acc = acc_ref[...]
    return acc[:, :LANES] / jnp.maximum(acc[:, LANES:], TINY)


def _with_ones(v):
    return jnp.concatenate([v, jnp.ones(v.shape, v.dtype)], axis=1)


def _pipelined_causal_tiles(n_full, scores, update, sa_ref, sb_ref, causal, filler=None):
    def pair(i, carry):
        scores(2 * i + 1, sb_ref)
        update(sa_ref, 2 * i, None)
        scores(2 * i + 2, sa_ref)
        update(sb_ref, 2 * i + 1, None)
        return carry

    scores(0, sa_ref)
    if filler is not None:
        filler()
    lax.fori_loop(0, n_full // 2, pair, 0)

    @pl.when(n_full % 2 == 1)
    def _():
        scores(n_full, sb_ref)
        update(sa_ref, n_full - 1, None)
        update(sb_ref, n_full, causal)

    @pl.when(n_full % 2 == 0)
    def _():
        update(sa_ref, n_full, causal)


def _mla_kernel(q_ref, k_ref, v_ref, o_ref, m_ref, acc_ref, sa_ref, sb_ref):
    qi = pl.program_id(1)
    heads, tq = q_ref.shape[1:3]
    tk = ATT_TK
    _softmax_init(m_ref, acc_ref)
    n_full = qi * tq // tk
    owns = [slice(hd * tq, (hd + 1) * tq) for hd in range(heads)]

    def scores(ki, s_ref):
        rows = pl.ds(pl.multiple_of(ki * tk, tk), tk)
        for hd, own in enumerate(owns):
            s_ref[own] = _dot_nt(q_ref[0, hd], k_ref[0, hd, rows, :])

    def update(s_ref, ki, mask):
        rows = pl.ds(pl.multiple_of(ki * tk, tk), tk)
        for hd, own in enumerate(owns):
            v_ext = _with_ones(v_ref[0, hd, rows, :])
            if mask is None:
                _softmax_tile(s_ref[own], None, v_ext, m_ref.at[own], acc_ref.at[own])
            else:
                half = tq // 2
                top = slice(own.start, own.start + half)
                low = slice(own.start + half, own.stop)
                _softmax_tile(s_ref[top, :half], mask[:, :half, :half], v_ext[:half],
                              m_ref.at[top], acc_ref.at[top])
                _softmax_tile(s_ref[low], mask[:, half:, :], v_ext, m_ref.at[low], acc_ref.at[low])

    qpos = qi * tq + lax.broadcasted_iota(jnp.int32, (1, tq, tk), 1)
    kpos = n_full * tk + lax.broadcasted_iota(jnp.int32, (1, tq, tk), 2)
    _pipelined_causal_tiles(n_full, scores, update, sa_ref, sb_ref, kpos <= qpos)
    for hd in range(heads):
        o_ref[0, :, hd * MLA_V:(hd + 1) * MLA_V] = _softmax_finish(
            acc_ref.at[hd * tq:(hd + 1) * tq]).astype(o_ref.dtype)


def _mla(q, k, v):
    bsz, heads, seq, dk = q.shape
    tq = ATT_TQ
    assert tq == ATT_TK
    return pl.pallas_call(
        _mla_kernel,
        out_shape=jax.ShapeDtypeStruct((bsz, seq, heads * MLA_V), BF16),
        grid=(bsz, seq // tq),
        in_specs=[pl.BlockSpec((1, heads, tq, dk), lambda b, qi: (b, 0, qi, 0)),
                  pl.BlockSpec((1, heads, seq, dk), lambda b, qi: (b, 0, 0, 0)),
                  pl.BlockSpec((1, heads, seq, MLA_V), lambda b, qi: (b, 0, 0, 0))],
        out_specs=pl.BlockSpec((1, tq, heads * MLA_V), lambda b, qi: (b, qi, 0)),
        scratch_shapes=[pltpu.VMEM((heads * tq, LANES), F32), pltpu.VMEM((heads * tq, 2 * LANES), F32),
                        pltpu.VMEM((heads * tq, ATT_TK), F32), pltpu.VMEM((heads * tq, ATT_TK), F32)],
        compiler_params=_params(("parallel", "arbitrary")),
        name="mla_attention",
    )(q, k, v)


def _ev_out_kernel(ya_ref, o_ref, zb_ref, x_ref, mod_ref, g_ref, wa_ref, wb_ref, out_ref):
    d = x_ref.shape[-1]
    yb = (o_ref[0].astype(F32) * _silu(zb_ref[0].astype(F32))).astype(BF16)
    y = _dot(ya_ref[0], wa_ref[...]) + _dot(yb, wb_ref[...])
    gate = mod_ref[0][:, 2 * d:]
    out_ref[0] = x_ref[0] + gate * _rms(y, g_ref[...])


def _ev_out(y_a, o_mla, z_b, x, mod, post_g, w_out):
    bsz, seq, d = x.shape
    tm = TOKEN_TILE
    wa = w_out[:y_a.shape[-1]].astype(BF16)
    wb = w_out[y_a.shape[-1]:].astype(BF16)
    g = post_g[None]
    tok = lambda n: pl.BlockSpec((1, tm, n), lambda b, i: (b, i, 0))
    full = lambda a: pl.BlockSpec(a.shape, lambda b, i: (0,) * a.ndim)
    return pl.pallas_call(
        _ev_out_kernel,
        out_shape=jax.ShapeDtypeStruct(x.shape, F32),
        grid=(bsz, seq // tm),
        in_specs=[tok(y_a.shape[-1]), tok(o_mla.shape[-1]), tok(z_b.shape[-1]), tok(d),
                  pl.BlockSpec((1, 1, 3 * d), lambda b, i: (b, 0, 0)), full(g), full(wa), full(wb)],
        out_specs=tok(d),
        compiler_params=_params(("parallel", "parallel")),
        name="even_out_proj",
    )(y_a, o_mla, z_b, x, mod[:, None, :], g, wa, wb)


OD_Q, OD_KV, OD_Z = 1024, 256, 1024
OD_NGATE = 3 * NSA_HEADS
OD_GATE = LANES
OD_OFF = np.cumsum([0, OD_Q] + [OD_KV] * 6 + [OD_NGATE, OD_Z])
OD_COLS = -(-OD_OFF[-1] // LANES) * LANES


def _od_in_kernel(x_ref, mod_ref, g_ref, w_ref, cn_ref, sn_ref,
                  q_ref, kc_ref, vc_ref, ks_ref, vs_ref, kw_ref, vw_ref, gate_ref, z_ref):
    h = _modulated_input(x_ref, mod_ref, g_ref)
    p = _dot(h, w_ref[...])
    o = OD_OFF
    cos, sin = cn_ref[0], sn_ref[0]

    def rope(t):
        return t * cos + pltpu.roll(t, NSA_DIM // 2, axis=1) * sin

    scale = NSA_DIM ** -0.5 * LOG2E
    for hd in range(NSA_HEADS):
        t = p[:, hd * NSA_DIM:(hd + 1) * NSA_DIM]
        q_ref[0, hd // NSA_REP, hd % NSA_REP] = (rope(t) * scale).astype(BF16)
    plain = (kc_ref, vc_ref, None, vs_ref, None, vw_ref)
    roped = (None, None, ks_ref, None, kw_ref, None)
    for j in range(6):
        for g in range(NSA_GROUPS):
            lo = o[1 + j] + g * NSA_DIM
            t = p[:, lo:lo + NSA_DIM]
            if plain[j] is not None:
                plain[j][0, g] = t.astype(plain[j].dtype)
            else:
                roped[j][0, g] = rope(t).astype(BF16)
    lane = lax.broadcasted_iota(jnp.int32, cos.shape, 1)
    gate_ref[0] = jnp.where(lane < OD_NGATE, jax.nn.sigmoid(p[:, o[7]:o[7] + LANES]), 0.0)
    z_ref[0] = p[:, o[8]:o[9]].astype(BF16)


def _od_in(x, mod, pre_g, w_in, cos_n, sin_n):
    bsz, seq, d = x.shape
    tm = TOKEN_TILE
    w = jnp.pad(w_in, ((0, 0), (0, OD_COLS - w_in.shape[1]))).astype(BF16)
    g = pre_g[None]
    tok = lambda n: pl.BlockSpec((1, tm, n), lambda b, i: (b, i, 0))
    full = lambda a: pl.BlockSpec(a.shape, lambda b, i: (0,) * a.ndim)
    q_spec = pl.BlockSpec((1, NSA_GROUPS, NSA_REP, tm, NSA_DIM), lambda b, i: (b, 0, 0, i, 0))
    kv_spec = pl.BlockSpec((1, NSA_GROUPS, tm, NSA_DIM), lambda b, i: (b, 0, i, 0))
    kv_shape = jax.ShapeDtypeStruct((bsz, NSA_GROUPS, seq, NSA_DIM), BF16)
    cmp_shape = jax.ShapeDtypeStruct(kv_shape.shape, F32)
    return pl.pallas_call(
        _od_in_kernel,
        out_shape=(jax.ShapeDtypeStruct((bsz, NSA_GROUPS, NSA_REP, seq, NSA_DIM), BF16),)
        + (cmp_shape,) * 2 + (kv_shape,) * 4
        + (jax.ShapeDtypeStruct((bsz, seq, OD_GATE), F32), jax.ShapeDtypeStruct((bsz, seq, OD_Z), BF16)),
        grid=(bsz, seq // tm),
        in_specs=[tok(d), pl.BlockSpec((1, 1, 3 * d), lambda b, i: (b, 0, 0)), full(g), full(w),
                  tok(LANES), tok(LANES)],
        out_specs=(q_spec,) + (kv_spec,) * 6 + (tok(OD_GATE), tok(OD_Z)),
        compiler_params=_params(("parallel", "parallel")),
        name="odd_in_proj",
    )(x, mod[:, None, :], g, w, cos_n, sin_n)


def _compress_kernel(x_ref, pe_ref, w1_ref, w2_ref, cos_ref, sin_ref, o_ref, *, use_rope):
    seq, d = x_ref.shape[2:]
    nb = seq // CMP_STRIDE
    lo = _dot(pe_ref[...], w1_ref[...])[0:1]
    hi = jnp.zeros((nb, w1_ref.shape[1]), F32)
    for l in range(CMP_STRIDE):
        xl = x_ref[0, 0, pl.ds(l, nb, stride=CMP_STRIDE), :].astype(BF16)
        lo = lo + _dot(xl, w1_ref[l * d:(l + 1) * d])
        hi = hi + _dot(xl, w1_ref[(CMP_STRIDE + l) * d:(CMP_STRIDE + l + 1) * d])
    row = lax.broadcasted_iota(jnp.int32, hi.shape, 0)
    pre = lo + jnp.where(row < nb - 1, pltpu.roll(hi, nb - 1, axis=0), 0.0)
    out = _dot(_gelu_tanh(pre).astype(BF16), w2_ref[...])
    if use_rope:
        out = out * cos_ref[0] + pltpu.roll(out, NSA_DIM // 2, axis=1) * sin_ref[0]
    o_ref[0, 0] = out.astype(BF16)


def _compress(kv, pe, w1, w2, cos_end, sin_end, use_rope):
    bsz, groups, seq, d = kv.shape
    nb = seq // CMP_STRIDE
    pe_rows = jnp.broadcast_to(pe.reshape(1, CMP_LEN * d), (SUBLANES, CMP_LEN * d)).astype(BF16)
    w1b, w2b = w1.astype(BF16), w2.astype(BF16)
    full = lambda a: pl.BlockSpec(a.shape, lambda b, g: (0,) * a.ndim)
    end = pl.BlockSpec((1, nb, d), lambda b, g: (b, 0, 0))
    return pl.pallas_call(
        functools.partial(_compress_kernel, use_rope=use_rope),
        out_shape=jax.ShapeDtypeStruct((bsz, groups, nb, d), BF16),
        grid=(bsz, groups),
        in_specs=[pl.BlockSpec((1, 1, seq, d), lambda b, g: (b, g, 0, 0)),
                  full(pe_rows), full(w1b), full(w2b), end, end],
        out_specs=pl.BlockSpec((1, 1, nb, d), lambda b, g: (b, g, 0, 0)),
        compiler_params=_params(("parallel", "parallel")),
        name="nsa_compress",
    )(kv, pe_rows, w1b, w2b, cos_end, sin_end)


def _cmp_sel_kernel(q_ref, kc_ref, vc_ref, pool_ref, o_ref, sel_ref, **static):
    for g in range(q_ref.shape[1]):
        _cmp_sel_group(g, q_ref, kc_ref, vc_ref, pool_ref, o_ref, sel_ref, **static)


def _cmp_sel_group(g, q_ref, kc_ref, vc_ref, pool_ref, o_ref, sel_ref, *, n_cmp, n_sel, n_top):
    qi = pl.program_id(1)
    rep, tq, d = q_ref.shape[2:]
    nb = kc_ref.shape[2]
    q = q_ref[0, g].reshape(rep * tq, d)
    s = _dot_nt(q, kc_ref[0, g]).reshape(rep, tq, nb)
    qpos = qi * tq + lax.broadcasted_iota(jnp.int32, (tq, nb), 0)
    blk = lax.broadcasted_iota(jnp.int32, (tq, nb), 1)
    mask = ((blk * CMP_STRIDE + (CMP_LEN - 1) <= qpos) & (blk < n_cmp))[None]
    s = jnp.where(mask, s, NEG_INF)
    m = jnp.max(s, axis=-1, keepdims=True)
    e = jnp.where(mask, jnp.exp2(s - m), 0.0)
    p = e / jnp.maximum(jnp.sum(e, axis=-1, keepdims=True), TINY)
    o = _dot(p.reshape(rep * tq, nb).astype(BF16), vc_ref[0, g])
    o_ref[0, g] = o.reshape(rep, tq, d).astype(o_ref.dtype)

    rows = pool_ref.shape[0]
    imp = _dot_nt(pool_ref[...], jnp.sum(p, axis=0), precision=HIGHEST)[:n_sel]
    bid = lax.broadcasted_iota(jnp.int32, (n_sel, tq), 0)
    cur = (qi * tq + lax.broadcasted_iota(jnp.int32, (n_sel, tq), 1)) // SEL_BLOCK
    forced = (bid == 0) | (bid == cur) | (bid == cur - 1)
    imp = jnp.where(forced, FORCE_SCORE, jnp.where(bid <= cur, imp, -1.0))
    groups = [imp[g:g + SUBLANES] for g in range(0, n_sel, SUBLANES)]
    sub = lax.broadcasted_iota(jnp.int32, (SUBLANES, tq), 0)
    ranks = [jnp.zeros((SUBLANES, tq), F32) for _ in groups]
    for j in range(n_sel):
        vj = jnp.broadcast_to(imp[j:j + 1, :], (SUBLANES, tq))
        for gi, grp in enumerate(groups):
            lo = gi * SUBLANES
            if lo > j:
                first = vj >= grp
            elif lo + SUBLANES - 1 <= j:
                first = vj > grp
            else:
                first = (vj > grp) | ((vj == grp) & (sub > j - lo))
            ranks[gi] = ranks[gi] + jnp.where(first, 1.0, 0.0)
    rank = jnp.concatenate(ranks, axis=0)
    bias = jnp.where((rank < n_top) & (bid <= cur), 0.0, NEG_INF)
    bias = jnp.concatenate([bias, jnp.full((rows - n_sel, tq), NEG_INF, F32)], axis=0)
    sel_ref[0, g] = bias.T.astype(sel_ref.dtype)


def _cmp_sel(q, kc, vc):
    bsz, groups, rep, seq, d = q.shape
    nb = kc.shape[2]
    tq = NSA_TQ
    n_cmp = (seq - CMP_LEN) // CMP_STRIDE + 1
    n_sel = seq // SEL_BLOCK
    n_top = min(SEL_TOP, n_sel)
    ratio = SEL_BLOCK // CMP_STRIDE
    assert n_sel <= LANES and n_sel * ratio == nb
    pool_np = np.zeros((LANES, nb), np.float32)
    pool_np[np.arange(nb) // ratio, np.arange(nb)] = 1.0
    pool = jnp.asarray(pool_np)
    kv_spec = pl.BlockSpec((1, groups, nb, d), lambda b, i: (b, 0, 0, 0))
    q_spec = pl.BlockSpec((1, groups, rep, tq, d), lambda b, i: (b, 0, 0, i, 0))
    return pl.pallas_call(
        functools.partial(_cmp_sel_kernel, n_cmp=n_cmp, n_sel=n_sel, n_top=n_top),
        out_shape=(jax.ShapeDtypeStruct(q.shape, BF16),
                   jax.ShapeDtypeStruct((bsz, groups, seq, LANES), BF16)),
        grid=(bsz, seq // tq),
        in_specs=[q_spec, kv_spec, kv_spec, pl.BlockSpec(pool.shape, lambda b, i: (0, 0))],
        out_specs=(q_spec, pl.BlockSpec((1, groups, tq, LANES), lambda b, i: (b, 0, i, 0))),
        compiler_params=_params(("parallel", "parallel")),
        name="nsa_cmp_select",
    )(q, kc, vc, pool)


def _sel_win_kernel(q_ref, bias_ref, k_ref, blk_ref, v_ref, kw_ref, vw_ref, o_ref, ow_ref,
                    m_ref, acc_ref, sa_ref, sb_ref, *, span):
    qi = pl.program_id(2)
    rep, tq, d = q_ref.shape[2:]
    tk = NSA_TK
    part = rep // SEL_CHAINS
    q_ext = [jnp.concatenate([q_ref[0, 0, c * part:(c + 1) * part].reshape(part * tq, d),
                              jnp.concatenate([bias_ref[0, 0]] * part, axis=0)], axis=1)
             for c in range(SEL_CHAINS)]
    _softmax_init(m_ref, acc_ref)
    n_full = qi * tq // tk

    owns = [slice(c * part * tq, (c + 1) * part * tq) for c in range(SEL_CHAINS)]

    def scores(ki, s_ref):
        rows = pl.ds(pl.multiple_of(ki * tk, tk), tk)
        k_ext = jnp.concatenate([k_ref[0, 0, rows, :], blk_ref[rows, :]], axis=1)
        for c, own in enumerate(owns):
            s_ref[own] = _dot_nt(q_ext[c], k_ext)

    def update(s_ref, ki, mask):
        def run(width):
            rows = pl.ds(pl.multiple_of(ki * tk, tk), width)
            v_ext = _with_ones(v_ref[0, 0, rows, :])
            sub = None if mask is None else mask[:, :, :width]
            for own in owns:
                _softmax_tile(s_ref[own, :width], sub, v_ext, m_ref.at[own], acc_ref.at[own])

        if mask is None:
            run(tk)
        else:
            short = qi * tq + tq <= ki * tk + tk // 2
            pl.when(short)(lambda: run(tk // 2))
            pl.when(jnp.logical_not(short))(lambda: run(tk))

    def window():
        start = pl.multiple_of(jnp.maximum(qi * tq + tq - span, 0), tq)
        rows = pl.ds(start, span)
        wq = qi * tq + lax.broadcasted_iota(jnp.int32, (1, tq, span), 1)
        wk = start + lax.broadcasted_iota(jnp.int32, (1, tq, span), 2)
        mask = (wq - wk >= 0) & (wq - wk < WINDOW)
        k, v_ext = kw_ref[0, 0, rows, :], _with_ones(vw_ref[0, 0, rows, :])
        wpart = rep // WIN_CHAINS
        for h0 in range(0, rep, wpart):
            q = q_ref[0, 0, h0:h0 + wpart].reshape(wpart * tq, d)
            s = jnp.where(mask, _dot_nt(q, k).reshape(wpart, tq, span), NEG_INF)
            m = jnp.max(s, axis=-1, keepdims=True)
            e = jnp.where(mask, jnp.exp2(s - m), 0.0).reshape(wpart * tq, span)
            acc = _dot(e.astype(BF16), v_ext)
            o = acc[:, :LANES] / jnp.maximum(acc[:, LANES:], TINY)
            ow_ref[0, 0, h0:h0 + wpart] = o.reshape(wpart, tq, d).astype(ow_ref.dtype)

    qpos = qi * tq + lax.broadcasted_iota(jnp.int32, (1, tq, tk), 1)
    kpos = n_full * tk + lax.broadcasted_iota(jnp.int32, (1, tq, tk), 2)
    _pipelined_causal_tiles(n_full, scores, update, sa_ref, sb_ref, kpos <= qpos, filler=window)
    o_ref[0, 0] = _softmax_finish(acc_ref).reshape(rep, tq, d).astype(o_ref.dtype)


def _sel_win_attention(q, bias, k, v, kw, vw):
    bsz, groups, rep, seq, d = q.shape
    tq = NSA_TQ
    assert NSA_TK % tq == 0
    span = (-(-(WINDOW - 1) // tq) + 1) * tq
    assert span <= seq
    onehot_np = np.zeros((seq, LANES), np.float32)
    onehot_np[np.arange(seq), np.arange(seq) // SEL_BLOCK] = 1.0
    onehot = jnp.asarray(onehot_np, BF16)
    q_spec = pl.BlockSpec((1, 1, rep, tq, d), lambda b, g, qi: (b, g, 0, qi, 0))
    kv_spec = pl.BlockSpec((1, 1, seq, d), lambda b, g, qi: (b, g, 0, 0))
    out = jax.ShapeDtypeStruct(q.shape, BF16)
    return pl.pallas_call(
        functools.partial(_sel_win_kernel, span=span),
        out_shape=(out, out),
        grid=(bsz, groups, seq // tq),
        in_specs=[q_spec, pl.BlockSpec((1, 1, tq, LANES), lambda b, g, qi: (b, g, qi, 0)),
                  kv_spec, pl.BlockSpec((seq, LANES), lambda b, g, qi: (0, 0)), kv_spec, kv_spec, kv_spec],
        out_specs=(q_spec, q_spec),
        scratch_shapes=[pltpu.VMEM((rep * tq, LANES), F32), pltpu.VMEM((rep * tq, 2 * LANES), F32),
                        pltpu.VMEM((rep * tq, NSA_TK), F32), pltpu.VMEM((rep * tq, NSA_TK), F32)],
        compiler_params=_params(("parallel", "parallel", "arbitrary")),
        name="nsa_selected_window_attention",
    )(q, bias, k, onehot, v, kw, vw)


def _od_out_kernel(oc_ref, os_ref, ow_ref, gate_ref, spread_ref, z_ref, x_ref, mod_ref, g_ref, w_ref, out_ref):
    d = x_ref.shape[-1]
    gates = gate_ref[0]
    hi = gates.astype(BF16)
    lo = (gates - hi.astype(F32)).astype(BF16)
    wide = _dot(jnp.concatenate([hi, lo], axis=-1), spread_ref[...])
    z = z_ref[0].astype(F32)
    parts = []
    for hd in range(NSA_HEADS):
        g, r = hd // NSA_REP, hd % NSA_REP
        c = 3 * hd * NSA_DIM
        o = (wide[:, c:c + NSA_DIM] * oc_ref[0, g, r].astype(F32)
             + wide[:, c + NSA_DIM:c + 2 * NSA_DIM] * os_ref[0, g, r].astype(F32)
             + wide[:, c + 2 * NSA_DIM:c + 3 * NSA_DIM] * ow_ref[0, g, r].astype(F32))
        parts.append((o * _silu(z[:, hd * NSA_DIM:(hd + 1) * NSA_DIM])).astype(BF16))
    y = _dot(jnp.concatenate(parts, axis=-1), w_ref[...])
    gate = mod_ref[0][:, 2 * d:]
    out_ref[0] = x_ref[0] + gate * _rms(y, g_ref[...])


def _od_out(o_cmp, o_sel, o_win, gates, z, x, mod, post_g, w_out):
    bsz, seq, d = x.shape
    tm = TOKEN_TILE
    w = w_out.astype(BF16)
    g = post_g[None]
    n_gate = 3 * NSA_HEADS
    spread_np = np.zeros((2 * OD_GATE, n_gate * NSA_DIM), np.float32)
    for c in range(n_gate):
        spread_np[[c, OD_GATE + c], c * NSA_DIM:(c + 1) * NSA_DIM] = 1.0
    spread = jnp.asarray(spread_np, BF16)
    tok = lambda n: pl.BlockSpec((1, tm, n), lambda b, i: (b, i, 0))
    full = lambda a: pl.BlockSpec(a.shape, lambda b, i: (0,) * a.ndim)
    o_spec = pl.BlockSpec((1, NSA_GROUPS, NSA_REP, tm, NSA_DIM), lambda b, i: (b, 0, 0, i, 0))
    return pl.pallas_call(
        _od_out_kernel,
        out_shape=jax.ShapeDtypeStruct(x.shape, F32),
        grid=(bsz, seq // tm),
        in_specs=[o_spec, o_spec, o_spec, tok(OD_GATE), full(spread), tok(OD_Z), tok(d),
                  pl.BlockSpec((1, 1, 3 * d), lambda b, i: (b, 0, 0)), full(g), full(w)],
        out_specs=tok(d),
        compiler_params=_params(("parallel", "parallel")),
        name="odd_out_proj",
    )(o_cmp, o_sel, o_win, gates, spread, z, x, mod[:, None, :], g, w)


def _even_layer(x, mod, pre_g, post_g, tabs, w_in, lam_re, lam_im, log_dt, b_re, b_im, c_re, c_im,
                d_skip, w_glu, b_glu, q_norm_g, kv_norm_g, w_uq, w_ukv, w_out):
    cos_m, sin_m = tabs[0], tabs[1]
    u, z_a, z_b, q, k, v = _ev_in(x, mod, pre_g, w_in, q_norm_g, kv_norm_g, w_uq, w_ukv, cos_m, sin_m)
    y_a = _s5(u, z_a, lam_re, lam_im, log_dt, b_re, b_im, c_re, c_im, d_skip, w_glu, b_glu)
    o_mla = _mla(q, k, v)
    return _ev_out(y_a, o_mla, z_b, x, mod, post_g, w_out)


def _odd_layer(x, mod, pre_g, post_g, tabs, end_tabs, w_in, k_pe, k_w1, k_w2, v_pe, v_w1, v_w2, w_out):
    cos_n, sin_n = tabs[2], tabs[3]
    cos_end, sin_end = end_tabs[2], end_tabs[3]
    q, k_c, v_c, k_s, v_s, k_w, v_w, gates, z = _od_in(x, mod, pre_g, w_in, cos_n, sin_n)
    kc = _compress(k_c, k_pe, k_w1, k_w2, cos_end, sin_end, True)
    vc = _compress(v_c, v_pe, v_w1, v_w2, cos_end, sin_end, False)
    o_cmp, sel = _cmp_sel(q, kc, vc)
    o_sel, o_win = _sel_win_attention(q, sel, k_s, v_s, k_w, v_w)
    return _od_out(o_cmp, o_sel, o_win, gates, z, x, mod, post_g, w_out)


def kernel(x, c, positions, pre_norm_g, post_norm_g, w_ada, b_ada, ev_w_in, ev_lam_re, ev_lam_im, ev_log_dt, ev_b_re, ev_b_im, ev_c_re, ev_c_im, ev_d_skip, ev_w_glu, ev_b_glu, ev_q_norm_g, ev_kv_norm_g, ev_w_uq, ev_w_ukv, ev_w_out, od_w_in, od_cmp_k_pe, od_cmp_k_w1, od_cmp_k_w2, od_cmp_v_pe, od_cmp_v_w1, od_cmp_v_w2, od_w_out):
    depth = pre_norm_g.shape[0]
    tabs = _rope_tables(positions)
    seq = positions.shape[1]
    pos_end = positions[:, CMP_LEN - 1::CMP_STRIDE]
    pos_end = jnp.pad(pos_end, ((0, 0), (0, seq // CMP_STRIDE - pos_end.shape[1])))
    end_tabs = _rope_tables(pos_end)
    mods = _modulation(c, w_ada, b_ada)
    for layer in range(depth):
        i = layer // 2
        if layer % 2 == 0:
            x = _even_layer(x, mods[layer], pre_norm_g[layer], post_norm_g[layer], tabs,
                            ev_w_in[i], ev_lam_re[i], ev_lam_im[i], ev_log_dt[i], ev_b_re[i], ev_b_im[i],
                            ev_c_re[i], ev_c_im[i], ev_d_skip[i], ev_w_glu[i], ev_b_glu[i],
                            ev_q_norm_g[i], ev_kv_norm_g[i], ev_w_uq[i], ev_w_ukv[i], ev_w_out[i])
        else:
            x = _odd_layer(x, mods[layer], pre_norm_g[layer], post_norm_g[layer], tabs, end_tabs,
                           od_w_in[i], od_cmp_k_pe[i], od_cmp_k_w1[i], od_cmp_k_w2[i],
                           od_cmp_v_pe[i], od_cmp_v_w1[i], od_cmp_v_w2[i], od_w_out[i])
    return x
```

```python
import functools
import math

import numpy as np
import jax
import jax.numpy as jnp
from jax import lax
from jax.experimental import pallas as pl
from jax.experimental.pallas import tpu as pltpu

F32 = jnp.float32
BF16 = jnp.bfloat16
HIGHEST = lax.Precision.HIGHEST

EPS = 1e-6
ROPE_THETA = 10000.0
NEG_INF = -1e30
TINY = 1e-30
FORCE_SCORE = 1e9
LOG2E = 1.0 / math.log(2.0)

S5_GROUP = 16
S5_STATE = 64
MLA_HEADS = 4
MLA_NOPE = 128
MLA_ROPE = 64
MLA_V = 128
NSA_HEADS = 8
NSA_GROUPS = 2
NSA_REP = NSA_HEADS // NSA_GROUPS
NSA_DIM = 128
CMP_LEN = 32
CMP_STRIDE = 16
CMP_HIDDEN = 256
SEL_BLOCK = 64
SEL_TOP = 16
WINDOW = 512

LANES = 128
SUBLANES = 8
VMEM_LIMIT = 48 * 1024 * 1024

TOKEN_TILE = 1024
S5_CHUNK = 256
S5_COLS = 512
ATT_TQ = 512
ATT_TK = 512
NSA_TQ = 256
NSA_TK = 512
SEL_CHAINS = 4
WIN_CHAINS = 4


def _params(sem):
    return pltpu.CompilerParams(dimension_semantics=sem, vmem_limit_bytes=VMEM_LIMIT)


def _dot(a, b, precision=None):
    return jnp.dot(a, b, preferred_element_type=F32, precision=precision)


def _dot_nt(a, b, precision=None):
    return lax.dot_general(a, b, (((1,), (1,)), ((), ())), preferred_element_type=F32,
                           precision=precision)


def _silu(v):
    return v * jax.nn.sigmoid(v)


def _gelu_tanh(v):
    return 0.5 * v * (1.0 + jnp.tanh(math.sqrt(2.0 / math.pi) * (v + 0.044715 * (v * v * v))))


def _rms(v, g):
    return v * lax.rsqrt(jnp.mean(v * v, axis=-1, keepdims=True) + EPS) * g


def _rope_tab_kernel(pos_ref, f_ref, sgm_ref, sgn_ref, cm_ref, sm_ref, cn_ref, sn_ref):
    hn, hm = NSA_DIM // 2, MLA_ROPE // 2
    ang = pos_ref[0] * f_ref[...]
    c, s = jnp.cos(ang), jnp.sin(ang)
    cn_ref[0] = jnp.concatenate([c[:, :hn]] * (LANES // hn), axis=1)
    sn_ref[0] = jnp.concatenate([s[:, :hn]] * (LANES // hn), axis=1) * sgn_ref[...]
    cm_ref[0] = jnp.concatenate([c[:, hn:hn + hm]] * (LANES // hm), axis=1)
    sm_ref[0] = jnp.concatenate([s[:, hn:hn + hm]] * (LANES // hm), axis=1) * sgm_ref[...]


def _rope_tables(positions):
    bsz, seq = positions.shape
    ts = min(TOKEN_TILE, seq)
    pos = positions.astype(F32)[..., None]
    half_m, half_n = MLA_ROPE // 2, NSA_DIM // 2
    fm = ROPE_THETA ** (-jnp.arange(half_m, dtype=F32) / half_m)
    fn = ROPE_THETA ** (-jnp.arange(half_n, dtype=F32) / half_n)
    freqs = jnp.concatenate([fn, fm, jnp.zeros((LANES - half_n - half_m,), F32)])[None]
    sgm = jnp.concatenate([-jnp.ones((half_m,), F32), jnp.ones((half_m,), F32),
                           jnp.zeros((LANES - MLA_ROPE,), F32)])[None]
    sgn = jnp.concatenate([-jnp.ones((half_n,), F32), jnp.ones((half_n,), F32)])[None]
    row = pl.BlockSpec((1, LANES), lambda b, i: (0, 0))
    tab = pl.BlockSpec((1, ts, LANES), lambda b, i: (b, i, 0))
    shp = jax.ShapeDtypeStruct((bsz, seq, LANES), F32)
    return pl.pallas_call(
        _rope_tab_kernel,
        out_shape=(shp, shp, shp, shp),
        grid=(bsz, seq // ts),
        in_specs=[pl.BlockSpec((1, ts, 1), lambda b, i: (b, i, 0)), row, row, row],
        out_specs=(tab, tab, tab, tab),
        compiler_params=_params(("parallel", "parallel")),
        name="rope_tables",
    )(pos, freqs, sgm, sgn)


def _mod_kernel(c_ref, w_ref, b_ref, o_ref):
    s = _silu(c_ref[...])
    o_ref[0] = _dot(s, w_ref[0], precision=HIGHEST) + b_ref[0]


def _modulation(c, w_ada, b_ada):
    depth, d, _ = w_ada.shape
    bsz = c.shape[0]
    rows = -(-bsz // SUBLANES) * SUBLANES
    c_pad = jnp.pad(c, ((0, rows - bsz), (0, 0)))
    out = pl.pallas_call(
        _mod_kernel,
        out_shape=jax.ShapeDtypeStruct((depth, rows, 3 * d), F32),
        grid=(depth, 3),
        in_specs=[pl.BlockSpec((rows, d), lambda l, j: (0, 0)),
                  pl.BlockSpec((1, d, d), lambda l, j: (l, 0, j)),
                  pl.BlockSpec((1, 1, d), lambda l, j: (l, 0, j))],
        out_specs=pl.BlockSpec((1, rows, d), lambda l, j: (l, 0, j)),
        compiler_params=_params(("parallel", "parallel")),
        name="modulation",
    )(c_pad, w_ada, b_ada[:, None, :])
    return out[:, :bsz]


def _modulated_input(x_ref, mod_ref, g_ref):
    d = x_ref.shape[-1]
    x = x_ref[0]
    mod = mod_ref[0]
    shift, scale = mod[:, :d], mod[:, d:2 * d]
    return (_rms(x, g_ref[...]) * (1.0 + scale) + shift).astype(BF16)


EV_U, EV_ZA, EV_CQ, EV_CKV, EV_KPE, EV_ZB = 512, 512, 768, 256, MLA_ROPE, 512
EV_OFF = np.cumsum([0, EV_U, EV_ZA, EV_CQ, EV_CKV, EV_KPE, EV_ZB])
EV_COLS = -(-EV_OFF[-1] // LANES) * LANES


def _ev_in_kernel(x_ref, mod_ref, g_ref, w_ref, gq_ref, gkv_ref, wuq_ref, wukv_ref, cm_ref, sm_ref,
                  u_ref, za_ref, zb_ref, q_ref, k_ref, v_ref):
    h = _modulated_input(x_ref, mod_ref, g_ref)
    p = _dot(h, w_ref[...])
    o = EV_OFF
    u_ref[0] = p[:, o[0]:o[1]].astype(BF16)
    za_ref[0] = p[:, o[1]:o[2]].astype(BF16)
    zb_ref[0] = p[:, o[5]:o[6]].astype(BF16)
    cos, sin = cm_ref[0], sm_ref[0]
    hq = MLA_ROPE // 2
    lane = lax.broadcasted_iota(jnp.int32, cos.shape, 1)
    first = lane < hq

    def rope(t):
        swapped = jnp.where(first, pltpu.roll(t, LANES - hq, axis=1), pltpu.roll(t, hq, axis=1))
        return t * cos + swapped * sin

    cq = _rms(p[:, o[2]:o[3]], gq_ref[...]).astype(BF16)
    q = _dot(cq, wuq_ref[...]) * ((MLA_NOPE + MLA_ROPE) ** -0.5 * LOG2E)
    ckv = _rms(p[:, o[3]:o[4]], gkv_ref[...]).astype(BF16)
    kv = _dot(ckv, wukv_ref[...])
    kpe = rope(jnp.where(lane < MLA_ROPE, p[:, o[4]:o[4] + LANES], 0.0)).astype(BF16)
    nope = MLA_HEADS * MLA_NOPE
    for hd in range(MLA_HEADS):
        b0 = hd * (MLA_NOPE + LANES)
        qpe = rope(q[:, b0 + MLA_NOPE:b0 + MLA_NOPE + LANES])
        q_ref[0, hd] = jnp.concatenate([q[:, b0:b0 + MLA_NOPE], qpe], axis=-1).astype(BF16)
        k_ref[0, hd] = jnp.concatenate([kv[:, hd * MLA_NOPE:(hd + 1) * MLA_NOPE].astype(BF16), kpe], axis=-1)
        v_ref[0, hd] = kv[:, nope + hd * MLA_V:nope + (hd + 1) * MLA_V].astype(BF16)


def _ev_in(x, mod, pre_g, w_in, q_norm_g, kv_norm_g, w_uq, w_ukv, cos_m, sin_m):
    bsz, seq, d = x.shape
    tm = TOKEN_TILE
    w = w_in
    rank = w_uq.shape[0]
    wuq = jnp.pad(w_uq.reshape(rank, MLA_HEADS, MLA_NOPE + MLA_ROPE), ((0, 0), (0, 0), (0, LANES - MLA_ROPE)))
    wuq = wuq.reshape(rank, MLA_HEADS * (MLA_NOPE + LANES)).astype(BF16)
    rank = w_ukv.shape[0]
    wukv = jnp.concatenate([w_ukv.reshape(rank, MLA_HEADS, MLA_NOPE + MLA_V)[:, :, :MLA_NOPE].reshape(rank, -1),
                            w_ukv.reshape(rank, MLA_HEADS, MLA_NOPE + MLA_V)[:, :, MLA_NOPE:].reshape(rank, -1)],
                           axis=1).astype(BF16)

    tok = lambda n: pl.BlockSpec((1, tm, n), lambda b, i: (b, i, 0))
    full = lambda a: pl.BlockSpec(a.shape, lambda b, i: (0,) * a.ndim)
    out = lambda n: jax.ShapeDtypeStruct((bsz, seq, n), BF16)
    head = lambda n: pl.BlockSpec((1, MLA_HEADS, tm, n), lambda b, i: (b, 0, i, 0))
    head_out = lambda n: jax.ShapeDtypeStruct((bsz, MLA_HEADS, seq, n), BF16)
    gq, gkv, g = q_norm_g[None], kv_norm_g[None], pre_g[None]
    return pl.pallas_call(
        _ev_in_kernel,
        out_shape=(out(512), out(512), out(512), head_out(2 * LANES), head_out(2 * LANES), head_out(MLA_V)),
        grid=(bsz, seq // tm),
        in_specs=[tok(d), pl.BlockSpec((1, 1, 3 * d), lambda b, i: (b, 0, 0)), full(g), full(w),
                  full(gq), full(gkv), full(wuq), full(wukv), tok(LANES), tok(LANES)],
        out_specs=(tok(512), tok(512), tok(512), head(2 * LANES), head(2 * LANES), head(MLA_V)),
        compiler_params=_params(("parallel", "parallel")),
        name="even_in_proj",
    )(x, mod[:, None, :], g, w, gq, gkv, wuq, wukv, cos_m, sin_m)


def _s5_kernel(u_ref, z_ref, perm_ref, permt_ref, wb_ref, wc_ref, a_ref, at_ref, pow_ref,
               d_ref, wglu_ref, bglu_ref, o_ref, bu_ref, xb_ref, state_ref, carry_ref):
    t_len = u_ref.shape[1]
    n = a_ref.shape[1] // 2
    steps = t_len // SUBLANES

    @pl.when(pl.program_id(1) == 0)
    def _():
        state_ref[...] = jnp.zeros_like(state_ref)

    perm = perm_ref[...]
    u_p = _dot(perm, u_ref[0])
    z_p = _dot(perm, z_ref[0])
    u_pb = u_p.astype(BF16)
    bre, bim = pl.ds(0, S5_COLS), pl.ds(S5_COLS, S5_COLS)
    y_parts = []

    for cb in range(n // S5_COLS):
        u_cb = u_pb[:, cb * LANES:(cb + 1) * LANES]
        bu_ref[...] = _dot(u_cb, wb_ref[cb])
        re = pl.ds(cb * S5_COLS, S5_COLS)
        im = pl.ds(n + cb * S5_COLS, S5_COLS)
        ar = jnp.broadcast_to(a_ref[:, re], (SUBLANES, S5_COLS))
        ai = jnp.broadcast_to(a_ref[:, im], (SUBLANES, S5_COLS))

        def step(t, carry):
            xr, xi = carry
            rows = pl.ds(pl.multiple_of(t * SUBLANES, SUBLANES), SUBLANES)
            nr = ar * xr - ai * xi + bu_ref[rows, bre]
            ni = ar * xi + ai * xr + bu_ref[rows, bim]
            bu_ref[rows, bre] = nr
            bu_ref[rows, bim] = ni
            return nr, ni

        zero = jnp.zeros((SUBLANES, S5_COLS), F32)
        er, ei = lax.fori_loop(0, steps, step, (zero, zero), unroll=True)

        sr, si = state_ref[:, re], state_ref[:, im]
        tr, ti = at_ref[:, re], at_ref[:, im]
        for j in range(SUBLANES):
            carry_ref[j:j + 1, re] = sr
            carry_ref[j:j + 1, im] = si
            sr, si = (tr * sr - ti * si + er[j:j + 1], tr * si + ti * sr + ei[j:j + 1])
        state_ref[:, re] = sr
        state_ref[:, im] = si
        cr, ci = carry_ref[:, re], carry_ref[:, im]

        def fix(t2, _):
            xr, xi = [], []
            for k in range(2):
                t = 2 * t2 + k
                rows = pl.ds(pl.multiple_of(t * SUBLANES, SUBLANES), SUBLANES)
                pr, pi = pow_ref[pl.ds(t, 1), re], pow_ref[pl.ds(t, 1), im]
                xr.append(bu_ref[rows, bre] + pr * cr - pi * ci)
                xi.append(bu_ref[rows, bim] + pr * ci + pi * cr)
            rows = pl.ds(pl.multiple_of(t2 * 2 * SUBLANES, 2 * SUBLANES), 2 * SUBLANES)
            xb_ref[rows, bre] = jnp.concatenate(xr, axis=0).astype(BF16)
            xb_ref[rows, bim] = jnp.concatenate(xi, axis=0).astype(BF16)
            return 0

        lax.fori_loop(0, steps // 2, fix, 0, unroll=True)
        y_parts.append(_dot(xb_ref[...], wc_ref[cb]))

    y = jnp.concatenate(y_parts, axis=1) + d_ref[...] * u_p
    g = _gelu_tanh(y)
    gate = jax.nn.sigmoid(_dot(g.astype(BF16), wglu_ref[...]) + bglu_ref[...])
    out = (g * gate * _silu(z_p)).astype(BF16)
    o_ref[0] = _dot(permt_ref[...], out).astype(BF16)


def _s5(u, z_a, lam_re, lam_im, log_dt, b_re, b_im, c_re, c_im, d_skip, w_glu, b_glu):
    bsz, seq, width = u.shape
    groups, state = lam_re.shape
    t_len = S5_CHUNK
    steps = t_len // SUBLANES
    n = groups * state
    dt = jnp.exp(log_dt)[:, None]
    lam_dt_re, lam_dt_im = lam_re * dt, lam_im * dt
    decay = jnp.exp(lam_dt_re)
    ab_re, ab_im = decay * jnp.cos(lam_dt_im), decay * jnp.sin(lam_dt_im)
    den = lam_re * lam_re + lam_im * lam_im
    nr, ni = ab_re - 1.0, ab_im
    f_re = (nr * lam_re + ni * lam_im) / den
    f_im = (ni * lam_re - nr * lam_im) / den
    bb_re = f_re[..., None] * b_re - f_im[..., None] * b_im
    bb_im = f_re[..., None] * b_im + f_im[..., None] * b_re
    gpb = S5_COLS // state
    nblk = groups // gpb
    hdim = width // groups
    assert gpb * hdim == LANES and nblk * S5_COLS == n
    eye = jnp.eye(gpb, dtype=F32)

    def in_blocks(bb):
        bb = bb.reshape(nblk, gpb, state, hdim)
        return jnp.einsum('mgph,gk->mghkp', bb, eye).reshape(nblk, gpb * hdim, gpb * state)

    def out_blocks(cc):
        cc = cc.reshape(nblk, gpb, hdim, state)
        return jnp.einsum('mghp,gk->mgpkh', cc, eye).reshape(nblk, gpb * state, gpb * hdim)

    wb = jnp.concatenate([in_blocks(bb_re), in_blocks(bb_im)], axis=2).astype(BF16)
    wc = jnp.concatenate([out_blocks(c_re), out_blocks(-c_im)], axis=1).astype(BF16)
    a_vec = jnp.concatenate([ab_re.reshape(1, n), ab_im.reshape(1, n)], axis=1)
    ks = jnp.arange(1, steps + 1, dtype=F32)[:, None, None]
    pw_mag = jnp.exp(lam_dt_re[None] * ks)
    pw_re, pw_im = pw_mag * jnp.cos(lam_dt_im[None] * ks), pw_mag * jnp.sin(lam_dt_im[None] * ks)
    pow_tab = jnp.concatenate([pw_re.reshape(steps, n), pw_im.reshape(steps, n)], axis=1)
    at_vec = pow_tab[steps - 1:steps]
    r = np.arange(t_len)
    perm_np = np.zeros((t_len, t_len), np.float32)
    perm_np[r, (r % SUBLANES) * steps + r // SUBLANES] = 1.0
    perm = jnp.asarray(perm_np, BF16)
    permt = jnp.asarray(perm_np.T, BF16)
    d_vec = d_skip.reshape(1, width)
    wglu = w_glu.astype(BF16)
    bglu = b_glu[None]

    tok = pl.BlockSpec((1, t_len, width), lambda b, i: (b, i, 0))
    full = lambda a: pl.BlockSpec(a.shape, lambda b, i: (0,) * a.ndim)
    consts = (perm, permt, wb, wc, a_vec, at_vec, pow_tab, d_vec, wglu, bglu)
    return pl.pallas_call(
        _s5_kernel,
        out_shape=jax.ShapeDtypeStruct((bsz, seq, width), BF16),
        grid=(bsz, seq // t_len),
        in_specs=[tok, tok] + [full(a) for a in consts],
        out_specs=tok,
        scratch_shapes=[pltpu.VMEM((t_len, 2 * S5_COLS), F32), pltpu.VMEM((t_len, 2 * S5_COLS), BF16),
                        pltpu.VMEM((1, 2 * n), F32), pltpu.VMEM((SUBLANES, 2 * n), F32)],
        compiler_params=_params(("parallel", "arbitrary")),
        name="s5_mixer",
    )(u, z_a, *consts)


def _softmax_init(m_ref, acc_ref):
    m_ref[...] = jnp.full_like(m_ref, NEG_INF)
    acc_ref[...] = jnp.zeros_like(acc_ref)


def _softmax_tile(s, mask, v_ext, m_ref, acc_ref):
    rows, tk = s.shape
    if mask is not None:
        heads = rows // mask.shape[1]
        s = jnp.where(mask, s.reshape(heads, *mask.shape[1:]), NEG_INF).reshape(rows, tk)
    m_old = m_ref[...]
    m_new = jnp.maximum(m_old, jnp.max(s, axis=-1, keepdims=True))
    e = jnp.exp2(s - jnp.concatenate([m_new] * (tk // LANES), axis=1))
    if mask is not None:
        e = jnp.where(mask, e.reshape(heads, *mask.shape[1:]), 0.0).reshape(rows, tk)
    alpha = jnp.exp2(m_old - m_new)
    acc_ref[...] = (jnp.concatenate([alpha, alpha], axis=1) * acc_ref[...]
                    + _dot(e.astype(BF16), v_ext))
    m_ref[...] = m_new


def _softmax_finish(acc_ref):
    acc = acc_ref[...]
    return acc[:, :LANES] / jnp.maximum(acc[:, LANES:], TINY)


def _with_ones(v):
    return jnp.concatenate([v, jnp.ones(v.shape, v.dtype)], axis=1)


def _pipelined_causal_tiles(n_full, scores, update, sa_ref, sb_ref, causal, filler=None):
    def pair(i, carry):
        scores(2 * i + 1, sb_ref)
        update(sa_ref, 2 * i, None)
        scores(2 * i + 2, sa_ref)
        update(sb_ref, 2 * i + 1, None)
        return carry

    scores(0, sa_ref)
    if filler is not None:
        filler()
    lax.fori_loop(0, n_full // 2, pair, 0)

    @pl.when(n_full % 2 == 1)
    def _():
        scores(n_full, sb_ref)
        update(sa_ref, n_full - 1, None)
        update(sb_ref, n_full, causal)

    @pl.when(n_full % 2 == 0)
    def _():
        update(sa_ref, n_full, causal)


def _mla_kernel(q_ref, k_ref, v_ref, o_ref, m_ref, acc_ref, sa_ref, sb_ref):
    qi = pl.program_id(1)
    heads, tq = q_ref.shape[1:3]
    tk = ATT_TK
    _softmax_init(m_ref, acc_ref)
    n_full = qi * tq // tk
    owns = [slice(hd * tq, (hd + 1) * tq) for hd in range(heads)]

    def scores(ki, s_ref):
        rows = pl.ds(pl.multiple_of(ki * tk, tk), tk)
        for hd, own in enumerate(owns):
            s_ref[own] = _dot_nt(q_ref[0, hd], k_ref[0, hd, rows, :])

    def update(s_ref, ki, mask):
        rows = pl.ds(pl.multiple_of(ki * tk, tk), tk)
        for hd, own in enumerate(owns):
            v_ext = _with_ones(v_ref[0, hd, rows, :])
            if mask is None:
                _softmax_tile(s_ref[own], None, v_ext, m_ref.at[own], acc_ref.at[own])
            else:
                half = tq // 2
                top = slice(own.start, own.start + half)
                low = slice(own.start + half, own.stop)
                _softmax_tile(s_ref[top, :half], mask[:, :half, :half], v_ext[:half],
                              m_ref.at[top], acc_ref.at[top])
                _softmax_tile(s_ref[low], mask[:, half:, :], v_ext, m_ref.at[low], acc_ref.at[low])

    qpos = qi * tq + lax.broadcasted_iota(jnp.int32, (1, tq, tk), 1)
    kpos = n_full * tk + lax.broadcasted_iota(jnp.int32, (1, tq, tk), 2)
    _pipelined_causal_tiles(n_full, scores, update, sa_ref, sb_ref, kpos <= qpos)
    for hd in range(heads):
        o_ref[0, :, hd * MLA_V:(hd + 1) * MLA_V] = _softmax_finish(
            acc_ref.at[hd * tq:(hd + 1) * tq]).astype(o_ref.dtype)


def _mla(q, k, v):
    bsz, heads, seq, dk = q.shape
    tq = ATT_TQ
    assert tq == ATT_TK
    return pl.pallas_call(
        _mla_kernel,
        out_shape=jax.ShapeDtypeStruct((bsz, seq, heads * MLA_V), BF16),
        grid=(bsz, seq // tq),
        in_specs=[pl.BlockSpec((1, heads, tq, dk), lambda b, qi: (b, 0, qi, 0)),
                  pl.BlockSpec((1, heads, seq, dk), lambda b, qi: (b, 0, 0, 0)),
                  pl.BlockSpec((1, heads, seq, MLA_V), lambda b, qi: (b, 0, 0, 0))],
        out_specs=pl.BlockSpec((1, tq, heads * MLA_V), lambda b, qi: (b, qi, 0)),
        scratch_shapes=[pltpu.VMEM((heads * tq, LANES), F32), pltpu.VMEM((heads * tq, 2 * LANES), F32),
                        pltpu.VMEM((heads * tq, ATT_TK), F32), pltpu.VMEM((heads * tq, ATT_TK), F32)],
        compiler_params=_params(("parallel", "arbitrary")),
        name="mla_attention",
    )(q, k, v)


def _ev_out_kernel(ya_ref, o_ref, zb_ref, x_ref, mod_ref, g_ref, wa_ref, wb_ref, out_ref):
    d = x_ref.shape[-1]
    yb = (o_ref[0].astype(F32) * _silu(zb_ref[0].astype(F32))).astype(BF16)
    y = _dot(ya_ref[0], wa_ref[...]) + _dot(yb, wb_ref[...])
    gate = mod_ref[0][:, 2 * d:]
    out_ref[0] = x_ref[0] + gate * _rms(y, g_ref[...])


def _ev_out(y_a, o_mla, z_b, x, mod, post_g, w_out):
    bsz, seq, d = x.shape
    tm = TOKEN_TILE
    wa = w_out[:y_a.shape[-1]].astype(BF16)
    wb = w_out[y_a.shape[-1]:].astype(BF16)
    g = post_g[None]
    tok = lambda n: pl.BlockSpec((1, tm, n), lambda b, i: (b, i, 0))
    full = lambda a: pl.BlockSpec(a.shape, lambda b, i: (0,) * a.ndim)
    return pl.pallas_call(
        _ev_out_kernel,
        out_shape=jax.ShapeDtypeStruct(x.shape, F32),
        grid=(bsz, seq // tm),
        in_specs=[tok(y_a.shape[-1]), tok(o_mla.shape[-1]), tok(z_b.shape[-1]), tok(d),
                  pl.BlockSpec((1, 1, 3 * d), lambda b, i: (b, 0, 0)), full(g), full(wa), full(wb)],
        out_specs=tok(d),
        compiler_params=_params(("parallel", "parallel")),
        name="even_out_proj",
    )(y_a, o_mla, z_b, x, mod[:, None, :], g, wa, wb)


OD_Q, OD_KV, OD_Z = 1024, 256, 1024
OD_NGATE = 3 * NSA_HEADS
OD_GATE = LANES
OD_OFF = np.cumsum([0, OD_Q] + [OD_KV] * 6 + [OD_NGATE, OD_Z])
OD_COLS = -(-OD_OFF[-1] // LANES) * LANES


def _od_in_kernel(x_ref, mod_ref, g_ref, w_ref, cn_ref, sn_ref,
                  q_ref, kc_ref, vc_ref, ks_ref, vs_ref, kw_ref, vw_ref, gate_ref, z_ref):
    h = _modulated_input(x_ref, mod_ref, g_ref)
    p = _dot(h, w_ref[...])
    o = OD_OFF
    cos, sin = cn_ref[0], sn_ref[0]

    def rope(t):
        return t * cos + pltpu.roll(t, NSA_DIM // 2, axis=1) * sin

    scale = NSA_DIM ** -0.5 * LOG2E
    for hd in range(NSA_HEADS):
        t = p[:, hd * NSA_DIM:(hd + 1) * NSA_DIM]
        q_ref[0, hd // NSA_REP, hd % NSA_REP] = (rope(t) * scale).astype(BF16)
    plain = (kc_ref, vc_ref, None, vs_ref, None, vw_ref)
    roped = (None, None, ks_ref, None, kw_ref, None)
    for j in range(6):
        for g in range(NSA_GROUPS):
            lo = o[1 + j] + g * NSA_DIM
            t = p[:, lo:lo + NSA_DIM]
            if plain[j] is not None:
                plain[j][0, g] = t.astype(plain[j].dtype)
            else:
                roped[j][0, g] = rope(t).astype(BF16)
    lane = lax.broadcasted_iota(jnp.int32, cos.shape, 1)
    gate_ref[0] = jnp.where(lane < OD_NGATE, jax.nn.sigmoid(p[:, o[7]:o[7] + LANES]), 0.0)
    z_ref[0] = p[:, o[8]:o[9]].astype(BF16)


def _od_in(x, mod, pre_g, w_in, cos_n, sin_n):
    bsz, seq, d = x.shape
    tm = TOKEN_TILE
    w = w_in
    g = pre_g[None]
    tok = lambda n: pl.BlockSpec((1, tm, n), lambda b, i: (b, i, 0))
    full = lambda a: pl.BlockSpec(a.shape, lambda b, i: (0,) * a.ndim)
    q_spec = pl.BlockSpec((1, NSA_GROUPS, NSA_REP, tm, NSA_DIM), lambda b, i: (b, 0, 0, i, 0))
    kv_spec = pl.BlockSpec((1, NSA_GROUPS, tm, NSA_DIM), lambda b, i: (b, 0, i, 0))
    kv_shape = jax.ShapeDtypeStruct((bsz, NSA_GROUPS, seq, NSA_DIM), BF16)
    cmp_shape = jax.ShapeDtypeStruct(kv_shape.shape, F32)
    return pl.pallas_call(
        _od_in_kernel,
        out_shape=(jax.ShapeDtypeStruct((bsz, NSA_GROUPS, NSA_REP, seq, NSA_DIM), BF16),)
        + (cmp_shape,) * 2 + (kv_shape,) * 4
        + (jax.ShapeDtypeStruct((bsz, seq, OD_GATE), F32), jax.ShapeDtypeStruct((bsz, seq, OD_Z), BF16)),
        grid=(bsz, seq // tm),
        in_specs=[tok(d), pl.BlockSpec((1, 1, 3 * d), lambda b, i: (b, 0, 0)), full(g), full(w),
                  tok(LANES), tok(LANES)],
        out_specs=(q_spec,) + (kv_spec,) * 6 + (tok(OD_GATE), tok(OD_Z)),
        compiler_params=_params(("parallel", "parallel")),
        name="odd_in_proj",
    )(x, mod[:, None, :], g, w, cos_n, sin_n)


def _compress_kernel(x_ref, pe_ref, w1_ref, w2_ref, cos_ref, sin_ref, o_ref, *, use_rope):
    seq, d = x_ref.shape[2:]
    nb = seq // CMP_STRIDE
    lo = _dot(pe_ref[...], w1_ref[...])[0:1]
    hi = jnp.zeros((nb, w1_ref.shape[1]), F32)
    for l in range(CMP_STRIDE):
        xl = x_ref[0, 0, pl.ds(l, nb, stride=CMP_STRIDE), :].astype(BF16)
        lo = lo + _dot(xl, w1_ref[l * d:(l + 1) * d])
        hi = hi + _dot(xl, w1_ref[(CMP_STRIDE + l) * d:(CMP_STRIDE + l + 1) * d])
    row = lax.broadcasted_iota(jnp.int32, hi.shape, 0)
    pre = lo + jnp.where(row < nb - 1, pltpu.roll(hi, nb - 1, axis=0), 0.0)
    out = _dot(_gelu_tanh(pre).astype(BF16), w2_ref[...])
    if use_rope:
        out = out * cos_ref[0] + pltpu.roll(out, NSA_DIM // 2, axis=1) * sin_ref[0]
    o_ref[0, 0] = out.astype(BF16)


def _compress(kv, pe, w1, w2, cos_end, sin_end, use_rope):
    bsz, groups, seq, d = kv.shape
    nb = seq // CMP_STRIDE
    pe_rows = jnp.broadcast_to(pe.reshape(1, CMP_LEN * d), (SUBLANES, CMP_LEN * d)).astype(BF16)
    w1b, w2b = w1.astype(BF16), w2.astype(BF16)
    full = lambda a: pl.BlockSpec(a.shape, lambda b, g: (0,) * a.ndim)
    end = pl.BlockSpec((1, nb, d), lambda b, g: (b, 0, 0))
    return pl.pallas_call(
        functools.partial(_compress_kernel, use_rope=use_rope),
        out_shape=jax.ShapeDtypeStruct((bsz, groups, nb, d), BF16),
        grid=(bsz, groups),
        in_specs=[pl.BlockSpec((1, 1, seq, d), lambda b, g: (b, g, 0, 0)),
                  full(pe_rows), full(w1b), full(w2b), end, end],
        out_specs=pl.BlockSpec((1, 1, nb, d), lambda b, g: (b, g, 0, 0)),
        compiler_params=_params(("parallel", "parallel")),
        name="nsa_compress",
    )(kv, pe_rows, w1b, w2b, cos_end, sin_end)


def _cmp_sel_kernel(q_ref, kc_ref, vc_ref, pool_ref, o_ref, sel_ref, **static):
    for g in range(q_ref.shape[1]):
        _cmp_sel_group(g, q_ref, kc_ref, vc_ref, pool_ref, o_ref, sel_ref, **static)


def _cmp_sel_group(g, q_ref, kc_ref, vc_ref, pool_ref, o_ref, sel_ref, *, n_cmp, n_sel, n_top):
    qi = pl.program_id(1)
    rep, tq, d = q_ref.shape[2:]
    nb = kc_ref.shape[2]
    q = q_ref[0, g].reshape(rep * tq, d)
    s = _dot_nt(q, kc_ref[0, g]).reshape(rep, tq, nb)
    qpos = qi * tq + lax.broadcasted_iota(jnp.int32, (tq, nb), 0)
    blk = lax.broadcasted_iota(jnp.int32, (tq, nb), 1)
    mask = ((blk * CMP_STRIDE + (CMP_LEN - 1) <= qpos) & (blk < n_cmp))[None]
    s = jnp.where(mask, s, NEG_INF)
    m = jnp.max(s, axis=-1, keepdims=True)
    e = jnp.where(mask, jnp.exp2(s - m), 0.0)
    p = e / jnp.maximum(jnp.sum(e, axis=-1, keepdims=True), TINY)
    o = _dot(p.reshape(rep * tq, nb).astype(BF16), vc_ref[0, g])
    o_ref[0, g] = o.reshape(rep, tq, d).astype(o_ref.dtype)

    rows = pool_ref.shape[0]
    imp = _dot_nt(pool_ref[...], jnp.sum(p, axis=0), precision=HIGHEST)[:n_sel]
    bid = lax.broadcasted_iota(jnp.int32, (n_sel, tq), 0)
    cur = (qi * tq + lax.broadcasted_iota(jnp.int32, (n_sel, tq), 1)) // SEL_BLOCK
    forced = (bid == 0) | (bid == cur) | (bid == cur - 1)
    imp = jnp.where(forced, FORCE_SCORE, jnp.where(bid <= cur, imp, -1.0))
    groups = [imp[g:g + SUBLANES] for g in range(0, n_sel, SUBLANES)]
    sub = lax.broadcasted_iota(jnp.int32, (SUBLANES, tq), 0)
    ranks = [jnp.zeros((SUBLANES, tq), F32) for _ in groups]
    for j in range(n_sel):
        vj = jnp.broadcast_to(imp[j:j + 1, :], (SUBLANES, tq))
        for gi, grp in enumerate(groups):
            lo = gi * SUBLANES
            if lo > j:
                first = vj >= grp
            elif lo + SUBLANES - 1 <= j:
                first = vj > grp
            else:
                first = (vj > grp) | ((vj == grp) & (sub > j - lo))
            ranks[gi] = ranks[gi] + jnp.where(first, 1.0, 0.0)
    rank = jnp.concatenate(ranks, axis=0)
    bias = jnp.where((rank < n_top) & (bid <= cur), 0.0, NEG_INF)
    bias = jnp.concatenate([bias, jnp.full((rows - n_sel, tq), NEG_INF, F32)], axis=0)
    sel_ref[0, g] = bias.T.astype(sel_ref.dtype)


def _cmp_sel(q, kc, vc):
    bsz, groups, rep, seq, d = q.shape
    nb = kc.shape[2]
    tq = NSA_TQ
    n_cmp = (seq - CMP_LEN) // CMP_STRIDE + 1
    n_sel = seq // SEL_BLOCK
    n_top = min(SEL_TOP, n_sel)
    ratio = SEL_BLOCK // CMP_STRIDE
    assert n_sel <= LANES and n_sel * ratio == nb
    pool_np = np.zeros((LANES, nb), np.float32)
    pool_np[np.arange(nb) // ratio, np.arange(nb)] = 1.0
    pool = jnp.asarray(pool_np)
    kv_spec = pl.BlockSpec((1, groups, nb, d), lambda b, i: (b, 0, 0, 0))
    q_spec = pl.BlockSpec((1, groups, rep, tq, d), lambda b, i: (b, 0, 0, i, 0))
    return pl.pallas_call(
        functools.partial(_cmp_sel_kernel, n_cmp=n_cmp, n_sel=n_sel, n_top=n_top),
        out_shape=(jax.ShapeDtypeStruct(q.shape, BF16),
                   jax.ShapeDtypeStruct((bsz, groups, seq, LANES), BF16)),
        grid=(bsz, seq // tq),
        in_specs=[q_spec, kv_spec, kv_spec, pl.BlockSpec(pool.shape, lambda b, i: (0, 0))],
        out_specs=(q_spec, pl.BlockSpec((1, groups, tq, LANES), lambda b, i: (b, 0, i, 0))),
        compiler_params=_params(("parallel", "parallel")),
        name="nsa_cmp_select",
    )(q, kc, vc, pool)


def _sel_win_kernel(q_ref, bias_ref, k_ref, blk_ref, v_ref, kw_ref, vw_ref, o_ref, ow_ref,
                    m_ref, acc_ref, sa_ref, sb_ref, *, span):
    qi = pl.program_id(2)
    rep, tq, d = q_ref.shape[2:]
    tk = NSA_TK
    part = rep // SEL_CHAINS
    q_ext = [jnp.concatenate([q_ref[0, 0, c * part:(c + 1) * part].reshape(part * tq, d),
                              jnp.concatenate([bias_ref[0, 0]] * part, axis=0)], axis=1)
             for c in range(SEL_CHAINS)]
    _softmax_init(m_ref, acc_ref)
    n_full = qi * tq // tk

    owns = [slice(c * part * tq, (c + 1) * part * tq) for c in range(SEL_CHAINS)]

    def scores(ki, s_ref):
        rows = pl.ds(pl.multiple_of(ki * tk, tk), tk)
        k_ext = jnp.concatenate([k_ref[0, 0, rows, :], blk_ref[rows, :]], axis=1)
        for c, own in enumerate(owns):
            s_ref[own] = _dot_nt(q_ext[c], k_ext)

    def update(s_ref, ki, mask):
        def run(width):
            rows = pl.ds(pl.multiple_of(ki * tk, tk), width)
            v_ext = _with_ones(v_ref[0, 0, rows, :])
            sub = None if mask is None else mask[:, :, :width]
            for own in owns:
                _softmax_tile(s_ref[own, :width], sub, v_ext, m_ref.at[own], acc_ref.at[own])

        if mask is None:
            run(tk)
        else:
            short = qi * tq + tq <= ki * tk + tk // 2
            pl.when(short)(lambda: run(tk // 2))
            pl.when(jnp.logical_not(short))(lambda: run(tk))

    def window():
        start = pl.multiple_of(jnp.maximum(qi * tq + tq - span, 0), tq)
        rows = pl.ds(start, span)
        wq = qi * tq + lax.broadcasted_iota(jnp.int32, (1, tq, span), 1)
        wk = start + lax.broadcasted_iota(jnp.int32, (1, tq, span), 2)
        mask = (wq - wk >= 0) & (wq - wk < WINDOW)
        k, v_ext = kw_ref[0, 0, rows, :], _with_ones(vw_ref[0, 0, rows, :])
        wpart = rep // WIN_CHAINS
        for h0 in range(0, rep, wpart):
            q = q_ref[0, 0, h0:h0 + wpart].reshape(wpart * tq, d)
            s = jnp.where(mask, _dot_nt(q, k).reshape(wpart, tq, span), NEG_INF)
            m = jnp.max(s, axis=-1, keepdims=True)
            e = jnp.where(mask, jnp.exp2(s - m), 0.0).reshape(wpart * tq, span)
            acc = _dot(e.astype(BF16), v_ext)
            o = acc[:, :LANES] / jnp.maximum(acc[:, LANES:], TINY)
            ow_ref[0, 0, h0:h0 + wpart] = o.reshape(wpart, tq, d).astype(ow_ref.dtype)

    qpos = qi * tq + lax.broadcasted_iota(jnp.int32, (1, tq, tk), 1)
    kpos = n_full * tk + lax.broadcasted_iota(jnp.int32, (1, tq, tk), 2)
    _pipelined_causal_tiles(n_full, scores, update, sa_ref, sb_ref, kpos <= qpos, filler=window)
    o_ref[0, 0] = _softmax_finish(acc_ref).reshape(rep, tq, d).astype(o_ref.dtype)


def _sel_win_attention(q, bias, k, v, kw, vw):
    bsz, groups, rep, seq, d = q.shape
    tq = NSA_TQ
    assert NSA_TK % tq == 0
    span = (-(-(WINDOW - 1) // tq) + 1) * tq
    assert span <= seq
    onehot_np = np.zeros((seq, LANES), np.float32)
    onehot_np[np.arange(seq), np.arange(seq) // SEL_BLOCK] = 1.0
    onehot = jnp.asarray(onehot_np, BF16)
    q_spec = pl.BlockSpec((1, 1, rep, tq, d), lambda b, g, qi: (b, g, 0, qi, 0))
    kv_spec = pl.BlockSpec((1, 1, seq, d), lambda b, g, qi: (b, g, 0, 0))
    out = jax.ShapeDtypeStruct(q.shape, BF16)
    return pl.pallas_call(
        functools.partial(_sel_win_kernel, span=span),
        out_shape=(out, out),
        grid=(bsz, groups, seq // tq),
        in_specs=[q_spec, pl.BlockSpec((1, 1, tq, LANES), lambda b, g, qi: (b, g, qi, 0)),
                  kv_spec, pl.BlockSpec((seq, LANES), lambda b, g, qi: (0, 0)), kv_spec, kv_spec, kv_spec],
        out_specs=(q_spec, q_spec),
        scratch_shapes=[pltpu.VMEM((rep * tq, LANES), F32), pltpu.VMEM((rep * tq, 2 * LANES), F32),
                        pltpu.VMEM((rep * tq, NSA_TK), F32), pltpu.VMEM((rep * tq, NSA_TK), F32)],
        compiler_params=_params(("parallel", "parallel", "arbitrary")),
        name="nsa_selected_window_attention",
    )(q, bias, k, onehot, v, kw, vw)


def _od_out_kernel(oc_ref, os_ref, ow_ref, gate_ref, spread_ref, z_ref, x_ref, mod_ref, g_ref, w_ref, out_ref):
    d = x_ref.shape[-1]
    gates = gate_ref[0]
    hi = gates.astype(BF16)
    lo = (gates - hi.astype(F32)).astype(BF16)
    wide = _dot(jnp.concatenate([hi, lo], axis=-1), spread_ref[...])
    z = z_ref[0].astype(F32)
    parts = []
    for hd in range(NSA_HEADS):
        g, r = hd // NSA_REP, hd % NSA_REP
        c = 3 * hd * NSA_DIM
        o = (wide[:, c:c + NSA_DIM] * oc_ref[0, g, r].astype(F32)
             + wide[:, c + NSA_DIM:c + 2 * NSA_DIM] * os_ref[0, g, r].astype(F32)
             + wide[:, c + 2 * NSA_DIM:c + 3 * NSA_DIM] * ow_ref[0, g, r].astype(F32))
        parts.append((o * _silu(z[:, hd * NSA_DIM:(hd + 1) * NSA_DIM])).astype(BF16))
    y = _dot(jnp.concatenate(parts, axis=-1), w_ref[...])
    gate = mod_ref[0][:, 2 * d:]
    out_ref[0] = x_ref[0] + gate * _rms(y, g_ref[...])


def _od_out(o_cmp, o_sel, o_win, gates, z, x, mod, post_g, w_out):
    bsz, seq, d = x.shape
    tm = TOKEN_TILE
    w = w_out.astype(BF16)
    g = post_g[None]
    n_gate = 3 * NSA_HEADS
    spread_np = np.zeros((2 * OD_GATE, n_gate * NSA_DIM), np.float32)
    for c in range(n_gate):
        spread_np[[c, OD_GATE + c], c * NSA_DIM:(c + 1) * NSA_DIM] = 1.0
    spread = jnp.asarray(spread_np, BF16)
    tok = lambda n: pl.BlockSpec((1, tm, n), lambda b, i: (b, i, 0))
    full = lambda a: pl.BlockSpec(a.shape, lambda b, i: (0,) * a.ndim)
    o_spec = pl.BlockSpec((1, NSA_GROUPS, NSA_REP, tm, NSA_DIM), lambda b, i: (b, 0, 0, i, 0))
    return pl.pallas_call(
        _od_out_kernel,
        out_shape=jax.ShapeDtypeStruct(x.shape, F32),
        grid=(bsz, seq // tm),
        in_specs=[o_spec, o_spec, o_spec, tok(OD_GATE), full(spread), tok(OD_Z), tok(d),
                  pl.BlockSpec((1, 1, 3 * d), lambda b, i: (b, 0, 0)), full(g), full(w)],
        out_specs=tok(d),
        compiler_params=_params(("parallel", "parallel")),
        name="odd_out_proj",
    )(o_cmp, o_sel, o_win, gates, spread, z, x, mod[:, None, :], g, w)


def _even_layer(x, mod, pre_g, post_g, tabs, w_in, lam_re, lam_im, log_dt, b_re, b_im, c_re, c_im,
                d_skip, w_glu, b_glu, q_norm_g, kv_norm_g, w_uq, w_ukv, w_out):
    cos_m, sin_m = tabs[0], tabs[1]
    u, z_a, z_b, q, k, v = _ev_in(x, mod, pre_g, w_in, q_norm_g, kv_norm_g, w_uq, w_ukv, cos_m, sin_m)
    y_a = _s5(u, z_a, lam_re, lam_im, log_dt, b_re, b_im, c_re, c_im, d_skip, w_glu, b_glu)
    o_mla = _mla(q, k, v)
    return _ev_out(y_a, o_mla, z_b, x, mod, post_g, w_out)


def _odd_layer(x, mod, pre_g, post_g, tabs, end_tabs, w_in, k_pe, k_w1, k_w2, v_pe, v_w1, v_w2, w_out):
    cos_n, sin_n = tabs[2], tabs[3]
    cos_end, sin_end = end_tabs[2], end_tabs[3]
    q, k_c, v_c, k_s, v_s, k_w, v_w, gates, z = _od_in(x, mod, pre_g, w_in, cos_n, sin_n)
    kc = _compress(k_c, k_pe, k_w1, k_w2, cos_end, sin_end, True)
    vc = _compress(v_c, v_pe, v_w1, v_w2, cos_end, sin_end, False)
    o_cmp, sel = _cmp_sel(q, kc, vc)
    o_sel, o_win = _sel_win_attention(q, sel, k_s, v_s, k_w, v_w)
    return _od_out(o_cmp, o_sel, o_win, gates, z, x, mod, post_g, w_out)


def _lane_padded(w, cols):
    return jnp.pad(w, ((0, 0), (0, 0), (0, cols - w.shape[-1]))).astype(BF16)


def kernel(x, c, positions, pre_norm_g, post_norm_g, w_ada, b_ada, ev_w_in, ev_lam_re, ev_lam_im, ev_log_dt, ev_b_re, ev_b_im, ev_c_re, ev_c_im, ev_d_skip, ev_w_glu, ev_b_glu, ev_q_norm_g, ev_kv_norm_g, ev_w_uq, ev_w_ukv, ev_w_out, od_w_in, od_cmp_k_pe, od_cmp_k_w1, od_cmp_k_w2, od_cmp_v_pe, od_cmp_v_w1, od_cmp_v_w2, od_w_out):
    depth = pre_norm_g.shape[0]
    tabs = _rope_tables(positions)
    seq = positions.shape[1]
    pos_end = positions[:, CMP_LEN - 1::CMP_STRIDE]
    pos_end = jnp.pad(pos_end, ((0, 0), (0, seq // CMP_STRIDE - pos_end.shape[1])))
    end_tabs = _rope_tables(pos_end)
    mods = _modulation(c, w_ada, b_ada)
    ev_w_in = _lane_padded(ev_w_in, EV_COLS)
    od_w_in = _lane_padded(od_w_in, OD_COLS)
    for layer in range(depth):
        i = layer // 2
        if layer % 2 == 0:
            x = _even_layer(x, mods[layer], pre_norm_g[layer], post_norm_g[layer], tabs,
                            ev_w_in[i], ev_lam_re[i], ev_lam_im[i], ev_log_dt[i], ev_b_re[i], ev_b_im[i],
                            ev_c_re[i], ev_c_im[i], ev_d_skip[i], ev_w_glu[i], ev_b_glu[i],
                            ev_q_norm_g[i], ev_kv_norm_g[i], ev_w_uq[i], ev_w_ukv[i], ev_w_out[i])
        else:
            x = _odd_layer(x, mods[layer], pre_norm_g[layer], post_norm_g[layer], tabs, end_tabs,
                           od_w_in[i], od_cmp_k_pe[i], od_cmp_k_w1[i], od_cmp_k_w2[i],
                           od_cmp_v_pe[i], od_cmp_v_w1[i], od_cmp_v_w2[i], od_w_out[i])
    return x
```

```python
import functools
import math

import numpy as np
import jax
import jax.numpy as jnp
from jax import lax
from jax.experimental import pallas as pl
from jax.experimental.pallas import tpu as pltpu

F32 = jnp.float32
BF16 = jnp.bfloat16
HIGHEST = lax.Precision.HIGHEST

EPS = 1e-6
ROPE_THETA = 10000.0
NEG_INF = -1e30
TINY = 1e-30
FORCE_SCORE = 1e9
LOG2E = 1.0 / math.log(2.0)

S5_GROUP = 16
S5_STATE = 64
MLA_HEADS = 4
MLA_NOPE = 128
MLA_ROPE = 64
MLA_V = 128
NSA_HEADS = 8
NSA_GROUPS = 2
NSA_REP = NSA_HEADS // NSA_GROUPS
NSA_DIM = 128
CMP_LEN = 32
CMP_STRIDE = 16
CMP_HIDDEN = 256
SEL_BLOCK = 64
SEL_TOP = 16
WINDOW = 512

LANES = 128
SUBLANES = 8
VMEM_LIMIT = 48 * 1024 * 1024

TOKEN_TILE = 1024
S5_CHUNK = 512
S5_COLS = 512
ATT_TQ = 512
ATT_TK = 512
NSA_TQ = 256
NSA_TK = 512
CMP_TQ = 1024
SEL_CHAINS = 4
WIN_CHAINS = 4


def _params(sem):
    return pltpu.CompilerParams(dimension_semantics=sem, vmem_limit_bytes=VMEM_LIMIT)


def _dot(a, b, precision=None):
    return jnp.dot(a, b, preferred_element_type=F32, precision=precision)


def _dot_nt(a, b, precision=None):
    return lax.dot_general(a, b, (((1,), (1,)), ((), ())), preferred_element_type=F32,
                           precision=precision)


def _silu(v):
    return v * jax.nn.sigmoid(v)


def _gelu_tanh(v):
    return 0.5 * v * (1.0 + jnp.tanh(math.sqrt(2.0 / math.pi) * (v + 0.044715 * (v * v * v))))


def _rms(v, g):
    return v * lax.rsqrt(jnp.mean(v * v, axis=-1, keepdims=True) + EPS) * g


def _rope_tab_kernel(pos_ref, f_ref, sgm_ref, sgn_ref, cm_ref, sm_ref, cn_ref, sn_ref):
    hn, hm = NSA_DIM // 2, MLA_ROPE // 2
    ang = pos_ref[0] * f_ref[...]
    c, s = jnp.cos(ang), jnp.sin(ang)
    cn_ref[0] = jnp.concatenate([c[:, :hn]] * (LANES // hn), axis=1)
    sn_ref[0] = jnp.concatenate([s[:, :hn]] * (LANES // hn), axis=1) * sgn_ref[...]
    cm_ref[0] = jnp.concatenate([c[:, hn:hn + hm]] * (LANES // hm), axis=1)
    sm_ref[0] = jnp.concatenate([s[:, hn:hn + hm]] * (LANES // hm), axis=1) * sgm_ref[...]


def _rope_tables(positions):
    bsz, seq = positions.shape
    ts = min(TOKEN_TILE, seq)
    pos = positions.astype(F32)[..., None]
    half_m, half_n = MLA_ROPE // 2, NSA_DIM // 2
    fm = ROPE_THETA ** (-jnp.arange(half_m, dtype=F32) / half_m)
    fn = ROPE_THETA ** (-jnp.arange(half_n, dtype=F32) / half_n)
    freqs = jnp.concatenate([fn, fm, jnp.zeros((LANES - half_n - half_m,), F32)])[None]
    sgm = jnp.concatenate([-jnp.ones((half_m,), F32), jnp.ones((half_m,), F32),
                           jnp.zeros((LANES - MLA_ROPE,), F32)])[None]
    sgn = jnp.concatenate([-jnp.ones((half_n,), F32), jnp.ones((half_n,), F32)])[None]
    row = pl.BlockSpec((1, LANES), lambda b, i: (0, 0))
    tab = pl.BlockSpec((1, ts, LANES), lambda b, i: (b, i, 0))
    shp = jax.ShapeDtypeStruct((bsz, seq, LANES), F32)
    return pl.pallas_call(
        _rope_tab_kernel,
        out_shape=(shp, shp, shp, shp),
        grid=(bsz, seq // ts),
        in_specs=[pl.BlockSpec((1, ts, 1), lambda b, i: (b, i, 0)), row, row, row],
        out_specs=(tab, tab, tab, tab),
        compiler_params=_params(("parallel", "parallel")),
        name="rope_tables",
    )(pos, freqs, sgm, sgn)


def _mod_kernel(c_ref, w_ref, b_ref, o_ref):
    s = _silu(c_ref[...])
    o_ref[0] = _dot(s, w_ref[0], precision=HIGHEST) + b_ref[0]


def _modulation(c, w_ada, b_ada):
    depth, d, _ = w_ada.shape
    bsz = c.shape[0]
    rows = -(-bsz // SUBLANES) * SUBLANES
    c_pad = jnp.pad(c, ((0, rows - bsz), (0, 0)))
    out = pl.pallas_call(
        _mod_kernel,
        out_shape=jax.ShapeDtypeStruct((depth, rows, 3 * d), F32),
        grid=(depth, 3),
        in_specs=[pl.BlockSpec((rows, d), lambda l, j: (0, 0)),
                  pl.BlockSpec((1, d, d), lambda l, j: (l, 0, j)),
                  pl.BlockSpec((1, 1, d), lambda l, j: (l, 0, j))],
        out_specs=pl.BlockSpec((1, rows, d), lambda l, j: (l, 0, j)),
        compiler_params=_params(("parallel", "parallel")),
        name="modulation",
    )(c_pad, w_ada, b_ada[:, None, :])
    return out[:, :bsz]


def _modulated_input(x_ref, mod_ref, g_ref):
    d = x_ref.shape[-1]
    x = x_ref[0]
    mod = mod_ref[0]
    shift, scale = mod[:, :d], mod[:, d:2 * d]
    return (_rms(x, g_ref[...]) * (1.0 + scale) + shift).astype(BF16)


EV_U, EV_ZA, EV_CQ, EV_CKV, EV_KPE, EV_ZB = 512, 512, 768, 256, MLA_ROPE, 512
EV_OFF = np.cumsum([0, EV_U, EV_ZA, EV_CQ, EV_CKV, EV_KPE, EV_ZB])
EV_COLS = -(-EV_OFF[-1] // LANES) * LANES


def _ev_in_kernel(x_ref, mod_ref, g_ref, w_ref, gq_ref, gkv_ref, wuq_ref, wukv_ref, cm_ref, sm_ref,
                  u_ref, za_ref, zb_ref, q_ref, k_ref, v_ref):
    h = _modulated_input(x_ref, mod_ref, g_ref)
    p = _dot(h, w_ref[...])
    o = EV_OFF
    u_ref[0] = p[:, o[0]:o[1]].astype(BF16)
    za_ref[0] = p[:, o[1]:o[2]].astype(BF16)
    zb_ref[0] = p[:, o[5]:o[6]].astype(BF16)
    cos, sin = cm_ref[0], sm_ref[0]
    hq = MLA_ROPE // 2
    lane = lax.broadcasted_iota(jnp.int32, cos.shape, 1)
    first = lane < hq

    def rope(t):
        swapped = jnp.where(first, pltpu.roll(t, LANES - hq, axis=1), pltpu.roll(t, hq, axis=1))
        return t * cos + swapped * sin

    cq = _rms(p[:, o[2]:o[3]], gq_ref[...]).astype(BF16)
    q = _dot(cq, wuq_ref[...]) * ((MLA_NOPE + MLA_ROPE) ** -0.5 * LOG2E)
    ckv = _rms(p[:, o[3]:o[4]], gkv_ref[...]).astype(BF16)
    kv = _dot(ckv, wukv_ref[...])
    kpe = rope(jnp.where(lane < MLA_ROPE, p[:, o[4]:o[4] + LANES], 0.0)).astype(BF16)
    nope = MLA_HEADS * MLA_NOPE
    for hd in range(MLA_HEADS):
        b0 = hd * (MLA_NOPE + LANES)
        qpe = rope(q[:, b0 + MLA_NOPE:b0 + MLA_NOPE + LANES])
        q_ref[0, hd] = jnp.concatenate([q[:, b0:b0 + MLA_NOPE], qpe], axis=-1).astype(BF16)
        k_ref[0, hd] = jnp.concatenate([kv[:, hd * MLA_NOPE:(hd + 1) * MLA_NOPE].astype(BF16), kpe], axis=-1)
        v_ref[0, hd] = kv[:, nope + hd * MLA_V:nope + (hd + 1) * MLA_V].astype(BF16)


def _ev_in(x, mod, pre_g, w_in, q_norm_g, kv_norm_g, w_uq, w_ukv, cos_m, sin_m):
    bsz, seq, d = x.shape
    tm = TOKEN_TILE
    w = jnp.pad(w_in, ((0, 0), (0, EV_COLS - w_in.shape[1]))).astype(BF16)
    rank = w_uq.shape[0]
    wuq = jnp.pad(w_uq.reshape(rank, MLA_HEADS, MLA_NOPE + MLA_ROPE), ((0, 0), (0, 0), (0, LANES - MLA_ROPE)))
    wuq = wuq.reshape(rank, MLA_HEADS * (MLA_NOPE + LANES)).astype(BF16)
    rank = w_ukv.shape[0]
    wukv = jnp.concatenate([w_ukv.reshape(rank, MLA_HEADS, MLA_NOPE + MLA_V)[:, :, :MLA_NOPE].reshape(rank, -1),
                            w_ukv.reshape(rank, MLA_HEADS, MLA_NOPE + MLA_V)[:, :, MLA_NOPE:].reshape(rank, -1)],
                           axis=1).astype(BF16)

    tok = lambda n: pl.BlockSpec((1, tm, n), lambda b, i: (b, i, 0))
    full = lambda a: pl.BlockSpec(a.shape, lambda b, i: (0,) * a.ndim)
    out = lambda n: jax.ShapeDtypeStruct((bsz, seq, n), BF16)
    head = lambda n: pl.BlockSpec((1, MLA_HEADS, tm, n), lambda b, i: (b, 0, i, 0))
    head_out = lambda n: jax.ShapeDtypeStruct((bsz, MLA_HEADS, seq, n), BF16)
    gq, gkv, g = q_norm_g[None], kv_norm_g[None], pre_g[None]
    return pl.pallas_call(
        _ev_in_kernel,
        out_shape=(out(512), out(512), out(512), head_out(2 * LANES), head_out(2 * LANES), head_out(MLA_V)),
        grid=(bsz, seq // tm),
        in_specs=[tok(d), pl.BlockSpec((1, 1, 3 * d), lambda b, i: (b, 0, 0)), full(g), full(w),
                  full(gq), full(gkv), full(wuq), full(wukv), tok(LANES), tok(LANES)],
        out_specs=(tok(512), tok(512), tok(512), head(2 * LANES), head(2 * LANES), head(MLA_V)),
        compiler_params=_params(("parallel", "parallel")),
        name="even_in_proj",
    )(x, mod[:, None, :], g, w, gq, gkv, wuq, wukv, cos_m, sin_m)


def _s5_kernel(u_ref, z_ref, perm_ref, permt_ref, wb_ref, wc_ref, a_ref, at_ref, pow_ref,
               d_ref, wglu_ref, bglu_ref, o_ref, bu_ref, xb_ref, state_ref, carry_ref):
    t_len = u_ref.shape[1]
    n = a_ref.shape[1] // 2
    steps = t_len // SUBLANES

    @pl.when(pl.program_id(1) == 0)
    def _():
        state_ref[...] = jnp.zeros_like(state_ref)

    perm = perm_ref[...]
    u_p = _dot(perm, u_ref[0])
    z_p = _dot(perm, z_ref[0])
    u_pb = u_p.astype(BF16)
    bre, bim = pl.ds(0, S5_COLS), pl.ds(S5_COLS, S5_COLS)
    y_parts = []

    for cb in range(n // S5_COLS):
        u_cb = u_pb[:, cb * LANES:(cb + 1) * LANES]
        bu_ref[...] = _dot(u_cb, wb_ref[cb])
        re = pl.ds(cb * S5_COLS, S5_COLS)
        im = pl.ds(n + cb * S5_COLS, S5_COLS)
        ar = jnp.broadcast_to(a_ref[:, re], (SUBLANES, S5_COLS))
        ai = jnp.broadcast_to(a_ref[:, im], (SUBLANES, S5_COLS))

        def step(t, carry):
            xr, xi = carry
            rows = pl.ds(pl.multiple_of(t * SUBLANES, SUBLANES), SUBLANES)
            nr = ar * xr - ai * xi + bu_ref[rows, bre]
            ni = ar * xi + ai * xr + bu_ref[rows, bim]
            bu_ref[rows, bre] = nr
            bu_ref[rows, bim] = ni
            return nr, ni

        zero = jnp.zeros((SUBLANES, S5_COLS), F32)
        er, ei = lax.fori_loop(0, steps, step, (zero, zero), unroll=True)

        sr, si = state_ref[:, re], state_ref[:, im]
        tr, ti = at_ref[:, re], at_ref[:, im]
        for j in range(SUBLANES):
            carry_ref[j:j + 1, re] = sr
            carry_ref[j:j + 1, im] = si
            sr, si = (tr * sr - ti * si + er[j:j + 1], tr * si + ti * sr + ei[j:j + 1])
        state_ref[:, re] = sr
        state_ref[:, im] = si
        cr, ci = carry_ref[:, re], carry_ref[:, im]

        def fix(t2, _):
            xr, xi = [], []
            for k in range(2):
                t = 2 * t2 + k
                rows = pl.ds(pl.multiple_of(t * SUBLANES, SUBLANES), SUBLANES)
                pr, pi = pow_ref[pl.ds(t, 1), re], pow_ref[pl.ds(t, 1), im]
                xr.append(bu_ref[rows, bre] + pr * cr - pi * ci)
                xi.append(bu_ref[rows, bim] + pr * ci + pi * cr)
            rows = pl.ds(pl.multiple_of(t2 * 2 * SUBLANES, 2 * SUBLANES), 2 * SUBLANES)
            xb_ref[rows, bre] = jnp.concatenate(xr, axis=0).astype(BF16)
            xb_ref[rows, bim] = jnp.concatenate(xi, axis=0).astype(BF16)
            return 0

        lax.fori_loop(0, steps // 2, fix, 0, unroll=True)
        y_parts.append(_dot(xb_ref[...], wc_ref[cb]))

    y = jnp.concatenate(y_parts, axis=1) + d_ref[...] * u_p
    g = _gelu_tanh(y)
    gate = jax.nn.sigmoid(_dot(g.astype(BF16), wglu_ref[...]) + bglu_ref[...])
    out = (g * gate * _silu(z_p)).astype(BF16)
    o_ref[0] = _dot(permt_ref[...], out).astype(BF16)


def _s5(u, z_a, lam_re, lam_im, log_dt, b_re, b_im, c_re, c_im, d_skip, w_glu, b_glu):
    bsz, seq, width = u.shape
    groups, state = lam_re.shape
    t_len = S5_CHUNK
    steps = t_len // SUBLANES
    n = groups * state
    dt = jnp.exp(log_dt)[:, None]
    lam_dt_re, lam_dt_im = lam_re * dt, lam_im * dt
    decay = jnp.exp(lam_dt_re)
    ab_re, ab_im = decay * jnp.cos(lam_dt_im), decay * jnp.sin(lam_dt_im)
    den = lam_re * lam_re + lam_im * lam_im
    nr, ni = ab_re - 1.0, ab_im
    f_re = (nr * lam_re + ni * lam_im) / den
    f_im = (ni * lam_re - nr * lam_im) / den
    bb_re = f_re[..., None] * b_re - f_im[..., None] * b_im
    bb_im = f_re[..., None] * b_im + f_im[..., None] * b_re
    gpb = S5_COLS // state
    nblk = groups // gpb
    hdim = width // groups
    assert gpb * hdim == LANES and nblk * S5_COLS == n
    eye = jnp.eye(gpb, dtype=F32)

    def in_blocks(bb):
        bb = bb.reshape(nblk, gpb, state, hdim)
        return jnp.einsum('mgph,gk->mghkp', bb, eye).reshape(nblk, gpb * hdim, gpb * state)

    def out_blocks(cc):
        cc = cc.reshape(nblk, gpb, hdim, state)
        return jnp.einsum('mghp,gk->mgpkh', cc, eye).reshape(nblk, gpb * state, gpb * hdim)

    wb = jnp.concatenate([in_blocks(bb_re), in_blocks(bb_im)], axis=2).astype(BF16)
    wc = jnp.concatenate([out_blocks(c_re), out_blocks(-c_im)], axis=1).astype(BF16)
    a_vec = jnp.concatenate([ab_re.reshape(1, n), ab_im.reshape(1, n)], axis=1)
    ks = jnp.arange(1, steps + 1, dtype=F32)[:, None, None]
    pw_mag = jnp.exp(lam_dt_re[None] * ks)
    pw_re, pw_im = pw_mag * jnp.cos(lam_dt_im[None] * ks), pw_mag * jnp.sin(lam_dt_im[None] * ks)
    pow_tab = jnp.concatenate([pw_re.reshape(steps, n), pw_im.reshape(steps, n)], axis=1)
    at_vec = pow_tab[steps - 1:steps]
    r = np.arange(t_len)
    perm_np = np.zeros((t_len, t_len), np.float32)
    perm_np[r, (r % SUBLANES) * steps + r // SUBLANES] = 1.0
    perm = jnp.asarray(perm_np, BF16)
    permt = jnp.asarray(perm_np.T, BF16)
    d_vec = d_skip.reshape(1, width)
    wglu = w_glu.astype(BF16)
    bglu = b_glu[None]

    tok = pl.BlockSpec((1, t_len, width), lambda b, i: (b, i, 0))
    full = lambda a: pl.BlockSpec(a.shape, lambda b, i: (0,) * a.ndim)
    consts = (perm, permt, wb, wc, a_vec, at_vec, pow_tab, d_vec, wglu, bglu)
    return pl.pallas_call(
        _s5_kernel,
        out_shape=jax.ShapeDtypeStruct((bsz, seq, width), BF16),
        grid=(bsz, seq // t_len),
        in_specs=[tok, tok] + [full(a) for a in consts],
        out_specs=tok,
        scratch_shapes=[pltpu.VMEM((t_len, 2 * S5_COLS), F32), pltpu.VMEM((t_len, 2 * S5_COLS), BF16),
                        pltpu.VMEM((1, 2 * n), F32), pltpu.VMEM((SUBLANES, 2 * n), F32)],
        compiler_params=_params(("parallel", "arbitrary")),
        name="s5_mixer",
    )(u, z_a, *consts)


def _softmax_init(m_ref, acc_ref):
    m_ref[...] = jnp.full_like(m_ref, NEG_INF)
    acc_ref[...] = jnp.zeros_like(acc_ref)


def _softmax_tile(s, mask, v_ext, m_ref, acc_ref):
    rows, tk = s.shape
    if mask is not None:
        heads = rows // mask.shape[1]
        s = jnp.where(mask, s.reshape(heads, *mask.shape[1:]), NEG_INF).reshape(rows, tk)
    m_old = m_ref[...]
    m_new = jnp.maximum(m_old, jnp.max(s, axis=-1, keepdims=True))
    e = jnp.exp2(s - jnp.concatenate([m_new] * (tk // LANES), axis=1))
    if mask is not None:
        e = jnp.where(mask, e.reshape(heads, *mask.shape[1:]), 0.0).reshape(rows, tk)
    alpha = jnp.exp2(m_old - m_new)
    acc_ref[...] = (jnp.concatenate([alpha, alpha], axis=1) * acc_ref[...]
                    + _dot(e.astype(BF16), v_ext))
    m_ref[...] = m_new


def _softmax_finish(acc_ref):
    acc = acc_ref[...]
    return acc[:, :LANES] / jnp.maximum(acc[:, LANES:], TINY)


def _with_ones(v):
    return jnp.concatenate([v, jnp.ones(v.shape, v.dtype)], axis=1)


def _pipelined_causal_tiles(n_full, scores, update, sa_ref, sb_ref, causal, filler=None):
    def pair(i, carry):
        scores(2 * i + 1, sb_ref)
        update(sa_ref, 2 * i, None)
        scores(2 * i + 2, sa_ref)
        update(sb_ref, 2 * i + 1, None)
        return carry

    scores(0, sa_ref)
    if filler is not None:
        filler()
    lax.fori_loop(0, n_full // 2, pair, 0)

    @pl.when(n_full % 2 == 1)
    def _():
        scores(n_full, sb_ref)
        update(sa_ref, n_full - 1, None)
        update(sb_ref, n_full, causal)

    @pl.when(n_full % 2 == 0)
    def _():
        update(sa_ref, n_full, causal)


def _mla_kernel(q_ref, k_ref, v_ref, o_ref, m_ref, acc_ref, sa_ref, sb_ref):
    qi = pl.program_id(1)
    heads, tq = q_ref.shape[1:3]
    tk = ATT_TK
    _softmax_init(m_ref, acc_ref)
    n_full = qi * tq // tk
    owns = [slice(hd * tq, (hd + 1) * tq) for hd in range(heads)]

    def scores(ki, s_ref):
        rows = pl.ds(pl.multiple_of(ki * tk, tk), tk)
        for hd, own in enumerate(owns):
            s_ref[own] = _dot_nt(q_ref[0, hd], k_ref[0, hd, rows, :])

    def update(s_ref, ki, mask):
        rows = pl.ds(pl.multiple_of(ki * tk, tk), tk)
        for hd, own in enumerate(owns):
            v_ext = _with_ones(v_ref[0, hd, rows, :])
            if mask is None:
                _softmax_tile(s_ref[own], None, v_ext, m_ref.at[own], acc_ref.at[own])
            else:
                half = tq // 2
                top = slice(own.start, own.start + half)
                low = slice(own.start + half, own.stop)
                _softmax_tile(s_ref[top, :half], mask[:, :half, :half], v_ext[:half],
                              m_ref.at[top], acc_ref.at[top])
                _softmax_tile(s_ref[low], mask[:, half:, :], v_ext, m_ref.at[low], acc_ref.at[low])

    qpos = qi * tq + lax.broadcasted_iota(jnp.int32, (1, tq, tk), 1)
    kpos = n_full * tk + lax.broadcasted_iota(jnp.int32, (1, tq, tk), 2)
    _pipelined_causal_tiles(n_full, scores, update, sa_ref, sb_ref, kpos <= qpos)
    for hd in range(heads):
        o_ref[0, :, hd * MLA_V:(hd + 1) * MLA_V] = _softmax_finish(
            acc_ref.at[hd * tq:(hd + 1) * tq]).astype(o_ref.dtype)


def _mla(q, k, v):
    bsz, heads, seq, dk = q.shape
    tq = ATT_TQ
    assert tq == ATT_TK
    return pl.pallas_call(
        _mla_kernel,
        out_shape=jax.ShapeDtypeStruct((bsz, seq, heads * MLA_V), BF16),
        grid=(bsz, seq // tq),
        in_specs=[pl.BlockSpec((1, heads, tq, dk), lambda b, qi: (b, 0, qi, 0)),
                  pl.BlockSpec((1, heads, seq, dk), lambda b, qi: (b, 0, 0, 0)),
                  pl.BlockSpec((1, heads, seq, MLA_V), lambda b, qi: (b, 0, 0, 0))],
        out_specs=pl.BlockSpec((1, tq, heads * MLA_V), lambda b, qi: (b, qi, 0)),
        scratch_shapes=[pltpu.VMEM((heads * tq, LANES), F32), pltpu.VMEM((heads * tq, 2 * LANES), F32),
                        pltpu.VMEM((heads * tq, ATT_TK), F32), pltpu.VMEM((heads * tq, ATT_TK), F32)],
        compiler_params=_params(("parallel", "arbitrary")),
        name="mla_attention",
    )(q, k, v)


def _ev_out_kernel(ya_ref, o_ref, zb_ref, x_ref, mod_ref, g_ref, wa_ref, wb_ref, out_ref):
    d = x_ref.shape[-1]
    yb = (o_ref[0].astype(F32) * _silu(zb_ref[0].astype(F32))).astype(BF16)
    y = _dot(ya_ref[0], wa_ref[...]) + _dot(yb, wb_ref[...])
    gate = mod_ref[0][:, 2 * d:]
    out_ref[0] = x_ref[0] + gate * _rms(y, g_ref[...])


def _ev_out(y_a, o_mla, z_b, x, mod, post_g, w_out):
    bsz, seq, d = x.shape
    tm = TOKEN_TILE
    wa = w_out[:y_a.shape[-1]].astype(BF16)
    wb = w_out[y_a.shape[-1]:].astype(BF16)
    g = post_g[None]
    tok = lambda n: pl.BlockSpec((1, tm, n), lambda b, i: (b, i, 0))
    full = lambda a: pl.BlockSpec(a.shape, lambda b, i: (0,) * a.ndim)
    return pl.pallas_call(
        _ev_out_kernel,
        out_shape=jax.ShapeDtypeStruct(x.shape, F32),
        grid=(bsz, seq // tm),
        in_specs=[tok(y_a.shape[-1]), tok(o_mla.shape[-1]), tok(z_b.shape[-1]), tok(d),
                  pl.BlockSpec((1, 1, 3 * d), lambda b, i: (b, 0, 0)), full(g), full(wa), full(wb)],
        out_specs=tok(d),
        compiler_params=_params(("parallel", "parallel")),
        name="even_out_proj",
    )(y_a, o_mla, z_b, x, mod[:, None, :], g, wa, wb)


OD_Q, OD_KV, OD_Z = 1024, 256, 1024
OD_NGATE = 3 * NSA_HEADS
OD_GATE = LANES
OD_OFF = np.cumsum([0, OD_Q] + [OD_KV] * 6 + [OD_NGATE, OD_Z])
OD_COLS = -(-OD_OFF[-1] // LANES) * LANES


def _od_in_kernel(x_ref, mod_ref, g_ref, w_ref, cn_ref, sn_ref,
                  q_ref, kc_ref, vc_ref, ks_ref, vs_ref, kw_ref, vw_ref, gate_ref, z_ref):
    h = _modulated_input(x_ref, mod_ref, g_ref)
    p = _dot(h, w_ref[...])
    o = OD_OFF
    cos, sin = cn_ref[0], sn_ref[0]

    def rope(t):
        return t * cos + pltpu.roll(t, NSA_DIM // 2, axis=1) * sin

    scale = NSA_DIM ** -0.5 * LOG2E
    for hd in range(NSA_HEADS):
        t = p[:, hd * NSA_DIM:(hd + 1) * NSA_DIM]
        q_ref[0, hd // NSA_REP, hd % NSA_REP] = (rope(t) * scale).astype(BF16)
    plain = (kc_ref, vc_ref, None, vs_ref, None, vw_ref)
    roped = (None, None, ks_ref, None, kw_ref, None)
    for j in range(6):
        for g in range(NSA_GROUPS):
            lo = o[1 + j] + g * NSA_DIM
            t = p[:, lo:lo + NSA_DIM]
            if plain[j] is not None:
                plain[j][0, g] = t.astype(plain[j].dtype)
            else:
                roped[j][0, g] = rope(t).astype(BF16)
    lane = lax.broadcasted_iota(jnp.int32, cos.shape, 1)
    gate_ref[0] = jnp.where(lane < OD_NGATE, jax.nn.sigmoid(p[:, o[7]:o[7] + LANES]), 0.0)
    z_ref[0] = p[:, o[8]:o[9]].astype(BF16)


def _od_in(x, mod, pre_g, w_in, cos_n, sin_n):
    bsz, seq, d = x.shape
    tm = TOKEN_TILE
    w = jnp.pad(w_in, ((0, 0), (0, OD_COLS - w_in.shape[1]))).astype(BF16)
    g = pre_g[None]
    tok = lambda n: pl.BlockSpec((1, tm, n), lambda b, i: (b, i, 0))
    full = lambda a: pl.BlockSpec(a.shape, lambda b, i: (0,) * a.ndim)
    q_spec = pl.BlockSpec((1, NSA_GROUPS, NSA_REP, tm, NSA_DIM), lambda b, i: (b, 0, 0, i, 0))
    kv_spec = pl.BlockSpec((1, NSA_GROUPS, tm, NSA_DIM), lambda b, i: (b, 0, i, 0))
    kv_shape = jax.ShapeDtypeStruct((bsz, NSA_GROUPS, seq, NSA_DIM), BF16)
    cmp_shape = jax.ShapeDtypeStruct(kv_shape.shape, F32)
    return pl.pallas_call(
        _od_in_kernel,
        out_shape=(jax.ShapeDtypeStruct((bsz, NSA_GROUPS, NSA_REP, seq, NSA_DIM), BF16),)
        + (cmp_shape,) * 2 + (kv_shape,) * 4
        + (jax.ShapeDtypeStruct((bsz, seq, OD_GATE), F32), jax.ShapeDtypeStruct((bsz, seq, OD_Z), BF16)),
        grid=(bsz, seq // tm),
        in_specs=[tok(d), pl.BlockSpec((1, 1, 3 * d), lambda b, i: (b, 0, 0)), full(g), full(w),
                  tok(LANES), tok(LANES)],
        out_specs=(q_spec,) + (kv_spec,) * 6 + (tok(OD_GATE), tok(OD_Z)),
        compiler_params=_params(("parallel", "parallel")),
        name="odd_in_proj",
    )(x, mod[:, None, :], g, w, cos_n, sin_n)


def _compress_kernel(x_ref, pe_ref, w1_ref, w2_ref, cos_ref, sin_ref, o_ref, *, use_rope):
    seq, d = x_ref.shape[2:]
    nb = seq // CMP_STRIDE
    lo = _dot(pe_ref[...], w1_ref[...])[0:1]
    hi = jnp.zeros((nb, w1_ref.shape[1]), F32)
    for l in range(CMP_STRIDE):
        xl = x_ref[0, 0, pl.ds(l, nb, stride=CMP_STRIDE), :].astype(BF16)
        lo = lo + _dot(xl, w1_ref[l * d:(l + 1) * d])
        hi = hi + _dot(xl, w1_ref[(CMP_STRIDE + l) * d:(CMP_STRIDE + l + 1) * d])
    row = lax.broadcasted_iota(jnp.int32, hi.shape, 0)
    pre = lo + jnp.where(row < nb - 1, pltpu.roll(hi, nb - 1, axis=0), 0.0)
    out = _dot(_gelu_tanh(pre).astype(BF16), w2_ref[...])
    if use_rope:
        out = out * cos_ref[0] + pltpu.roll(out, NSA_DIM // 2, axis=1) * sin_ref[0]
    o_ref[0, 0] = out.astype(BF16)


def _compress(kv, pe, w1, w2, cos_end, sin_end, use_rope):
    bsz, groups, seq, d = kv.shape
    nb = seq // CMP_STRIDE
    pe_rows = jnp.broadcast_to(pe.reshape(1, CMP_LEN * d), (SUBLANES, CMP_LEN * d)).astype(BF16)
    w1b, w2b = w1.astype(BF16), w2.astype(BF16)
    full = lambda a: pl.BlockSpec(a.shape, lambda b, g: (0,) * a.ndim)
    end = pl.BlockSpec((1, nb, d), lambda b, g: (b, 0, 0))
    return pl.pallas_call(
        functools.partial(_compress_kernel, use_rope=use_rope),
        out_shape=jax.ShapeDtypeStruct((bsz, groups, nb, d), BF16),
        grid=(bsz, groups),
        in_specs=[pl.BlockSpec((1, 1, seq, d), lambda b, g: (b, g, 0, 0)),
                  full(pe_rows), full(w1b), full(w2b), end, end],
        out_specs=pl.BlockSpec((1, 1, nb, d), lambda b, g: (b, g, 0, 0)),
        compiler_params=_params(("parallel", "parallel")),
        name="nsa_compress",
    )(kv, pe_rows, w1b, w2b, cos_end, sin_end)


def _cmp_sel_kernel(q_ref, kc_ref, vc_ref, pool_ref, o_ref, sel_ref, **static):
    for g in range(q_ref.shape[1]):
        _cmp_sel_group(g, q_ref, kc_ref, vc_ref, pool_ref, o_ref, sel_ref, **static)


def _cmp_sel_group(g, q_ref, kc_ref, vc_ref, pool_ref, o_ref, sel_ref, *, n_cmp, n_sel, n_top):
    qi = pl.program_id(1)
    rep, tq, d = q_ref.shape[2:]
    nb = kc_ref.shape[2]
    q = q_ref[0, g].reshape(rep * tq, d)
    s = _dot_nt(q, kc_ref[0, g]).reshape(rep, tq, nb)
    qpos = qi * tq + lax.broadcasted_iota(jnp.int32, (tq, nb), 0)
    blk = lax.broadcasted_iota(jnp.int32, (tq, nb), 1)
    mask = ((blk * CMP_STRIDE + (CMP_LEN - 1) <= qpos) & (blk < n_cmp))[None]
    s = jnp.where(mask, s, NEG_INF)
    m = jnp.max(s, axis=-1, keepdims=True)
    e = jnp.where(mask, jnp.exp2(s - m), 0.0)
    p = e / jnp.maximum(jnp.sum(e, axis=-1, keepdims=True), TINY)
    o = _dot(p.reshape(rep * tq, nb).astype(BF16), vc_ref[0, g])
    o_ref[0, g] = o.reshape(rep, tq, d).astype(o_ref.dtype)

    rows = pool_ref.shape[0]
    imp = _dot_nt(pool_ref[...], jnp.sum(p, axis=0), precision=HIGHEST)[:n_sel]
    bid = lax.broadcasted_iota(jnp.int32, (n_sel, tq), 0)
    cur = (qi * tq + lax.broadcasted_iota(jnp.int32, (n_sel, tq), 1)) // SEL_BLOCK
    forced = (bid == 0) | (bid == cur) | (bid == cur - 1)
    imp = jnp.where(forced, FORCE_SCORE, jnp.where(bid <= cur, imp, -1.0))
    groups = [imp[g:g + SUBLANES] for g in range(0, n_sel, SUBLANES)]
    sub = lax.broadcasted_iota(jnp.int32, (SUBLANES, tq), 0)
    ranks = [jnp.zeros((SUBLANES, tq), F32) for _ in groups]
    for j in range(n_sel):
        vj = jnp.broadcast_to(imp[j:j + 1, :], (SUBLANES, tq))
        for gi, grp in enumerate(groups):
            lo = gi * SUBLANES
            if lo > j:
                first = vj >= grp
            elif lo + SUBLANES - 1 <= j:
                first = vj > grp
            else:
                first = (vj > grp) | ((vj == grp) & (sub > j - lo))
            ranks[gi] = ranks[gi] + jnp.where(first, 1.0, 0.0)
    rank = jnp.concatenate(ranks, axis=0)
    bias = jnp.where((rank < n_top) & (bid <= cur), 0.0, NEG_INF)
    bias = jnp.concatenate([bias, jnp.full((rows - n_sel, tq), NEG_INF, F32)], axis=0)
    sel_ref[0, g] = bias.T.astype(sel_ref.dtype)


def _cmp_sel(q, kc, vc):
    bsz, groups, rep, seq, d = q.shape
    nb = kc.shape[2]
    tq = CMP_TQ
    n_cmp = (seq - CMP_LEN) // CMP_STRIDE + 1
    n_sel = seq // SEL_BLOCK
    n_top = min(SEL_TOP, n_sel)
    ratio = SEL_BLOCK // CMP_STRIDE
    assert n_sel <= LANES and n_sel * ratio == nb
    pool_np = np.zeros((LANES, nb), np.float32)
    pool_np[np.arange(nb) // ratio, np.arange(nb)] = 1.0
    pool = jnp.asarray(pool_np)
    kv_spec = pl.BlockSpec((1, groups, nb, d), lambda b, i: (b, 0, 0, 0))
    q_spec = pl.BlockSpec((1, groups, rep, tq, d), lambda b, i: (b, 0, 0, i, 0))
    return pl.pallas_call(
        functools.partial(_cmp_sel_kernel, n_cmp=n_cmp, n_sel=n_sel, n_top=n_top),
        out_shape=(jax.ShapeDtypeStruct(q.shape, BF16),
                   jax.ShapeDtypeStruct((bsz, groups, seq, LANES), BF16)),
        grid=(bsz, seq // tq),
        in_specs=[q_spec, kv_spec, kv_spec, pl.BlockSpec(pool.shape, lambda b, i: (0, 0))],
        out_specs=(q_spec, pl.BlockSpec((1, groups, tq, LANES), lambda b, i: (b, 0, i, 0))),
        compiler_params=_params(("parallel", "parallel")),
        name="nsa_cmp_select",
    )(q, kc, vc, pool)


def _sel_win_kernel(q_ref, bias_ref, k_ref, blk_ref, v_ref, kw_ref, vw_ref, o_ref, ow_ref,
                    m_ref, acc_ref, sa_ref, sb_ref, *, span):
    qi = pl.program_id(2)
    rep, tq, d = q_ref.shape[2:]
    tk = NSA_TK
    part = rep // SEL_CHAINS
    q_ext = [jnp.concatenate([q_ref[0, 0, c * part:(c + 1) * part].reshape(part * tq, d),
                              jnp.concatenate([bias_ref[0, 0]] * part, axis=0)], axis=1)
             for c in range(SEL_CHAINS)]
    _softmax_init(m_ref, acc_ref)
    n_full = qi * tq // tk

    owns = [slice(c * part * tq, (c + 1) * part * tq) for c in range(SEL_CHAINS)]

    def scores(ki, s_ref):
        rows = pl.ds(pl.multiple_of(ki * tk, tk), tk)
        k_ext = jnp.concatenate([k_ref[0, 0, rows, :], blk_ref[rows, :]], axis=1)
        for c, own in enumerate(owns):
            s_ref[own] = _dot_nt(q_ext[c], k_ext)

    def update(s_ref, ki, mask):
        def run(width):
            rows = pl.ds(pl.multiple_of(ki * tk, tk), width)
            v_ext = _with_ones(v_ref[0, 0, rows, :])
            sub = None if mask is None else mask[:, :, :width]
            for own in owns:
                _softmax_tile(s_ref[own, :width], sub, v_ext, m_ref.at[own], acc_ref.at[own])

        if mask is None:
            run(tk)
        else:
            short = qi * tq + tq <= ki * tk + tk // 2
            pl.when(short)(lambda: run(tk // 2))
            pl.when(jnp.logical_not(short))(lambda: run(tk))

    def window():
        start = pl.multiple_of(jnp.maximum(qi * tq + tq - span, 0), tq)
        rows = pl.ds(start, span)
        wq = qi * tq + lax.broadcasted_iota(jnp.int32, (1, tq, span), 1)
        wk = start + lax.broadcasted_iota(jnp.int32, (1, tq, span), 2)
        mask = (wq - wk >= 0) & (wq - wk < WINDOW)
        k, v_ext = kw_ref[0, 0, rows, :], _with_ones(vw_ref[0, 0, rows, :])
        wpart = rep // WIN_CHAINS
        for h0 in range(0, rep, wpart):
            q = q_ref[0, 0, h0:h0 + wpart].reshape(wpart * tq, d)
            s = jnp.where(mask, _dot_nt(q, k).reshape(wpart, tq, span), NEG_INF)
            m = jnp.max(s, axis=-1, keepdims=True)
            e = jnp.where(mask, jnp.exp2(s - m), 0.0).reshape(wpart * tq, span)
            acc = _dot(e.astype(BF16), v_ext)
            o = acc[:, :LANES] / jnp.maximum(acc[:, LANES:], TINY)
            ow_ref[0, 0, h0:h0 + wpart] = o.reshape(wpart, tq, d).astype(ow_ref.dtype)

    qpos = qi * tq + lax.broadcasted_iota(jnp.int32, (1, tq, tk), 1)
    kpos = n_full * tk + lax.broadcasted_iota(jnp.int32, (1, tq, tk), 2)
    _pipelined_causal_tiles(n_full, scores, update, sa_ref, sb_ref, kpos <= qpos, filler=window)
    o_ref[0, 0] = _softmax_finish(acc_ref).reshape(rep, tq, d).astype(o_ref.dtype)


def _sel_win_attention(q, bias, k, v, kw, vw):
    bsz, groups, rep, seq, d = q.shape
    tq = NSA_TQ
    assert NSA_TK % tq == 0
    span = (-(-(WINDOW - 1) // tq) + 1) * tq
    assert span <= seq
    onehot_np = np.zeros((seq, LANES), np.float32)
    onehot_np[np.arange(seq), np.arange(seq) // SEL_BLOCK] = 1.0
    onehot = jnp.asarray(onehot_np, BF16)
    q_spec = pl.BlockSpec((1, 1, rep, tq, d), lambda b, g, qi: (b, g, 0, qi, 0))
    kv_spec = pl.BlockSpec((1, 1, seq, d), lambda b, g, qi: (b, g, 0, 0))
    out = jax.ShapeDtypeStruct(q.shape, BF16)
    return pl.pallas_call(
        functools.partial(_sel_win_kernel, span=span),
        out_shape=(out, out),
        grid=(bsz, groups, seq // tq),
        in_specs=[q_spec, pl.BlockSpec((1, 1, tq, LANES), lambda b, g, qi: (b, g, qi, 0)),
                  kv_spec, pl.BlockSpec((seq, LANES), lambda b, g, qi: (0, 0)), kv_spec, kv_spec, kv_spec],
        out_specs=(q_spec, q_spec),
        scratch_shapes=[pltpu.VMEM((rep * tq, LANES), F32), pltpu.VMEM((rep * tq, 2 * LANES), F32),
                        pltpu.VMEM((rep * tq, NSA_TK), F32), pltpu.VMEM((rep * tq, NSA_TK), F32)],
        compiler_params=_params(("parallel", "parallel", "arbitrary")),
        name="nsa_selected_window_attention",
    )(q, bias, k, onehot, v, kw, vw)


def _od_out_kernel(oc_ref, os_ref, ow_ref, gate_ref, spread_ref, z_ref, x_ref, mod_ref, g_ref, w_ref, out_ref):
    d = x_ref.shape[-1]
    gates = gate_ref[0]
    hi = gates.astype(BF16)
    lo = (gates - hi.astype(F32)).astype(BF16)
    wide = _dot(jnp.concatenate([hi, lo], axis=-1), spread_ref[...])
    z = z_ref[0].astype(F32)
    parts = []
    for hd in range(NSA_HEADS):
        g, r = hd // NSA_REP, hd % NSA_REP
        c = 3 * hd * NSA_DIM
        o = (wide[:, c:c + NSA_DIM] * oc_ref[0, g, r].astype(F32)
             + wide[:, c + NSA_DIM:c + 2 * NSA_DIM] * os_ref[0, g, r].astype(F32)
             + wide[:, c + 2 * NSA_DIM:c + 3 * NSA_DIM] * ow_ref[0, g, r].astype(F32))
        parts.append((o * _silu(z[:, hd * NSA_DIM:(hd + 1) * NSA_DIM])).astype(BF16))
    y = _dot(jnp.concatenate(parts, axis=-1), w_ref[...])
    gate = mod_ref[0][:, 2 * d:]
    out_ref[0] = x_ref[0] + gate * _rms(y, g_ref[...])


def _od_out(o_cmp, o_sel, o_win, gates, z, x, mod, post_g, w_out):
    bsz, seq, d = x.shape
    tm = TOKEN_TILE
    w = w_out.astype(BF16)
    g = post_g[None]
    n_gate = 3 * NSA_HEADS
    spread_np = np.zeros((2 * OD_GATE, n_gate * NSA_DIM), np.float32)
    for c in range(n_gate):
        spread_np[[c, OD_GATE + c], c * NSA_DIM:(c + 1) * NSA_DIM] = 1.0
    spread = jnp.asarray(spread_np, BF16)
    tok = lambda n: pl.BlockSpec((1, tm, n), lambda b, i: (b, i, 0))
    full = lambda a: pl.BlockSpec(a.shape, lambda b, i: (0,) * a.ndim)
    o_spec = pl.BlockSpec((1, NSA_GROUPS, NSA_REP, tm, NSA_DIM), lambda b, i: (b, 0, 0, i, 0))
    return pl.pallas_call(
        _od_out_kernel,
        out_shape=jax.ShapeDtypeStruct(x.shape, F32),
        grid=(bsz, seq // tm),
        in_specs=[o_spec, o_spec, o_spec, tok(OD_GATE), full(spread), tok(OD_Z), tok(d),
                  pl.BlockSpec((1, 1, 3 * d), lambda b, i: (b, 0, 0)), full(g), full(w)],
        out_specs=tok(d),
        compiler_params=_params(("parallel", "parallel")),
        name="odd_out_proj",
    )(o_cmp, o_sel, o_win, gates, spread, z, x, mod[:, None, :], g, w)


def _even_layer(x, mod, pre_g, post_g, tabs, w_in, lam_re, lam_im, log_dt, b_re, b_im, c_re, c_im,
                d_skip, w_glu, b_glu, q_norm_g, kv_norm_g, w_uq, w_ukv, w_out):
    cos_m, sin_m = tabs[0], tabs[1]
    u, z_a, z_b, q, k, v = _ev_in(x, mod, pre_g, w_in, q_norm_g, kv_norm_g, w_uq, w_ukv, cos_m, sin_m)
    y_a = _s5(u, z_a, lam_re, lam_im, log_dt, b_re, b_im, c_re, c_im, d_skip, w_glu, b_glu)
    o_mla = _mla(q, k, v)
    return _ev_out(y_a, o_mla, z_b, x, mod, post_g, w_out)


def _odd_layer(x, mod, pre_g, post_g, tabs, end_tabs, w_in, k_pe, k_w1, k_w2, v_pe, v_w1, v_w2, w_out):
    cos_n, sin_n = tabs[2], tabs[3]
    cos_end, sin_end = end_tabs[2], end_tabs[3]
    q, k_c, v_c, k_s, v_s, k_w, v_w, gates, z = _od_in(x, mod, pre_g, w_in, cos_n, sin_n)
    kc = _compress(k_c, k_pe, k_w1, k_w2, cos_end, sin_end, True)
    vc = _compress(v_c, v_pe, v_w1, v_w2, cos_end, sin_end, False)
    o_cmp, sel = _cmp_sel(q, kc, vc)
    o_sel, o_win = _sel_win_attention(q, sel, k_s, v_s, k_w, v_w)
    return _od_out(o_cmp, o_sel, o_win, gates, z, x, mod, post_g, w_out)


def kernel(x, c, positions, pre_norm_g, post_norm_g, w_ada, b_ada, ev_w_in, ev_lam_re, ev_lam_im, ev_log_dt, ev_b_re, ev_b_im, ev_c_re, ev_c_im, ev_d_skip, ev_w_glu, ev_b_glu, ev_q_norm_g, ev_kv_norm_g, ev_w_uq, ev_w_ukv, ev_w_out, od_w_in, od_cmp_k_pe, od_cmp_k_w1, od_cmp_k_w2, od_cmp_v_pe, od_cmp_v_w1, od_cmp_v_w2, od_w_out):
    depth = pre_norm_g.shape[0]
    tabs = _rope_tables(positions)
    seq = positions.shape[1]
    pos_end = positions[:, CMP_LEN - 1::CMP_STRIDE]
    pos_end = jnp.pad(pos_end, ((0, 0), (0, seq // CMP_STRIDE - pos_end.shape[1])))
    end_tabs = _rope_tables(pos_end)
    mods = _modulation(c, w_ada, b_ada)
    for layer in range(depth):
        i = layer // 2
        if layer % 2 == 0:
            x = _even_layer(x, mods[layer], pre_norm_g[layer], post_norm_g[layer], tabs,
                            ev_w_in[i], ev_lam_re[i], ev_lam_im[i], ev_log_dt[i], ev_b_re[i], ev_b_im[i],
                            ev_c_re[i], ev_c_im[i], ev_d_skip[i], ev_w_glu[i], ev_b_glu[i],
                            ev_q_norm_g[i], ev_kv_norm_g[i], ev_w_uq[i], ev_w_ukv[i], ev_w_out[i])
        else:
            x = _odd_layer(x, mods[layer], pre_norm_g[layer], post_norm_g[layer], tabs, end_tabs,
                           od_w_in[i], od_cmp_k_pe[i], od_cmp_k_w1[i], od_cmp_k_w2[i],
                           od_cmp_v_pe[i], od_cmp_v_w1[i], od_cmp_v_w2[i], od_w_out[i])
    return x
```

```python
import functools
import math

import numpy as np
import jax
import jax.numpy as jnp
from jax import lax
from jax.experimental import pallas as pl
from jax.experimental.pallas import tpu as pltpu

F32 = jnp.float32
BF16 = jnp.bfloat16
HIGHEST = lax.Precision.HIGHEST

EPS = 1e-6
ROPE_THETA = 10000.0
NEG_INF = -1e30
TINY = 1e-30
FORCE_SCORE = 1e9
LOG2E = 1.0 / math.log(2.0)

S5_GROUP = 16
S5_STATE = 64
MLA_HEADS = 4
MLA_NOPE = 128
MLA_ROPE = 64
MLA_V = 128
NSA_HEADS = 8
NSA_GROUPS = 2
NSA_REP = NSA_HEADS // NSA_GROUPS
NSA_DIM = 128
CMP_LEN = 32
CMP_STRIDE = 16
CMP_HIDDEN = 256
SEL_BLOCK = 64
SEL_TOP = 16
WINDOW = 512

LANES = 128
SUBLANES = 8
VMEM_LIMIT = 48 * 1024 * 1024

TOKEN_TILE = 1024
S5_CHUNK = 512
S5_COLS = 512
ATT_TQ = 512
ATT_TK = 512
NSA_TQ = 256
NSA_TK = 512
CMP_TQ = 1024
SEL_CHAINS = 4
WIN_CHAINS = 4


def _params(sem):
    return pltpu.CompilerParams(dimension_semantics=sem, vmem_limit_bytes=VMEM_LIMIT)


def _dot(a, b, precision=None):
    return jnp.dot(a, b, preferred_element_type=F32, precision=precision)


def _dot_nt(a, b, precision=None):
    return lax.dot_general(a, b, (((1,), (1,)), ((), ())), preferred_element_type=F32,
                           precision=precision)


def _silu(v):
    return v * jax.nn.sigmoid(v)


def _gelu_tanh(v):
    return 0.5 * v * (1.0 + jnp.tanh(math.sqrt(2.0 / math.pi) * (v + 0.044715 * (v * v * v))))


def _rms(v, g):
    return v * lax.rsqrt(jnp.mean(v * v, axis=-1, keepdims=True) + EPS) * g


def _rope_tab_kernel(pos_ref, f_ref, sgm_ref, sgn_ref, cm_ref, sm_ref, cn_ref, sn_ref):
    hn, hm = NSA_DIM // 2, MLA_ROPE // 2
    ang = pos_ref[0] * f_ref[...]
    c, s = jnp.cos(ang), jnp.sin(ang)
    cn_ref[0] = jnp.concatenate([c[:, :hn]] * (LANES // hn), axis=1)
    sn_ref[0] = jnp.concatenate([s[:, :hn]] * (LANES // hn), axis=1) * sgn_ref[...]
    cm_ref[0] = jnp.concatenate([c[:, hn:hn + hm]] * (LANES // hm), axis=1)
    sm_ref[0] = jnp.concatenate([s[:, hn:hn + hm]] * (LANES // hm), axis=1) * sgm_ref[...]


def _rope_tables(positions):
    bsz, seq = positions.shape
    ts = min(TOKEN_TILE, seq)
    pos = positions.astype(F32)[..., None]
    half_m, half_n = MLA_ROPE // 2, NSA_DIM // 2
    fm = ROPE_THETA ** (-jnp.arange(half_m, dtype=F32) / half_m)
    fn = ROPE_THETA ** (-jnp.arange(half_n, dtype=F32) / half_n)
    freqs = jnp.concatenate([fn, fm, jnp.zeros((LANES - half_n - half_m,), F32)])[None]
    sgm = jnp.concatenate([-jnp.ones((half_m,), F32), jnp.ones((half_m,), F32),
                           jnp.zeros((LANES - MLA_ROPE,), F32)])[None]
    sgn = jnp.concatenate([-jnp.ones((half_n,), F32), jnp.ones((half_n,), F32)])[None]
    row = pl.BlockSpec((1, LANES), lambda b, i: (0, 0))
    tab = pl.BlockSpec((1, ts, LANES), lambda b, i: (b, i, 0))
    shp = jax.ShapeDtypeStruct((bsz, seq, LANES), F32)
    return pl.pallas_call(
        _rope_tab_kernel,
        out_shape=(shp, shp, shp, shp),
        grid=(bsz, seq // ts),
        in_specs=[pl.BlockSpec((1, ts, 1), lambda b, i: (b, i, 0)), row, row, row],
        out_specs=(tab, tab, tab, tab),
        compiler_params=_params(("parallel", "parallel")),
        name="rope_tables",
    )(pos, freqs, sgm, sgn)


def _mod_kernel(c_ref, w_ref, b_ref, o_ref):
    s = _silu(c_ref[...])
    o_ref[0] = _dot(s, w_ref[0], precision=HIGHEST) + b_ref[0]


def _modulation(c, w_ada, b_ada):
    depth, d, _ = w_ada.shape
    bsz = c.shape[0]
    rows = -(-bsz // SUBLANES) * SUBLANES
    c_pad = jnp.pad(c, ((0, rows - bsz), (0, 0)))
    out = pl.pallas_call(
        _mod_kernel,
        out_shape=jax.ShapeDtypeStruct((depth, rows, 3 * d), F32),
        grid=(depth, 3),
        in_specs=[pl.BlockSpec((rows, d), lambda l, j: (0, 0)),
                  pl.BlockSpec((1, d, d), lambda l, j: (l, 0, j)),
                  pl.BlockSpec((1, 1, d), lambda l, j: (l, 0, j))],
        out_specs=pl.BlockSpec((1, rows, d), lambda l, j: (l, 0, j)),
        compiler_params=_params(("parallel", "parallel")),
        name="modulation",
    )(c_pad, w_ada, b_ada[:, None, :])
    return out[:, :bsz]


def _modulated_input(x_ref, mod_ref, g_ref):
    d = x_ref.shape[-1]
    x = x_ref[0]
    mod = mod_ref[0]
    shift, scale = mod[:, :d], mod[:, d:2 * d]
    return (_rms(x, g_ref[...]) * (1.0 + scale) + shift).astype(BF16)


EV_U, EV_ZA, EV_CQ, EV_CKV, EV_KPE, EV_ZB = 512, 512, 768, 256, MLA_ROPE, 512
EV_OFF = np.cumsum([0, EV_U, EV_ZA, EV_CQ, EV_CKV, EV_KPE, EV_ZB])
EV_COLS = -(-EV_OFF[-1] // LANES) * LANES


def _ev_in_kernel(x_ref, mod_ref, g_ref, w_ref, gq_ref, gkv_ref, wuq_ref, wukv_ref, cm_ref, sm_ref,
                  u_ref, za_ref, zb_ref, q_ref, k_ref, v_ref):
    h = _modulated_input(x_ref, mod_ref, g_ref)
    p = _dot(h, w_ref[...])
    o = EV_OFF
    u_ref[0] = p[:, o[0]:o[1]].astype(BF16)
    za_ref[0] = p[:, o[1]:o[2]].astype(BF16)
    zb_ref[0] = p[:, o[5]:o[6]].astype(BF16)
    cos, sin = cm_ref[0], sm_ref[0]
    hq = MLA_ROPE // 2
    lane = lax.broadcasted_iota(jnp.int32, cos.shape, 1)
    first = lane < hq

    def rope(t):
        swapped = jnp.where(first, pltpu.roll(t, LANES - hq, axis=1), pltpu.roll(t, hq, axis=1))
        return t * cos + swapped * sin

    cq = _rms(p[:, o[2]:o[3]], gq_ref[...]).astype(BF16)
    q = _dot(cq, wuq_ref[...]) * ((MLA_NOPE + MLA_ROPE) ** -0.5 * LOG2E)
    ckv = _rms(p[:, o[3]:o[4]], gkv_ref[...]).astype(BF16)
    kv = _dot(ckv, wukv_ref[...])
    kpe = rope(jnp.where(lane < MLA_ROPE, p[:, o[4]:o[4] + LANES], 0.0)).astype(BF16)
    nope = MLA_HEADS * MLA_NOPE
    for hd in range(MLA_HEADS):
        b0 = hd * (MLA_NOPE + LANES)
        qpe = rope(q[:, b0 + MLA_NOPE:b0 + MLA_NOPE + LANES])
        q_ref[0, hd] = jnp.concatenate([q[:, b0:b0 + MLA_NOPE], qpe], axis=-1).astype(BF16)
        k_ref[0, hd] = jnp.concatenate([kv[:, hd * MLA_NOPE:(hd + 1) * MLA_NOPE].astype(BF16), kpe], axis=-1)
        v_ref[0, hd] = kv[:, nope + hd * MLA_V:nope + (hd + 1) * MLA_V].astype(BF16)


def _ev_in(x, mod, pre_g, w_in, q_norm_g, kv_norm_g, w_uq, w_ukv, cos_m, sin_m):
    bsz, seq, d = x.shape
    tm = TOKEN_TILE
    w = jnp.pad(w_in, ((0, 0), (0, EV_COLS - w_in.shape[1]))).astype(BF16)
    rank = w_uq.shape[0]
    wuq = jnp.pad(w_uq.reshape(rank, MLA_HEADS, MLA_NOPE + MLA_ROPE), ((0, 0), (0, 0), (0, LANES - MLA_ROPE)))
    wuq = wuq.reshape(rank, MLA_HEADS * (MLA_NOPE + LANES)).astype(BF16)
    rank = w_ukv.shape[0]
    wukv = jnp.concatenate([w_ukv.reshape(rank, MLA_HEADS, MLA_NOPE + MLA_V)[:, :, :MLA_NOPE].reshape(rank, -1),
                            w_ukv.reshape(rank, MLA_HEADS, MLA_NOPE + MLA_V)[:, :, MLA_NOPE:].reshape(rank, -1)],
                           axis=1).astype(BF16)

    tok = lambda n: pl.BlockSpec((1, tm, n), lambda b, i: (b, i, 0))
    full = lambda a: pl.BlockSpec(a.shape, lambda b, i: (0,) * a.ndim)
    out = lambda n: jax.ShapeDtypeStruct((bsz, seq, n), BF16)
    head = lambda n: pl.BlockSpec((1, MLA_HEADS, tm, n), lambda b, i: (b, 0, i, 0))
    head_out = lambda n: jax.ShapeDtypeStruct((bsz, MLA_HEADS, seq, n), BF16)
    gq, gkv, g = q_norm_g[None], kv_norm_g[None], pre_g[None]
    return pl.pallas_call(
        _ev_in_kernel,
        out_shape=(out(512), out(512), out(512), head_out(2 * LANES), head_out(2 * LANES), head_out(MLA_V)),
        grid=(bsz, seq // tm),
        in_specs=[tok(d), pl.BlockSpec((1, 1, 3 * d), lambda b, i: (b, 0, 0)), full(g), full(w),
                  full(gq), full(gkv), full(wuq), full(wukv), tok(LANES), tok(LANES)],
        out_specs=(tok(512), tok(512), tok(512), head(2 * LANES), head(2 * LANES), head(MLA_V)),
        compiler_params=_params(("parallel", "parallel")),
        name="even_in_proj",
    )(x, mod[:, None, :], g, w, gq, gkv, wuq, wukv, cos_m, sin_m)


def _s5_kernel(u_ref, z_ref, perm_ref, permt_ref, wb_ref, wc_ref, a_ref, at_ref, pow_ref,
               d_ref, wglu_ref, bglu_ref, o_ref, bu_ref, xb_ref, state_ref, carry_ref):
    t_len = u_ref.shape[1]
    n = a_ref.shape[1] // 2
    steps = t_len // SUBLANES

    @pl.when(pl.program_id(1) == 0)
    def _():
        state_ref[...] = jnp.zeros_like(state_ref)

    perm = perm_ref[...]
    u_p = _dot(perm, u_ref[0])
    z_p = _dot(perm, z_ref[0])
    u_pb = u_p.astype(BF16)
    bre, bim = pl.ds(0, S5_COLS), pl.ds(S5_COLS, S5_COLS)
    y_parts = []

    for cb in range(n // S5_COLS):
        u_cb = u_pb[:, cb * LANES:(cb + 1) * LANES]
        bu_ref[...] = _dot(u_cb, wb_ref[cb])
        re = pl.ds(cb * S5_COLS, S5_COLS)
        im = pl.ds(n + cb * S5_COLS, S5_COLS)
        ar = jnp.broadcast_to(a_ref[:, re], (SUBLANES, S5_COLS))
        ai = jnp.broadcast_to(a_ref[:, im], (SUBLANES, S5_COLS))

        def step(t, carry):
            xr, xi = carry
            rows = pl.ds(pl.multiple_of(t * SUBLANES, SUBLANES), SUBLANES)
            nr = ar * xr - ai * xi + bu_ref[rows, bre]
            ni = ar * xi + ai * xr + bu_ref[rows, bim]
            bu_ref[rows, bre] = nr
            bu_ref[rows, bim] = ni
            return nr, ni

        zero = jnp.zeros((SUBLANES, S5_COLS), F32)
        er, ei = lax.fori_loop(0, steps, step, (zero, zero), unroll=True)

        sr, si = state_ref[:, re], state_ref[:, im]
        tr, ti = at_ref[:, re], at_ref[:, im]
        for j in range(SUBLANES):
            carry_ref[j:j + 1, re] = sr
            carry_ref[j:j + 1, im] = si
            sr, si = (tr * sr - ti * si + er[j:j + 1], tr * si + ti * sr + ei[j:j + 1])
        state_ref[:, re] = sr
        state_ref[:, im] = si
        cr, ci = carry_ref[:, re], carry_ref[:, im]

        def fix(t2, _):
            xr, xi = [], []
            for k in range(2):
                t = 2 * t2 + k
                rows = pl.ds(pl.multiple_of(t * SUBLANES, SUBLANES), SUBLANES)
                pr, pi = pow_ref[pl.ds(t, 1), re], pow_ref[pl.ds(t, 1), im]
                xr.append(bu_ref[rows, bre] + pr * cr - pi * ci)
                xi.append(bu_ref[rows, bim] + pr * ci + pi * cr)
            rows = pl.ds(pl.multiple_of(t2 * 2 * SUBLANES, 2 * SUBLANES), 2 * SUBLANES)
            xb_ref[rows, bre] = jnp.concatenate(xr, axis=0).astype(BF16)
            xb_ref[rows, bim] = jnp.concatenate(xi, axis=0).astype(BF16)
            return 0

        lax.fori_loop(0, steps // 2, fix, 0, unroll=True)
        y_parts.append(_dot(xb_ref[...], wc_ref[cb]))

    y = jnp.concatenate(y_parts, axis=1) + d_ref[...] * u_p
    g = _gelu_tanh(y)
    gate = jax.nn.sigmoid(_dot(g.astype(BF16), wglu_ref[...]) + bglu_ref[...])
    out = (g * gate * _silu(z_p)).astype(BF16)
    o_ref[0] = _dot(permt_ref[...], out).astype(BF16)


def _s5(u, z_a, lam_re, lam_im, log_dt, b_re, b_im, c_re, c_im, d_skip, w_glu, b_glu):
    bsz, seq, width = u.shape
    groups, state = lam_re.shape
    t_len = S5_CHUNK
    steps = t_len // SUBLANES
    n = groups * state
    dt = jnp.exp(log_dt)[:, None]
    lam_dt_re, lam_dt_im = lam_re * dt, lam_im * dt
    decay = jnp.exp(lam_dt_re)
    ab_re, ab_im = decay * jnp.cos(lam_dt_im), decay * jnp.sin(lam_dt_im)
    den = lam_re * lam_re + lam_im * lam_im
    nr, ni = ab_re - 1.0, ab_im
    f_re = (nr * lam_re + ni * lam_im) / den
    f_im = (ni * lam_re - nr * lam_im) / den
    bb_re = f_re[..., None] * b_re - f_im[..., None] * b_im
    bb_im = f_re[..., None] * b_im + f_im[..., None] * b_re
    gpb = S5_COLS // state
    nblk = groups // gpb
    hdim = width // groups
    assert gpb * hdim == LANES and nblk * S5_COLS == n
    eye = jnp.eye(gpb, dtype=F32)

    def in_blocks(bb):
        bb = bb.reshape(nblk, gpb, state, hdim)
        return jnp.einsum('mgph,gk->mghkp', bb, eye).reshape(nblk, gpb * hdim, gpb * state)

    def out_blocks(cc):
        cc = cc.reshape(nblk, gpb, hdim, state)
        return jnp.einsum('mghp,gk->mgpkh', cc, eye).reshape(nblk, gpb * state, gpb * hdim)

    wb = jnp.concatenate([in_blocks(bb_re), in_blocks(bb_im)], axis=2).astype(BF16)
    wc = jnp.concatenate([out_blocks(c_re), out_blocks(-c_im)], axis=1).astype(BF16)
    a_vec = jnp.concatenate([ab_re.reshape(1, n), ab_im.reshape(1, n)], axis=1)
    ks = jnp.arange(1, steps + 1, dtype=F32)[:, None, None]
    pw_mag = jnp.exp(lam_dt_re[None] * ks)
    pw_re, pw_im = pw_mag * jnp.cos(lam_dt_im[None] * ks), pw_mag * jnp.sin(lam_dt_im[None] * ks)
    pow_tab = jnp.concatenate([pw_re.reshape(steps, n), pw_im.reshape(steps, n)], axis=1)
    at_vec = pow_tab[steps - 1:steps]
    r = np.arange(t_len)
    perm_np = np.zeros((t_len, t_len), np.float32)
    perm_np[r, (r % SUBLANES) * steps + r // SUBLANES] = 1.0
    perm = jnp.asarray(perm_np, BF16)
    permt = jnp.asarray(perm_np.T, BF16)
    d_vec = d_skip.reshape(1, width)
    wglu = w_glu.astype(BF16)
    bglu = b_glu[None]

    tok = pl.BlockSpec((1, t_len, width), lambda b, i: (b, i, 0))
    full = lambda a: pl.BlockSpec(a.shape, lambda b, i: (0,) * a.ndim)
    consts = (perm, permt, wb, wc, a_vec, at_vec, pow_tab, d_vec, wglu, bglu)
    return pl.pallas_call(
        _s5_kernel,
        out_shape=jax.ShapeDtypeStruct((bsz, seq, width), BF16),
        grid=(bsz, seq // t_len),
        in_specs=[tok, tok] + [full(a) for a in consts],
        out_specs=tok,
        scratch_shapes=[pltpu.VMEM((t_len, 2 * S5_COLS), F32), pltpu.VMEM((t_len, 2 * S5_COLS), BF16),
                        pltpu.VMEM((1, 2 * n), F32), pltpu.VMEM((SUBLANES, 2 * n), F32)],
        compiler_params=_params(("parallel", "arbitrary")),
        name="s5_mixer",
    )(u, z_a, *consts)


def _softmax_init(m_ref, acc_ref):
    m_ref[...] = jnp.full_like(m_ref, NEG_INF)
    acc_ref[...] = jnp.zeros_like(acc_ref)


def _softmax_tile(s, mask, v_ext, m_ref, acc_ref):
    rows, tk = s.shape
    if mask is not None:
        heads = rows // mask.shape[1]
        s = jnp.where(mask, s.reshape(heads, *mask.shape[1:]), NEG_INF).reshape(rows, tk)
    m_old = m_ref[...]
    m_new = jnp.maximum(m_old, jnp.max(s, axis=-1, keepdims=True))
    e = jnp.exp2(s - jnp.concatenate([m_new] * (tk // LANES), axis=1))
    if mask is not None:
        e = jnp.where(mask, e.reshape(heads, *mask.shape[1:]), 0.0).reshape(rows, tk)
    alpha = jnp.exp2(m_old - m_new)
    acc_ref[...] = (jnp.concatenate([alpha, alpha], axis=1) * acc_ref[...]
                    + _dot(e.astype(BF16), v_ext))
    m_ref[...] = m_new


def _softmax_finish(acc_ref):
    acc = acc_ref[...]
    return acc[:, :LANES] / jnp.maximum(acc[:, LANES:], TINY)


def _with_ones(v):
    return jnp.concatenate([v, jnp.ones(v.shape, v.dtype)], axis=1)


def _pipelined_causal_tiles(n_full, scores, update, sa_ref, sb_ref, causal, filler=None):
    def pair(i, carry):
        scores(2 * i + 1, sb_ref)
        update(sa_ref, 2 * i, None)
        scores(2 * i + 2, sa_ref)
        update(sb_ref, 2 * i + 1, None)
        return carry

    scores(0, sa_ref)
    if filler is not None:
        filler()
    lax.fori_loop(0, n_full // 2, pair, 0)

    @pl.when(n_full % 2 == 1)
    def _():
        scores(n_full, sb_ref)
        update(sa_ref, n_full - 1, None)
        update(sb_ref, n_full, causal)

    @pl.when(n_full % 2 == 0)
    def _():
        update(sa_ref, n_full, causal)


def _mla_kernel(q_ref, k_ref, v_ref, o_ref, m_ref, acc_ref, sa_ref, sb_ref):
    qi = pl.program_id(1)
    heads, tq = q_ref.shape[1:3]
    tk = ATT_TK
    _softmax_init(m_ref, acc_ref)
    n_full = qi * tq // tk
    owns = [slice(hd * tq, (hd + 1) * tq) for hd in range(heads)]

    def scores(ki, s_ref):
        rows = pl.ds(pl.multiple_of(ki * tk, tk), tk)
        for hd, own in enumerate(owns):
            s_ref[own] = _dot_nt(q_ref[0, hd], k_ref[0, hd, rows, :])

    def update(s_ref, ki, mask):
        rows = pl.ds(pl.multiple_of(ki * tk, tk), tk)
        for hd, own in enumerate(owns):
            v_ext = _with_ones(v_ref[0, hd, rows, :])
            if mask is None:
                _softmax_tile(s_ref[own], None, v_ext, m_ref.at[own], acc_ref.at[own])
            else:
                half = tq // 2
                top = slice(own.start, own.start + half)
                low = slice(own.start + half, own.stop)
                _softmax_tile(s_ref[top, :half], mask[:, :half, :half], v_ext[:half],
                              m_ref.at[top], acc_ref.at[top])
                _softmax_tile(s_ref[low], mask[:, half:, :], v_ext, m_ref.at[low], acc_ref.at[low])

    qpos = qi * tq + lax.broadcasted_iota(jnp.int32, (1, tq, tk), 1)
    kpos = n_full * tk + lax.broadcasted_iota(jnp.int32, (1, tq, tk), 2)
    _pipelined_causal_tiles(n_full, scores, update, sa_ref, sb_ref, kpos <= qpos)
    for hd in range(heads):
        o_ref[0, :, hd * MLA_V:(hd + 1) * MLA_V] = _softmax_finish(
            acc_ref.at[hd * tq:(hd + 1) * tq]).astype(o_ref.dtype)


def _mla(q, k, v):
    bsz, heads, seq, dk = q.shape
    tq = ATT_TQ
    assert tq == ATT_TK
    return pl.pallas_call(
        _mla_kernel,
        out_shape=jax.ShapeDtypeStruct((bsz, seq, heads * MLA_V), BF16),
        grid=(bsz, seq // tq),
        in_specs=[pl.BlockSpec((1, heads, tq, dk), lambda b, qi: (b, 0, qi, 0)),
                  pl.BlockSpec((1, heads, seq, dk), lambda b, qi: (b, 0, 0, 0)),
                  pl.BlockSpec((1, heads, seq, MLA_V), lambda b, qi: (b, 0, 0, 0))],
        out_specs=pl.BlockSpec((1, tq, heads * MLA_V), lambda b, qi: (b, qi, 0)),
        scratch_shapes=[pltpu.VMEM((heads * tq, LANES), F32), pltpu.VMEM((heads * tq, 2 * LANES), F32),
                        pltpu.VMEM((heads * tq, ATT_TK), F32), pltpu.VMEM((heads * tq, ATT_TK), F32)],
        compiler_params=_params(("parallel", "arbitrary")),
        name="mla_attention",
    )(q, k, v)


def _ev_out_kernel(ya_ref, o_ref, zb_ref, x_ref, mod_ref, g_ref, wa_ref, wb_ref, out_ref):
    d = x_ref.shape[-1]
    yb = (o_ref[0].astype(F32) * _silu(zb_ref[0].astype(F32))).astype(BF16)
    y = _dot(ya_ref[0], wa_ref[...]) + _dot(yb, wb_ref[...])
    gate = mod_ref[0][:, 2 * d:]
    out_ref[0] = x_ref[0] + gate * _rms(y, g_ref[...])


def _ev_out(y_a, o_mla, z_b, x, mod, post_g, w_out):
    bsz, seq, d = x.shape
    tm = TOKEN_TILE
    wa = w_out[:y_a.shape[-1]].astype(BF16)
    wb = w_out[y_a.shape[-1]:].astype(BF16)
    g = post_g[None]
    tok = lambda n: pl.BlockSpec((1, tm, n), lambda b, i: (b, i, 0))
    full = lambda a: pl.BlockSpec(a.shape, lambda b, i: (0,) * a.ndim)
    return pl.pallas_call(
        _ev_out_kernel,
        out_shape=jax.ShapeDtypeStruct(x.shape, F32),
        grid=(bsz, seq // tm),
        in_specs=[tok(y_a.shape[-1]), tok(o_mla.shape[-1]), tok(z_b.shape[-1]), tok(d),
                  pl.BlockSpec((1, 1, 3 * d), lambda b, i: (b, 0, 0)), full(g), full(wa), full(wb)],
        out_specs=tok(d),
        compiler_params=_params(("parallel", "parallel")),
        name="even_out_proj",
    )(y_a, o_mla, z_b, x, mod[:, None, :], g, wa, wb)


OD_Q, OD_KV, OD_Z = 1024, 256, 1024
OD_NGATE = 3 * NSA_HEADS
OD_GATE = LANES
OD_OFF = np.cumsum([0, OD_Q] + [OD_KV] * 6 + [OD_NGATE, OD_Z])
OD_COLS = -(-OD_OFF[-1] // LANES) * LANES


def _od_in_kernel(x_ref, mod_ref, g_ref, w_ref, cn_ref, sn_ref,
                  q_ref, kc_ref, vc_ref, ks_ref, vs_ref, kw_ref, vw_ref, gate_ref, z_ref):
    h = _modulated_input(x_ref, mod_ref, g_ref)
    p = _dot(h, w_ref[...])
    o = OD_OFF
    cos, sin = cn_ref[0], sn_ref[0]

    def rope(t):
        return t * cos + pltpu.roll(t, NSA_DIM // 2, axis=1) * sin

    scale = NSA_DIM ** -0.5 * LOG2E
    for hd in range(NSA_HEADS):
        t = p[:, hd * NSA_DIM:(hd + 1) * NSA_DIM]
        q_ref[0, hd // NSA_REP, hd % NSA_REP] = (rope(t) * scale).astype(BF16)
    plain = (kc_ref, vc_ref, None, vs_ref, None, vw_ref)
    roped = (None, None, ks_ref, None, kw_ref, None)
    for j in range(6):
        for g in range(NSA_GROUPS):
            lo = o[1 + j] + g * NSA_DIM
            t = p[:, lo:lo + NSA_DIM]
            if plain[j] is not None:
                plain[j][0, g] = t.astype(plain[j].dtype)
            else:
                roped[j][0, g] = rope(t).astype(BF16)
    lane = lax.broadcasted_iota(jnp.int32, cos.shape, 1)
    gate_ref[0] = jnp.where(lane < OD_NGATE, jax.nn.sigmoid(p[:, o[7]:o[7] + LANES]), 0.0)
    z_ref[0] = p[:, o[8]:o[9]].astype(BF16)


def _od_in(x, mod, pre_g, w_in, cos_n, sin_n):
    bsz, seq, d = x.shape
    tm = TOKEN_TILE
    w = jnp.pad(w_in, ((0, 0), (0, OD_COLS - w_in.shape[1]))).astype(BF16)
    g = pre_g[None]
    tok = lambda n: pl.BlockSpec((1, tm, n), lambda b, i: (b, i, 0))
    full = lambda a: pl.BlockSpec(a.shape, lambda b, i: (0,) * a.ndim)
    q_spec = pl.BlockSpec((1, NSA_GROUPS, NSA_REP, tm, NSA_DIM), lambda b, i: (b, 0, 0, i, 0))
    kv_spec = pl.BlockSpec((1, NSA_GROUPS, tm, NSA_DIM), lambda b, i: (b, 0, i, 0))
    kv_shape = jax.ShapeDtypeStruct((bsz, NSA_GROUPS, seq, NSA_DIM), BF16)
    cmp_shape = jax.ShapeDtypeStruct(kv_shape.shape, F32)
    return pl.pallas_call(
        _od_in_kernel,
        out_shape=(jax.ShapeDtypeStruct((bsz, NSA_GROUPS, NSA_REP, seq, NSA_DIM), BF16),)
        + (cmp_shape,) * 2 + (kv_shape,) * 4
        + (jax.ShapeDtypeStruct((bsz, seq, OD_GATE), F32), jax.ShapeDtypeStruct((bsz, seq, OD_Z), BF16)),
        grid=(bsz, seq // tm),
        in_specs=[tok(d), pl.BlockSpec((1, 1, 3 * d), lambda b, i: (b, 0, 0)), full(g), full(w),
                  tok(LANES), tok(LANES)],
        out_specs=(q_spec,) + (kv_spec,) * 6 + (tok(OD_GATE), tok(OD_Z)),
        compiler_params=_params(("parallel", "parallel")),
        name="odd_in_proj",
    )(x, mod[:, None, :], g, w, cos_n, sin_n)


def _compress_kernel(x_ref, pe_ref, w1_ref, w2_ref, cos_ref, sin_ref, o_ref, *, use_rope):
    seq, d = x_ref.shape[2:]
    nb = seq // CMP_STRIDE
    lo = _dot(pe_ref[...], w1_ref[...])[0:1]
    hi = jnp.zeros((nb, w1_ref.shape[1]), F32)
    for l in range(CMP_STRIDE):
        xl = x_ref[0, 0, pl.ds(l, nb, stride=CMP_STRIDE), :].astype(BF16)
        lo = lo + _dot(xl, w1_ref[l * d:(l + 1) * d])
        hi = hi + _dot(xl, w1_ref[(CMP_STRIDE + l) * d:(CMP_STRIDE + l + 1) * d])
    row = lax.broadcasted_iota(jnp.int32, hi.shape, 0)
    pre = lo + jnp.where(row < nb - 1, pltpu.roll(hi, nb - 1, axis=0), 0.0)
    out = _dot(_gelu_tanh(pre).astype(BF16), w2_ref[...])
    if use_rope:
        out = out * cos_ref[0] + pltpu.roll(out, NSA_DIM // 2, axis=1) * sin_ref[0]
    o_ref[0, 0] = out.astype(BF16)


def _compress(kv, pe, w1, w2, cos_end, sin_end, use_rope):
    bsz, groups, seq, d = kv.shape
    nb = seq // CMP_STRIDE
    pe_rows = jnp.broadcast_to(pe.reshape(1, CMP_LEN * d), (SUBLANES, CMP_LEN * d)).astype(BF16)
    w1b, w2b = w1.astype(BF16), w2.astype(BF16)
    full = lambda a: pl.BlockSpec(a.shape, lambda b, g: (0,) * a.ndim)
    end = pl.BlockSpec((1, nb, d), lambda b, g: (b, 0, 0))
    return pl.pallas_call(
        functools.partial(_compress_kernel, use_rope=use_rope),
        out_shape=jax.ShapeDtypeStruct((bsz, groups, nb, d), BF16),
        grid=(bsz, groups),
        in_specs=[pl.BlockSpec((1, 1, seq, d), lambda b, g: (b, g, 0, 0)),
                  full(pe_rows), full(w1b), full(w2b), end, end],
        out_specs=pl.BlockSpec((1, 1, nb, d), lambda b, g: (b, g, 0, 0)),
        compiler_params=_params(("parallel", "parallel")),
        name="nsa_compress",
    )(kv, pe_rows, w1b, w2b, cos_end, sin_end)


def _cmp_sel_kernel(q_ref, kc_ref, vc_ref, pool_ref, o_ref, sel_ref, *, n_tiles, **static):
    for k in range(n_tiles):
        @pl.when(pl.program_id(1) == k)
        def _(k=k):
            for g in range(q_ref.shape[1]):
                _cmp_sel_group(k, g, q_ref, kc_ref, vc_ref, pool_ref, o_ref, sel_ref, **static)


def _cmp_sel_group(qi, g, q_ref, kc_ref, vc_ref, pool_ref, o_ref, sel_ref, *, n_cmp, n_sel, n_top):
    rep, tq, d = q_ref.shape[2:]
    last_q = qi * tq + tq - 1
    n_vis = min(n_cmp, max(last_q - (CMP_LEN - 1), 0) // CMP_STRIDE + 1)
    nb = min(kc_ref.shape[2], -(-n_vis // LANES) * LANES)
    n_blk = min(n_sel, last_q // SEL_BLOCK + 1)
    n_sel = min(n_sel, -(-n_blk // SUBLANES) * SUBLANES)
    q = q_ref[0, g].reshape(rep * tq, d)
    s = _dot_nt(q, kc_ref[0, g, :nb, :]).reshape(rep, tq, nb)
    qpos = qi * tq + lax.broadcasted_iota(jnp.int32, (tq, nb), 0)
    blk = lax.broadcasted_iota(jnp.int32, (tq, nb), 1)
    mask = ((blk * CMP_STRIDE + (CMP_LEN - 1) <= qpos) & (blk < n_cmp))[None]
    s = jnp.where(mask, s, NEG_INF)
    m = jnp.max(s, axis=-1, keepdims=True)
    e = jnp.where(mask, jnp.exp2(s - m), 0.0)
    p = e / jnp.maximum(jnp.sum(e, axis=-1, keepdims=True), TINY)
    o = _dot(p.reshape(rep * tq, nb).astype(BF16), vc_ref[0, g, :nb, :])
    o_ref[0, g] = o.reshape(rep, tq, d).astype(o_ref.dtype)

    rows = pool_ref.shape[0]
    imp = _dot_nt(pool_ref[:, :nb], jnp.sum(p, axis=0), precision=HIGHEST)[:n_sel]
    bid = lax.broadcasted_iota(jnp.int32, (n_sel, tq), 0)
    cur = (qi * tq + lax.broadcasted_iota(jnp.int32, (n_sel, tq), 1)) // SEL_BLOCK
    forced = (bid == 0) | (bid == cur) | (bid == cur - 1)
    imp = jnp.where(forced, FORCE_SCORE, jnp.where(bid <= cur, imp, -1.0))
    groups = [imp[g:g + SUBLANES] for g in range(0, n_sel, SUBLANES)]
    sub = lax.broadcasted_iota(jnp.int32, (SUBLANES, tq), 0)
    ranks = [jnp.zeros((SUBLANES, tq), F32) for _ in groups]
    for j in range(n_blk):
        vj = jnp.broadcast_to(imp[j:j + 1, :], (SUBLANES, tq))
        for gi, grp in enumerate(groups):
            lo = gi * SUBLANES
            if lo > j:
                first = vj >= grp
            elif lo + SUBLANES - 1 <= j:
                first = vj > grp
            else:
                first = (vj > grp) | ((vj == grp) & (sub > j - lo))
            ranks[gi] = ranks[gi] + jnp.where(first, 1.0, 0.0)
    rank = jnp.concatenate(ranks, axis=0)
    bias = jnp.where((rank < n_top) & (bid <= cur), 0.0, NEG_INF)
    bias = jnp.concatenate([bias, jnp.full((rows - n_sel, tq), NEG_INF, F32)], axis=0)
    sel_ref[0, g] = bias.T.astype(sel_ref.dtype)


def _cmp_sel(q, kc, vc):
    bsz, groups, rep, seq, d = q.shape
    nb = kc.shape[2]
    tq = CMP_TQ
    n_cmp = (seq - CMP_LEN) // CMP_STRIDE + 1
    n_sel = seq // SEL_BLOCK
    n_top = min(SEL_TOP, n_sel)
    ratio = SEL_BLOCK // CMP_STRIDE
    assert n_sel <= LANES and n_sel * ratio == nb
    pool_np = np.zeros((LANES, nb), np.float32)
    pool_np[np.arange(nb) // ratio, np.arange(nb)] = 1.0
    pool = jnp.asarray(pool_np)
    kv_spec = pl.BlockSpec((1, groups, nb, d), lambda b, i: (b, 0, 0, 0))
    q_spec = pl.BlockSpec((1, groups, rep, tq, d), lambda b, i: (b, 0, 0, i, 0))
    return pl.pallas_call(
        functools.partial(_cmp_sel_kernel, n_tiles=seq // tq, n_cmp=n_cmp, n_sel=n_sel, n_top=n_top),
        out_shape=(jax.ShapeDtypeStruct(q.shape, BF16),
                   jax.ShapeDtypeStruct((bsz, groups, seq, LANES), BF16)),
        grid=(bsz, seq // tq),
        in_specs=[q_spec, kv_spec, kv_spec, pl.BlockSpec(pool.shape, lambda b, i: (0, 0))],
        out_specs=(q_spec, pl.BlockSpec((1, groups, tq, LANES), lambda b, i: (b, 0, i, 0))),
        compiler_params=_params(("parallel", "parallel")),
        name="nsa_cmp_select",
    )(q, kc, vc, pool)


def _sel_win_kernel(q_ref, bias_ref, k_ref, blk_ref, v_ref, kw_ref, vw_ref, o_ref, ow_ref,
                    m_ref, acc_ref, sa_ref, sb_ref, *, span):
    qi = pl.program_id(2)
    rep, tq, d = q_ref.shape[2:]
    tk = NSA_TK
    part = rep // SEL_CHAINS
    q_ext = [jnp.concatenate([q_ref[0, 0, c * part:(c + 1) * part].reshape(part * tq, d),
                              jnp.concatenate([bias_ref[0, 0]] * part, axis=0)], axis=1)
             for c in range(SEL_CHAINS)]
    _softmax_init(m_ref, acc_ref)
    n_full = qi * tq // tk

    owns = [slice(c * part * tq, (c + 1) * part * tq) for c in range(SEL_CHAINS)]

    def scores(ki, s_ref):
        rows = pl.ds(pl.multiple_of(ki * tk, tk), tk)
        k_ext = jnp.concatenate([k_ref[0, 0, rows, :], blk_ref[rows, :]], axis=1)
        for c, own in enumerate(owns):
            s_ref[own] = _dot_nt(q_ext[c], k_ext)

    def update(s_ref, ki, mask):
        def run(width):
            rows = pl.ds(pl.multiple_of(ki * tk, tk), width)
            v_ext = _with_ones(v_ref[0, 0, rows, :])
            sub = None if mask is None else mask[:, :, :width]
            for own in owns:
                _softmax_tile(s_ref[own, :width], sub, v_ext, m_ref.at[own], acc_ref.at[own])

        if mask is None:
            run(tk)
        else:
            short = qi * tq + tq <= ki * tk + tk // 2
            pl.when(short)(lambda: run(tk // 2))
            pl.when(jnp.logical_not(short))(lambda: run(tk))

    def window():
        start = pl.multiple_of(jnp.maximum(qi * tq + tq - span, 0), tq)
        rows = pl.ds(start, span)
        wq = qi * tq + lax.broadcasted_iota(jnp.int32, (1, tq, span), 1)
        wk = start + lax.broadcasted_iota(jnp.int32, (1, tq, span), 2)
        mask = (wq - wk >= 0) & (wq - wk < WINDOW)
        k, v_ext = kw_ref[0, 0, rows, :], _with_ones(vw_ref[0, 0, rows, :])
        wpart = rep // WIN_CHAINS
        for h0 in range(0, rep, wpart):
            q = q_ref[0, 0, h0:h0 + wpart].reshape(wpart * tq, d)
            s = jnp.where(mask, _dot_nt(q, k).reshape(wpart, tq, span), NEG_INF)
            m = jnp.max(s, axis=-1, keepdims=True)
            e = jnp.where(mask, jnp.exp2(s - m), 0.0).reshape(wpart * tq, span)
            acc = _dot(e.astype(BF16), v_ext)
            o = acc[:, :LANES] / jnp.maximum(acc[:, LANES:], TINY)
            ow_ref[0, 0, h0:h0 + wpart] = o.reshape(wpart, tq, d).astype(ow_ref.dtype)

    qpos = qi * tq + lax.broadcasted_iota(jnp.int32, (1, tq, tk), 1)
    kpos = n_full * tk + lax.broadcasted_iota(jnp.int32, (1, tq, tk), 2)
    _pipelined_causal_tiles(n_full, scores, update, sa_ref, sb_ref, kpos <= qpos, filler=window)
    o_ref[0, 0] = _softmax_finish(acc_ref).reshape(rep, tq, d).astype(o_ref.dtype)


def _sel_win_attention(q, bias, k, v, kw, vw):
    bsz, groups, rep, seq, d = q.shape
    tq = NSA_TQ
    assert NSA_TK % tq == 0
    span = (-(-(WINDOW - 1) // tq) + 1) * tq
    assert span <= seq
    onehot_np = np.zeros((seq, LANES), np.float32)
    onehot_np[np.arange(seq), np.arange(seq) // SEL_BLOCK] = 1.0
    onehot = jnp.asarray(onehot_np, BF16)
    q_spec = pl.BlockSpec((1, 1, rep, tq, d), lambda b, g, qi: (b, g, 0, qi, 0))
    kv_spec = pl.BlockSpec((1, 1, seq, d), lambda b, g, qi: (b, g, 0, 0))
    out = jax.ShapeDtypeStruct(q.shape, BF16)
    return pl.pallas_call(
        functools.partial(_sel_win_kernel, span=span),
        out_shape=(out, out),
        grid=(bsz, groups, seq // tq),
        in_specs=[q_spec, pl.BlockSpec((1, 1, tq, LANES), lambda b, g, qi: (b, g, qi, 0)),
                  kv_spec, pl.BlockSpec((seq, LANES), lambda b, g, qi: (0, 0)), kv_spec, kv_spec, kv_spec],
        out_specs=(q_spec, q_spec),
        scratch_shapes=[pltpu.VMEM((rep * tq, LANES), F32), pltpu.VMEM((rep * tq, 2 * LANES), F32),
                        pltpu.VMEM((rep * tq, NSA_TK), F32), pltpu.VMEM((rep * tq, NSA_TK), F32)],
        compiler_params=_params(("parallel", "parallel", "arbitrary")),
        name="nsa_selected_window_attention",
    )(q, bias, k, onehot, v, kw, vw)


def _od_out_kernel(oc_ref, os_ref, ow_ref, gate_ref, spread_ref, z_ref, x_ref, mod_ref, g_ref, w_ref, out_ref):
    d = x_ref.shape[-1]
    gates = gate_ref[0]
    hi = gates.astype(BF16)
    lo = (gates - hi.astype(F32)).astype(BF16)
    wide = _dot(jnp.concatenate([hi, lo], axis=-1), spread_ref[...])
    z = z_ref[0].astype(F32)
    parts = []
    for hd in range(NSA_HEADS):
        g, r = hd // NSA_REP, hd % NSA_REP
        c = 3 * hd * NSA_DIM
        o = (wide[:, c:c + NSA_DIM] * oc_ref[0, g, r].astype(F32)
             + wide[:, c + NSA_DIM:c + 2 * NSA_DIM] * os_ref[0, g, r].astype(F32)
             + wide[:, c + 2 * NSA_DIM:c + 3 * NSA_DIM] * ow_ref[0, g, r].astype(F32))
        parts.append((o * _silu(z[:, hd * NSA_DIM:(hd + 1) * NSA_DIM])).astype(BF16))
    y = _dot(jnp.concatenate(parts, axis=-1), w_ref[...])
    gate = mod_ref[0][:, 2 * d:]
    out_ref[0] = x_ref[0] + gate * _rms(y, g_ref[...])


def _od_out(o_cmp, o_sel, o_win, gates, z, x, mod, post_g, w_out):
    bsz, seq, d = x.shape
    tm = TOKEN_TILE
    w = w_out.astype(BF16)
    g = post_g[None]
    n_gate = 3 * NSA_HEADS
    spread_np = np.zeros((2 * OD_GATE, n_gate * NSA_DIM), np.float32)
    for c in range(n_gate):
        spread_np[[c, OD_GATE + c], c * NSA_DIM:(c + 1) * NSA_DIM] = 1.0
    spread = jnp.asarray(spread_np, BF16)
    tok = lambda n: pl.BlockSpec((1, tm, n), lambda b, i: (b, i, 0))
    full = lambda a: pl.BlockSpec(a.shape, lambda b, i: (0,) * a.ndim)
    o_spec = pl.BlockSpec((1, NSA_GROUPS, NSA_REP, tm, NSA_DIM), lambda b, i: (b, 0, 0, i, 0))
    return pl.pallas_call(
        _od_out_kernel,
        out_shape=jax.ShapeDtypeStruct(x.shape, F32),
        grid=(bsz, seq // tm),
        in_specs=[o_spec, o_spec, o_spec, tok(OD_GATE), full(spread), tok(OD_Z), tok(d),
                  pl.BlockSpec((1, 1, 3 * d), lambda b, i: (b, 0, 0)), full(g), full(w)],
        out_specs=tok(d),
        compiler_params=_params(("parallel", "parallel")),
        name="odd_out_proj",
    )(o_cmp, o_sel, o_win, gates, spread, z, x, mod[:, None, :], g, w)


def _even_layer(x, mod, pre_g, post_g, tabs, w_in, lam_re, lam_im, log_dt, b_re, b_im, c_re, c_im,
                d_skip, w_glu, b_glu, q_norm_g, kv_norm_g, w_uq, w_ukv, w_out):
    cos_m, sin_m = tabs[0], tabs[1]
    u, z_a, z_b, q, k, v = _ev_in(x, mod, pre_g, w_in, q_norm_g, kv_norm_g, w_uq, w_ukv, cos_m, sin_m)
    y_a = _s5(u, z_a, lam_re, lam_im, log_dt, b_re, b_im, c_re, c_im, d_skip, w_glu, b_glu)
    o_mla = _mla(q, k, v)
    return _ev_out(y_a, o_mla, z_b, x, mod, post_g, w_out)


def _odd_layer(x, mod, pre_g, post_g, tabs, end_tabs, w_in, k_pe, k_w1, k_w2, v_pe, v_w1, v_w2, w_out):
    cos_n, sin_n = tabs[2], tabs[3]
    cos_end, sin_end = end_tabs[2], end_tabs[3]
    q, k_c, v_c, k_s, v_s, k_w, v_w, gates, z = _od_in(x, mod, pre_g, w_in, cos_n, sin_n)
    kc = _compress(k_c, k_pe, k_w1, k_w2, cos_end, sin_end, True)
    vc = _compress(v_c, v_pe, v_w1, v_w2, cos_end, sin_end, False)
    o_cmp, sel = _cmp_sel(q, kc, vc)
    o_sel, o_win = _sel_win_attention(q, sel, k_s, v_s, k_w, v_w)
    return _od_out(o_cmp, o_sel, o_win, gates, z, x, mod, post_g, w_out)


def kernel(x, c, positions, pre_norm_g, post_norm_g, w_ada, b_ada, ev_w_in, ev_lam_re, ev_lam_im, ev_log_dt, ev_b_re, ev_b_im, ev_c_re, ev_c_im, ev_d_skip, ev_w_glu, ev_b_glu, ev_q_norm_g, ev_kv_norm_g, ev_w_uq, ev_w_ukv, ev_w_out, od_w_in, od_cmp_k_pe, od_cmp_k_w1, od_cmp_k_w2, od_cmp_v_pe, od_cmp_v_w1, od_cmp_v_w2, od_w_out):
    depth = pre_norm_g.shape[0]
    tabs = _rope_tables(positions)
    seq = positions.shape[1]
    pos_end = positions[:, CMP_LEN - 1::CMP_STRIDE]
    pos_end = jnp.pad(pos_end, ((0, 0), (0, seq // CMP_STRIDE - pos_end.shape[1])))
    end_tabs = _rope_tables(pos_end)
    mods = _modulation(c, w_ada, b_ada)
    for layer in range(depth):
        i = layer // 2
        if layer % 2 == 0:
            x = _even_layer(x, mods[layer], pre_norm_g[layer], post_norm_g[layer], tabs,
                            ev_w_in[i], ev_lam_re[i], ev_lam_im[i], ev_log_dt[i], ev_b_re[i], ev_b_im[i],
                            ev_c_re[i], ev_c_im[i], ev_d_skip[i], ev_w_glu[i], ev_b_glu[i],
                            ev_q_norm_g[i], ev_kv_norm_g[i], ev_w_uq[i], ev_w_ukv[i], ev_w_out[i])
        else:
            x = _odd_layer(x, mods[layer], pre_norm_g[layer], post_norm_g[layer], tabs, end_tabs,
                           od_w_in[i], od_cmp_k_pe[i], od_cmp_k_w1[i], od_cmp_k_w2[i],
                           od_cmp_v_pe[i], od_cmp_v_w1[i], od_cmp_v_w2[i], od_w_out[i])
    return x
```

```python
import functools
import math

import numpy as np
import jax
import jax.numpy as jnp
from jax import lax
from jax.experimental import pallas as pl
from jax.experimental.pallas import tpu as pltpu

F32 = jnp.float32
BF16 = jnp.bfloat16
HIGHEST = lax.Precision.HIGHEST

EPS = 1e-6
ROPE_THETA = 10000.0
NEG_INF = -1e30
TINY = 1e-30
FORCE_SCORE = 1e9
LOG2E = 1.0 / math.log(2.0)

S5_GROUP = 16
S5_STATE = 64
MLA_HEADS = 4
MLA_NOPE = 128
MLA_ROPE = 64
MLA_V = 128
NSA_HEADS = 8
NSA_GROUPS = 2
NSA_REP = NSA_HEADS // NSA_GROUPS
NSA_DIM = 128
CMP_LEN = 32
CMP_STRIDE = 16
CMP_HIDDEN = 256
SEL_BLOCK = 64
SEL_TOP = 16
WINDOW = 512

LANES = 128
SUBLANES = 8
VMEM_LIMIT = 48 * 1024 * 1024

TOKEN_TILE = 1024
S5_CHUNK = 512
S5_COLS = 512
ATT_TQ = 512
ATT_TK = 512
NSA_TQ = 256
NSA_TK = 512
CMP_TQ = 1024
SEL_CHAINS = 4
WIN_CHAINS = 4


def _params(sem):
    return pltpu.CompilerParams(dimension_semantics=sem, vmem_limit_bytes=VMEM_LIMIT)


def _dot(a, b, precision=None):
    return jnp.dot(a, b, preferred_element_type=F32, precision=precision)


def _dot_nt(a, b, precision=None):
    return lax.dot_general(a, b, (((1,), (1,)), ((), ())), preferred_element_type=F32,
                           precision=precision)


def _silu(v):
    return v * jax.nn.sigmoid(v)


def _gelu_tanh(v):
    return 0.5 * v * (1.0 + jnp.tanh(math.sqrt(2.0 / math.pi) * (v + 0.044715 * (v * v * v))))


def _rms(v, g):
    return v * lax.rsqrt(jnp.mean(v * v, axis=-1, keepdims=True) + EPS) * g


def _rope_tab_kernel(pos_ref, f_ref, sgm_ref, sgn_ref, cm_ref, sm_ref, cn_ref, sn_ref):
    hn, hm = NSA_DIM // 2, MLA_ROPE // 2
    ang = pos_ref[0] * f_ref[...]
    c, s = jnp.cos(ang), jnp.sin(ang)
    cn_ref[0] = jnp.concatenate([c[:, :hn]] * (LANES // hn), axis=1)
    sn_ref[0] = jnp.concatenate([s[:, :hn]] * (LANES // hn), axis=1) * sgn_ref[...]
    cm_ref[0] = jnp.concatenate([c[:, hn:hn + hm]] * (LANES // hm), axis=1)
    sm_ref[0] = jnp.concatenate([s[:, hn:hn + hm]] * (LANES // hm), axis=1) * sgm_ref[...]


def _rope_tables(positions):
    bsz, seq = positions.shape
    ts = min(TOKEN_TILE, seq)
    pos = positions.astype(F32)[..., None]
    half_m, half_n = MLA_ROPE // 2, NSA_DIM // 2
    fm = ROPE_THETA ** (-jnp.arange(half_m, dtype=F32) / half_m)
    fn = ROPE_THETA ** (-jnp.arange(half_n, dtype=F32) / half_n)
    freqs = jnp.concatenate([fn, fm, jnp.zeros((LANES - half_n - half_m,), F32)])[None]
    sgm = jnp.concatenate([-jnp.ones((half_m,), F32), jnp.ones((half_m,), F32),
                           jnp.zeros((LANES - MLA_ROPE,), F32)])[None]
    sgn = jnp.concatenate([-jnp.ones((half_n,), F32), jnp.ones((half_n,), F32)])[None]
    row = pl.BlockSpec((1, LANES), lambda b, i: (0, 0))
    tab = pl.BlockSpec((1, ts, LANES), lambda b, i: (b, i, 0))
    shp = jax.ShapeDtypeStruct((bsz, seq, LANES), F32)
    return pl.pallas_call(
        _rope_tab_kernel,
        out_shape=(shp, shp, shp, shp),
        grid=(bsz, seq // ts),
        in_specs=[pl.BlockSpec((1, ts, 1), lambda b, i: (b, i, 0)), row, row, row],
        out_specs=(tab, tab, tab, tab),
        compiler_params=_params(("parallel", "parallel")),
        name="rope_tables",
    )(pos, freqs, sgm, sgn)


def _mod_kernel(c_ref, w_ref, b_ref, o_ref):
    s = _silu(c_ref[...])
    o_ref[0] = _dot(s, w_ref[0], precision=HIGHEST) + b_ref[0]


def _modulation(c, w_ada, b_ada):
    depth, d, _ = w_ada.shape
    bsz = c.shape[0]
    rows = -(-bsz // SUBLANES) * SUBLANES
    c_pad = jnp.pad(c, ((0, rows - bsz), (0, 0)))
    out = pl.pallas_call(
        _mod_kernel,
        out_shape=jax.ShapeDtypeStruct((depth, rows, 3 * d), F32),
        grid=(depth, 3),
        in_specs=[pl.BlockSpec((rows, d), lambda l, j: (0, 0)),
                  pl.BlockSpec((1, d, d), lambda l, j: (l, 0, j)),
                  pl.BlockSpec((1, 1, d), lambda l, j: (l, 0, j))],
        out_specs=pl.BlockSpec((1, rows, d), lambda l, j: (l, 0, j)),
        compiler_params=_params(("parallel", "parallel")),
        name="modulation",
    )(c_pad, w_ada, b_ada[:, None, :])
    return out[:, :bsz]


def _modulated_input(x_ref, mod_ref, g_ref):
    d = x_ref.shape[-1]
    x = x_ref[0]
    mod = mod_ref[0]
    shift, scale = mod[:, :d], mod[:, d:2 * d]
    return (_rms(x, g_ref[...]) * (1.0 + scale) + shift).astype(BF16)


EV_U, EV_ZA, EV_CQ, EV_CKV, EV_KPE, EV_ZB = 512, 512, 768, 256, MLA_ROPE, 512
EV_OFF = np.cumsum([0, EV_U, EV_ZA, EV_CQ, EV_CKV, EV_KPE, EV_ZB])
EV_COLS = -(-EV_OFF[-1] // LANES) * LANES


def _ev_in_kernel(x_ref, mod_ref, g_ref, w_ref, gq_ref, gkv_ref, wuq_ref, wukv_ref, cm_ref, sm_ref,
                  u_ref, za_ref, zb_ref, q_ref, k_ref, v_ref):
    h = _modulated_input(x_ref, mod_ref, g_ref)
    p = _dot(h, w_ref[...])
    o = EV_OFF
    u_ref[0] = p[:, o[0]:o[1]].astype(BF16)
    za_ref[0] = p[:, o[1]:o[2]].astype(BF16)
    zb_ref[0] = p[:, o[5]:o[6]].astype(BF16)
    cos, sin = cm_ref[0], sm_ref[0]
    hq = MLA_ROPE // 2
    lane = lax.broadcasted_iota(jnp.int32, cos.shape, 1)
    first = lane < hq

    def rope(t):
        swapped = jnp.where(first, pltpu.roll(t, LANES - hq, axis=1), pltpu.roll(t, hq, axis=1))
        return t * cos + swapped * sin

    cq = _rms(p[:, o[2]:o[3]], gq_ref[...]).astype(BF16)
    q = _dot(cq, wuq_ref[...]) * ((MLA_NOPE + MLA_ROPE) ** -0.5 * LOG2E)
    ckv = _rms(p[:, o[3]:o[4]], gkv_ref[...]).astype(BF16)
    kv = _dot(ckv, wukv_ref[...])
    kpe = rope(jnp.where(lane < MLA_ROPE, p[:, o[4]:o[4] + LANES], 0.0)).astype(BF16)
    nope = MLA_HEADS * MLA_NOPE
    for hd in range(MLA_HEADS):
        b0 = hd * (MLA_NOPE + LANES)
        qpe = rope(q[:, b0 + MLA_NOPE:b0 + MLA_NOPE + LANES])
        q_ref[0, hd] = jnp.concatenate([q[:, b0:b0 + MLA_NOPE], qpe], axis=-1).astype(BF16)
        k_ref[0, hd] = jnp.concatenate([kv[:, hd * MLA_NOPE:(hd + 1) * MLA_NOPE].astype(BF16), kpe], axis=-1)
        v_ref[0, hd] = kv[:, nope + hd * MLA_V:nope + (hd + 1) * MLA_V].astype(BF16)


def _ev_in(x, mod, pre_g, w_in, q_norm_g, kv_norm_g, w_uq, w_ukv, cos_m, sin_m):
    bsz, seq, d = x.shape
    tm = TOKEN_TILE
    w = jnp.pad(w_in, ((0, 0), (0, EV_COLS - w_in.shape[1]))).astype(BF16)
    rank = w_uq.shape[0]
    wuq = jnp.pad(w_uq.reshape(rank, MLA_HEADS, MLA_NOPE + MLA_ROPE), ((0, 0), (0, 0), (0, LANES - MLA_ROPE)))
    wuq = wuq.reshape(rank, MLA_HEADS * (MLA_NOPE + LANES)).astype(BF16)
    rank = w_ukv.shape[0]
    wukv = jnp.concatenate([w_ukv.reshape(rank, MLA_HEADS, MLA_NOPE + MLA_V)[:, :, :MLA_NOPE].reshape(rank, -1),
                            w_ukv.reshape(rank, MLA_HEADS, MLA_NOPE + MLA_V)[:, :, MLA_NOPE:].reshape(rank, -1)],
                           axis=1).astype(BF16)

    tok = lambda n: pl.BlockSpec((1, tm, n), lambda b, i: (b, i, 0))
    full = lambda a: pl.BlockSpec(a.shape, lambda b, i: (0,) * a.ndim)
    out = lambda n: jax.ShapeDtypeStruct((bsz, seq, n), BF16)
    head = lambda n: pl.BlockSpec((1, MLA_HEADS, tm, n), lambda b, i: (b, 0, i, 0))
    head_out = lambda n: jax.ShapeDtypeStruct((bsz, MLA_HEADS, seq, n), BF16)
    gq, gkv, g = q_norm_g[None], kv_norm_g[None], pre_g[None]
    return pl.pallas_call(
        _ev_in_kernel,
        out_shape=(out(512), out(512), out(512), head_out(2 * LANES), head_out(2 * LANES), head_out(MLA_V)),
        grid=(bsz, seq // tm),
        in_specs=[tok(d), pl.BlockSpec((1, 1, 3 * d), lambda b, i: (b, 0, 0)), full(g), full(w),
                  full(gq), full(gkv), full(wuq), full(wukv), tok(LANES), tok(LANES)],
        out_specs=(tok(512), tok(512), tok(512), head(2 * LANES), head(2 * LANES), head(MLA_V)),
        compiler_params=_params(("parallel", "parallel")),
        name="even_in_proj",
    )(x, mod[:, None, :], g, w, gq, gkv, wuq, wukv, cos_m, sin_m)


def _s5_kernel(u_ref, z_ref, perm_ref, permt_ref, wb_ref, wc_ref, a_ref, at_ref, pow_ref,
               d_ref, wglu_ref, bglu_ref, o_ref, bu_ref, xb_ref, state_ref, carry_ref):
    t_len = u_ref.shape[1]
    n = a_ref.shape[1] // 2
    steps = t_len // SUBLANES

    @pl.when(pl.program_id(1) == 0)
    def _():
        state_ref[...] = jnp.zeros_like(state_ref)

    perm = perm_ref[...]
    u_p = _dot(perm, u_ref[0])
    z_p = _dot(perm, z_ref[0])
    u_pb = u_p.astype(BF16)
    bre, bim = pl.ds(0, S5_COLS), pl.ds(S5_COLS, S5_COLS)
    y_parts = []

    for cb in range(n // S5_COLS):
        u_cb = u_pb[:, cb * LANES:(cb + 1) * LANES]
        bu_ref[...] = _dot(u_cb, wb_ref[cb])
        re = pl.ds(cb * S5_COLS, S5_COLS)
        im = pl.ds(n + cb * S5_COLS, S5_COLS)
        ar = jnp.broadcast_to(a_ref[:, re], (SUBLANES, S5_COLS))
        ai = jnp.broadcast_to(a_ref[:, im], (SUBLANES, S5_COLS))

        def step(t, carry):
            xr, xi = carry
            rows = pl.ds(pl.multiple_of(t * SUBLANES, SUBLANES), SUBLANES)
            nr = ar * xr - ai * xi + bu_ref[rows, bre]
            ni = ar * xi + ai * xr + bu_ref[rows, bim]
            bu_ref[rows, bre] = nr
            bu_ref[rows, bim] = ni
            return nr, ni

        zero = jnp.zeros((SUBLANES, S5_COLS), F32)
        er, ei = lax.fori_loop(0, steps, step, (zero, zero), unroll=True)

        sr, si = state_ref[:, re], state_ref[:, im]
        tr, ti = at_ref[:, re], at_ref[:, im]
        for j in range(SUBLANES):
            carry_ref[j:j + 1, re] = sr
            carry_ref[j:j + 1, im] = si
            sr, si = (tr * sr - ti * si + er[j:j + 1], tr * si + ti * sr + ei[j:j + 1])
        state_ref[:, re] = sr
        state_ref[:, im] = si
        cr, ci = carry_ref[:, re], carry_ref[:, im]

        def fix(t2, _):
            xr, xi = [], []
            for k in range(2):
                t = 2 * t2 + k
                rows = pl.ds(pl.multiple_of(t * SUBLANES, SUBLANES), SUBLANES)
                pr, pi = pow_ref[pl.ds(t, 1), re], pow_ref[pl.ds(t, 1), im]
                xr.append(bu_ref[rows, bre] + pr * cr - pi * ci)
                xi.append(bu_ref[rows, bim] + pr * ci + pi * cr)
            rows = pl.ds(pl.multiple_of(t2 * 2 * SUBLANES, 2 * SUBLANES), 2 * SUBLANES)
            xb_ref[rows, bre] = jnp.concatenate(xr, axis=0).astype(BF16)
            xb_ref[rows, bim] = jnp.concatenate(xi, axis=0).astype(BF16)
            return 0

        lax.fori_loop(0, steps // 2, fix, 0, unroll=True)
        y_parts.append(_dot(xb_ref[...], wc_ref[cb]))

    y = jnp.concatenate(y_parts, axis=1) + d_ref[...] * u_p
    g = _gelu_tanh(y)
    gate = jax.nn.sigmoid(_dot(g.astype(BF16), wglu_ref[...]) + bglu_ref[...])
    out = (g * gate * _silu(z_p)).astype(BF16)
    o_ref[0] = _dot(permt_ref[...], out).astype(BF16)


def _s5(u, z_a, lam_re, lam_im, log_dt, b_re, b_im, c_re, c_im, d_skip, w_glu, b_glu):
    bsz, seq, width = u.shape
    groups, state = lam_re.shape
    t_len = S5_CHUNK
    steps = t_len // SUBLANES
    n = groups * state
    dt = jnp.exp(log_dt)[:, None]
    lam_dt_re, lam_dt_im = lam_re * dt, lam_im * dt
    decay = jnp.exp(lam_dt_re)
    ab_re, ab_im = decay * jnp.cos(lam_dt_im), decay * jnp.sin(lam_dt_im)
    den = lam_re * lam_re + lam_im * lam_im
    nr, ni = ab_re - 1.0, ab_im
    f_re = (nr * lam_re + ni * lam_im) / den
    f_im = (ni * lam_re - nr * lam_im) / den
    bb_re = f_re[..., None] * b_re - f_im[..., None] * b_im
    bb_im = f_re[..., None] * b_im + f_im[..., None] * b_re
    gpb = S5_COLS // state
    nblk = groups // gpb
    hdim = width // groups
    assert gpb * hdim == LANES and nblk * S5_COLS == n
    eye = jnp.eye(gpb, dtype=F32)

    def in_blocks(bb):
        bb = bb.reshape(nblk, gpb, state, hdim)
        return jnp.einsum('mgph,gk->mghkp', bb, eye).reshape(nblk, gpb * hdim, gpb * state)

    def out_blocks(cc):
        cc = cc.reshape(nblk, gpb, hdim, state)
        return jnp.einsum('mghp,gk->mgpkh', cc, eye).reshape(nblk, gpb * state, gpb * hdim)

    wb = jnp.concatenate([in_blocks(bb_re), in_blocks(bb_im)], axis=2).astype(BF16)
    wc = jnp.concatenate([out_blocks(c_re), out_blocks(-c_im)], axis=1).astype(BF16)
    a_vec = jnp.concatenate([ab_re.reshape(1, n), ab_im.reshape(1, n)], axis=1)
    ks = jnp.arange(1, steps + 1, dtype=F32)[:, None, None]
    pw_mag = jnp.exp(lam_dt_re[None] * ks)
    pw_re, pw_im = pw_mag * jnp.cos(lam_dt_im[None] * ks), pw_mag * jnp.sin(lam_dt_im[None] * ks)
    pow_tab = jnp.concatenate([pw_re.reshape(steps, n), pw_im.reshape(steps, n)], axis=1)
    at_vec = pow_tab[steps - 1:steps]
    r = np.arange(t_len)
    perm_np = np.zeros((t_len, t_len), np.float32)
    perm_np[r, (r % SUBLANES) * steps + r // SUBLANES] = 1.0
    perm = jnp.asarray(perm_np, BF16)
    permt = jnp.asarray(perm_np.T, BF16)
    d_vec = d_skip.reshape(1, width)
    wglu = w_glu.astype(BF16)
    bglu = b_glu[None]

    tok = pl.BlockSpec((1, t_len, width), lambda b, i: (b, i, 0))
    full = lambda a: pl.BlockSpec(a.shape, lambda b, i: (0,) * a.ndim)
    consts = (perm, permt, wb, wc, a_vec, at_vec, pow_tab, d_vec, wglu, bglu)
    return pl.pallas_call(
        _s5_kernel,
        out_shape=jax.ShapeDtypeStruct((bsz, seq, width), BF16),
        grid=(bsz, seq // t_len),
        in_specs=[tok, tok] + [full(a) for a in consts],
        out_specs=tok,
        scratch_shapes=[pltpu.VMEM((t_len, 2 * S5_COLS), F32), pltpu.VMEM((t_len, 2 * S5_COLS), BF16),
                        pltpu.VMEM((1, 2 * n), F32), pltpu.VMEM((SUBLANES, 2 * n), F32)],
        compiler_params=_params(("parallel", "arbitrary")),
        name="s5_mixer",
    )(u, z_a, *consts)


def _softmax_init(m_ref, acc_ref):
    m_ref[...] = jnp.full_like(m_ref, NEG_INF)
    acc_ref[...] = jnp.zeros_like(acc_ref)


def _softmax_tile(s, mask, v_ext, m_ref, acc_ref):
    rows, tk = s.shape
    if mask is not None:
        heads = rows // mask.shape[1]
        s = jnp.where(mask, s.reshape(heads, *mask.shape[1:]), NEG_INF).reshape(rows, tk)
    m_old = m_ref[...]
    m_new = jnp.maximum(m_old, jnp.max(s, axis=-1, keepdims=True))
    e = jnp.exp2(s - jnp.concatenate([m_new] * (tk // LANES), axis=1))
    if mask is not None:
        e = jnp.where(mask, e.reshape(heads, *mask.shape[1:]), 0.0).reshape(rows, tk)
    alpha = jnp.exp2(m_old - m_new)
    acc_ref[...] = (jnp.concatenate([alpha, alpha], axis=1) * acc_ref[...]
                    + _dot(e.astype(BF16), v_ext))
    m_ref[...] = m_new


def _softmax_finish(acc_ref):
    acc = acc_ref[...]
    return acc[:, :LANES] / jnp.maximum(acc[:, LANES:], TINY)


def _with_ones(v):
    return jnp.concatenate([v, jnp.ones(v.shape, v.dtype)], axis=1)


def _pipelined_causal_tiles(n_full, scores, update, sa_ref, sb_ref, causal, filler=None):
    def pair(i, carry):
        scores(2 * i + 1, sb_ref)
        update(sa_ref, 2 * i, None)
        scores(2 * i + 2, sa_ref)
        update(sb_ref, 2 * i + 1, None)
        return carry

    scores(0, sa_ref)
    if filler is not None:
        filler()
    lax.fori_loop(0, n_full // 2, pair, 0)

    @pl.when(n_full % 2 == 1)
    def _():
        scores(n_full, sb_ref)
        update(sa_ref, n_full - 1, None)
        update(sb_ref, n_full, causal)

    @pl.when(n_full % 2 == 0)
    def _():
        update(sa_ref, n_full, causal)


def _mla_kernel(q_ref, k_ref, v_ref, o_ref, m_ref, acc_ref, sa_ref, sb_ref):
    qi = pl.program_id(1)
    heads, tq = q_ref.shape[1:3]
    tk = ATT_TK
    _softmax_init(m_ref, acc_ref)
    n_full = qi * tq // tk
    owns = [slice(hd * tq, (hd + 1) * tq) for hd in range(heads)]

    def scores(ki, s_ref):
        rows = pl.ds(pl.multiple_of(ki * tk, tk), tk)
        for hd, own in enumerate(owns):
            s_ref[own] = _dot_nt(q_ref[0, hd], k_ref[0, hd, rows, :])

    def update(s_ref, ki, mask):
        rows = pl.ds(pl.multiple_of(ki * tk, tk), tk)
        for hd, own in enumerate(owns):
            v_ext = _with_ones(v_ref[0, hd, rows, :])
            if mask is None:
                _softmax_tile(s_ref[own], None, v_ext, m_ref.at[own], acc_ref.at[own])
            else:
                half = tq // 2
                top = slice(own.start, own.start + half)
                low = slice(own.start + half, own.stop)
                _softmax_tile(s_ref[top, :half], mask[:, :half, :half], v_ext[:half],
                              m_ref.at[top], acc_ref.at[top])
                _softmax_tile(s_ref[low], mask[:, half:, :], v_ext, m_ref.at[low], acc_ref.at[low])

    qpos = qi * tq + lax.broadcasted_iota(jnp.int32, (1, tq, tk), 1)
    kpos = n_full * tk + lax.broadcasted_iota(jnp.int32, (1, tq, tk), 2)
    _pipelined_causal_tiles(n_full, scores, update, sa_ref, sb_ref, kpos <= qpos)
    for hd in range(heads):
        o_ref[0, :, hd * MLA_V:(hd + 1) * MLA_V] = _softmax_finish(
            acc_ref.at[hd * tq:(hd + 1) * tq]).astype(o_ref.dtype)


def _mla(q, k, v):
    bsz, heads, seq, dk = q.shape
    tq = ATT_TQ
    assert tq == ATT_TK
    return pl.pallas_call(
        _mla_kernel,
        out_shape=jax.ShapeDtypeStruct((bsz, seq, heads * MLA_V), BF16),
        grid=(bsz, seq // tq),
        in_specs=[pl.BlockSpec((1, heads, tq, dk), lambda b, qi: (b, 0, qi, 0)),
                  pl.BlockSpec((1, heads, seq, dk), lambda b, qi: (b, 0, 0, 0)),
                  pl.BlockSpec((1, heads, seq, MLA_V), lambda b, qi: (b, 0, 0, 0))],
        out_specs=pl.BlockSpec((1, tq, heads * MLA_V), lambda b, qi: (b, qi, 0)),
        scratch_shapes=[pltpu.VMEM((heads * tq, LANES), F32), pltpu.VMEM((heads * tq, 2 * LANES), F32),
                        pltpu.VMEM((heads * tq, ATT_TK), F32), pltpu.VMEM((heads * tq, ATT_TK), F32)],
        compiler_params=_params(("parallel", "arbitrary")),
        name="mla_attention",
    )(q, k, v)


def _ev_out_kernel(ya_ref, o_ref, zb_ref, x_ref, mod_ref, g_ref, wa_ref, wb_ref, out_ref):
    d = x_ref.shape[-1]
    yb = (o_ref[0].astype(F32) * _silu(zb_ref[0].astype(F32))).astype(BF16)
    y = _dot(ya_ref[0], wa_ref[...]) + _dot(yb, wb_ref[...])
    gate = mod_ref[0][:, 2 * d:]
    out_ref[0] = x_ref[0] + gate * _rms(y, g_ref[...])


def _ev_out(y_a, o_mla, z_b, x, mod, post_g, w_out):
    bsz, seq, d = x.shape
    tm = TOKEN_TILE
    wa = w_out[:y_a.shape[-1]].astype(BF16)
    wb = w_out[y_a.shape[-1]:].astype(BF16)
    g = post_g[None]
    tok = lambda n: pl.BlockSpec((1, tm, n), lambda b, i: (b, i, 0))
    full = lambda a: pl.BlockSpec(a.shape, lambda b, i: (0,) * a.ndim)
    return pl.pallas_call(
        _ev_out_kernel,
        out_shape=jax.ShapeDtypeStruct(x.shape, F32),
        grid=(bsz, seq // tm),
        in_specs=[tok(y_a.shape[-1]), tok(o_mla.shape[-1]), tok(z_b.shape[-1]), tok(d),
                  pl.BlockSpec((1, 1, 3 * d), lambda b, i: (b, 0, 0)), full(g), full(wa), full(wb)],
        out_specs=tok(d),
        compiler_params=_params(("parallel", "parallel")),
        name="even_out_proj",
    )(y_a, o_mla, z_b, x, mod[:, None, :], g, wa, wb)


OD_Q, OD_KV, OD_Z = 1024, 256, 1024
OD_NGATE = 3 * NSA_HEADS
OD_GATE = LANES
OD_OFF = np.cumsum([0, OD_Q] + [OD_KV] * 6 + [OD_NGATE, OD_Z])
OD_COLS = -(-OD_OFF[-1] // LANES) * LANES


def _od_in_kernel(x_ref, mod_ref, g_ref, w_ref, cn_ref, sn_ref,
                  q_ref, kc_ref, vc_ref, ks_ref, vs_ref, kw_ref, vw_ref, gate_ref, z_ref):
    h = _modulated_input(x_ref, mod_ref, g_ref)
    p = _dot(h, w_ref[...])
    o = OD_OFF
    cos, sin = cn_ref[0], sn_ref[0]

    def rope(t):
        return t * cos + pltpu.roll(t, NSA_DIM // 2, axis=1) * sin

    scale = NSA_DIM ** -0.5 * LOG2E
    for hd in range(NSA_HEADS):
        t = p[:, hd * NSA_DIM:(hd + 1) * NSA_DIM]
        q_ref[0, hd // NSA_REP, hd % NSA_REP] = (rope(t) * scale).astype(BF16)
    plain = (kc_ref, vc_ref, None, vs_ref, None, vw_ref)
    roped = (None, None, ks_ref, None, kw_ref, None)
    for j in range(6):
        for g in range(NSA_GROUPS):
            lo = o[1 + j] + g * NSA_DIM
            t = p[:, lo:lo + NSA_DIM]
            if plain[j] is not None:
                plain[j][0, g] = t.astype(plain[j].dtype)
            else:
                roped[j][0, g] = rope(t).astype(BF16)
    lane = lax.broadcasted_iota(jnp.int32, cos.shape, 1)
    gate_ref[0] = jnp.where(lane < OD_NGATE, jax.nn.sigmoid(p[:, o[7]:o[7] + LANES]), 0.0)
    z_ref[0] = p[:, o[8]:o[9]].astype(BF16)


def _od_in(x, mod, pre_g, w_in, cos_n, sin_n):
    bsz, seq, d = x.shape
    tm = TOKEN_TILE
    w = jnp.pad(w_in, ((0, 0), (0, OD_COLS - w_in.shape[1]))).astype(BF16)
    g = pre_g[None]
    tok = lambda n: pl.BlockSpec((1, tm, n), lambda b, i: (b, i, 0))
    full = lambda a: pl.BlockSpec(a.shape, lambda b, i: (0,) * a.ndim)
    q_spec = pl.BlockSpec((1, NSA_GROUPS, NSA_REP, tm, NSA_DIM), lambda b, i: (b, 0, 0, i, 0))
    kv_spec = pl.BlockSpec((1, NSA_GROUPS, tm, NSA_DIM), lambda b, i: (b, 0, i, 0))
    kv_shape = jax.ShapeDtypeStruct((bsz, NSA_GROUPS, seq, NSA_DIM), BF16)
    cmp_shape = jax.ShapeDtypeStruct(kv_shape.shape, F32)
    return pl.pallas_call(
        _od_in_kernel,
        out_shape=(jax.ShapeDtypeStruct((bsz, NSA_GROUPS, NSA_REP, seq, NSA_DIM), BF16),)
        + (cmp_shape,) * 2 + (kv_shape,) * 4
        + (jax.ShapeDtypeStruct((bsz, seq, OD_GATE), F32), jax.ShapeDtypeStruct((bsz, seq, OD_Z), BF16)),
        grid=(bsz, seq // tm),
        in_specs=[tok(d), pl.BlockSpec((1, 1, 3 * d), lambda b, i: (b, 0, 0)), full(g), full(w),
                  tok(LANES), tok(LANES)],
        out_specs=(q_spec,) + (kv_spec,) * 6 + (tok(OD_GATE), tok(OD_Z)),
        compiler_params=_params(("parallel", "parallel")),
        name="odd_in_proj",
    )(x, mod[:, None, :], g, w, cos_n, sin_n)


def _compress_kernel(x_ref, pe_ref, w1_ref, w2_ref, cos_ref, sin_ref, o_ref, *, use_rope):
    seq, d = x_ref.shape[2:]
    nb = seq // CMP_STRIDE
    lo = _dot(pe_ref[...], w1_ref[...])[0:1]
    hi = jnp.zeros((nb, w1_ref.shape[1]), F32)
    for l in range(0, CMP_STRIDE, 2):
        xl = jnp.concatenate([x_ref[0, 0, pl.ds(l + k, nb, stride=CMP_STRIDE), :].astype(BF16)
                              for k in range(2)], axis=1)
        lo = lo + _dot(xl, w1_ref[l * d:(l + 2) * d])
        hi = hi + _dot(xl, w1_ref[(CMP_STRIDE + l) * d:(CMP_STRIDE + l + 2) * d])
    row = lax.broadcasted_iota(jnp.int32, hi.shape, 0)
    pre = lo + jnp.where(row < nb - 1, pltpu.roll(hi, nb - 1, axis=0), 0.0)
    out = _dot(_gelu_tanh(pre).astype(BF16), w2_ref[...])
    if use_rope:
        out = out * cos_ref[0] + pltpu.roll(out, NSA_DIM // 2, axis=1) * sin_ref[0]
    o_ref[0, 0] = out.astype(BF16)


def _compress(kv, pe, w1, w2, cos_end, sin_end, use_rope):
    bsz, groups, seq, d = kv.shape
    nb = seq // CMP_STRIDE
    pe_rows = jnp.broadcast_to(pe.reshape(1, CMP_LEN * d), (SUBLANES, CMP_LEN * d)).astype(BF16)
    w1b, w2b = w1.astype(BF16), w2.astype(BF16)
    full = lambda a: pl.BlockSpec(a.shape, lambda b, g: (0,) * a.ndim)
    end = pl.BlockSpec((1, nb, d), lambda b, g: (b, 0, 0))
    return pl.pallas_call(
        functools.partial(_compress_kernel, use_rope=use_rope),
        out_shape=jax.ShapeDtypeStruct((bsz, groups, nb, d), BF16),
        grid=(bsz, groups),
        in_specs=[pl.BlockSpec((1, 1, seq, d), lambda b, g: (b, g, 0, 0)),
                  full(pe_rows), full(w1b), full(w2b), end, end],
        out_specs=pl.BlockSpec((1, 1, nb, d), lambda b, g: (b, g, 0, 0)),
        compiler_params=_params(("parallel", "parallel")),
        name="nsa_compress",
    )(kv, pe_rows, w1b, w2b, cos_end, sin_end)


def _cmp_sel_kernel(q_ref, kc_ref, vc_ref, pool_ref, o_ref, sel_ref, *, n_tiles, **static):
    for k in range(n_tiles):
        @pl.when(pl.program_id(1) == k)
        def _(k=k):
            for g in range(q_ref.shape[1]):
                _cmp_sel_group(k, g, q_ref, kc_ref, vc_ref, pool_ref, o_ref, sel_ref, **static)


def _cmp_sel_group(qi, g, q_ref, kc_ref, vc_ref, pool_ref, o_ref, sel_ref, *, n_cmp, n_sel, n_top):
    rep, tq, d = q_ref.shape[2:]
    last_q = qi * tq + tq - 1
    n_vis = min(n_cmp, max(last_q - (CMP_LEN - 1), 0) // CMP_STRIDE + 1)
    nb = min(kc_ref.shape[2], -(-n_vis // LANES) * LANES)
    n_blk = min(n_sel, last_q // SEL_BLOCK + 1)
    n_sel = min(n_sel, -(-n_blk // SUBLANES) * SUBLANES)
    q = q_ref[0, g].reshape(rep * tq, d)
    s = _dot_nt(q, kc_ref[0, g, :nb, :]).reshape(rep, tq, nb)
    qpos = qi * tq + lax.broadcasted_iota(jnp.int32, (tq, nb), 0)
    blk = lax.broadcasted_iota(jnp.int32, (tq, nb), 1)
    mask = ((blk * CMP_STRIDE + (CMP_LEN - 1) <= qpos) & (blk < n_cmp))[None]
    s = jnp.where(mask, s, NEG_INF)
    m = jnp.max(s, axis=-1, keepdims=True)
    e = jnp.where(mask, jnp.exp2(s - m), 0.0)
    p = e / jnp.maximum(jnp.sum(e, axis=-1, keepdims=True), TINY)
    o = _dot(p.reshape(rep * tq, nb).astype(BF16), vc_ref[0, g, :nb, :])
    o_ref[0, g] = o.reshape(rep, tq, d).astype(o_ref.dtype)

    rows = pool_ref.shape[0]
    imp = _dot_nt(pool_ref[:, :nb], jnp.sum(p, axis=0), precision=HIGHEST)[:n_sel]
    bid = lax.broadcasted_iota(jnp.int32, (n_sel, tq), 0)
    cur = (qi * tq + lax.broadcasted_iota(jnp.int32, (n_sel, tq), 1)) // SEL_BLOCK
    forced = (bid == 0) | (bid == cur) | (bid == cur - 1)
    imp = jnp.where(forced, FORCE_SCORE, jnp.where(bid <= cur, imp, -1.0))
    groups = [imp[g:g + SUBLANES] for g in range(0, n_sel, SUBLANES)]
    sub = lax.broadcasted_iota(jnp.int32, (SUBLANES, tq), 0)
    ranks = [jnp.zeros((SUBLANES, tq), F32) for _ in groups]
    for j in range(n_blk):
        vj = jnp.broadcast_to(imp[j:j + 1, :], (SUBLANES, tq))
        for gi, grp in enumerate(groups):
            lo = gi * SUBLANES
            if lo > j:
                first = vj >= grp
            elif lo + SUBLANES - 1 <= j:
                first = vj > grp
            else:
                first = (vj > grp) | ((vj == grp) & (sub > j - lo))
            ranks[gi] = ranks[gi] + jnp.where(first, 1.0, 0.0)
    rank = jnp.concatenate(ranks, axis=0)
    bias = jnp.where((rank < n_top) & (bid <= cur), 0.0, NEG_INF)
    bias = jnp.concatenate([bias, jnp.full((rows - n_sel, tq), NEG_INF, F32)], axis=0)
    sel_ref[0, g] = bias.T.astype(sel_ref.dtype)


def _cmp_sel(q, kc, vc):
    bsz, groups, rep, seq, d = q.shape
    nb = kc.shape[2]
    tq = CMP_TQ
    n_cmp = (seq - CMP_LEN) // CMP_STRIDE + 1
    n_sel = seq // SEL_BLOCK
    n_top = min(SEL_TOP, n_sel)
    ratio = SEL_BLOCK // CMP_STRIDE
    assert n_sel <= LANES and n_sel * ratio == nb
    pool_np = np.zeros((LANES, nb), np.float32)
    pool_np[np.arange(nb) // ratio, np.arange(nb)] = 1.0
    pool = jnp.asarray(pool_np)
    kv_spec = pl.BlockSpec((1, groups, nb, d), lambda b, i: (b, 0, 0, 0))
    q_spec = pl.BlockSpec((1, groups, rep, tq, d), lambda b, i: (b, 0, 0, i, 0))
    return pl.pallas_call(
        functools.partial(_cmp_sel_kernel, n_tiles=seq // tq, n_cmp=n_cmp, n_sel=n_sel, n_top=n_top),
        out_shape=(jax.ShapeDtypeStruct(q.shape, BF16),
                   jax.ShapeDtypeStruct((bsz, groups, seq, LANES), BF16)),
        grid=(bsz, seq // tq),
        in_specs=[q_spec, kv_spec, kv_spec, pl.BlockSpec(pool.shape, lambda b, i: (0, 0))],
        out_specs=(q_spec, pl.BlockSpec((1, groups, tq, LANES), lambda b, i: (b, 0, i, 0))),
        compiler_params=_params(("parallel", "parallel")),
        name="nsa_cmp_select",
    )(q, kc, vc, pool)


def _sel_win_kernel(q_ref, bias_ref, k_ref, blk_ref, v_ref, kw_ref, vw_ref, o_ref, ow_ref,
                    m_ref, acc_ref, sa_ref, sb_ref, *, span):
    qi = pl.program_id(2)
    rep, tq, d = q_ref.shape[2:]
    tk = NSA_TK
    part = rep // SEL_CHAINS
    q_ext = [jnp.concatenate([q_ref[0, 0, c * part:(c + 1) * part].reshape(part * tq, d),
                              jnp.concatenate([bias_ref[0, 0]] * part, axis=0)], axis=1)
             for c in range(SEL_CHAINS)]
    _softmax_init(m_ref, acc_ref)
    n_full = qi * tq // tk

    owns = [slice(c * part * tq, (c + 1) * part * tq) for c in range(SEL_CHAINS)]

    def scores(ki, s_ref):
        rows = pl.ds(pl.multiple_of(ki * tk, tk), tk)
        k_ext = jnp.concatenate([k_ref[0, 0, rows, :], blk_ref[rows, :]], axis=1)
        for c, own in enumerate(owns):
            s_ref[own] = _dot_nt(q_ext[c], k_ext)

    def update(s_ref, ki, mask):
        def run(width):
            rows = pl.ds(pl.multiple_of(ki * tk, tk), width)
            v_ext = _with_ones(v_ref[0, 0, rows, :])
            sub = None if mask is None else mask[:, :, :width]
            for own in owns:
                _softmax_tile(s_ref[own, :width], sub, v_ext, m_ref.at[own], acc_ref.at[own])

        if mask is None:
            run(tk)
        else:
            short = qi * tq + tq <= ki * tk + tk // 2
            pl.when(short)(lambda: run(tk // 2))
            pl.when(jnp.logical_not(short))(lambda: run(tk))

    def window():
        start = pl.multiple_of(jnp.maximum(qi * tq + tq - span, 0), tq)
        rows = pl.ds(start, span)
        wq = qi * tq + lax.broadcasted_iota(jnp.int32, (1, tq, span), 1)
        wk = start + lax.broadcasted_iota(jnp.int32, (1, tq, span), 2)
        mask = (wq - wk >= 0) & (wq - wk < WINDOW)
        k, v_ext = kw_ref[0, 0, rows, :], _with_ones(vw_ref[0, 0, rows, :])
        wpart = rep // WIN_CHAINS
        for h0 in range(0, rep, wpart):
            q = q_ref[0, 0, h0:h0 + wpart].reshape(wpart * tq, d)
            s = jnp.where(mask, _dot_nt(q, k).reshape(wpart, tq, span), NEG_INF)
            m = jnp.max(s, axis=-1, keepdims=True)
            e = jnp.where(mask, jnp.exp2(s - m), 0.0).reshape(wpart * tq, span)
            acc = _dot(e.astype(BF16), v_ext)
            o = acc[:, :LANES] / jnp.maximum(acc[:, LANES:], TINY)
            ow_ref[0, 0, h0:h0 + wpart] = o.reshape(wpart, tq, d).astype(ow_ref.dtype)

    qpos = qi * tq + lax.broadcasted_iota(jnp.int32, (1, tq, tk), 1)
    kpos = n_full * tk + lax.broadcasted_iota(jnp.int32, (1, tq, tk), 2)
    _pipelined_causal_tiles(n_full, scores, update, sa_ref, sb_ref, kpos <= qpos, filler=window)
    o_ref[0, 0] = _softmax_finish(acc_ref).reshape(rep, tq, d).astype(o_ref.dtype)


def _sel_win_attention(q, bias, k, v, kw, vw):
    bsz, groups, rep, seq, d = q.shape
    tq = NSA_TQ
    assert NSA_TK % tq == 0
    span = (-(-(WINDOW - 1) // tq) + 1) * tq
    assert span <= seq
    onehot_np = np.zeros((seq, LANES), np.float32)
    onehot_np[np.arange(seq), np.arange(seq) // SEL_BLOCK] = 1.0
    onehot = jnp.asarray(onehot_np, BF16)
    q_spec = pl.BlockSpec((1, 1, rep, tq, d), lambda b, g, qi: (b, g, 0, qi, 0))
    kv_spec = pl.BlockSpec((1, 1, seq, d), lambda b, g, qi: (b, g, 0, 0))
    out = jax.ShapeDtypeStruct(q.shape, BF16)
    return pl.pallas_call(
        functools.partial(_sel_win_kernel, span=span),
        out_shape=(out, out),
        grid=(bsz, groups, seq // tq),
        in_specs=[q_spec, pl.BlockSpec((1, 1, tq, LANES), lambda b, g, qi: (b, g, qi, 0)),
                  kv_spec, pl.BlockSpec((seq, LANES), lambda b, g, qi: (0, 0)), kv_spec, kv_spec, kv_spec],
        out_specs=(q_spec, q_spec),
        scratch_shapes=[pltpu.VMEM((rep * tq, LANES), F32), pltpu.VMEM((rep * tq, 2 * LANES), F32),
                        pltpu.VMEM((rep * tq, NSA_TK), F32), pltpu.VMEM((rep * tq, NSA_TK), F32)],
        compiler_params=_params(("parallel", "parallel", "arbitrary")),
        name="nsa_selected_window_attention",
    )(q, bias, k, onehot, v, kw, vw)


def _od_out_kernel(oc_ref, os_ref, ow_ref, gate_ref, spread_ref, z_ref, x_ref, mod_ref, g_ref, w_ref, out_ref):
    d = x_ref.shape[-1]
    gates = gate_ref[0]
    hi = gates.astype(BF16)
    lo = (gates - hi.astype(F32)).astype(BF16)
    wide = _dot(jnp.concatenate([hi, lo], axis=-1), spread_ref[...])
    z = z_ref[0].astype(F32)
    parts = []
    for hd in range(NSA_HEADS):
        g, r = hd // NSA_REP, hd % NSA_REP
        c = 3 * hd * NSA_DIM
        o = (wide[:, c:c + NSA_DIM] * oc_ref[0, g, r].astype(F32)
             + wide[:, c + NSA_DIM:c + 2 * NSA_DIM] * os_ref[0, g, r].astype(F32)
             + wide[:, c + 2 * NSA_DIM:c + 3 * NSA_DIM] * ow_ref[0, g, r].astype(F32))
        parts.append((o * _silu(z[:, hd * NSA_DIM:(hd + 1) * NSA_DIM])).astype(BF16))
    y = _dot(jnp.concatenate(parts, axis=-1), w_ref[...])
    gate = mod_ref[0][:, 2 * d:]
    out_ref[0] = x_ref[0] + gate * _rms(y, g_ref[...])


def _od_out(o_cmp, o_sel, o_win, gates, z, x, mod, post_g, w_out):
    bsz, seq, d = x.shape
    tm = TOKEN_TILE
    w = w_out.astype(BF16)
    g = post_g[None]
    n_gate = 3 * NSA_HEADS
    spread_np = np.zeros((2 * OD_GATE, n_gate * NSA_DIM), np.float32)
    for c in range(n_gate):
        spread_np[[c, OD_GATE + c], c * NSA_DIM:(c + 1) * NSA_DIM] = 1.0
    spread = jnp.asarray(spread_np, BF16)
    tok = lambda n: pl.BlockSpec((1, tm, n), lambda b, i: (b, i, 0))
    full = lambda a: pl.BlockSpec(a.shape, lambda b, i: (0,) * a.ndim)
    o_spec = pl.BlockSpec((1, NSA_GROUPS, NSA_REP, tm, NSA_DIM), lambda b, i: (b, 0, 0, i, 0))
    return pl.pallas_call(
        _od_out_kernel,
        out_shape=jax.ShapeDtypeStruct(x.shape, F32),
        grid=(bsz, seq // tm),
        in_specs=[o_spec, o_spec, o_spec, tok(OD_GATE), full(spread), tok(OD_Z), tok(d),
                  pl.BlockSpec((1, 1, 3 * d), lambda b, i: (b, 0, 0)), full(g), full(w)],
        out_specs=tok(d),
        compiler_params=_params(("parallel", "parallel")),
        name="odd_out_proj",
    )(o_cmp, o_sel, o_win, gates, spread, z, x, mod[:, None, :], g, w)


def _even_layer(x, mod, pre_g, post_g, tabs, w_in, lam_re, lam_im, log_dt, b_re, b_im, c_re, c_im,
                d_skip, w_glu, b_glu, q_norm_g, kv_norm_g, w_uq, w_ukv, w_out):
    cos_m, sin_m = tabs[0], tabs[1]
    u, z_a, z_b, q, k, v = _ev_in(x, mod, pre_g, w_in, q_norm_g, kv_norm_g, w_uq, w_ukv, cos_m, sin_m)
    y_a = _s5(u, z_a, lam_re, lam_im, log_dt, b_re, b_im, c_re, c_im, d_skip, w_glu, b_glu)
    o_mla = _mla(q, k, v)
    return _ev_out(y_a, o_mla, z_b, x, mod, post_g, w_out)


def _odd_layer(x, mod, pre_g, post_g, tabs, end_tabs, w_in, k_pe, k_w1, k_w2, v_pe, v_w1, v_w2, w_out):
    cos_n, sin_n = tabs[2], tabs[3]
    cos_end, sin_end = end_tabs[2], end_tabs[3]
    q, k_c, v_c, k_s, v_s, k_w, v_w, gates, z = _od_in(x, mod, pre_g, w_in, cos_n, sin_n)
    kc = _compress(k_c, k_pe, k_w1, k_w2, cos_end, sin_end, True)
    vc = _compress(v_c, v_pe, v_w1, v_w2, cos_end, sin_end, False)
    o_cmp, sel = _cmp_sel(q, kc, vc)
    o_sel, o_win = _sel_win_attention(q, sel, k_s, v_s, k_w, v_w)
    return _od_out(o_cmp, o_sel, o_win, gates, z, x, mod, post_g, w_out)


def kernel(x, c, positions, pre_norm_g, post_norm_g, w_ada, b_ada, ev_w_in, ev_lam_re, ev_lam_im, ev_log_dt, ev_b_re, ev_b_im, ev_c_re, ev_c_im, ev_d_skip, ev_w_glu, ev_b_glu, ev_q_norm_g, ev_kv_norm_g, ev_w_uq, ev_w_ukv, ev_w_out, od_w_in, od_cmp_k_pe, od_cmp_k_w1, od_cmp_k_w2, od_cmp_v_pe, od_cmp_v_w1, od_cmp_v_w2, od_w_out):
    depth = pre_norm_g.shape[0]
    tabs = _rope_tables(positions)
    seq = positions.shape[1]
    pos_end = positions[:, CMP_LEN - 1::CMP_STRIDE]
    pos_end = jnp.pad(pos_end, ((0, 0), (0, seq // CMP_STRIDE - pos_end.shape[1])))
    end_tabs = _rope_tables(pos_end)
    mods = _modulation(c, w_ada, b_ada)
    for layer in range(depth):
        i = layer // 2
        if layer % 2 == 0:
            x = _even_layer(x, mods[layer], pre_norm_g[layer], post_norm_g[layer], tabs,
                            ev_w_in[i], ev_lam_re[i], ev_lam_im[i], ev_log_dt[i], ev_b_re[i], ev_b_im[i],
                            ev_c_re[i], ev_c_im[i], ev_d_skip[i], ev_w_glu[i], ev_b_glu[i],
                            ev_q_norm_g[i], ev_kv_norm_g[i], ev_w_uq[i], ev_w_ukv[i], ev_w_out[i])
        else:
            x = _odd_layer(x, mods[layer], pre_norm_g[layer], post_norm_g[layer], tabs, end_tabs,
                           od_w_in[i], od_cmp_k_pe[i], od_cmp_k_w1[i], od_cmp_k_w2[i],
                           od_cmp_v_pe[i], od_cmp_v_w1[i], od_cmp_v_w2[i], od_w_out[i])
    return x
```
